```python
import math
import jax, jax.numpy as jnp
from jax import lax
import numpy as np

D_MODEL = 1024
BATCH = 8
SEQ = 2048
DEPTH = 1

D_MIX = D_MODEL
DN_WIDTH = D_MIX // 2
DN_HEAD_K = 128
DN_HEAD_V = 128
DN_HEADS = DN_WIDTH // DN_HEAD_V
CONV_WIDTH = 4
CHUNK = 64
DF_WIDTH = D_MIX - DN_WIDTH
DF_HEAD_QK = 64
DF_HEAD_V = 2 * DF_HEAD_QK
DF_HEADS = DF_WIDTH // DF_HEAD_V
Q_BLOCK = 128
EPS = 1e-6

DN_QK = DN_HEADS * DN_HEAD_K
DN_V = DN_HEADS * DN_HEAD_V
DF_QK = DF_HEADS * 2 * DF_HEAD_QK
DF_V = DF_HEADS * DF_HEAD_V
COL_SIZES = (DN_QK, DN_QK, DN_V, DN_V, DN_HEADS, DN_HEADS, DF_QK, DF_QK, DF_V, DF_V)
IN_COLS = sum(COL_SIZES)
COL_OFFSETS = tuple(int(o) for o in np.cumsum(COL_SIZES)[:-1])

kernel_name = "hybrid_gdn_diffattn_parallel_heads"


def rmsnorm(x, gain):
    xf = x.astype(jnp.float32)
    y = xf * lax.rsqrt(jnp.mean(xf * xf, axis=-1, keepdims=True) + EPS)
    return (y * gain.astype(jnp.float32)).astype(x.dtype)


def l2norm(x):
    return x * lax.rsqrt(jnp.sum(x * x, axis=-1, keepdims=True) + EPS)


def causal_depthwise_conv(x, w):
    c = x.shape[-1]
    return lax.conv_general_dilated(
        x, w[:, None, :].astype(x.dtype), window_strides=(1,),
        padding=[(CONV_WIDTH - 1, 0)], dimension_numbers=("NWC", "WIO", "NWC"),
        feature_group_count=c)


def gated_delta_rule(q, k, v, g, beta):
    b, t, h, dk = q.shape
    dv = v.shape[-1]
    n = t // CHUNK

    def chunks(a):
        a = jnp.moveaxis(a, 2, 1)
        return a.reshape((b, h, n, CHUNK) + a.shape[3:])

    qc, kc, vc = chunks(q), chunks(k), chunks(v)
    gc = jnp.cumsum(chunks(g), axis=-1)
    bc = chunks(beta)
    k_beta = kc * bc[..., None]
    v_beta = vc * bc[..., None]

    tril = jnp.tril(jnp.ones((CHUNK, CHUNK), bool))
    strict = jnp.tril(jnp.ones((CHUNK, CHUNK), bool), -1)
    diff = gc[..., :, None] - gc[..., None, :]
    decay = jnp.exp(jnp.where(tril, diff, -jnp.inf))

    m = jnp.where(strict, jnp.einsum('bhncd,bhnsd->bhncs', k_beta, kc) * decay, 0.0)
    lmat = m + jnp.eye(CHUNK, dtype=m.dtype)
    rhs = jnp.concatenate([v_beta, k_beta * jnp.exp(gc)[..., None]], axis=-1)
    sol = lax.linalg.triangular_solve(lmat, rhs, left_side=True, lower=True,
                                      unit_diagonal=True)
    u, w = sol[..., :dv], sol[..., dv:]
    a_qk = jnp.einsum('bhncd,bhnsd->bhncs', qc, kc) * decay

    def step(state, inp):
        q_i, k_i, u_i, w_i, g_i, a_i = inp
        v_new = u_i - jnp.einsum('bhcd,bhdv->bhcv', w_i, state)
        o_i = (jnp.einsum('bhcd,bhdv->bhcv', q_i * jnp.exp(g_i)[..., None], state)
               + jnp.einsum('bhcs,bhsv->bhcv', a_i, v_new))
        g_last = g_i[..., -1]
        k_dec = k_i * jnp.exp(g_last[..., None] - g_i)[..., None]
        state = (state * jnp.exp(g_last)[..., None, None]
                 + jnp.einsum('bhcd,bhcv->bhdv', k_dec, v_new))
        return state, o_i

    xs = tuple(jnp.moveaxis(a, 2, 0) for a in (qc, kc, u, w, gc, a_qk))
    s0 = jnp.zeros((b, h, dk, dv), jnp.float32)
    _, o = lax.scan(step, s0, xs)
    o = jnp.transpose(o, (1, 0, 3, 2, 4))
    return o.reshape(b, t, h, dv)


def gated_deltanet_group(q, k, v, z, bb, aa, conv_w, a_log, dt_bias, out_gain):
    b, t, _ = q.shape
    qkv = jax.nn.silu(causal_depthwise_conv(jnp.concatenate([q, k, v], -1), conv_w))
    q, k, v = jnp.split(qkv, [DN_QK, 2 * DN_QK], axis=-1)
    q = l2norm(q.reshape(b, t, DN_HEADS, DN_HEAD_K).astype(jnp.float32)) * (DN_HEAD_K ** -0.5)
    k = l2norm(k.reshape(b, t, DN_HEADS, DN_HEAD_K).astype(jnp.float32))
    v = v.reshape(b, t, DN_HEADS, DN_HEAD_V).astype(jnp.float32)
    beta = jax.nn.sigmoid(bb.astype(jnp.float32))
    g = -jnp.exp(a_log.astype(jnp.float32)) * jax.nn.softplus(
        aa.astype(jnp.float32) + dt_bias.astype(jnp.float32))
    o = gated_delta_rule(q, k, v, g, beta)
    zf = z.reshape(b, t, DN_HEADS, DN_HEAD_V).astype(jnp.float32)
    o = rmsnorm(o, out_gain) * jax.nn.silu(zf)
    return o.reshape(b, t, DN_V).astype(z.dtype)


def diff_attention_group(q, k, v, z, q_gain, k_gain, lq1, lk1, lq2, lk2, out_gain, lambda_init):
    b, t, _ = q.shape
    q = rmsnorm(q.reshape(b, t, DF_HEADS, 2, DF_HEAD_QK), q_gain) * (DF_HEAD_QK ** -0.5)
    k = rmsnorm(k.reshape(b, t, DF_HEADS, 2, DF_HEAD_QK), k_gain)
    v = v.reshape(b, t, DF_HEADS, DF_HEAD_V)
    lam = (jnp.exp(jnp.sum(lq1.astype(jnp.float32) * lk1.astype(jnp.float32)))
           - jnp.exp(jnp.sum(lq2.astype(jnp.float32) * lk2.astype(jnp.float32)))
           + lambda_init)
    nb = t // Q_BLOCK
    qb = jnp.moveaxis(q.reshape(b, nb, Q_BLOCK, DF_HEADS, 2, DF_HEAD_QK), 1, 0)
    starts = jnp.arange(nb, dtype=jnp.int32) * Q_BLOCK
    key_pos = jnp.arange(t, dtype=jnp.int32)
    neg = jnp.finfo(jnp.float32).min

    def block(args):
        q_blk, start = args
        s = jnp.einsum('bqhcd,bkhcd->bhcqk', q_blk, k).astype(jnp.float32)
        q_pos = start + jnp.arange(Q_BLOCK, dtype=jnp.int32)
        mask = key_pos[None, :] <= q_pos[:, None]
        p = jax.nn.softmax(jnp.where(mask, s, neg), axis=-1)
        wts = p[:, :, 0] - lam * p[:, :, 1]
        return jnp.einsum('bhqk,bkhv->bqhv', wts.astype(v.dtype), v)

    o = lax.map(block, (qb, starts))
    o = jnp.moveaxis(o, 0, 1).reshape(b, t, DF_HEADS, DF_HEAD_V)
    o = rmsnorm(o, out_gain).astype(jnp.float32) * (1.0 - lambda_init)
    zf = z.reshape(b, t, DF_HEADS, DF_HEAD_V).astype(jnp.float32)
    o = o * jax.nn.silu(zf)
    return o.reshape(b, t, DF_V).astype(z.dtype)


def setup_inputs(seed: int = 0) -> dict:
    key = jax.random.key(seed)
    ks = jax.random.split(key, 16)
    f32 = jnp.float32
    x = jax.random.normal(ks[0], (BATCH, SEQ, D_MODEL), f32)
    norm_gain = 1.0 + 0.05 * jax.random.normal(ks[1], (DEPTH, D_MODEL), f32)
    w_in = jax.random.normal(ks[2], (DEPTH, D_MODEL, IN_COLS), f32) * D_MODEL ** -0.5
    conv_w = jax.random.normal(ks[3], (DEPTH, CONV_WIDTH, DN_QK * 2 + DN_V), f32) * CONV_WIDTH ** -0.5
    a_log = jnp.log(jax.random.uniform(ks[4], (DEPTH, DN_HEADS), f32, 1.0, 16.0))
    dt = jnp.exp(jax.random.uniform(ks[5], (DEPTH, DN_HEADS), f32,
                                    math.log(1e-3), math.log(1e-1)))
    dt_bias = dt + jnp.log(-jnp.expm1(-dt))
    dn_out_gain = 1.0 + 0.05 * jax.random.normal(ks[6], (DEPTH, DN_HEAD_V), f32)
    q_gain = 1.0 + 0.05 * jax.random.normal(ks[7], (DEPTH, DF_HEAD_QK), f32)
    k_gain = 1.0 + 0.05 * jax.random.normal(ks[8], (DEPTH, DF_HEAD_QK), f32)
    lambda_q1 = 0.1 * jax.random.normal(ks[9], (DEPTH, DF_HEAD_QK), f32)
    lambda_k1 = 0.1 * jax.random.normal(ks[10], (DEPTH, DF_HEAD_QK), f32)
    lambda_q2 = 0.1 * jax.random.normal(ks[11], (DEPTH, DF_HEAD_QK), f32)
    lambda_k2 = 0.1 * jax.random.normal(ks[12], (DEPTH, DF_HEAD_QK), f32)
    df_out_gain = 1.0 + 0.05 * jax.random.normal(ks[13], (DEPTH, DF_HEAD_V), f32)
    w_out = jax.random.normal(ks[14], (DEPTH, D_MIX, D_MODEL), f32) * D_MIX ** -0.5
    return {"x": x, "norm_gain": norm_gain, "w_in": w_in, "conv_w": conv_w,
            "a_log": a_log, "dt_bias": dt_bias, "dn_out_gain": dn_out_gain,
            "q_gain": q_gain, "k_gain": k_gain, "lambda_q1": lambda_q1,
            "lambda_k1": lambda_k1, "lambda_q2": lambda_q2, "lambda_k2": lambda_k2,
            "df_out_gain": df_out_gain, "w_out": w_out}


def reference(x, norm_gain, w_in, conv_w, a_log, dt_bias, dn_out_gain, q_gain, k_gain,
              lambda_q1, lambda_k1, lambda_q2, lambda_k2, df_out_gain, w_out):
    for l in range(DEPTH):
        lambda_init = 0.8 - 0.6 * math.exp(-0.3 * l)
        h = rmsnorm(x, norm_gain[l])
        proj = jnp.einsum('btd,dc->btc', h, w_in[l])
        (dn_q, dn_k, dn_v, dn_z, dn_b, dn_a,
         df_q, df_k, df_v, df_z) = jnp.split(proj, COL_OFFSETS, axis=-1)
        o_dn = gated_deltanet_group(dn_q, dn_k, dn_v, dn_z, dn_b, dn_a, conv_w[l],
                                    a_log[l], dt_bias[l], dn_out_gain[l])
        o_df = diff_attention_group(df_q, df_k, df_v, df_z, q_gain[l], k_gain[l],
                                    lambda_q1[l], lambda_k1[l], lambda_q2[l], lambda_k2[l],
                                    df_out_gain[l], lambda_init)
        mixed = jnp.concatenate([o_dn, o_df], axis=-1)
        x = x + jnp.einsum('btc,cd->btd', mixed, w_out[l])
    return x
```

```python
import functools
import math

import jax
import jax.numpy as jnp
from jax import lax
from jax.experimental import pallas as pl
from jax.experimental.pallas import tpu as pltpu

F32 = jnp.float32
BF16 = jnp.bfloat16

D_MODEL = 1024
DN_HEADS = 4
DN_HEAD = 128
DN_WIDTH = DN_HEADS * DN_HEAD
CONV_WIDTH = 4
CHUNK = 64
DF_HEADS = 4
DF_HEAD_QK = 64
DF_HEAD_V = 128
DF_WIDTH = DF_HEADS * DF_HEAD_V
EPS = 1e-6
LAMBDA_INIT = 0.8 - 0.6 * math.exp(-0.3 * 0)

N_GATES = 2 * DN_HEADS
PROJ_COLS = 4 * DN_WIDTH + 4 * DF_WIDTH
LANES = 128
VMEM_LIMIT = 56 * 1024 * 1024

_BLK_DN_Z = 3 * DN_HEADS
_BLK_DF_Q = 4 * DN_HEADS
_BLK_DF_K = _BLK_DF_Q + DF_HEADS
_BLK_DF_V = _BLK_DF_K + DF_HEADS
_BLK_DF_Z = _BLK_DF_V + DF_HEADS


def _silu(x):
    return x * jax.nn.sigmoid(x)


def _dot(a, b):
    return jnp.dot(a.astype(BF16), b.astype(BF16), preferred_element_type=F32)


def _dot_nt(a, b):
    return lax.dot_general(a.astype(BF16), b.astype(BF16), (((1,), (1,)), ((), ())),
                           preferred_element_type=F32)


def _dot_tn(a, b):
    return lax.dot_general(a.astype(BF16), b.astype(BF16), (((0,), (0,)), ((), ())),
                           preferred_element_type=F32)


IN_TM = 512
IN_TN = 512


def _inproj_kernel(x_ref, gain_ref, w_ref, wg_ref, proj_ref, gate_ref):
    x = x_ref[...]
    ms = jnp.mean(x * x, axis=-1, keepdims=True)
    h = (x * lax.rsqrt(ms + EPS) * gain_ref[...]).astype(BF16)
    for j in range(PROJ_COLS // IN_TN):
        cols = slice(j * IN_TN, (j + 1) * IN_TN)
        proj_ref[:, cols] = jnp.dot(h, w_ref[:, cols], preferred_element_type=F32).astype(BF16)
    gate_ref[...] = jnp.dot(h, wg_ref[...], preferred_element_type=F32)[:, :N_GATES]


def _in_projection(x2d, gain, w_main, w_gate):
    m = x2d.shape[0]
    return pl.pallas_call(
        _inproj_kernel,
        grid=(m // IN_TM,),
        in_specs=[
            pl.BlockSpec((IN_TM, D_MODEL), lambda i: (i, 0)),
            pl.BlockSpec((1, D_MODEL), lambda i: (0, 0)),
            pl.BlockSpec((D_MODEL, PROJ_COLS), lambda i: (0, 0)),
            pl.BlockSpec((D_MODEL, LANES), lambda i: (0, 0)),
        ],
        out_specs=[
            pl.BlockSpec((IN_TM, PROJ_COLS), lambda i: (i, 0)),
            pl.BlockSpec((IN_TM, N_GATES), lambda i: (i, 0)),
        ],
        out_shape=[
            jax.ShapeDtypeStruct((m, PROJ_COLS), BF16),
            jax.ShapeDtypeStruct((m, N_GATES), F32),
        ],
        compiler_params=pltpu.CompilerParams(
            dimension_semantics=("parallel",), vmem_limit_bytes=VMEM_LIMIT),
        name="in_projection",
    )(x2d, gain, w_main, w_gate)


DN_ROWS = 256
QKV_COLS = 3 * DN_WIDTH


def _gdn_kernel(qkv_ref, z_ref, gate_ref, convw_ref, alog_ref, dtb_ref, gain_ref,
                out_ref, q_s, k_s, v_s, gate_s, state_s):
    t = qkv_ref.shape[0]
    n_chunks = t // CHUNK

    def conv_rows(r, carry):
        r0 = pl.multiple_of(r * DN_ROWS, DN_ROWS)
        p0 = pl.multiple_of(jnp.maximum(r0 - 16, 0), 16)
        for g in range(QKV_COLS // LANES):
            cols = slice(g * LANES, (g + 1) * LANES)
            cur = qkv_ref[pl.ds(r0, DN_ROWS), cols].astype(F32)
            prev = qkv_ref[pl.ds(p0, 16), cols].astype(F32)[8:]
            prev = jnp.where(r > 0, prev, 0.0)
            xb = jnp.concatenate([prev, cur], axis=0)
            w = convw_ref[:, cols]
            acc = xb * w[CONV_WIDTH - 1:CONV_WIDTH]
            for j in range(1, CONV_WIDTH):
                acc = acc + pltpu.roll(xb, j, 0) * w[CONV_WIDTH - 1 - j:CONV_WIDTH - j]
            y = _silu(acc[8:])
            if g < 2 * DN_HEADS:
                y = y * lax.rsqrt(jnp.sum(y * y, axis=-1, keepdims=True) + EPS)
            hcols = slice((g % DN_HEADS) * LANES, (g % DN_HEADS + 1) * LANES)
            if g < DN_HEADS:
                q_s[pl.ds(r0, DN_ROWS), hcols] = y * (DN_HEAD ** -0.5)
            elif g < 2 * DN_HEADS:
                k_s[pl.ds(r0, DN_ROWS), hcols] = y
            else:
                v_s[pl.ds(r0, DN_ROWS), hcols] = y
        return carry

    lax.fori_loop(0, t // DN_ROWS, conv_rows, 0)

    gate = gate_ref[...]
    beta = jax.nn.sigmoid(gate)
    g = -jnp.exp(alog_ref[...]) * jax.nn.softplus(gate + dtb_ref[...])
    pos = lax.broadcasted_iota(jnp.int32, g.shape, 0) % CHUNK
    s = 1
    while s < CHUNK:
        g = g + jnp.where(pos >= s, pltpu.roll(g, s, 0), 0.0)
        s *= 2
    lane = lax.broadcasted_iota(jnp.int32, g.shape, 1)
    gate_s[...] = jnp.where(lane < DN_HEADS, beta, g)

    state_s[...] = jnp.zeros_like(state_s)

    row = lax.broadcasted_iota(jnp.int32, (CHUNK, CHUNK), 0)
    col = lax.broadcasted_iota(jnp.int32, (CHUNK, CHUNK), 1)
    eye = row == col
    lower = row >= col
    strict = row > col
    eye_f = eye.astype(F32)
    gain = gain_ref[...]

    def chunk_body(c, carry):
        rows = pl.ds(pl.multiple_of(c * CHUNK, CHUNK), CHUNK)
        gates = gate_s[rows, :]
        for h in range(DN_HEADS):
            cols = slice(h * LANES, (h + 1) * LANES)
            q = q_s[rows, cols]
            k = k_s[rows, cols]
            v = v_s[rows, cols]
            beta_c = gates[:, h:h + 1]
            gc = gates[:, DN_HEADS + h:DN_HEADS + h + 1]
            gcb = jnp.broadcast_to(gc, (CHUNK, CHUNK))
            gr = jnp.sum(jnp.where(eye, gcb, 0.0), axis=0, keepdims=True)
            decay = jnp.where(lower, jnp.exp(jnp.where(lower, gcb - gr, 0.0)), 0.0)
            g_last = gc[CHUNK - 1:CHUNK, :]
            eg = jnp.exp(gc)
            kb = k * beta_c

            kq = _dot_nt(jnp.concatenate([kb, q], axis=0), k)
            m = jnp.where(strict, kq[:CHUNK] * decay, 0.0)
            a_qk = kq[CHUNK:] * decay

            inv = eye_f - m
            pw = _dot(m, m)
            for _ in range(4):
                inv = inv + _dot(pw, inv)
                pw = _dot(pw, pw)
            inv = inv + _dot(pw, inv)

            rhs = jnp.concatenate([v * beta_c, kb * eg], axis=1)
            sol = _dot(inv, rhs)
            u = sol[:, :DN_HEAD]
            w = sol[:, DN_HEAD:]

            st = state_s[h]
            r = _dot(jnp.concatenate([w, q * eg], axis=0), st)
            v_new = u - r[:CHUNK]
            o = r[CHUNK:] + _dot(a_qk, v_new)
            k_dec = k * jnp.exp(g_last - gc)
            state_s[h] = st * jnp.exp(g_last) + _dot_tn(k_dec, v_new)

            y = o * lax.rsqrt(jnp.mean(o * o, axis=-1, keepdims=True) + EPS) * gain
            zf = z_ref[rows, cols].astype(F32)
            out_ref[rows, cols] = (y * _silu(zf)).astype(out_ref.dtype)
        return carry

    lax.fori_loop(0, n_chunks, chunk_body, 0)


def _gated_deltanet(proj, gates, conv_w, alog8, dtb8, out_gain):
    b, t, _ = proj.shape
    return pl.pallas_call(
        _gdn_kernel,
        grid=(b,),
        in_specs=[
            pl.BlockSpec((None, t, QKV_COLS), lambda i: (i, 0, 0)),
            pl.BlockSpec((None, t, DN_WIDTH), lambda i: (i, 0, QKV_COLS // DN_WIDTH)),
            pl.BlockSpec((None, t, N_GATES), lambda i: (i, 0, 0)),
            pl.BlockSpec((CONV_WIDTH, QKV_COLS), lambda i: (0, 0)),
            pl.BlockSpec((1, N_GATES), lambda i: (0, 0)),
            pl.BlockSpec((1, N_GATES), lambda i: (0, 0)),
            pl.BlockSpec((1, DN_HEAD), lambda i: (0, 0)),
        ],
        out_specs=pl.BlockSpec((None, t, DN_WIDTH), lambda i: (i, 0, 0)),
        out_shape=jax.ShapeDtypeStruct((b, t, DN_WIDTH), BF16),
        scratch_shapes=[
            pltpu.VMEM((t, DN_WIDTH), F32),
            pltpu.VMEM((t, DN_WIDTH), F32),
            pltpu.VMEM((t, DN_WIDTH), F32),
            pltpu.VMEM((t, N_GATES), F32),
            pltpu.VMEM((DN_HEADS, DN_HEAD, DN_HEAD), F32),
        ],
        compiler_params=pltpu.CompilerParams(
            dimension_semantics=("parallel",), vmem_limit_bytes=VMEM_LIMIT),
        name="gated_deltanet",
    )(proj, proj, gates, conv_w, alog8, dtb8, out_gain)


AT_BQ = 256
AT_BK = 256
AT_NORM_ROWS = 256


def _half_rmsnorm(x, gain2):
    lane = lax.broadcasted_iota(jnp.int32, x.shape, 1)
    first = lane < DF_HEAD_QK
    sq = x * x
    s_all = jnp.sum(sq, axis=-1, keepdims=True)
    s_first = jnp.sum(jnp.where(first, sq, 0.0), axis=-1, keepdims=True)
    ms = jnp.where(first, s_first, s_all - s_first) * (1.0 / DF_HEAD_QK)
    return x * lax.rsqrt(ms + EPS) * gain2


def _diff_attn_kernel(q_ref, k_ref, v_ref, z_ref, qg_ref, kg_ref, lq1_ref, lk1_ref,
                      lq2_ref, lk2_ref, og_ref, out_ref, kn_s, m_s, l_s, acc_s):
    qi = pl.program_id(2)
    t = k_ref.shape[0]

    @pl.when(qi == 0)
    def _():
        def norm_rows(r, carry):
            rows = pl.ds(pl.multiple_of(r * AT_NORM_ROWS, AT_NORM_ROWS), AT_NORM_ROWS)
            kn_s[rows, :] = _half_rmsnorm(k_ref[rows, :].astype(F32), kg_ref[...]).astype(BF16)
            return carry
        lax.fori_loop(0, t // AT_NORM_ROWS, norm_rows, 0)

    qn = _half_rmsnorm(q_ref[...].astype(F32), qg_ref[...]) * (DF_HEAD_QK ** -0.5)
    lane = lax.broadcasted_iota(jnp.int32, qn.shape, 1)
    first = lane < DF_HEAD_QK
    qq = jnp.concatenate([jnp.where(first, qn, 0.0), jnp.where(first, 0.0, qn)],
                         axis=0).astype(BF16)

    m_s[...] = jnp.full_like(m_s, -jnp.inf)
    l_s[...] = jnp.zeros_like(l_s)
    acc_s[...] = jnp.zeros_like(acc_s)

    def step(j, masked):
        rows = pl.ds(pl.multiple_of(j * AT_BK, AT_BK), AT_BK)
        s = _dot_nt(qq, kn_s[rows, :])
        if masked:
            qpos = lax.broadcasted_iota(jnp.int32, s.shape, 0) % AT_BQ
            kpos = lax.broadcasted_iota(jnp.int32, s.shape, 1)
            s = jnp.where(kpos <= qpos, s, -jnp.inf)
        m_old = m_s[...]
        m_new = jnp.maximum(m_old, jnp.max(s, axis=-1, keepdims=True))
        alpha = jnp.exp(m_old - m_new)
        p = jnp.exp(s - m_new)
        l_s[...] = alpha * l_s[...] + jnp.sum(p, axis=-1, keepdims=True)
        acc_s[...] = alpha * acc_s[...] + _dot(p, v_ref[rows, :])
        m_s[...] = m_new

    def body(j, carry):
        step(j, False)
        return carry

    lax.fori_loop(0, qi, body, 0)
    step(qi, True)

    lam = (jnp.exp(jnp.sum(lq1_ref[...] * lk1_ref[...], axis=-1, keepdims=True))
           - jnp.exp(jnp.sum(lq2_ref[...] * lk2_ref[...], axis=-1, keepdims=True))
           + LAMBDA_INIT)
    o_all = acc_s[...] / l_s[...]
    o = o_all[:AT_BQ] - lam * o_all[AT_BQ:]
    y = o * lax.rsqrt(jnp.mean(o * o, axis=-1, keepdims=True) + EPS) * og_ref[...]
    y = y * (1.0 - LAMBDA_INIT)
    out_ref[...] = (y * _silu(z_ref[...].astype(F32))).astype(out_ref.dtype)


def _diff_attention(proj, q_gain2, k_gain2, lq1, lk1, lq2, lk2, out_gain):
    b, t, _ = proj.shape
    assert AT_BQ == AT_BK
    small = lambda n: pl.BlockSpec((1, n), lambda i, h, q: (0, 0))
    return pl.pallas_call(
        _diff_attn_kernel,
        grid=(b, DF_HEADS, t // AT_BQ),
        in_specs=[
            pl.BlockSpec((None, AT_BQ, LANES), lambda i, h, q: (i, q, _BLK_DF_Q + h)),
            pl.BlockSpec((None, t, LANES), lambda i, h, q: (i, 0, _BLK_DF_K + h)),
            pl.BlockSpec((None, t, LANES), lambda i, h, q: (i, 0, _BLK_DF_V + h)),
            pl.BlockSpec((None, AT_BQ, LANES), lambda i, h, q: (i, q, _BLK_DF_Z + h)),
            small(LANES), small(LANES),
            small(DF_HEAD_QK), small(DF_HEAD_QK), small(DF_HEAD_QK), small(DF_HEAD_QK),
            small(DF_HEAD_V),
        ],
        out_specs=pl.BlockSpec((None, AT_BQ, LANES), lambda i, h, q: (i, q, h)),
        out_shape=jax.ShapeDtypeStruct((b, t, DF_WIDTH), BF16),
        scratch_shapes=[
            pltpu.VMEM((t, LANES), BF16),
            pltpu.VMEM((2 * AT_BQ, 1), F32),
            pltpu.VMEM((2 * AT_BQ, 1), F32),
            pltpu.VMEM((2 * AT_BQ, DF_HEAD_V), F32),
        ],
        compiler_params=pltpu.CompilerParams(
            dimension_semantics=("parallel", "parallel", "arbitrary"),
            vmem_limit_bytes=VMEM_LIMIT),
        name="diff_attention",
    )(proj, proj, proj, proj, q_gain2, k_gain2, lq1, lk1, lq2, lk2, out_gain)


OUT_TM = 512


def _outproj_kernel(x_ref, dn_ref, df_ref, w_ref, out_ref):
    acc = jnp.dot(dn_ref[...], w_ref[:DN_WIDTH, :], preferred_element_type=F32)
    acc = acc + jnp.dot(df_ref[...], w_ref[DN_WIDTH:, :], preferred_element_type=F32)
    out_ref[...] = x_ref[...] + acc


def _out_projection(x2d, mixed_dn, mixed_df, w_out):
    m = x2d.shape[0]
    return pl.pallas_call(
        _outproj_kernel,
        grid=(m // OUT_TM,),
        in_specs=[
            pl.BlockSpec((OUT_TM, D_MODEL), lambda i: (i, 0)),
            pl.BlockSpec((OUT_TM, DN_WIDTH), lambda i: (i, 0)),
            pl.BlockSpec((OUT_TM, DF_WIDTH), lambda i: (i, 0)),
            pl.BlockSpec((DN_WIDTH + DF_WIDTH, D_MODEL), lambda i: (0, 0)),
        ],
        out_specs=pl.BlockSpec((OUT_TM, D_MODEL), lambda i: (i, 0)),
        out_shape=jax.ShapeDtypeStruct((m, D_MODEL), F32),
        compiler_params=pltpu.CompilerParams(
            dimension_semantics=("parallel",), vmem_limit_bytes=VMEM_LIMIT),
        name="out_projection",
    )(x2d, mixed_dn, mixed_df, w_out)


def kernel(x, norm_gain, w_in, conv_w, a_log, dt_bias, dn_out_gain, q_gain, k_gain,
           lambda_q1, lambda_k1, lambda_q2, lambda_k2, df_out_gain, w_out):
    b, t, d = x.shape
    assert d == D_MODEL and norm_gain.shape[0] == 1 and t % AT_BQ == 0 and t % DN_ROWS == 0
    x2d = x.reshape(b * t, d)

    w = w_in[0]
    gate_lo = 4 * DN_WIDTH
    w_main = jnp.concatenate([w[:, :gate_lo], w[:, gate_lo + N_GATES:]], axis=1).astype(BF16)
    w_gate = jnp.pad(w[:, gate_lo:gate_lo + N_GATES], ((0, 0), (0, LANES - N_GATES))).astype(BF16)

    proj, gates = _in_projection(x2d, norm_gain, w_main, w_gate)
    proj = proj.reshape(b, t, PROJ_COLS)
    gates = gates.reshape(b, t, N_GATES)

    zeros4 = jnp.zeros((1, DN_HEADS), F32)
    alog8 = jnp.concatenate([zeros4, a_log], axis=1)
    dtb8 = jnp.concatenate([zeros4, dt_bias], axis=1)
    mixed_dn = _gated_deltanet(proj, gates, conv_w[0], alog8, dtb8, dn_out_gain)

    q_gain2 = jnp.concatenate([q_gain, q_gain], axis=1)
    k_gain2 = jnp.concatenate([k_gain, k_gain], axis=1)
    mixed_df = _diff_attention(proj, q_gain2, k_gain2, lambda_q1, lambda_k1,
                               lambda_q2, lambda_k2, df_out_gain)

    out = _out_projection(x2d, mixed_dn.reshape(b * t, DN_WIDTH),
                          mixed_df.reshape(b * t, DF_WIDTH), w_out[0].astype(BF16))
    return out.reshape(b, t, d)
```

```python
import math

import jax
import jax.numpy as jnp
from jax import lax
from jax.experimental import pallas as pl
from jax.experimental.pallas import tpu as pltpu

F32 = jnp.float32
BF16 = jnp.bfloat16

D_MODEL = 1024
DN_HEADS = 4
DN_HEAD = 128
DN_WIDTH = DN_HEADS * DN_HEAD
CONV_WIDTH = 4
CHUNK = 64
DF_HEADS = 4
DF_HEAD_QK = 64
DF_HEAD_V = 128
DF_WIDTH = DF_HEADS * DF_HEAD_V
EPS = 1e-6
LAMBDA_INIT = 0.8 - 0.6 * math.exp(-0.3 * 0)

N_GATES = 2 * DN_HEADS
PROJ_COLS = 4 * DN_WIDTH + 4 * DF_WIDTH
LANES = 128
VMEM_LIMIT = 56 * 1024 * 1024

_BLK_DN_Z = 3 * DN_HEADS
_BLK_DF_Q = 4 * DN_HEADS
_BLK_DF_K = _BLK_DF_Q + DF_HEADS
_BLK_DF_V = _BLK_DF_K + DF_HEADS
_BLK_DF_Z = _BLK_DF_V + DF_HEADS


def _silu(x):
    return x * jax.nn.sigmoid(x)


def _dot(a, b):
    return jnp.dot(a.astype(BF16), b.astype(BF16), preferred_element_type=F32)


def _dot_nt(a, b):
    return lax.dot_general(a.astype(BF16), b.astype(BF16), (((1,), (1,)), ((), ())),
                           preferred_element_type=F32)


IN_TM = 512
IN_TN = 512


def _inproj_kernel(x_ref, gain_ref, w_ref, wg_ref, proj_ref, gate_ref):
    x = x_ref[...]
    ms = jnp.mean(x * x, axis=-1, keepdims=True)
    h = (x * lax.rsqrt(ms + EPS) * gain_ref[...]).astype(BF16)
    for j in range(PROJ_COLS // IN_TN):
        cols = slice(j * IN_TN, (j + 1) * IN_TN)
        proj_ref[:, cols] = jnp.dot(h, w_ref[:, cols], preferred_element_type=F32).astype(BF16)
    gate_ref[...] = jnp.dot(h, wg_ref[...], preferred_element_type=F32)[:, :N_GATES]


def _in_projection(x2d, gain, w_main, w_gate):
    m = x2d.shape[0]
    return pl.pallas_call(
        _inproj_kernel,
        grid=(m // IN_TM,),
        in_specs=[
            pl.BlockSpec((IN_TM, D_MODEL), lambda i: (i, 0)),
            pl.BlockSpec((1, D_MODEL), lambda i: (0, 0)),
            pl.BlockSpec((D_MODEL, PROJ_COLS), lambda i: (0, 0)),
            pl.BlockSpec((D_MODEL, LANES), lambda i: (0, 0)),
        ],
        out_specs=[
            pl.BlockSpec((IN_TM, PROJ_COLS), lambda i: (i, 0)),
            pl.BlockSpec((IN_TM, N_GATES), lambda i: (i, 0)),
        ],
        out_shape=[
            jax.ShapeDtypeStruct((m, PROJ_COLS), BF16),
            jax.ShapeDtypeStruct((m, N_GATES), F32),
        ],
        compiler_params=pltpu.CompilerParams(
            dimension_semantics=("parallel",), vmem_limit_bytes=VMEM_LIMIT),
        name="in_projection",
    )(x2d, gain, w_main, w_gate)


DN_ROWS = 256
QKV_COLS = 3 * DN_WIDTH
PREP_CHUNKS = 2


def _gdn_kernel(qkv_ref, z_ref, gate_ref, convw_ref, alog_ref, dtb_ref, gain_ref,
                out_ref, q_s, k_s, v_s, wq_s, aqk_s, kdt_s, gate_s, state_s):
    t = qkv_ref.shape[0]
    n_chunks = t // CHUNK

    def conv_rows(r, carry):
        r0 = pl.multiple_of(r * DN_ROWS, DN_ROWS)
        p0 = pl.multiple_of(jnp.maximum(r0 - 16, 0), 16)
        for g in range(QKV_COLS // LANES):
            cols = slice(g * LANES, (g + 1) * LANES)
            cur = qkv_ref[pl.ds(r0, DN_ROWS), cols].astype(F32)
            prev = qkv_ref[pl.ds(p0, 16), cols].astype(F32)[8:]
            prev = jnp.where(r > 0, prev, 0.0)
            xb = jnp.concatenate([prev, cur], axis=0)
            w = convw_ref[:, cols]
            acc = xb * w[CONV_WIDTH - 1:CONV_WIDTH]
            for j in range(1, CONV_WIDTH):
                acc = acc + pltpu.roll(xb, j, 0) * w[CONV_WIDTH - 1 - j:CONV_WIDTH - j]
            y = _silu(acc[8:])
            if g < 2 * DN_HEADS:
                y = y * lax.rsqrt(jnp.sum(y * y, axis=-1, keepdims=True) + EPS)
            hcols = slice((g % DN_HEADS) * LANES, (g % DN_HEADS + 1) * LANES)
            if g < DN_HEADS:
                q_s[pl.ds(r0, DN_ROWS), hcols] = y * (DN_HEAD ** -0.5)
            elif g < 2 * DN_HEADS:
                k_s[pl.ds(r0, DN_ROWS), hcols] = y
            else:
                v_s[pl.ds(r0, DN_ROWS), hcols] = y
        return carry

    lax.fori_loop(0, t // DN_ROWS, conv_rows, 0)

    gate = gate_ref[...]
    beta = jax.nn.sigmoid(gate)
    g = -jnp.exp(alog_ref[...]) * jax.nn.softplus(gate + dtb_ref[...])
    pos = lax.broadcasted_iota(jnp.int32, g.shape, 0) % CHUNK
    s = 1
    while s < CHUNK:
        g = g + jnp.where(pos >= s, pltpu.roll(g, s, 0), 0.0)
        s *= 2
    lane = lax.broadcasted_iota(jnp.int32, g.shape, 1)
    gate_s[...] = jnp.where(lane < DN_HEADS, beta, g)

    row = lax.broadcasted_iota(jnp.int32, (CHUNK, CHUNK), 0)
    col = lax.broadcasted_iota(jnp.int32, (CHUNK, CHUNK), 1)
    eye = row == col
    lower = row >= col
    strict = row > col
    eye_f = eye.astype(F32)

    items = [(cc, h) for cc in range(PREP_CHUNKS) for h in range(DN_HEADS)]

    def prep_body(i, carry):
        base = pl.multiple_of(i * (PREP_CHUNKS * CHUNK), PREP_CHUNKS * CHUNK)
        gates = [gate_s[pl.ds(base + cc * CHUNK, CHUNK), :] for cc in range(PREP_CHUNKS)]
        ks, lhs, decay, rhs, qe, kdt = [], [], [], [], [], []
        for cc, h in items:
            rows = pl.ds(base + cc * CHUNK, CHUNK)
            cols = slice(h * LANES, (h + 1) * LANES)
            q = q_s[rows, cols]
            k = k_s[rows, cols]
            v = v_s[rows, cols]
            beta_c = gates[cc][:, h:h + 1]
            gc = gates[cc][:, DN_HEADS + h:DN_HEADS + h + 1]
            gcb = jnp.broadcast_to(gc, (CHUNK, CHUNK))
            gr = jnp.sum(jnp.where(eye, gcb, 0.0), axis=0, keepdims=True)
            decay.append(jnp.where(lower, jnp.exp(jnp.where(lower, gcb - gr, 0.0)), 0.0))
            g_last = gc[CHUNK - 1:CHUNK, :]
            eg = jnp.exp(gc)
            kb = k * beta_c
            ks.append(k.astype(BF16))
            lhs.append(jnp.concatenate([kb, q], axis=0).astype(BF16))
            rhs.append(jnp.concatenate([v * beta_c, kb * eg], axis=1).astype(BF16))
            qe.append((q * eg).astype(BF16))
            kdt.append((k * jnp.exp(g_last - gc)).T.astype(BF16))

        kq = [_dot_nt(a, b) for a, b in zip(lhs, ks)]
        m = [jnp.where(strict, x[:CHUNK] * d, 0.0) for x, d in zip(kq, decay)]
        a_qk = [(x[CHUNK:] * d).astype(BF16) for x, d in zip(kq, decay)]

        inv = [eye_f - x for x in m]
        pw = [_dot(x, x) for x in m]
        for _ in range(4):
            inv = [a + _dot(p, a) for a, p in zip(inv, pw)]
            pw = [_dot(p, p) for p in pw]
        inv = [a + _dot(p, a) for a, p in zip(inv, pw)]
        sol = [_dot(a, b) for a, b in zip(inv, rhs)]

        for n, (cc, h) in enumerate(items):
            rows = pl.ds(base + cc * CHUNK, CHUNK)
            rows_w = pl.ds(2 * base + cc * 2 * CHUNK, CHUNK)
            rows_q = pl.ds(2 * base + cc * 2 * CHUNK + CHUNK, CHUNK)
            rows_t = pl.ds(2 * base + cc * 2 * CHUNK, 2 * CHUNK)
            cols = slice(h * LANES, (h + 1) * LANES)
            cols_c = slice(h * LANES, h * LANES + CHUNK)
            v_s[rows, cols] = sol[n][:, :DN_HEAD]
            wq_s[rows_w, cols] = sol[n][:, DN_HEAD:].astype(BF16)
            wq_s[rows_q, cols] = qe[n]
            aqk_s[rows, cols_c] = a_qk[n]
            kdt_s[rows_t, cols_c] = kdt[n]
        return carry

    lax.fori_loop(0, n_chunks // PREP_CHUNKS, prep_body, 0)

    state_s[...] = jnp.zeros_like(state_s)
    gain = gain_ref[...]
    heads = range(DN_HEADS)

    def rec_body(c, carry):
        rows = pl.ds(pl.multiple_of(c * CHUNK, CHUNK), CHUNK)
        rows2 = pl.ds(pl.multiple_of(c * 2 * CHUNK, 2 * CHUNK), 2 * CHUNK)
        gates = gate_s[rows, :]
        cols = [slice(h * LANES, (h + 1) * LANES) for h in heads]
        cols_c = [slice(h * LANES, h * LANES + CHUNK) for h in heads]
        st = [state_s[h] for h in heads]
        r = [jnp.dot(wq_s[rows2, cols[h]], st[h].astype(BF16), preferred_element_type=F32)
             for h in heads]
        v_new = [(v_s[rows, cols[h]] - r[h][:CHUNK]).astype(BF16) for h in heads]
        upd = [jnp.dot(kdt_s[rows2, cols_c[h]], v_new[h], preferred_element_type=F32)
               for h in heads]
        for h in heads:
            g_last = gates[CHUNK - 1:CHUNK, DN_HEADS + h:DN_HEADS + h + 1]
            state_s[h] = st[h] * jnp.exp(g_last) + upd[h]
        o = [r[h][CHUNK:] + jnp.dot(aqk_s[rows, cols_c[h]], v_new[h], preferred_element_type=F32)
             for h in heads]
        for h in heads:
            y = o[h] * lax.rsqrt(jnp.mean(o[h] * o[h], axis=-1, keepdims=True) + EPS) * gain
            zf = z_ref[rows, cols[h]].astype(F32)
            out_ref[rows, cols[h]] = (y * _silu(zf)).astype(out_ref.dtype)
        return carry

    lax.fori_loop(0, n_chunks, rec_body, 0)


def _gated_deltanet(proj, gates, conv_w, alog8, dtb8, out_gain):
    b, t, _ = proj.shape
    assert t % (PREP_CHUNKS * CHUNK) == 0 and t % DN_ROWS == 0
    return pl.pallas_call(
        _gdn_kernel,
        grid=(b,),
        in_specs=[
            pl.BlockSpec((None, t, QKV_COLS), lambda i: (i, 0, 0)),
            pl.BlockSpec((None, t, DN_WIDTH), lambda i: (i, 0, QKV_COLS // DN_WIDTH)),
            pl.BlockSpec((None, t, N_GATES), lambda i: (i, 0, 0)),
            pl.BlockSpec((CONV_WIDTH, QKV_COLS), lambda i: (0, 0)),
            pl.BlockSpec((1, N_GATES), lambda i: (0, 0)),
            pl.BlockSpec((1, N_GATES), lambda i: (0, 0)),
            pl.BlockSpec((1, DN_HEAD), lambda i: (0, 0)),
        ],
        out_specs=pl.BlockSpec((None, t, DN_WIDTH), lambda i: (i, 0, 0)),
        out_shape=jax.ShapeDtypeStruct((b, t, DN_WIDTH), BF16),
        scratch_shapes=[
            pltpu.VMEM((t, DN_WIDTH), F32),
            pltpu.VMEM((t, DN_WIDTH), F32),
            pltpu.VMEM((t, DN_WIDTH), F32),
            pltpu.VMEM((2 * t, DN_WIDTH), BF16),
            pltpu.VMEM((t, DN_WIDTH), BF16),
            pltpu.VMEM((2 * t, DN_WIDTH), BF16),
            pltpu.VMEM((t, N_GATES), F32),
            pltpu.VMEM((DN_HEADS, DN_HEAD, DN_HEAD), F32),
        ],
        compiler_params=pltpu.CompilerParams(
            dimension_semantics=("parallel",), vmem_limit_bytes=VMEM_LIMIT),
        name="gated_deltanet",
    )(proj, proj, gates, conv_w, alog8, dtb8, out_gain)


AT_BQ = 512
AT_BK = 256
AT_CW = 256
AT_NQ = 2 * AT_BQ
AT_CHAINS = AT_NQ // AT_CW
AT_CPM = AT_BQ // AT_CW


def _half_rmsnorm(x, gain2):
    lane = lax.broadcasted_iota(jnp.int32, x.shape, 1)
    first = lane < DF_HEAD_QK
    sq = x * x
    s_all = jnp.sum(sq, axis=-1, keepdims=True)
    s_first = jnp.sum(jnp.where(first, sq, 0.0), axis=-1, keepdims=True)
    ms = jnp.where(first, s_first, s_all - s_first) * (1.0 / DF_HEAD_QK)
    return x * lax.rsqrt(ms + EPS) * gain2


def _diff_attn_kernel(q_ref, k_ref, v_ref, z_ref, qg_ref, kg_ref, lq1_ref, lk1_ref,
                      lq2_ref, lk2_ref, og_ref, out_ref, kn_s, vt_s, qq_s, m_s, l_s, acc_s):
    qi = pl.program_id(2)
    t = k_ref.shape[0]

    @pl.when(qi == 0)
    def _():
        def prep_rows(r, carry):
            rows = pl.ds(pl.multiple_of(r * AT_BK, AT_BK), AT_BK)
            kn_s[rows, :] = _half_rmsnorm(k_ref[rows, :].astype(F32), kg_ref[...]).astype(BF16)
            vt_s[r] = v_ref[rows, :].astype(F32).T.astype(BF16)
            return carry
        lax.fori_loop(0, t // AT_BK, prep_rows, 0)

    qn = _half_rmsnorm(q_ref[...].astype(F32), qg_ref[...]) * (DF_HEAD_QK ** -0.5)
    first = lax.broadcasted_iota(jnp.int32, qn.shape, 1) < DF_HEAD_QK
    qq_s[:AT_BQ, :] = jnp.where(first, qn, 0.0).astype(BF16)
    qq_s[AT_BQ:, :] = jnp.where(first, 0.0, qn).astype(BF16)

    m_s[...] = jnp.full_like(m_s, -jnp.inf)
    l_s[...] = jnp.zeros_like(l_s)
    acc_s[...] = jnp.zeros_like(acc_s)

    krow = lax.broadcasted_iota(jnp.int32, (AT_BK, AT_CW), 0)
    qcol = lax.broadcasted_iota(jnp.int32, (AT_BK, AT_CW), 1)
    causal = krow <= qcol

    def tile(j, chains, masked):
        rows = pl.ds(pl.multiple_of(j * AT_BK, AT_BK), AT_BK)
        kn = kn_s[rows, :]
        vt = vt_s[j]
        scores = [_dot_nt(kn, qq_s[c * AT_CW:(c + 1) * AT_CW, :]) for c in chains]
        for c, s in zip(chains, scores):
            lanes = slice(c * AT_CW, (c + 1) * AT_CW)
            if masked(c):
                s = jnp.where(causal, s, -jnp.inf)
            m_old = m_s[:, lanes]
            m_new = jnp.maximum(m_old, jnp.max(s, axis=0, keepdims=True))
            alpha = jnp.exp(m_old - m_new)
            p = jnp.exp(s - m_new)
            l_s[:, lanes] = alpha * l_s[:, lanes] + jnp.sum(p, axis=0, keepdims=True)
            m_s[:, lanes] = m_new
            acc_s[:, lanes] = alpha * acc_s[:, lanes] + jnp.dot(
                vt, p.astype(BF16), preferred_element_type=F32)

    def body(j, carry):
        tile(j, range(AT_CHAINS), lambda c: False)
        return carry

    tiles_per_q = AT_BQ // AT_BK
    lax.fori_loop(0, qi * tiles_per_q, body, 0)
    for d in range(tiles_per_q):
        chains = [c for c in range(AT_CHAINS) if c % AT_CPM >= d]
        tile(qi * tiles_per_q + d, chains, lambda c, d=d: c % AT_CPM == d)

    lam = (jnp.exp(jnp.sum(lq1_ref[...] * lk1_ref[...], axis=-1, keepdims=True))
           - jnp.exp(jnp.sum(lq2_ref[...] * lk2_ref[...], axis=-1, keepdims=True))
           + LAMBDA_INIT)
    o_all = acc_s[...] / l_s[...]
    o = o_all[:, :AT_BQ] - lam * o_all[:, AT_BQ:]
    y = (o * lax.rsqrt(jnp.mean(o * o, axis=0, keepdims=True) + EPS)).T
    y = y * og_ref[...] * (1.0 - LAMBDA_INIT)
    out_ref[...] = (y * _silu(z_ref[...].astype(F32))).astype(out_ref.dtype)


def _diff_attention(proj, q_gain2, k_gain2, lq1, lk1, lq2, lk2, out_gain):
    b, t, _ = proj.shape
    assert AT_BQ % AT_BK == 0 and AT_CW == AT_BK and t % AT_BQ == 0
    small = lambda n: pl.BlockSpec((1, n), lambda i, h, q: (0, 0))
    return pl.pallas_call(
        _diff_attn_kernel,
        grid=(b, DF_HEADS, t // AT_BQ),
        in_specs=[
            pl.BlockSpec((None, AT_BQ, LANES), lambda i, h, q: (i, q, _BLK_DF_Q + h)),
            pl.BlockSpec((None, t, LANES), lambda i, h, q: (i, 0, _BLK_DF_K + h)),
            pl.BlockSpec((None, t, LANES), lambda i, h, q: (i, 0, _BLK_DF_V + h)),
            pl.BlockSpec((None, AT_BQ, LANES), lambda i, h, q: (i, q, _BLK_DF_Z + h)),
            small(LANES), small(LANES),
            small(DF_HEAD_QK), small(DF_HEAD_QK), small(DF_HEAD_QK), small(DF_HEAD_QK),
            small(DF_HEAD_V),
        ],
        out_specs=pl.BlockSpec((None, AT_BQ, LANES), lambda i, h, q: (i, q, h)),
        out_shape=jax.ShapeDtypeStruct((b, t, DF_WIDTH), BF16),
        scratch_shapes=[
            pltpu.VMEM((t, LANES), BF16),
            pltpu.VMEM((t // AT_BK, DF_HEAD_V, AT_BK), BF16),
            pltpu.VMEM((AT_NQ, LANES), BF16),
            pltpu.VMEM((1, AT_NQ), F32),
            pltpu.VMEM((1, AT_NQ), F32),
            pltpu.VMEM((DF_HEAD_V, AT_NQ), F32),
        ],
        compiler_params=pltpu.CompilerParams(
            dimension_semantics=("parallel", "parallel", "arbitrary"),
            vmem_limit_bytes=VMEM_LIMIT),
        name="diff_attention",
    )(proj, proj, proj, proj, q_gain2, k_gain2, lq1, lk1, lq2, lk2, out_gain)


OUT_TM = 512


def _outproj_kernel(x_ref, dn_ref, df_ref, w_ref, out_ref):
    acc = jnp.dot(dn_ref[...], w_ref[:DN_WIDTH, :], preferred_element_type=F32)
    acc = acc + jnp.dot(df_ref[...], w_ref[DN_WIDTH:, :], preferred_element_type=F32)
    out_ref[...] = x_ref[...] + acc


def _out_projection(x2d, mixed_dn, mixed_df, w_out):
    m = x2d.shape[0]
    return pl.pallas_call(
        _outproj_kernel,
        grid=(m // OUT_TM,),
        in_specs=[
            pl.BlockSpec((OUT_TM, D_MODEL), lambda i: (i, 0)),
            pl.BlockSpec((OUT_TM, DN_WIDTH), lambda i: (i, 0)),
            pl.BlockSpec((OUT_TM, DF_WIDTH), lambda i: (i, 0)),
            pl.BlockSpec((DN_WIDTH + DF_WIDTH, D_MODEL), lambda i: (0, 0)),
        ],
        out_specs=pl.BlockSpec((OUT_TM, D_MODEL), lambda i: (i, 0)),
        out_shape=jax.ShapeDtypeStruct((m, D_MODEL), F32),
        compiler_params=pltpu.CompilerParams(
            dimension_semantics=("parallel",), vmem_limit_bytes=VMEM_LIMIT),
        name="out_projection",
    )(x2d, mixed_dn, mixed_df, w_out)


def kernel(x, norm_gain, w_in, conv_w, a_log, dt_bias, dn_out_gain, q_gain, k_gain,
           lambda_q1, lambda_k1, lambda_q2, lambda_k2, df_out_gain, w_out):
    b, t, d = x.shape
    assert d == D_MODEL and norm_gain.shape[0] == 1
    x2d = x.reshape(b * t, d)

    w = w_in[0]
    gate_lo = 4 * DN_WIDTH
    w_main = jnp.concatenate([w[:, :gate_lo], w[:, gate_lo + N_GATES:]], axis=1).astype(BF16)
    w_gate = jnp.pad(w[:, gate_lo:gate_lo + N_GATES], ((0, 0), (0, LANES - N_GATES))).astype(BF16)

    proj, gates = _in_projection(x2d, norm_gain, w_main, w_gate)
    proj = proj.reshape(b, t, PROJ_COLS)
    gates = gates.reshape(b, t, N_GATES)

    zeros4 = jnp.zeros((1, DN_HEADS), F32)
    alog8 = jnp.concatenate([zeros4, a_log], axis=1)
    dtb8 = jnp.concatenate([zeros4, dt_bias], axis=1)
    mixed_dn = _gated_deltanet(proj, gates, conv_w[0], alog8, dtb8, dn_out_gain)

    q_gain2 = jnp.concatenate([q_gain, q_gain], axis=1)
    k_gain2 = jnp.concatenate([k_gain, k_gain], axis=1)
    mixed_df = _diff_attention(proj, q_gain2, k_gain2, lambda_q1, lambda_k1,
                               lambda_q2, lambda_k2, df_out_gain)

    out = _out_projection(x2d, mixed_dn.reshape(b * t, DN_WIDTH),
                          mixed_df.reshape(b * t, DF_WIDTH), w_out[0].astype(BF16))
    return out.reshape(b, t, d)
```

```python
import math

import jax
import jax.numpy as jnp
from jax import lax
from jax.experimental import pallas as pl
from jax.experimental.pallas import tpu as pltpu

F32 = jnp.float32
BF16 = jnp.bfloat16

D_MODEL = 1024
DN_HEADS = 4
DN_HEAD = 128
DN_WIDTH = DN_HEADS * DN_HEAD
CONV_WIDTH = 4
CHUNK = 64
DF_HEADS = 4
DF_HEAD_QK = 64
DF_HEAD_V = 128
DF_WIDTH = DF_HEADS * DF_HEAD_V
EPS = 1e-6
LAMBDA_INIT = 0.8 - 0.6 * math.exp(-0.3 * 0)
LOG2_E = math.log2(math.e)

N_GATES = 2 * DN_HEADS
PROJ_COLS = 4 * DN_WIDTH + 4 * DF_WIDTH
LANES = 128
VMEM_LIMIT = 56 * 1024 * 1024

_BLK_DN_Z = 3 * DN_HEADS
_BLK_DF_Q = 4 * DN_HEADS
_BLK_DF_K = _BLK_DF_Q + DF_HEADS
_BLK_DF_V = _BLK_DF_K + DF_HEADS
_BLK_DF_Z = _BLK_DF_V + DF_HEADS


def _silu(x):
    return x * jax.nn.sigmoid(x)


def _dot(a, b):
    return jnp.dot(a.astype(BF16), b.astype(BF16), preferred_element_type=F32)


def _dot_nt(a, b):
    return lax.dot_general(a.astype(BF16), b.astype(BF16), (((1,), (1,)), ((), ())),
                           preferred_element_type=F32)


IN_TM = 512
IN_TN = 512


def _inproj_kernel(x_ref, gain_ref, w_ref, wg_ref, proj_ref, gate_ref):
    x = x_ref[...]
    ms = jnp.mean(x * x, axis=-1, keepdims=True)
    h = (x * lax.rsqrt(ms + EPS) * gain_ref[...]).astype(BF16)
    for j in range(PROJ_COLS // IN_TN):
        cols = slice(j * IN_TN, (j + 1) * IN_TN)
        proj_ref[:, cols] = jnp.dot(h, w_ref[:, cols], preferred_element_type=F32).astype(BF16)
    gate_ref[...] = jnp.dot(h, wg_ref[...], preferred_element_type=F32)[:, :N_GATES]


def _in_projection(x2d, gain, w_main, w_gate):
    m = x2d.shape[0]
    return pl.pallas_call(
        _inproj_kernel,
        grid=(m // IN_TM,),
        in_specs=[
            pl.BlockSpec((IN_TM, D_MODEL), lambda i: (i, 0)),
            pl.BlockSpec((1, D_MODEL), lambda i: (0, 0)),
            pl.BlockSpec((D_MODEL, PROJ_COLS), lambda i: (0, 0)),
            pl.BlockSpec((D_MODEL, LANES), lambda i: (0, 0)),
        ],
        out_specs=[
            pl.BlockSpec((IN_TM, PROJ_COLS), lambda i: (i, 0)),
            pl.BlockSpec((IN_TM, N_GATES), lambda i: (i, 0)),
        ],
        out_shape=[
            jax.ShapeDtypeStruct((m, PROJ_COLS), BF16),
            jax.ShapeDtypeStruct((m, N_GATES), F32),
        ],
        compiler_params=pltpu.CompilerParams(
            dimension_semantics=("parallel",), vmem_limit_bytes=VMEM_LIMIT),
        name="in_projection",
    )(x2d, gain, w_main, w_gate)


DN_ROWS = 256
QKV_COLS = 3 * DN_WIDTH
PREP_CHUNKS = 2


def _gdn_kernel(qkv_ref, z_ref, gate_ref, convw_ref, alog_ref, dtb_ref, gain_ref,
                out_ref, q_s, k_s, v_s, wq_s, aqk_s, kdt_s, gate_s, state_s):
    t = qkv_ref.shape[0]
    n_chunks = t // CHUNK

    def conv_rows(r, carry):
        r0 = pl.multiple_of(r * DN_ROWS, DN_ROWS)
        p0 = pl.multiple_of(jnp.maximum(r0 - 16, 0), 16)
        for g in range(QKV_COLS // LANES):
            cols = slice(g * LANES, (g + 1) * LANES)
            cur = qkv_ref[pl.ds(r0, DN_ROWS), cols].astype(F32)
            prev = qkv_ref[pl.ds(p0, 16), cols].astype(F32)[8:]
            prev = jnp.where(r > 0, prev, 0.0)
            xb = jnp.concatenate([prev, cur], axis=0)
            w = convw_ref[:, cols]
            acc = xb * w[CONV_WIDTH - 1:CONV_WIDTH]
            for j in range(1, CONV_WIDTH):
                acc = acc + pltpu.roll(xb, j, 0) * w[CONV_WIDTH - 1 - j:CONV_WIDTH - j]
            y = _silu(acc[8:])
            if g < 2 * DN_HEADS:
                y = y * lax.rsqrt(jnp.sum(y * y, axis=-1, keepdims=True) + EPS)
            hcols = slice((g % DN_HEADS) * LANES, (g % DN_HEADS + 1) * LANES)
            if g < DN_HEADS:
                q_s[pl.ds(r0, DN_ROWS), hcols] = y * (DN_HEAD ** -0.5)
            elif g < 2 * DN_HEADS:
                k_s[pl.ds(r0, DN_ROWS), hcols] = y
            else:
                v_s[pl.ds(r0, DN_ROWS), hcols] = y
        return carry

    lax.fori_loop(0, t // DN_ROWS, conv_rows, 0)

    gate = gate_ref[...]
    beta = jax.nn.sigmoid(gate)
    g = -jnp.exp(alog_ref[...]) * jax.nn.softplus(gate + dtb_ref[...])
    pos = lax.broadcasted_iota(jnp.int32, g.shape, 0) % CHUNK
    s = 1
    while s < CHUNK:
        g = g + jnp.where(pos >= s, pltpu.roll(g, s, 0), 0.0)
        s *= 2
    lane = lax.broadcasted_iota(jnp.int32, g.shape, 1)
    gate_s[...] = jnp.where(lane < DN_HEADS, beta, g)

    row = lax.broadcasted_iota(jnp.int32, (CHUNK, CHUNK), 0)
    col = lax.broadcasted_iota(jnp.int32, (CHUNK, CHUNK), 1)
    eye = row == col
    lower = row >= col
    strict = row > col
    eye_f = eye.astype(F32)

    items = [(cc, h) for cc in range(PREP_CHUNKS) for h in range(DN_HEADS)]

    def prep_body(i, carry):
        base = pl.multiple_of(i * (PREP_CHUNKS * CHUNK), PREP_CHUNKS * CHUNK)
        gates = [gate_s[pl.ds(base + cc * CHUNK, CHUNK), :] for cc in range(PREP_CHUNKS)]
        ks, lhs, decay, rhs, qe, kdt = [], [], [], [], [], []
        for cc, h in items:
            rows = pl.ds(base + cc * CHUNK, CHUNK)
            cols = slice(h * LANES, (h + 1) * LANES)
            q = q_s[rows, cols]
            k = k_s[rows, cols]
            v = v_s[rows, cols]
            beta_c = gates[cc][:, h:h + 1]
            gc = gates[cc][:, DN_HEADS + h:DN_HEADS + h + 1]
            gcb = jnp.broadcast_to(gc, (CHUNK, CHUNK))
            gr = jnp.sum(jnp.where(eye, gcb, 0.0), axis=0, keepdims=True)
            decay.append(jnp.where(lower, jnp.exp(jnp.where(lower, gcb - gr, 0.0)), 0.0))
            g_last = gc[CHUNK - 1:CHUNK, :]
            eg = jnp.exp(gc)
            kb = k * beta_c
            ks.append(k.astype(BF16))
            lhs.append(jnp.concatenate([kb, q], axis=0).astype(BF16))
            rhs.append(jnp.concatenate([v * beta_c, kb * eg], axis=1).astype(BF16))
            qe.append((q * eg).astype(BF16))
            kdt.append((k * jnp.exp(g_last - gc)).T.astype(BF16))

        kq = [_dot_nt(a, b) for a, b in zip(lhs, ks)]
        m = [jnp.where(strict, x[:CHUNK] * d, 0.0) for x, d in zip(kq, decay)]
        a_qk = [(x[CHUNK:] * d).astype(BF16) for x, d in zip(kq, decay)]

        inv = [eye_f - x for x in m]
        pw = [_dot(x, x) for x in m]
        for _ in range(4):
            inv = [a + _dot(p, a) for a, p in zip(inv, pw)]
            pw = [_dot(p, p) for p in pw]
        inv = [a + _dot(p, a) for a, p in zip(inv, pw)]
        sol = [_dot(a, b) for a, b in zip(inv, rhs)]

        for n, (cc, h) in enumerate(items):
            rows = pl.ds(base + cc * CHUNK, CHUNK)
            rows_w = pl.ds(2 * base + cc * 2 * CHUNK, CHUNK)
            rows_q = pl.ds(2 * base + cc * 2 * CHUNK + CHUNK, CHUNK)
            rows_t = pl.ds(2 * base + cc * 2 * CHUNK, 2 * CHUNK)
            cols = slice(h * LANES, (h + 1) * LANES)
            cols_c = slice(h * LANES, h * LANES + CHUNK)
            v_s[rows, cols] = sol[n][:, :DN_HEAD]
            wq_s[rows_w, cols] = sol[n][:, DN_HEAD:].astype(BF16)
            wq_s[rows_q, cols] = qe[n]
            aqk_s[rows, cols_c] = a_qk[n]
            kdt_s[rows_t, cols_c] = kdt[n]
        return carry

    lax.fori_loop(0, n_chunks // PREP_CHUNKS, prep_body, 0)

    state_s[...] = jnp.zeros_like(state_s)
    gain = gain_ref[...]
    heads = range(DN_HEADS)

    def rec_body(c, carry):
        rows = pl.ds(pl.multiple_of(c * CHUNK, CHUNK), CHUNK)
        rows2 = pl.ds(pl.multiple_of(c * 2 * CHUNK, 2 * CHUNK), 2 * CHUNK)
        gates = gate_s[rows, :]
        cols = [slice(h * LANES, (h + 1) * LANES) for h in heads]
        cols_c = [slice(h * LANES, h * LANES + CHUNK) for h in heads]
        st = [state_s[h] for h in heads]
        r = [jnp.dot(wq_s[rows2, cols[h]], st[h].astype(BF16), preferred_element_type=F32)
             for h in heads]
        v_new = [(v_s[rows, cols[h]] - r[h][:CHUNK]).astype(BF16) for h in heads]
        upd = [jnp.dot(kdt_s[rows2, cols_c[h]], v_new[h], preferred_element_type=F32)
               for h in heads]
        for h in heads:
            g_last = gates[CHUNK - 1:CHUNK, DN_HEADS + h:DN_HEADS + h + 1]
            state_s[h] = st[h] * jnp.exp(g_last) + upd[h]
        o = [r[h][CHUNK:] + jnp.dot(aqk_s[rows, cols_c[h]], v_new[h], preferred_element_type=F32)
             for h in heads]
        for h in heads:
            y = o[h] * lax.rsqrt(jnp.mean(o[h] * o[h], axis=-1, keepdims=True) + EPS) * gain
            zf = z_ref[rows, cols[h]].astype(F32)
            out_ref[rows, cols[h]] = (y * _silu(zf)).astype(out_ref.dtype)
        return carry

    lax.fori_loop(0, n_chunks, rec_body, 0)


def _gated_deltanet(proj, gates, conv_w, alog8, dtb8, out_gain):
    b, t, _ = proj.shape
    assert t % (PREP_CHUNKS * CHUNK) == 0 and t % DN_ROWS == 0
    return pl.pallas_call(
        _gdn_kernel,
        grid=(b,),
        in_specs=[
            pl.BlockSpec((None, t, QKV_COLS), lambda i: (i, 0, 0)),
            pl.BlockSpec((None, t, DN_WIDTH), lambda i: (i, 0, QKV_COLS // DN_WIDTH)),
            pl.BlockSpec((None, t, N_GATES), lambda i: (i, 0, 0)),
            pl.BlockSpec((CONV_WIDTH, QKV_COLS), lambda i: (0, 0)),
            pl.BlockSpec((1, N_GATES), lambda i: (0, 0)),
            pl.BlockSpec((1, N_GATES), lambda i: (0, 0)),
            pl.BlockSpec((1, DN_HEAD), lambda i: (0, 0)),
        ],
        out_specs=pl.BlockSpec((None, t, DN_WIDTH), lambda i: (i, 0, 0)),
        out_shape=jax.ShapeDtypeStruct((b, t, DN_WIDTH), BF16),
        scratch_shapes=[
            pltpu.VMEM((t, DN_WIDTH), F32),
            pltpu.VMEM((t, DN_WIDTH), F32),
            pltpu.VMEM((t, DN_WIDTH), F32),
            pltpu.VMEM((2 * t, DN_WIDTH), BF16),
            pltpu.VMEM((t, DN_WIDTH), BF16),
            pltpu.VMEM((2 * t, DN_WIDTH), BF16),
            pltpu.VMEM((t, N_GATES), F32),
            pltpu.VMEM((DN_HEADS, DN_HEAD, DN_HEAD), F32),
        ],
        compiler_params=pltpu.CompilerParams(
            dimension_semantics=("parallel",), vmem_limit_bytes=VMEM_LIMIT),
        name="gated_deltanet",
    )(proj, proj, gates, conv_w, alog8, dtb8, out_gain)


AT_BQ = 512
AT_BK = 256
AT_CW = 256
AT_NQ = 2 * AT_BQ
AT_CHAINS = AT_NQ // AT_CW
AT_CPM = AT_BQ // AT_CW


def _half_rmsnorm(x, gain2):
    lane = lax.broadcasted_iota(jnp.int32, x.shape, 1)
    first = lane < DF_HEAD_QK
    sq = x * x
    s_all = jnp.sum(sq, axis=-1, keepdims=True)
    s_first = jnp.sum(jnp.where(first, sq, 0.0), axis=-1, keepdims=True)
    ms = jnp.where(first, s_first, s_all - s_first) * (1.0 / DF_HEAD_QK)
    return x * lax.rsqrt(ms + EPS) * gain2


def _diff_attn_kernel(q_ref, k_ref, v_ref, z_ref, qg_ref, kg_ref, lq1_ref, lk1_ref,
                      lq2_ref, lk2_ref, og_ref, out_ref, kn_s, vt_s, qq_s, m_s, l_s, acc_s,
                      s_a, s_b, c_a, c_b, p_a, p_b):
    qi = pl.program_id(2)
    t = k_ref.shape[0]

    @pl.when(qi == 0)
    def _():
        def prep_rows(r, carry):
            rows = pl.ds(pl.multiple_of(r * AT_BK, AT_BK), AT_BK)
            kn_s[rows, :] = _half_rmsnorm(k_ref[rows, :].astype(F32), kg_ref[...]).astype(BF16)
            vt_s[r] = v_ref[rows, :].astype(F32).T.astype(BF16)
            return carry
        lax.fori_loop(0, t // AT_BK, prep_rows, 0)

    qn = _half_rmsnorm(q_ref[...].astype(F32), qg_ref[...]) * (DF_HEAD_QK ** -0.5 * LOG2_E)
    first = lax.broadcasted_iota(jnp.int32, qn.shape, 1) < DF_HEAD_QK
    qq_s[:AT_BQ, :] = jnp.where(first, qn, 0.0).astype(BF16)
    qq_s[AT_BQ:, :] = jnp.where(first, 0.0, qn).astype(BF16)

    m_s[...] = jnp.full_like(m_s, -jnp.inf)
    l_s[...] = jnp.zeros_like(l_s)
    acc_s[...] = jnp.zeros_like(acc_s)
    p_b[...] = jnp.zeros_like(p_b)

    krow = lax.broadcasted_iota(jnp.int32, (AT_BK, AT_CW), 0)
    qcol = lax.broadcasted_iota(jnp.int32, (AT_BK, AT_CW), 1)
    causal = krow <= qcol
    all_chains = tuple(range(AT_CHAINS))

    def lanes(c):
        return slice(c * AT_CW, (c + 1) * AT_CW)

    def scores(k, c, bufs, masked):
        rows = pl.ds(pl.multiple_of(k * AT_BK, AT_BK), AT_BK)
        s = _dot_nt(kn_s[rows, :], qq_s[lanes(c), :])
        if masked:
            s = jnp.where(causal, s, -jnp.inf)
        bufs[0][:, lanes(c)] = s
        bufs[1][:, lanes(c)] = jnp.max(s, axis=0, keepdims=True)

    def values(k, p_ref, c):
        return jnp.dot(vt_s[k], p_ref[:, lanes(c)], preferred_element_type=F32)

    def step(k, cur_bufs, nxt_bufs, p_prv, cur, nxt, nxt_masked, prv):
        for c in all_chains:
            alpha = None
            if c in cur:
                s_cur, c_cur, p_cur = cur_bufs
                m_old = m_s[:, lanes(c)]
                m_new = jnp.maximum(m_old, c_cur[:, lanes(c)])
                alpha = jnp.exp2(m_old - m_new)
                p = jnp.exp2(s_cur[:, lanes(c)] - m_new)
                l_s[:, lanes(c)] = alpha * l_s[:, lanes(c)] + jnp.sum(p, axis=0, keepdims=True)
                m_s[:, lanes(c)] = m_new
                p_cur[:, lanes(c)] = p.astype(BF16)
            if c in nxt:
                scores(k + 1, c, nxt_bufs, c in nxt_masked)
            if c in prv:
                acc = acc_s[:, lanes(c)] + values(jnp.maximum(k - 1, 0), p_prv, c)
                acc_s[:, lanes(c)] = acc if alpha is None else alpha * acc

    lo = tuple(c for c in all_chains if c % AT_CPM == 0)
    hi = tuple(c for c in all_chains if c % AT_CPM == 1)
    bufs_a = (s_a, c_a, p_a)
    bufs_b = (s_b, c_b, p_b)
    d0 = 2 * qi

    @pl.when(qi == 0)
    def _():
        for c in all_chains:
            scores(0, c, bufs_a, c in lo)

    @pl.when(qi > 0)
    def _():
        for c in all_chains:
            scores(0, c, bufs_a, False)

    def pair(i, nxt_masked):
        step(2 * i, bufs_a, bufs_b, p_b, all_chains, all_chains, (), all_chains)
        step(2 * i + 1, bufs_b, bufs_a, p_a, all_chains, all_chains, nxt_masked, all_chains)

    def pair_body(i, carry):
        pair(i, ())
        return carry

    lax.fori_loop(0, qi - 1, pair_body, 0)

    @pl.when(qi > 0)
    def _():
        pair(qi - 1, lo)

    step(d0, bufs_a, bufs_b, p_b, all_chains, hi, hi, all_chains)
    step(d0 + 1, bufs_b, None, p_a, hi, (), (), all_chains)
    for c in hi:
        acc_s[:, lanes(c)] = acc_s[:, lanes(c)] + values(d0 + 1, p_b, c)

    lam = (jnp.exp(jnp.sum(lq1_ref[...] * lk1_ref[...], axis=-1, keepdims=True))
           - jnp.exp(jnp.sum(lq2_ref[...] * lk2_ref[...], axis=-1, keepdims=True))
           + LAMBDA_INIT)
    o_all = acc_s[...] / l_s[...]
    o = o_all[:, :AT_BQ] - lam * o_all[:, AT_BQ:]
    y = (o * lax.rsqrt(jnp.mean(o * o, axis=0, keepdims=True) + EPS)).T
    y = y * og_ref[...] * (1.0 - LAMBDA_INIT)
    out_ref[...] = (y * _silu(z_ref[...].astype(F32))).astype(out_ref.dtype)


def _diff_attention(proj, q_gain2, k_gain2, lq1, lk1, lq2, lk2, out_gain):
    b, t, _ = proj.shape
    assert AT_BQ == 2 * AT_BK and AT_CW == AT_BK and AT_CPM == 2 and t % AT_BQ == 0
    small = lambda n: pl.BlockSpec((1, n), lambda i, h, q: (0, 0))
    return pl.pallas_call(
        _diff_attn_kernel,
        grid=(b, DF_HEADS, t // AT_BQ),
        in_specs=[
            pl.BlockSpec((None, AT_BQ, LANES), lambda i, h, q: (i, q, _BLK_DF_Q + h)),
            pl.BlockSpec((None, t, LANES), lambda i, h, q: (i, 0, _BLK_DF_K + h)),
            pl.BlockSpec((None, t, LANES), lambda i, h, q: (i, 0, _BLK_DF_V + h)),
            pl.BlockSpec((None, AT_BQ, LANES), lambda i, h, q: (i, q, _BLK_DF_Z + h)),
            small(LANES), small(LANES),
            small(DF_HEAD_QK), small(DF_HEAD_QK), small(DF_HEAD_QK), small(DF_HEAD_QK),
            small(DF_HEAD_V),
        ],
        out_specs=pl.BlockSpec((None, AT_BQ, LANES), lambda i, h, q: (i, q, h)),
        out_shape=jax.ShapeDtypeStruct((b, t, DF_WIDTH), BF16),
        scratch_shapes=[
            pltpu.VMEM((t, LANES), BF16),
            pltpu.VMEM((t // AT_BK, DF_HEAD_V, AT_BK), BF16),
            pltpu.VMEM((AT_NQ, LANES), BF16),
            pltpu.VMEM((1, AT_NQ), F32),
            pltpu.VMEM((1, AT_NQ), F32),
            pltpu.VMEM((DF_HEAD_V, AT_NQ), F32),
            pltpu.VMEM((AT_BK, AT_NQ), F32),
            pltpu.VMEM((AT_BK, AT_NQ), F32),
            pltpu.VMEM((1, AT_NQ), F32),
            pltpu.VMEM((1, AT_NQ), F32),
            pltpu.VMEM((AT_BK, AT_NQ), BF16),
            pltpu.VMEM((AT_BK, AT_NQ), BF16),
        ],
        compiler_params=pltpu.CompilerParams(
            dimension_semantics=("parallel", "parallel", "arbitrary"),
            vmem_limit_bytes=VMEM_LIMIT),
        name="diff_attention",
    )(proj, proj, proj, proj, q_gain2, k_gain2, lq1, lk1, lq2, lk2, out_gain)


OUT_TM = 512


def _outproj_kernel(x_ref, dn_ref, df_ref, w_ref, out_ref):
    acc = jnp.dot(dn_ref[...], w_ref[:DN_WIDTH, :], preferred_element_type=F32)
    acc = acc + jnp.dot(df_ref[...], w_ref[DN_WIDTH:, :], preferred_element_type=F32)
    out_ref[...] = x_ref[...] + acc


def _out_projection(x2d, mixed_dn, mixed_df, w_out):
    m = x2d.shape[0]
    return pl.pallas_call(
        _outproj_kernel,
        grid=(m // OUT_TM,),
        in_specs=[
            pl.BlockSpec((OUT_TM, D_MODEL), lambda i: (i, 0)),
            pl.BlockSpec((OUT_TM, DN_WIDTH), lambda i: (i, 0)),
            pl.BlockSpec((OUT_TM, DF_WIDTH), lambda i: (i, 0)),
            pl.BlockSpec((DN_WIDTH + DF_WIDTH, D_MODEL), lambda i: (0, 0)),
        ],
        out_specs=pl.BlockSpec((OUT_TM, D_MODEL), lambda i: (i, 0)),
        out_shape=jax.ShapeDtypeStruct((m, D_MODEL), F32),
        compiler_params=pltpu.CompilerParams(
            dimension_semantics=("parallel",), vmem_limit_bytes=VMEM_LIMIT),
        name="out_projection",
    )(x2d, mixed_dn, mixed_df, w_out)


def kernel(x, norm_gain, w_in, conv_w, a_log, dt_bias, dn_out_gain, q_gain, k_gain,
           lambda_q1, lambda_k1, lambda_q2, lambda_k2, df_out_gain, w_out):
    b, t, d = x.shape
    assert d == D_MODEL and norm_gain.shape[0] == 1
    x2d = x.reshape(b * t, d)

    w = w_in[0]
    gate_lo = 4 * DN_WIDTH
    w_main = jnp.concatenate([w[:, :gate_lo], w[:, gate_lo + N_GATES:]], axis=1).astype(BF16)
    w_gate = jnp.pad(w[:, gate_lo:gate_lo + N_GATES], ((0, 0), (0, LANES - N_GATES))).astype(BF16)

    proj, gates = _in_projection(x2d, norm_gain, w_main, w_gate)
    proj = proj.reshape(b, t, PROJ_COLS)
    gates = gates.reshape(b, t, N_GATES)

    zeros4 = jnp.zeros((1, DN_HEADS), F32)
    alog8 = jnp.concatenate([zeros4, a_log], axis=1)
    dtb8 = jnp.concatenate([zeros4, dt_bias], axis=1)
    mixed_dn = _gated_deltanet(proj, gates, conv_w[0], alog8, dtb8, dn_out_gain)

    q_gain2 = jnp.concatenate([q_gain, q_gain], axis=1)
    k_gain2 = jnp.concatenate([k_gain, k_gain], axis=1)
    mixed_df = _diff_attention(proj, q_gain2, k_gain2, lambda_q1, lambda_k1,
                               lambda_q2, lambda_k2, df_out_gain)

    out = _out_projection(x2d, mixed_dn.reshape(b * t, DN_WIDTH),
                          mixed_df.reshape(b * t, DF_WIDTH), w_out[0].astype(BF16))
    return out.reshape(b, t, d)
```

```python
import math

import jax
import jax.numpy as jnp
from jax import lax
from jax.experimental import pallas as pl
from jax.experimental.pallas import tpu as pltpu

F32 = jnp.float32
BF16 = jnp.bfloat16

D_MODEL = 1024
DN_HEADS = 4
DN_HEAD = 128
DN_WIDTH = DN_HEADS * DN_HEAD
CONV_WIDTH = 4
CHUNK = 64
DF_HEADS = 4
DF_HEAD_QK = 64
DF_HEAD_V = 128
DF_WIDTH = DF_HEADS * DF_HEAD_V
EPS = 1e-6
LAMBDA_INIT = 0.8 - 0.6 * math.exp(-0.3 * 0)
LOG2_E = math.log2(math.e)

N_GATES = 2 * DN_HEADS
PROJ_COLS = 4 * DN_WIDTH + 3 * DF_WIDTH
LANES = 128
VMEM_LIMIT = 56 * 1024 * 1024

_BLK_DN_Z = 3 * DN_HEADS
_BLK_DF_Q = 4 * DN_HEADS
_BLK_DF_K = _BLK_DF_Q + DF_HEADS
_BLK_DF_Z = _BLK_DF_K + DF_HEADS


def _silu(x):
    return x * jax.nn.sigmoid(x)


def _dot(a, b):
    return jnp.dot(a.astype(BF16), b.astype(BF16), preferred_element_type=F32)


def _dot_nt(a, b):
    return lax.dot_general(a.astype(BF16), b.astype(BF16), (((1,), (1,)), ((), ())),
                           preferred_element_type=F32)


IN_TM = 512
IN_TN = 512
SEG_W = 256


def _inproj_kernel(x_ref, gain_ref, w_ref, wvt_ref, wg_ref, seg_ref, qkg_ref,
                   proj_ref, vt_ref, gate_ref):
    x = x_ref[...]
    ms = jnp.mean(x * x, axis=-1, keepdims=True)
    h = (x * lax.rsqrt(ms + EPS) * gain_ref[...]).astype(BF16)
    for j in range(PROJ_COLS // IN_TN):
        cols = slice(j * IN_TN, (j + 1) * IN_TN)
        y = jnp.dot(h, w_ref[:, cols], preferred_element_type=F32)
        if j in (_BLK_DF_Q * LANES // IN_TN, _BLK_DF_K * LANES // IN_TN):
            which = 0 if j == _BLK_DF_Q * LANES // IN_TN else 1
            sq = (y * y).astype(BF16)
            ssq = jnp.concatenate(
                [jnp.dot(sq[:, i * SEG_W:(i + 1) * SEG_W], seg_ref[...],
                         preferred_element_type=F32) for i in range(IN_TN // SEG_W)], axis=1)
            y = y * lax.rsqrt(ssq * (1.0 / DF_HEAD_QK) + EPS) * qkg_ref[which:which + 1, :]
        proj_ref[:, cols] = y.astype(BF16)
    vt_ref[...] = _dot_nt(wvt_ref[...], h).astype(BF16)
    gate_ref[...] = jnp.dot(h, wg_ref[...], preferred_element_type=F32)[:, :N_GATES]


def _in_projection(x2d, t, gain, w_main, w_vt, w_gate, seg, qk_gain):
    m = x2d.shape[0]
    assert DF_WIDTH == IN_TN and t % IN_TM == 0
    tiles_per_seq = t // IN_TM
    const = lambda shape: pl.BlockSpec(shape, lambda i: (0, 0))
    return pl.pallas_call(
        _inproj_kernel,
        grid=(m // IN_TM,),
        in_specs=[
            pl.BlockSpec((IN_TM, D_MODEL), lambda i: (i, 0)),
            const((1, D_MODEL)),
            const((D_MODEL, PROJ_COLS)),
            const((DF_WIDTH, D_MODEL)),
            const((D_MODEL, LANES)),
            const((SEG_W, SEG_W)),
            const((2, DF_WIDTH)),
        ],
        out_specs=[
            pl.BlockSpec((IN_TM, PROJ_COLS), lambda i: (i, 0)),
            pl.BlockSpec((None, DF_WIDTH, IN_TM),
                         lambda i: (i // tiles_per_seq, 0, i % tiles_per_seq)),
            pl.BlockSpec((IN_TM, N_GATES), lambda i: (i, 0)),
        ],
        out_shape=[
            jax.ShapeDtypeStruct((m, PROJ_COLS), BF16),
            jax.ShapeDtypeStruct((m // t, DF_WIDTH, t), BF16),
            jax.ShapeDtypeStruct((m, N_GATES), F32),
        ],
        compiler_params=pltpu.CompilerParams(
            dimension_semantics=("parallel",), vmem_limit_bytes=VMEM_LIMIT),
        name="in_projection",
    )(x2d, gain, w_main, w_vt, w_gate, seg, qk_gain)


DN_ROWS = 256
QKV_COLS = 3 * DN_WIDTH
PREP_CHUNKS = 2


def _gdn_kernel(qkv_ref, z_ref, gate_ref, convw_ref, alog_ref, dtb_ref, gain_ref,
                out_ref, q_s, k_s, v_s, wq_s, aqk_s, kdt_s, gate_s, state_s):
    t = qkv_ref.shape[0]
    n_chunks = t // CHUNK

    def conv_rows(r, carry):
        r0 = pl.multiple_of(r * DN_ROWS, DN_ROWS)
        p0 = pl.multiple_of(jnp.maximum(r0 - 16, 0), 16)
        for g in range(QKV_COLS // LANES):
            cols = slice(g * LANES, (g + 1) * LANES)
            cur = qkv_ref[pl.ds(r0, DN_ROWS), cols].astype(F32)
            prev = qkv_ref[pl.ds(p0, 16), cols].astype(F32)[8:]
            prev = jnp.where(r > 0, prev, 0.0)
            xb = jnp.concatenate([prev, cur], axis=0)
            w = convw_ref[:, cols]
            acc = xb * w[CONV_WIDTH - 1:CONV_WIDTH]
            for j in range(1, CONV_WIDTH):
                acc = acc + pltpu.roll(xb, j, 0) * w[CONV_WIDTH - 1 - j:CONV_WIDTH - j]
            y = _silu(acc[8:])
            if g < 2 * DN_HEADS:
                y = y * lax.rsqrt(jnp.sum(y * y, axis=-1, keepdims=True) + EPS)
            hcols = slice((g % DN_HEADS) * LANES, (g % DN_HEADS + 1) * LANES)
            if g < DN_HEADS:
                q_s[pl.ds(r0, DN_ROWS), hcols] = y * (DN_HEAD ** -0.5)
            elif g < 2 * DN_HEADS:
                k_s[pl.ds(r0, DN_ROWS), hcols] = y
            else:
                v_s[pl.ds(r0, DN_ROWS), hcols] = y
        return carry

    lax.fori_loop(0, t // DN_ROWS, conv_rows, 0)

    gate = gate_ref[...]
    beta = jax.nn.sigmoid(gate)
    g = -jnp.exp(alog_ref[...]) * jax.nn.softplus(gate + dtb_ref[...])
    pos = lax.broadcasted_iota(jnp.int32, g.shape, 0) % CHUNK
    s = 1
    while s < CHUNK:
        g = g + jnp.where(pos >= s, pltpu.roll(g, s, 0), 0.0)
        s *= 2
    lane = lax.broadcasted_iota(jnp.int32, g.shape, 1)
    gate_s[...] = jnp.where(lane < DN_HEADS, beta, g)

    row = lax.broadcasted_iota(jnp.int32, (CHUNK, CHUNK), 0)
    col = lax.broadcasted_iota(jnp.int32, (CHUNK, CHUNK), 1)
    eye = row == col
    lower = row >= col
    strict = row > col
    eye_f = eye.astype(F32)

    items = [(cc, h) for cc in range(PREP_CHUNKS) for h in range(DN_HEADS)]

    def prep_body(i, carry):
        base = pl.multiple_of(i * (PREP_CHUNKS * CHUNK), PREP_CHUNKS * CHUNK)
        gates = [gate_s[pl.ds(base + cc * CHUNK, CHUNK), :] for cc in range(PREP_CHUNKS)]
        ks, lhs, decay, rhs, qe, kdt = [], [], [], [], [], []
        for cc, h in items:
            rows = pl.ds(base + cc * CHUNK, CHUNK)
            cols = slice(h * LANES, (h + 1) * LANES)
            q = q_s[rows, cols]
            k = k_s[rows, cols]
            v = v_s[rows, cols]
            beta_c = gates[cc][:, h:h + 1]
            gc = gates[cc][:, DN_HEADS + h:DN_HEADS + h + 1]
            gcb = jnp.broadcast_to(gc, (CHUNK, CHUNK))
            gr = jnp.sum(jnp.where(eye, gcb, 0.0), axis=0, keepdims=True)
            decay.append(jnp.where(lower, jnp.exp(jnp.where(lower, gcb - gr, 0.0)), 0.0))
            g_last = gc[CHUNK - 1:CHUNK, :]
            eg = jnp.exp(gc)
            kb = k * beta_c
            ks.append(k.astype(BF16))
            lhs.append(jnp.concatenate([kb, q], axis=0).astype(BF16))
            rhs.append(jnp.concatenate([v * beta_c, kb * eg], axis=1).astype(BF16))
            qe.append((q * eg).astype(BF16))
            kdt.append((k * jnp.exp(g_last - gc)).T.astype(BF16))

        kq = [_dot_nt(a, b) for a, b in zip(lhs, ks)]
        m = [jnp.where(strict, x[:CHUNK] * d, 0.0) for x, d in zip(kq, decay)]
        a_qk = [(x[CHUNK:] * d).astype(BF16) for x, d in zip(kq, decay)]

        inv = [eye_f - x for x in m]
        pw = [_dot(x, x) for x in m]
        for _ in range(4):
            inv = [a + _dot(p, a) for a, p in zip(inv, pw)]
            pw = [_dot(p, p) for p in pw]
        inv = [a + _dot(p, a) for a, p in zip(inv, pw)]
        sol = [_dot(a, b) for a, b in zip(inv, rhs)]

        for n, (cc, h) in enumerate(items):
            rows = pl.ds(base + cc * CHUNK, CHUNK)
            rows_w = pl.ds(2 * base + cc * 2 * CHUNK, CHUNK)
            rows_q = pl.ds(2 * base + cc * 2 * CHUNK + CHUNK, CHUNK)
            rows_t = pl.ds(2 * base + cc * 2 * CHUNK, 2 * CHUNK)
            cols = slice(h * LANES, (h + 1) * LANES)
            cols_c = slice(h * LANES, h * LANES + CHUNK)
            v_s[rows, cols] = sol[n][:, :DN_HEAD]
            wq_s[rows_w, cols] = sol[n][:, DN_HEAD:].astype(BF16)
            wq_s[rows_q, cols] = qe[n]
            aqk_s[rows, cols_c] = a_qk[n]
            kdt_s[rows_t, cols_c] = kdt[n]
        return carry

    lax.fori_loop(0, n_chunks // PREP_CHUNKS, prep_body, 0)

    state_s[...] = jnp.zeros_like(state_s)
    gain = gain_ref[...]
    heads = range(DN_HEADS)

    def rec_body(c, carry):
        rows = pl.ds(pl.multiple_of(c * CHUNK, CHUNK), CHUNK)
        rows2 = pl.ds(pl.multiple_of(c * 2 * CHUNK, 2 * CHUNK), 2 * CHUNK)
        gates = gate_s[rows, :]
        cols = [slice(h * LANES, (h + 1) * LANES) for h in heads]
        cols_c = [slice(h * LANES, h * LANES + CHUNK) for h in heads]
        st = [state_s[h] for h in heads]
        r = [jnp.dot(wq_s[rows2, cols[h]], st[h].astype(BF16), preferred_element_type=F32)
             for h in heads]
        v_new = [(v_s[rows, cols[h]] - r[h][:CHUNK]).astype(BF16) for h in heads]
        upd = [jnp.dot(kdt_s[rows2, cols_c[h]], v_new[h], preferred_element_type=F32)
               for h in heads]
        for h in heads:
            g_last = gates[CHUNK - 1:CHUNK, DN_HEADS + h:DN_HEADS + h + 1]
            state_s[h] = st[h] * jnp.exp(g_last) + upd[h]
        o = [r[h][CHUNK:] + jnp.dot(aqk_s[rows, cols_c[h]], v_new[h], preferred_element_type=F32)
             for h in heads]
        for h in heads:
            y = o[h] * lax.rsqrt(jnp.mean(o[h] * o[h], axis=-1, keepdims=True) + EPS) * gain
            zf = z_ref[rows, cols[h]].astype(F32)
            out_ref[rows, cols[h]] = (y * _silu(zf)).astype(out_ref.dtype)
        return carry

    lax.fori_loop(0, n_chunks, rec_body, 0)


def _gated_deltanet(proj, gates, conv_w, alog8, dtb8, out_gain):
    b, t, _ = proj.shape
    assert t % (PREP_CHUNKS * CHUNK) == 0 and t % DN_ROWS == 0
    return pl.pallas_call(
        _gdn_kernel,
        grid=(b,),
        in_specs=[
            pl.BlockSpec((None, t, QKV_COLS), lambda i: (i, 0, 0)),
            pl.BlockSpec((None, t, DN_WIDTH), lambda i: (i, 0, QKV_COLS // DN_WIDTH)),
            pl.BlockSpec((None, t, N_GATES), lambda i: (i, 0, 0)),
            pl.BlockSpec((CONV_WIDTH, QKV_COLS), lambda i: (0, 0)),
            pl.BlockSpec((1, N_GATES), lambda i: (0, 0)),
            pl.BlockSpec((1, N_GATES), lambda i: (0, 0)),
            pl.BlockSpec((1, DN_HEAD), lambda i: (0, 0)),
        ],
        out_specs=pl.BlockSpec((None, t, DN_WIDTH), lambda i: (i, 0, 0)),
        out_shape=jax.ShapeDtypeStruct((b, t, DN_WIDTH), BF16),
        scratch_shapes=[
            pltpu.VMEM((t, DN_WIDTH), F32),
            pltpu.VMEM((t, DN_WIDTH), F32),
            pltpu.VMEM((t, DN_WIDTH), F32),
            pltpu.VMEM((2 * t, DN_WIDTH), BF16),
            pltpu.VMEM((t, DN_WIDTH), BF16),
            pltpu.VMEM((2 * t, DN_WIDTH), BF16),
            pltpu.VMEM((t, N_GATES), F32),
            pltpu.VMEM((DN_HEADS, DN_HEAD, DN_HEAD), F32),
        ],
        compiler_params=pltpu.CompilerParams(
            dimension_semantics=("parallel",), vmem_limit_bytes=VMEM_LIMIT),
        name="gated_deltanet",
    )(proj, proj, gates, conv_w, alog8, dtb8, out_gain)


AT_BQ = 512
AT_BK = 256
AT_CW = 256
AT_NQ = 2 * AT_BQ
AT_CHAINS = AT_NQ // AT_CW
AT_CPM = AT_BQ // AT_CW


def _attention_tiles(n_qblocks):
    all_chains = tuple(range(AT_CHAINS))
    lo = tuple(c for c in all_chains if c % AT_CPM == 0)
    hi = tuple(c for c in all_chains if c % AT_CPM == 1)
    tiles = []
    for qi in range(n_qblocks):
        for kt in range(AT_CPM * qi + AT_CPM):
            d = kt - AT_CPM * qi
            chains, masked = (all_chains, ()) if d < 0 else (all_chains, lo) if d == 0 else (hi, hi)
            tiles.append(dict(qi=qi, kt=kt, chains=chains, masked=masked, first=kt == 0))
    return tiles


def _diff_attn_kernel(q_ref, k_ref, vt_ref, z_ref, lq1_ref, lk1_ref, lq2_ref, lk2_ref,
                      og_ref, out_ref, qq_s, m_s, l_s, acc_s, s_a, s_b, c_a, c_b, p_a, p_b):
    t = q_ref.shape[0]
    n_qblocks = t // AT_BQ
    all_chains = tuple(range(AT_CHAINS))

    def lanes(c):
        return slice(c * AT_CW, (c + 1) * AT_CW)

    first = lax.broadcasted_iota(jnp.int32, (AT_BQ, LANES), 1) < DF_HEAD_QK
    zero = jnp.zeros((AT_BQ, LANES), BF16)
    for qi in range(n_qblocks):
        blk = q_ref[qi * AT_BQ:(qi + 1) * AT_BQ, :]
        qq_s[qi, :AT_BQ, :] = jnp.where(first, blk, zero)
        qq_s[qi, AT_BQ:, :] = jnp.where(first, zero, blk)

    lam = (jnp.exp(jnp.sum(lq1_ref[...] * lk1_ref[...], axis=-1, keepdims=True))
           - jnp.exp(jnp.sum(lq2_ref[...] * lk2_ref[...], axis=-1, keepdims=True))
           + LAMBDA_INIT)

    krow = lax.broadcasted_iota(jnp.int32, (AT_BK, AT_CW), 0)
    qcol = lax.broadcasted_iota(jnp.int32, (AT_BK, AT_CW), 1)
    causal = krow <= qcol
    bufs = ((s_a, c_a, p_a), (s_b, c_b, p_b))

    def scores(tile, c, buf):
        rows = slice(tile["kt"] * AT_BK, (tile["kt"] + 1) * AT_BK)
        s = _dot_nt(k_ref[rows, :], qq_s[tile["qi"], lanes(c), :])
        if c in tile["masked"]:
            s = jnp.where(causal, s, -jnp.inf)
        buf[0][:, lanes(c)] = s
        buf[1][:, lanes(c)] = jnp.max(s, axis=0, keepdims=True)

    def values(tile, c, buf):
        rows = slice(tile["kt"] * AT_BK, (tile["kt"] + 1) * AT_BK)
        return jnp.dot(vt_ref[:, rows], buf[2][:, lanes(c)], preferred_element_type=F32)

    def finish(qi):
        rows = slice(qi * AT_BQ, (qi + 1) * AT_BQ)
        o_all = acc_s[qi] * (1.0 / l_s[qi])
        o = o_all[:, :AT_BQ] - lam * o_all[:, AT_BQ:]
        y = (o * lax.rsqrt(jnp.mean(o * o, axis=0, keepdims=True) + EPS)).T
        y = y * og_ref[...] * (1.0 - LAMBDA_INIT)
        out_ref[rows, :] = (y * _silu(z_ref[rows, :].astype(F32))).astype(out_ref.dtype)

    tiles = _attention_tiles(n_qblocks)
    for c in tiles[0]["chains"]:
        scores(tiles[0], c, bufs[0])
    for n, cur in enumerate(tiles):
        nxt = tiles[n + 1] if n + 1 < len(tiles) else None
        prv = tiles[n - 1] if n > 0 else None
        s_cur, c_cur, p_cur = bufs[n % 2]
        qi = cur["qi"]
        for c in all_chains:
            alpha = None
            if c in cur["chains"]:
                if cur["first"]:
                    m_new = c_cur[:, lanes(c)]
                else:
                    m_old = m_s[qi, :, lanes(c)]
                    m_new = jnp.maximum(m_old, c_cur[:, lanes(c)])
                    alpha = jnp.exp2(m_old - m_new)
                p = jnp.exp2(s_cur[:, lanes(c)] - m_new)
                l_new = jnp.sum(p, axis=0, keepdims=True)
                if alpha is not None:
                    l_new = alpha * l_s[qi, :, lanes(c)] + l_new
                l_s[qi, :, lanes(c)] = l_new
                m_s[qi, :, lanes(c)] = m_new
                p_cur[:, lanes(c)] = p.astype(BF16)
            if nxt is not None and c in nxt["chains"]:
                scores(nxt, c, bufs[(n + 1) % 2])
            if prv is not None and c in prv["chains"]:
                pv = values(prv, c, bufs[(n - 1) % 2])
                pq = prv["qi"]
                acc = pv if prv["first"] else acc_s[pq, :, lanes(c)] + pv
                if pq == qi and alpha is not None:
                    acc = alpha * acc
                acc_s[pq, :, lanes(c)] = acc
        if prv is not None and prv["qi"] != qi:
            finish(prv["qi"])
    last = tiles[-1]
    for c in last["chains"]:
        acc_s[last["qi"], :, lanes(c)] = (acc_s[last["qi"], :, lanes(c)]
                                          + values(last, c, bufs[(len(tiles) - 1) % 2]))
    finish(last["qi"])


def _diff_attention(proj, vt, lq1, lk1, lq2, lk2, out_gain):
    b, t, _ = proj.shape
    assert AT_BQ == AT_CPM * AT_BK and AT_CW == AT_BK and AT_CPM == 2 and t % AT_BQ == 0
    n_qblocks = t // AT_BQ
    small = lambda n: pl.BlockSpec((1, n), lambda i, h: (0, 0))
    return pl.pallas_call(
        _diff_attn_kernel,
        grid=(b, DF_HEADS),
        in_specs=[
            pl.BlockSpec((None, t, LANES), lambda i, h: (i, 0, _BLK_DF_Q + h)),
            pl.BlockSpec((None, t, LANES), lambda i, h: (i, 0, _BLK_DF_K + h)),
            pl.BlockSpec((None, DF_HEAD_V, t), lambda i, h: (i, h, 0)),
            pl.BlockSpec((None, t, LANES), lambda i, h: (i, 0, _BLK_DF_Z + h)),
            small(DF_HEAD_QK), small(DF_HEAD_QK), small(DF_HEAD_QK), small(DF_HEAD_QK),
            small(DF_HEAD_V),
        ],
        out_specs=pl.BlockSpec((None, t, LANES), lambda i, h: (i, 0, h)),
        out_shape=jax.ShapeDtypeStruct((b, t, DF_WIDTH), BF16),
        scratch_shapes=[
            pltpu.VMEM((n_qblocks, AT_NQ, LANES), BF16),
            pltpu.VMEM((n_qblocks, 1, AT_NQ), F32),
            pltpu.VMEM((n_qblocks, 1, AT_NQ), F32),
            pltpu.VMEM((n_qblocks, DF_HEAD_V, AT_NQ), F32),
            pltpu.VMEM((AT_BK, AT_NQ), F32),
            pltpu.VMEM((AT_BK, AT_NQ), F32),
            pltpu.VMEM((1, AT_NQ), F32),
            pltpu.VMEM((1, AT_NQ), F32),
            pltpu.VMEM((AT_BK, AT_NQ), BF16),
            pltpu.VMEM((AT_BK, AT_NQ), BF16),
        ],
        compiler_params=pltpu.CompilerParams(
            dimension_semantics=("parallel", "parallel"), vmem_limit_bytes=VMEM_LIMIT),
        name="diff_attention",
    )(proj, proj, vt, proj, lq1, lk1, lq2, lk2, out_gain)


OUT_TM = 512


def _outproj_kernel(x_ref, dn_ref, df_ref, w_ref, out_ref):
    acc = jnp.dot(dn_ref[...], w_ref[:DN_WIDTH, :], preferred_element_type=F32)
    acc = acc + jnp.dot(df_ref[...], w_ref[DN_WIDTH:, :], preferred_element_type=F32)
    out_ref[...] = x_ref[...] + acc


def _out_projection(x2d, mixed_dn, mixed_df, w_out):
    m = x2d.shape[0]
    return pl.pallas_call(
        _outproj_kernel,
        grid=(m // OUT_TM,),
        in_specs=[
            pl.BlockSpec((OUT_TM, D_MODEL), lambda i: (i, 0)),
            pl.BlockSpec((OUT_TM, DN_WIDTH), lambda i: (i, 0)),
            pl.BlockSpec((OUT_TM, DF_WIDTH), lambda i: (i, 0)),
            pl.BlockSpec((DN_WIDTH + DF_WIDTH, D_MODEL), lambda i: (0, 0)),
        ],
        out_specs=pl.BlockSpec((OUT_TM, D_MODEL), lambda i: (i, 0)),
        out_shape=jax.ShapeDtypeStruct((m, D_MODEL), F32),
        compiler_params=pltpu.CompilerParams(
            dimension_semantics=("parallel",), vmem_limit_bytes=VMEM_LIMIT),
        name="out_projection",
    )(x2d, mixed_dn, mixed_df, w_out)


def kernel(x, norm_gain, w_in, conv_w, a_log, dt_bias, dn_out_gain, q_gain, k_gain,
           lambda_q1, lambda_k1, lambda_q2, lambda_k2, df_out_gain, w_out):
    b, t, d = x.shape
    assert d == D_MODEL and norm_gain.shape[0] == 1
    x2d = x.reshape(b * t, d)

    w = w_in[0]
    gate_lo = 4 * DN_WIDTH
    df_lo = gate_lo + N_GATES
    w_main = jnp.concatenate([w[:, :gate_lo], w[:, df_lo:df_lo + 2 * DF_WIDTH],
                              w[:, df_lo + 3 * DF_WIDTH:]], axis=1).astype(BF16)
    w_vt = w[:, df_lo + 2 * DF_WIDTH:df_lo + 3 * DF_WIDTH].T.astype(BF16)
    w_gate = jnp.pad(w[:, gate_lo:df_lo], ((0, 0), (0, LANES - N_GATES))).astype(BF16)
    half = jnp.arange(SEG_W, dtype=jnp.int32) // DF_HEAD_QK
    seg = (half[:, None] == half[None, :]).astype(BF16)
    reps = DF_WIDTH // DF_HEAD_QK
    qk_gain = jnp.concatenate([jnp.tile(q_gain, (1, reps)) * (DF_HEAD_QK ** -0.5 * LOG2_E),
                               jnp.tile(k_gain, (1, reps))], axis=0)

    proj, vt, gates = _in_projection(x2d, t, norm_gain, w_main, w_vt, w_gate, seg, qk_gain)
    proj = proj.reshape(b, t, PROJ_COLS)
    gates = gates.reshape(b, t, N_GATES)

    zeros4 = jnp.zeros((1, DN_HEADS), F32)
    alog8 = jnp.concatenate([zeros4, a_log], axis=1)
    dtb8 = jnp.concatenate([zeros4, dt_bias], axis=1)
    mixed_dn = _gated_deltanet(proj, gates, conv_w[0], alog8, dtb8, dn_out_gain)

    mixed_df = _diff_attention(proj, vt, lambda_q1, lambda_k1, lambda_q2, lambda_k2,
                               df_out_gain)

    out = _out_projection(x2d, mixed_dn.reshape(b * t, DN_WIDTH),
                          mixed_df.reshape(b * t, DF_WIDTH), w_out[0].astype(BF16))
    return out.reshape(b, t, d)
```

```python
import math

import jax
import jax.numpy as jnp
from jax import lax
from jax.experimental import pallas as pl
from jax.experimental.pallas import tpu as pltpu

F32 = jnp.float32
BF16 = jnp.bfloat16

D_MODEL = 1024
DN_HEADS = 4
DN_HEAD = 128
DN_WIDTH = DN_HEADS * DN_HEAD
CONV_WIDTH = 4
CHUNK = 64
DF_HEADS = 4
DF_HEAD_QK = 64
DF_HEAD_V = 128
DF_WIDTH = DF_HEADS * DF_HEAD_V
EPS = 1e-6
LAMBDA_INIT = 0.8 - 0.6 * math.exp(-0.3 * 0)
LOG2_E = math.log2(math.e)

N_GATES = 2 * DN_HEADS
PROJ_COLS = 4 * DN_WIDTH + 3 * DF_WIDTH
LANES = 128
VMEM_LIMIT = 56 * 1024 * 1024

_BLK_DN_Z = 3 * DN_HEADS
_BLK_DF_Q = 4 * DN_HEADS
_BLK_DF_K = _BLK_DF_Q + DF_HEADS
_BLK_DF_Z = _BLK_DF_K + DF_HEADS


def _silu(x):
    return x * jax.nn.sigmoid(x)


def _dot(a, b):
    return jnp.dot(a.astype(BF16), b.astype(BF16), preferred_element_type=F32)


def _dot_nt(a, b):
    return lax.dot_general(a.astype(BF16), b.astype(BF16), (((1,), (1,)), ((), ())),
                           preferred_element_type=F32)


IN_TM = 512
IN_TN = 512
SEG_W = 256


def _inproj_kernel(x_ref, gain_ref, w_ref, wvt_ref, wg_ref, seg_ref, qkg_ref,
                   proj_ref, vt_ref, gate_ref):
    x = x_ref[...]
    ms = jnp.mean(x * x, axis=-1, keepdims=True)
    h = (x * lax.rsqrt(ms + EPS) * gain_ref[...]).astype(BF16)
    for j in range(PROJ_COLS // IN_TN):
        cols = slice(j * IN_TN, (j + 1) * IN_TN)
        y = jnp.dot(h, w_ref[:, cols], preferred_element_type=F32)
        if j in (_BLK_DF_Q * LANES // IN_TN, _BLK_DF_K * LANES // IN_TN):
            which = 0 if j == _BLK_DF_Q * LANES // IN_TN else 1
            sq = (y * y).astype(BF16)
            ssq = jnp.concatenate(
                [jnp.dot(sq[:, i * SEG_W:(i + 1) * SEG_W], seg_ref[...],
                         preferred_element_type=F32) for i in range(IN_TN // SEG_W)], axis=1)
            y = y * lax.rsqrt(ssq * (1.0 / DF_HEAD_QK) + EPS) * qkg_ref[which:which + 1, :]
        proj_ref[:, cols] = y.astype(BF16)
    vt_ref[...] = _dot_nt(wvt_ref[...], h).astype(BF16)
    gate_ref[...] = jnp.dot(h, wg_ref[...], preferred_element_type=F32)[:, :N_GATES]


def _in_projection(x2d, t, gain, w_main, w_vt, w_gate, seg, qk_gain):
    m = x2d.shape[0]
    assert DF_WIDTH == IN_TN and t % IN_TM == 0
    tiles_per_seq = t // IN_TM
    const = lambda shape: pl.BlockSpec(shape, lambda i: (0, 0))
    return pl.pallas_call(
        _inproj_kernel,
        grid=(m // IN_TM,),
        in_specs=[
            pl.BlockSpec((IN_TM, D_MODEL), lambda i: (i, 0)),
            const((1, D_MODEL)),
            const((D_MODEL, PROJ_COLS)),
            const((DF_WIDTH, D_MODEL)),
            const((D_MODEL, LANES)),
            const((SEG_W, SEG_W)),
            const((2, DF_WIDTH)),
        ],
        out_specs=[
            pl.BlockSpec((IN_TM, PROJ_COLS), lambda i: (i, 0)),
            pl.BlockSpec((None, DF_WIDTH, IN_TM),
                         lambda i: (i // tiles_per_seq, 0, i % tiles_per_seq)),
            pl.BlockSpec((IN_TM, N_GATES), lambda i: (i, 0)),
        ],
        out_shape=[
            jax.ShapeDtypeStruct((m, PROJ_COLS), BF16),
            jax.ShapeDtypeStruct((m // t, DF_WIDTH, t), BF16),
            jax.ShapeDtypeStruct((m, N_GATES), F32),
        ],
        compiler_params=pltpu.CompilerParams(
            dimension_semantics=("parallel",), vmem_limit_bytes=VMEM_LIMIT),
        name="in_projection",
    )(x2d, gain, w_main, w_vt, w_gate, seg, qk_gain)


DN_ROWS = 256
QKV_COLS = 3 * DN_WIDTH
PREP_CHUNKS = 4


def _gdn_kernel(qkv_ref, z_ref, gate_ref, convw_ref, alog_ref, dtb_ref, gain_ref,
                out_ref, q_s, k_s, v_s, wq_s, aqk_s, kdt_s, gate_s, state_s):
    t = qkv_ref.shape[0]
    n_chunks = t // CHUNK

    def conv_rows(r, carry):
        r0 = pl.multiple_of(r * DN_ROWS, DN_ROWS)
        p0 = pl.multiple_of(jnp.maximum(r0 - 16, 0), 16)
        for g in range(QKV_COLS // LANES):
            cols = slice(g * LANES, (g + 1) * LANES)
            cur = qkv_ref[pl.ds(r0, DN_ROWS), cols].astype(F32)
            prev = qkv_ref[pl.ds(p0, 16), cols].astype(F32)[8:]
            prev = jnp.where(r > 0, prev, 0.0)
            xb = jnp.concatenate([prev, cur], axis=0)
            w = convw_ref[:, cols]
            acc = xb * w[CONV_WIDTH - 1:CONV_WIDTH]
            for j in range(1, CONV_WIDTH):
                acc = acc + pltpu.roll(xb, j, 0) * w[CONV_WIDTH - 1 - j:CONV_WIDTH - j]
            y = _silu(acc[8:])
            if g < 2 * DN_HEADS:
                y = y * lax.rsqrt(jnp.sum(y * y, axis=-1, keepdims=True) + EPS)
            hcols = slice((g % DN_HEADS) * LANES, (g % DN_HEADS + 1) * LANES)
            if g < DN_HEADS:
                q_s[pl.ds(r0, DN_ROWS), hcols] = y * (DN_HEAD ** -0.5)
            elif g < 2 * DN_HEADS:
                k_s[pl.ds(r0, DN_ROWS), hcols] = y
            else:
                v_s[pl.ds(r0, DN_ROWS), hcols] = y
        return carry

    lax.fori_loop(0, t // DN_ROWS, conv_rows, 0)

    gate = gate_ref[...]
    beta = jax.nn.sigmoid(gate)
    g = -jnp.exp(alog_ref[...]) * jax.nn.softplus(gate + dtb_ref[...])
    pos = lax.broadcasted_iota(jnp.int32, g.shape, 0) % CHUNK
    s = 1
    while s < CHUNK:
        g = g + jnp.where(pos >= s, pltpu.roll(g, s, 0), 0.0)
        s *= 2
    lane = lax.broadcasted_iota(jnp.int32, g.shape, 1)
    gate_s[...] = jnp.where(lane < DN_HEADS, beta, g)

    row = lax.broadcasted_iota(jnp.int32, (CHUNK, 2 * CHUNK), 0)
    col = lax.broadcasted_iota(jnp.int32, (CHUNK, 2 * CHUNK), 1)
    eye = row == col
    lower = row >= col
    strict = row > col
    upper_half = col >= CHUNK
    eye_hi = (col - CHUNK == row).astype(F32)
    zero_rows = jnp.zeros((CHUNK, DN_HEAD), BF16)

    items = [(cc, h) for cc in range(PREP_CHUNKS) for h in range(DN_HEADS)]

    def prep_body(i, carry):
        base = pl.multiple_of(i * (PREP_CHUNKS * CHUNK), PREP_CHUNKS * CHUNK)
        gates =[gate_s[pl.ds(base + cc * CHUNK, CHUNK), :] for cc in range(PREP_CHUNKS)]
        ks, lhs, decay, rhs, qe, kdt = [], [], [], [], [], []
        for cc, h in items:
            rows = pl.ds(base + cc * CHUNK, CHUNK)
            cols = slice(h * LANES, (h + 1) * LANES)
            q = q_s[rows, cols]
            k = k_s[rows, cols]
            v = v_s[rows, cols]
            beta_b = jnp.broadcast_to(gates[cc][:, h:h + 1], (CHUNK, DN_HEAD))
            gcb = jnp.broadcast_to(gates[cc][:, DN_HEADS + h:DN_HEADS + h + 1], (CHUNK, DN_HEAD))
            gr = jnp.sum(jnp.where(eye, gcb, 0.0), axis=0, keepdims=True)
            decay.append(jnp.where(lower, jnp.exp(jnp.where(lower, gcb - gr, 0.0)), 0.0))
            g_last = gcb[CHUNK - 1:CHUNK, :]
            eg = jnp.exp(gcb)
            kb = k * beta_b
            ks.append(jnp.concatenate([k.astype(BF16), zero_rows], axis=0))
            lhs.append(jnp.concatenate([kb, q], axis=0).astype(BF16))
            rhs.append(jnp.concatenate([v * beta_b, kb * eg], axis=1).astype(BF16))
            qe.append((q * eg).astype(BF16))
            kdt.append((k * jnp.exp(g_last - gcb)).T.astype(BF16))

        kq = [_dot_nt(a, b) for a, b in zip(lhs, ks)]
        m = [jnp.where(strict, x[:CHUNK] * d, 0.0) for x, d in zip(kq, decay)]
        a_qk = [(x[CHUNK:, :CHUNK] * d[:, :CHUNK]).astype(BF16) for x, d in zip(kq, decay)]

        w = [_dot(x[:, :CHUNK], x - eye_hi) + eye_hi for x in m]
        for _ in range(5):
            w = [_dot(x[:, :CHUNK], x) + jnp.where(upper_half, x, 0.0) for x in w]
        sol = [_dot(x[:, CHUNK:], b) for x, b in zip(w, rhs)]

        for n, (cc, h) in enumerate(items):
            rows = pl.ds(base + cc * CHUNK, CHUNK)
            rows_w = pl.ds(2 * base + cc * 2 * CHUNK, CHUNK)
            rows_q = pl.ds(2 * base + cc * 2 * CHUNK + CHUNK, CHUNK)
            rows_t = pl.ds(2 * base + cc * 2 * CHUNK, 2 * CHUNK)
            cols = slice(h * LANES, (h + 1) * LANES)
            cols_c = slice(h * LANES, h * LANES + CHUNK)
            v_s[rows, cols] = sol[n][:, :DN_HEAD]
            wq_s[rows_w, cols] = sol[n][:, DN_HEAD:].astype(BF16)
            wq_s[rows_q, cols] = qe[n]
            aqk_s[rows, cols_c] = a_qk[n]
            kdt_s[rows_t, cols_c] = kdt[n]
        return carry

    lax.fori_loop(0, n_chunks // PREP_CHUNKS, prep_body, 0)

    state_s[...] = jnp.zeros_like(state_s)
    gain = gain_ref[...]
    heads = range(DN_HEADS)

    def rec_body(c, carry):
        rows = pl.ds(pl.multiple_of(c * CHUNK, CHUNK), CHUNK)
        rows2 = pl.ds(pl.multiple_of(c * 2 * CHUNK, 2 * CHUNK), 2 * CHUNK)
        gates = gate_s[rows, :]
        cols = [slice(h * LANES, (h + 1) * LANES) for h in heads]
        cols_c = [slice(h * LANES, h * LANES + CHUNK) for h in heads]
        st = [state_s[h] for h in heads]
        r = [jnp.dot(wq_s[rows2, cols[h]], st[h].astype(BF16), preferred_element_type=F32)
             for h in heads]
        v_new = [(v_s[rows, cols[h]] - r[h][:CHUNK]).astype(BF16) for h in heads]
        upd = [jnp.dot(kdt_s[rows2, cols_c[h]], v_new[h], preferred_element_type=F32)
               for h in heads]
        for h in heads:
            g_last = gates[CHUNK - 1:CHUNK, DN_HEADS + h:DN_HEADS + h + 1]
            state_s[h] = st[h] * jnp.exp(g_last) + upd[h]
        o = [r[h][CHUNK:] + jnp.dot(aqk_s[rows, cols_c[h]], v_new[h], preferred_element_type=F32)
             for h in heads]
        for h in heads:
            y = o[h] * lax.rsqrt(jnp.mean(o[h] * o[h], axis=-1, keepdims=True) + EPS) * gain
            zf = z_ref[rows, cols[h]].astype(F32)
            out_ref[rows, cols[h]] = (y * _silu(zf)).astype(out_ref.dtype)
        return carry

    lax.fori_loop(0, n_chunks, rec_body, 0)


def _gated_deltanet(proj, gates, conv_w, alog8, dtb8, out_gain):
    b, t, _ = proj.shape
    assert t % (PREP_CHUNKS * CHUNK) == 0 and t % DN_ROWS == 0
    return pl.pallas_call(
        _gdn_kernel,
        grid=(b,),
        in_specs=[
            pl.BlockSpec((None, t, QKV_COLS), lambda i: (i, 0, 0)),
            pl.BlockSpec((None, t, DN_WIDTH), lambda i: (i, 0, QKV_COLS // DN_WIDTH)),
            pl.BlockSpec((None, t, N_GATES), lambda i: (i, 0, 0)),
            pl.BlockSpec((CONV_WIDTH, QKV_COLS), lambda i: (0, 0)),
            pl.BlockSpec((1, N_GATES), lambda i: (0, 0)),
            pl.BlockSpec((1, N_GATES), lambda i: (0, 0)),
            pl.BlockSpec((1, DN_HEAD), lambda i: (0, 0)),
        ],
        out_specs=pl.BlockSpec((None, t, DN_WIDTH), lambda i: (i, 0, 0)),
        out_shape=jax.ShapeDtypeStruct((b, t, DN_WIDTH), BF16),
        scratch_shapes=[
            pltpu.VMEM((t, DN_WIDTH), F32),
            pltpu.VMEM((t, DN_WIDTH), F32),
            pltpu.VMEM((t, DN_WIDTH), F32),
            pltpu.VMEM((2 * t, DN_WIDTH), BF16),
            pltpu.VMEM((t, DN_WIDTH), BF16),
            pltpu.VMEM((2 * t, DN_WIDTH), BF16),
            pltpu.VMEM((t, N_GATES), F32),
            pltpu.VMEM((DN_HEADS, DN_HEAD, DN_HEAD), F32),
        ],
        compiler_params=pltpu.CompilerParams(
            dimension_semantics=("parallel",), vmem_limit_bytes=VMEM_LIMIT),
        name="gated_deltanet",
    )(proj, proj, gates, conv_w, alog8, dtb8, out_gain)


AT_BQ = 512
AT_BK = 256
AT_CW = 256
AT_NQ = 2 * AT_BQ
AT_CHAINS = AT_NQ // AT_CW
AT_CPM = AT_BQ // AT_CW


def _attention_tiles(n_qblocks):
    all_chains = tuple(range(AT_CHAINS))
    lo = tuple(c for c in all_chains if c % AT_CPM == 0)
    hi = tuple(c for c in all_chains if c % AT_CPM == 1)
    tiles = []
    for qi in range(n_qblocks):
        for kt in range(AT_CPM * qi + AT_CPM):
            d = kt - AT_CPM * qi
            chains, masked = (all_chains, ()) if d < 0 else (all_chains, lo) if d == 0 else (hi, hi)
            tiles.append(dict(qi=qi, kt=kt, chains=chains, masked=masked, first=kt == 0))
    return tiles


def _diff_attn_kernel(q_ref, k_ref, vt_ref, z_ref, lq1_ref, lk1_ref, lq2_ref, lk2_ref,
                      og_ref, out_ref, qq_s, m_s, l_s, acc_s, s_a, s_b, c_a, c_b, p_a, p_b):
    t = q_ref.shape[0]
    n_qblocks = t // AT_BQ
    all_chains = tuple(range(AT_CHAINS))

    def lanes(c):
        return slice(c * AT_CW, (c + 1) * AT_CW)

    first = lax.broadcasted_iota(jnp.int32, (AT_BQ, LANES), 1) < DF_HEAD_QK
    zero = jnp.zeros((AT_BQ, LANES), BF16)
    for qi in range(n_qblocks):
        blk = q_ref[qi * AT_BQ:(qi + 1) * AT_BQ, :]
        qq_s[qi, :AT_BQ, :] = jnp.where(first, blk, zero)
        qq_s[qi, AT_BQ:, :] = jnp.where(first, zero, blk)

    lam = (jnp.exp(jnp.sum(lq1_ref[...] * lk1_ref[...], axis=-1, keepdims=True))
           - jnp.exp(jnp.sum(lq2_ref[...] * lk2_ref[...], axis=-1, keepdims=True))
           + LAMBDA_INIT)

    krow = lax.broadcasted_iota(jnp.int32, (AT_BK, AT_CW), 0)
    qcol = lax.broadcasted_iota(jnp.int32, (AT_BK, AT_CW), 1)
    causal = krow <= qcol
    bufs = ((s_a, c_a, p_a), (s_b, c_b, p_b))

    def scores(tile, c, buf):
        rows = slice(tile["kt"] * AT_BK, (tile["kt"] + 1) * AT_BK)
        s = _dot_nt(k_ref[rows, :], qq_s[tile["qi"], lanes(c), :])
        if c in tile["masked"]:
            s = jnp.where(causal, s, -jnp.inf)
        buf[0][:, lanes(c)] = s
        buf[1][:, lanes(c)] = jnp.max(s, axis=0, keepdims=True)

    def values(tile, c, buf):
        rows = slice(tile["kt"] * AT_BK, (tile["kt"] + 1) * AT_BK)
        return jnp.dot(vt_ref[:, rows], buf[2][:, lanes(c)], preferred_element_type=F32)

    def finish(qi):
        rows = slice(qi * AT_BQ, (qi + 1) * AT_BQ)
        o_all = acc_s[qi] * (1.0 / l_s[qi])
        o = o_all[:, :AT_BQ] - lam * o_all[:, AT_BQ:]
        y = (o * lax.rsqrt(jnp.mean(o * o, axis=0, keepdims=True) + EPS)).T
        y = y * og_ref[...] * (1.0 - LAMBDA_INIT)
        out_ref[rows, :] = (y * _silu(z_ref[rows, :].astype(F32))).astype(out_ref.dtype)

    tiles = _attention_tiles(n_qblocks)
    for c in tiles[0]["chains"]:
        scores(tiles[0], c, bufs[0])
    for n, cur in enumerate(tiles):
        nxt = tiles[n + 1] if n + 1 < len(tiles) else None
        prv = tiles[n - 1] if n > 0 else None
        s_cur, c_cur, p_cur = bufs[n % 2]
        qi = cur["qi"]
        for c in all_chains:
            alpha = None
            if c in cur["chains"]:
                if cur["first"]:
                    m_new = c_cur[:, lanes(c)]
                else:
                    m_old = m_s[qi, :, lanes(c)]
                    m_new = jnp.maximum(m_old, c_cur[:, lanes(c)])
                    alpha = jnp.exp2(m_old - m_new)
                p = jnp.exp2(s_cur[:, lanes(c)] - m_new)
                l_new = jnp.sum(p, axis=0, keepdims=True)
                if alpha is not None:
                    l_new = alpha * l_s[qi, :, lanes(c)] + l_new
                l_s[qi, :, lanes(c)] = l_new
                m_s[qi, :, lanes(c)] = m_new
                p_cur[:, lanes(c)] = p.astype(BF16)
            if nxt is not None and c in nxt["chains"]:
                scores(nxt, c, bufs[(n + 1) % 2])
            if prv is not None and c in prv["chains"]:
                pv = values(prv, c, bufs[(n - 1) % 2])
                pq = prv["qi"]
                acc = pv if prv["first"] else acc_s[pq, :, lanes(c)] + pv
                if pq == qi and alpha is not None:
                    acc = alpha * acc
                acc_s[pq, :, lanes(c)] = acc
        if prv is not None and prv["qi"] != qi:
            finish(prv["qi"])
    last = tiles[-1]
    for c in last["chains"]:
        acc_s[last["qi"], :, lanes(c)] = (acc_s[last["qi"], :, lanes(c)]
                                          + values(last, c, bufs[(len(tiles) - 1) % 2]))
    finish(last["qi"])


def _diff_attention(proj, vt, lq1, lk1, lq2, lk2, out_gain):
    b, t, _ = proj.shape
    assert AT_BQ == AT_CPM * AT_BK and AT_CW == AT_BK and AT_CPM == 2 and t % AT_BQ == 0
    n_qblocks = t // AT_BQ
    small = lambda n: pl.BlockSpec((1, n), lambda i, h: (0, 0))
    return pl.pallas_call(
        _diff_attn_kernel,
        grid=(b, DF_HEADS),
        in_specs=[
            pl.BlockSpec((None, t, LANES), lambda i, h: (i, 0, _BLK_DF_Q + h)),
            pl.BlockSpec((None, t, LANES), lambda i, h: (i, 0, _BLK_DF_K + h)),
            pl.BlockSpec((None, DF_HEAD_V, t), lambda i, h: (i, h, 0)),
            pl.BlockSpec((None, t, LANES), lambda i, h: (i, 0, _BLK_DF_Z + h)),
            small(DF_HEAD_QK), small(DF_HEAD_QK), small(DF_HEAD_QK), small(DF_HEAD_QK),
            small(DF_HEAD_V),
        ],
        out_specs=pl.BlockSpec((None, t, LANES), lambda i, h: (i, 0, h)),
        out_shape=jax.ShapeDtypeStruct((b, t, DF_WIDTH), BF16),
        scratch_shapes=[
            pltpu.VMEM((n_qblocks, AT_NQ, LANES), BF16),
            pltpu.VMEM((n_qblocks, 1, AT_NQ), F32),
            pltpu.VMEM((n_qblocks, 1, AT_NQ), F32),
            pltpu.VMEM((n_qblocks, DF_HEAD_V, AT_NQ), F32),
            pltpu.VMEM((AT_BK, AT_NQ), F32),
            pltpu.VMEM((AT_BK, AT_NQ), F32),
            pltpu.VMEM((1, AT_NQ), F32),
            pltpu.VMEM((1, AT_NQ), F32),
            pltpu.VMEM((AT_BK, AT_NQ), BF16),
            pltpu.VMEM((AT_BK, AT_NQ), BF16),
        ],
        compiler_params=pltpu.CompilerParams(
            dimension_semantics=("parallel", "parallel"), vmem_limit_bytes=VMEM_LIMIT),
        name="diff_attention",
    )(proj, proj, vt, proj, lq1, lk1, lq2, lk2, out_gain)


OUT_TM = 512


def _outproj_kernel(x_ref, dn_ref, df_ref, w_ref, out_ref):
    acc = jnp.dot(dn_ref[...], w_ref[:DN_WIDTH, :], preferred_element_type=F32)
    acc = acc + jnp.dot(df_ref[...], w_ref[DN_WIDTH:, :], preferred_element_type=F32)
    out_ref[...] = x_ref[...] + acc


def _out_projection(x2d, mixed_dn, mixed_df, w_out):
    m = x2d.shape[0]
    return pl.pallas_call(
        _outproj_kernel,
        grid=(m // OUT_TM,),
        in_specs=[
            pl.BlockSpec((OUT_TM, D_MODEL), lambda i: (i, 0)),
            pl.BlockSpec((OUT_TM, DN_WIDTH), lambda i: (i, 0)),
            pl.BlockSpec((OUT_TM, DF_WIDTH), lambda i: (i, 0)),
            pl.BlockSpec((DN_WIDTH + DF_WIDTH, D_MODEL), lambda i: (0, 0)),
        ],
        out_specs=pl.BlockSpec((OUT_TM, D_MODEL), lambda i: (i, 0)),
        out_shape=jax.ShapeDtypeStruct((m, D_MODEL), F32),
        compiler_params=pltpu.CompilerParams(
            dimension_semantics=("parallel",), vmem_limit_bytes=VMEM_LIMIT),
        name="out_projection",
    )(x2d, mixed_dn, mixed_df, w_out)


def kernel(x, norm_gain, w_in, conv_w, a_log, dt_bias, dn_out_gain, q_gain, k_gain,
           lambda_q1, lambda_k1, lambda_q2, lambda_k2, df_out_gain, w_out):
    b, t, d = x.shape
    assert d == D_MODEL and norm_gain.shape[0] == 1
    x2d = x.reshape(b * t, d)

    w = w_in[0]
    gate_lo = 4 * DN_WIDTH
    df_lo = gate_lo + N_GATES
    w_main = jnp.concatenate([w[:, :gate_lo], w[:, df_lo:df_lo + 2 * DF_WIDTH],
                              w[:, df_lo + 3 * DF_WIDTH:]], axis=1).astype(BF16)
    w_vt = w[:, df_lo + 2 * DF_WIDTH:df_lo + 3 * DF_WIDTH].T.astype(BF16)
    w_gate = jnp.pad(w[:, gate_lo:df_lo], ((0, 0), (0, LANES - N_GATES))).astype(BF16)
    half = jnp.arange(SEG_W, dtype=jnp.int32) // DF_HEAD_QK
    seg = (half[:, None] == half[None, :]).astype(BF16)
    reps = DF_WIDTH // DF_HEAD_QK
    qk_gain = jnp.concatenate([jnp.tile(q_gain, (1, reps)) * (DF_HEAD_QK ** -0.5 * LOG2_E),
                               jnp.tile(k_gain, (1, reps))], axis=0)

    proj, vt, gates = _in_projection(x2d, t, norm_gain, w_main, w_vt, w_gate, seg, qk_gain)
    proj = proj.reshape(b, t, PROJ_COLS)
    gates = gates.reshape(b, t, N_GATES)

    zeros4 = jnp.zeros((1, DN_HEADS), F32)
    alog8 = jnp.concatenate([zeros4, a_log], axis=1)
    dtb8 = jnp.concatenate([zeros4, dt_bias], axis=1)
    mixed_dn = _gated_deltanet(proj, gates, conv_w[0], alog8, dtb8, dn_out_gain)

    mixed_df = _diff_attention(proj, vt, lambda_q1, lambda_k1, lambda_q2, lambda_k2,
                               df_out_gain)

    out = _out_projection(x2d, mixed_dn.reshape(b * t, DN_WIDTH),
                          mixed_df.reshape(b * t, DF_WIDTH), w_out[0].astype(BF16))
    return out.reshape(b, t, d)
```

```python
import math

import jax
import jax.numpy as jnp
from jax import lax
from jax.experimental import pallas as pl
from jax.experimental.pallas import tpu as pltpu

F32 = jnp.float32
BF16 = jnp.bfloat16

D_MODEL = 1024
DN_HEADS = 4
DN_HEAD = 128
DN_WIDTH = DN_HEADS * DN_HEAD
CONV_WIDTH = 4
CHUNK = 64
DF_HEADS = 4
DF_HEAD_QK = 64
DF_HEAD_V = 128
DF_WIDTH = DF_HEADS * DF_HEAD_V
EPS = 1e-6
LAMBDA_INIT = 0.8 - 0.6 * math.exp(-0.3 * 0)
LOG2_E = math.log2(math.e)

N_GATES = 2 * DN_HEADS
PROJ_COLS = 4 * DN_WIDTH + 3 * DF_WIDTH
LANES = 128
VMEM_LIMIT = 56 * 1024 * 1024

_BLK_DN_Z = 3 * DN_HEADS
_BLK_DF_Q = 4 * DN_HEADS
_BLK_DF_K = _BLK_DF_Q + DF_HEADS
_BLK_DF_Z = _BLK_DF_K + DF_HEADS


def _silu(x):
    return x * jax.nn.sigmoid(x)


def _dot(a, b):
    return jnp.dot(a.astype(BF16), b.astype(BF16), preferred_element_type=F32)


def _dot_nt(a, b):
    return lax.dot_general(a.astype(BF16), b.astype(BF16), (((1,), (1,)), ((), ())),
                           preferred_element_type=F32)


IN_TM = 512
IN_TN = 512
SEG_W = 256


def _inproj_kernel(x_ref, gain_ref, w_ref, wvt_ref, wg_ref, seg_ref, qkg_ref,
                   proj_ref, vt_ref, gate_ref):
    x = x_ref[...]
    ms = jnp.mean(x * x, axis=-1, keepdims=True)
    h = (x * lax.rsqrt(ms + EPS) * gain_ref[...]).astype(BF16)
    for j in range(PROJ_COLS // IN_TN):
        cols = slice(j * IN_TN, (j + 1) * IN_TN)
        y = jnp.dot(h, w_ref[:, cols], preferred_element_type=F32)
        if j in (_BLK_DF_Q * LANES // IN_TN, _BLK_DF_K * LANES // IN_TN):
            which = 0 if j == _BLK_DF_Q * LANES // IN_TN else 1
            sq = (y * y).astype(BF16)
            ssq = jnp.concatenate(
                [jnp.dot(sq[:, i * SEG_W:(i + 1) * SEG_W], seg_ref[...],
                         preferred_element_type=F32) for i in range(IN_TN // SEG_W)], axis=1)
            y = y * lax.rsqrt(ssq * (1.0 / DF_HEAD_QK) + EPS) * qkg_ref[which:which + 1, :]
        proj_ref[:, cols] = y.astype(BF16)
    vt_ref[...] = _dot_nt(wvt_ref[...], h).astype(BF16)
    gate_ref[...] = jnp.dot(h, wg_ref[...], preferred_element_type=F32)[:, :N_GATES]


def _in_projection(x2d, t, gain, w_main, w_vt, w_gate, seg, qk_gain):
    m = x2d.shape[0]
    assert DF_WIDTH == IN_TN and t % IN_TM == 0
    tiles_per_seq = t // IN_TM
    const = lambda shape: pl.BlockSpec(shape, lambda i: (0, 0))
    return pl.pallas_call(
        _inproj_kernel,
        grid=(m // IN_TM,),
        in_specs=[
            pl.BlockSpec((IN_TM, D_MODEL), lambda i: (i, 0)),
            const((1, D_MODEL)),
            const((D_MODEL, PROJ_COLS)),
            const((DF_WIDTH, D_MODEL)),
            const((D_MODEL, LANES)),
            const((SEG_W, SEG_W)),
            const((2, DF_WIDTH)),
        ],
        out_specs=[
            pl.BlockSpec((IN_TM, PROJ_COLS), lambda i: (i, 0)),
            pl.BlockSpec((None, DF_WIDTH, IN_TM),
                         lambda i: (i // tiles_per_seq, 0, i % tiles_per_seq)),
            pl.BlockSpec((IN_TM, N_GATES), lambda i: (i, 0)),
        ],
        out_shape=[
            jax.ShapeDtypeStruct((m, PROJ_COLS), BF16),
            jax.ShapeDtypeStruct((m // t, DF_WIDTH, t), BF16),
            jax.ShapeDtypeStruct((m, N_GATES), F32),
        ],
        compiler_params=pltpu.CompilerParams(
            dimension_semantics=("parallel",), vmem_limit_bytes=VMEM_LIMIT),
        name="in_projection",
    )(x2d, gain, w_main, w_vt, w_gate, seg, qk_gain)


DN_ROWS = 256
QKV_COLS = 3 * DN_WIDTH
PREP_CHUNKS = 4


def _gdn_kernel(qkv_ref, z_ref, gate_ref, convw_ref, alog_ref, dtb_ref, gain_ref,
                out_ref, q_s, k_s, v_s, wq_s, aqk_s, kdt_s, gate_s, state_s):
    t = qkv_ref.shape[0]
    n_chunks = t // CHUNK

    def conv_rows(r, carry):
        r0 = pl.multiple_of(r * DN_ROWS, DN_ROWS)
        p0 = pl.multiple_of(jnp.maximum(r0 - 16, 0), 16)
        for g in range(QKV_COLS // LANES):
            cols = slice(g * LANES, (g + 1) * LANES)
            cur = qkv_ref[pl.ds(r0, DN_ROWS), cols].astype(F32)
            prev = qkv_ref[pl.ds(p0, 16), cols].astype(F32)[8:]
            prev = jnp.where(r > 0, prev, 0.0)
            xb = jnp.concatenate([prev, cur], axis=0)
            w = convw_ref[:, cols]
            acc = xb * w[CONV_WIDTH - 1:CONV_WIDTH]
            for j in range(1, CONV_WIDTH):
                acc = acc + pltpu.roll(xb, j, 0) * w[CONV_WIDTH - 1 - j:CONV_WIDTH - j]
            y = _silu(acc[8:])
            if g < 2 * DN_HEADS:
                y = y * lax.rsqrt(jnp.sum(y * y, axis=-1, keepdims=True) + EPS)
            hcols = slice((g % DN_HEADS) * LANES, (g % DN_HEADS + 1) * LANES)
            if g < DN_HEADS:
                q_s[pl.ds(r0, DN_ROWS), hcols] = y * (DN_HEAD ** -0.5)
            elif g < 2 * DN_HEADS:
                k_s[pl.ds(r0, DN_ROWS), hcols] = y
            else:
                v_s[pl.ds(r0, DN_ROWS), hcols] = y
        return carry

    lax.fori_loop(0, t // DN_ROWS, conv_rows, 0)

    gate = gate_ref[...]
    beta = jax.nn.sigmoid(gate)
    g = -jnp.exp(alog_ref[...]) * jax.nn.softplus(gate + dtb_ref[...])
    pos = lax.broadcasted_iota(jnp.int32, g.shape, 0) % CHUNK
    s = 1
    while s < CHUNK:
        g = g + jnp.where(pos >= s, pltpu.roll(g, s, 0), 0.0)
        s *= 2
    lane = lax.broadcasted_iota(jnp.int32, g.shape, 1)
    gate_s[...] = jnp.where(lane < DN_HEADS, beta, g)

    row = lax.broadcasted_iota(jnp.int32, (CHUNK, 2 * CHUNK), 0)
    col = lax.broadcasted_iota(jnp.int32, (CHUNK, 2 * CHUNK), 1)
    eye = row == col
    lower = row >= col
    strict = row > col
    upper_half = col >= CHUNK
    eye_hi = (col - CHUNK == row).astype(F32)
    zero_rows = jnp.zeros((CHUNK, DN_HEAD), BF16)

    items = [(cc, h) for cc in range(PREP_CHUNKS) for h in range(DN_HEADS)]
    heads = range(DN_HEADS)
    gain = gain_ref[...]
    group_rows = PREP_CHUNKS * CHUNK

    def group_base(g):
        return g * group_rows if isinstance(g, int) else pl.multiple_of(g * group_rows, group_rows)

    def halves(fn, seq):
        out = []
        mid = len(seq) // 2
        for part in (seq[:mid], seq[mid:]):
            out.extend(fn(x) for x in part)
            yield out

    def prep(g):
        base = group_base(g)
        gates = [gate_s[pl.ds(base + cc * CHUNK, CHUNK), :] for cc in range(PREP_CHUNKS)]
        ks, lhs, decay, rhs, qe, kdt = [], [], [], [], [], []
        for cc, h in items:
            rows = pl.ds(base + cc * CHUNK, CHUNK)
            cols = slice(h * LANES, (h + 1) * LANES)
            q = q_s[rows, cols]
            k = k_s[rows, cols]
            v = v_s[rows, cols]
            beta_b = jnp.broadcast_to(gates[cc][:, h:h + 1], (CHUNK, DN_HEAD))
            gcb = jnp.broadcast_to(gates[cc][:, DN_HEADS + h:DN_HEADS + h + 1], (CHUNK, DN_HEAD))
            gr = jnp.sum(jnp.where(eye, gcb, 0.0), axis=0, keepdims=True)
            decay.append(jnp.where(lower, jnp.exp(jnp.where(lower, gcb - gr, 0.0)), 0.0))
            g_last = gcb[CHUNK - 1:CHUNK, :]
            eg = jnp.exp(gcb)
            kb = k * beta_b
            ks.append(jnp.concatenate([k.astype(BF16), zero_rows], axis=0))
            lhs.append(jnp.concatenate([kb, q], axis=0).astype(BF16))
            rhs.append(jnp.concatenate([v * beta_b, kb * eg], axis=1).astype(BF16))
            qe.append((q * eg).astype(BF16))
            kdt.append((k * jnp.exp(g_last - gcb)).T.astype(BF16))
        yield

        kq = None
        for kq in halves(lambda ab: _dot_nt(*ab), list(zip(lhs, ks))):
            yield
        m = [jnp.where(strict, x[:CHUNK] * d, 0.0) for x, d in zip(kq, decay)]
        a_qk = [(x[CHUNK:, :CHUNK] * d[:, :CHUNK]).astype(BF16) for x, d in zip(kq, decay)]

        w = None
        for w in halves(lambda x: _dot(x[:, :CHUNK], x - eye_hi) + eye_hi, m):
            yield
        for _ in range(5):
            prev = w
            for w in halves(lambda x: _dot(x[:, :CHUNK], x) + jnp.where(upper_half, x, 0.0), prev):
                yield
        sol = None
        for sol in halves(lambda xb: _dot(xb[0][:, CHUNK:], xb[1]), list(zip(w, rhs))):
            yield

        for n, (cc, h) in enumerate(items):
            rows = pl.ds(base + cc * CHUNK, CHUNK)
            rows_w = pl.ds(2 * base + cc * 2 * CHUNK, CHUNK)
            rows_q = pl.ds(2 * base + cc * 2 * CHUNK + CHUNK, CHUNK)
            rows_t = pl.ds(2 * base + cc * 2 * CHUNK, 2 * CHUNK)
            cols = slice(h * LANES, (h + 1) * LANES)
            cols_c = slice(h * LANES, h * LANES + CHUNK)
            v_s[rows, cols] = sol[n][:, :DN_HEAD]
            wq_s[rows_w, cols] = sol[n][:, DN_HEAD:].astype(BF16)
            wq_s[rows_q, cols] = qe[n]
            aqk_s[rows, cols_c] = a_qk[n]
            kdt_s[rows_t, cols_c] = kdt[n]
        yield

    def rec(g):
        base = group_base(g)
        cols = [slice(h * LANES, (h + 1) * LANES) for h in heads]
        cols_c = [slice(h * LANES, h * LANES + CHUNK) for h in heads]
        ld = []
        for cc in range(PREP_CHUNKS):
            rows = pl.ds(base + cc * CHUNK, CHUNK)
            rows2 = pl.ds(2 * base + cc * 2 * CHUNK, 2 * CHUNK)
            ld.append(dict(
                rows=rows,
                gates=gate_s[rows, :],
                wq=[wq_s[rows2, cols[h]] for h in heads],
                u=[v_s[rows, cols[h]] for h in heads],
                kdt=[kdt_s[rows2, cols_c[h]] for h in heads],
                aqk=[aqk_s[rows, cols_c[h]] for h in heads]))
        yield
        for c in ld:
            st = [state_s[h] for h in heads]
            r = [jnp.dot(c["wq"][h], st[h].astype(BF16), preferred_element_type=F32)
                 for h in heads]
            yield
            v_new = [(c["u"][h] - r[h][:CHUNK]).astype(BF16) for h in heads]
            upd = [jnp.dot(c["kdt"][h], v_new[h], preferred_element_type=F32) for h in heads]
            for h in heads:
                g_last = c["gates"][CHUNK - 1:CHUNK, DN_HEADS + h:DN_HEADS + h + 1]
                state_s[h] = st[h] * jnp.exp(g_last) + upd[h]
            yield
            o = [r[h][CHUNK:] + jnp.dot(c["aqk"][h], v_new[h], preferred_element_type=F32)
                 for h in heads]
            for h in heads:
                y = o[h] * lax.rsqrt(jnp.mean(o[h] * o[h], axis=-1, keepdims=True) + EPS) * gain
                zf = z_ref[c["rows"], cols[h]].astype(F32)
                out_ref[c["rows"], cols[h]] = (y * _silu(zf)).astype(out_ref.dtype)
            yield

    def run_interleaved(*gens):
        gens = list(gens)
        while gens:
            for gen in list(gens):
                if next(gen, StopIteration) is StopIteration:
                    gens.remove(gen)

    n_groups = n_chunks // PREP_CHUNKS
    state_s[...] = jnp.zeros_like(state_s)
    run_interleaved(prep(0))

    def group_body(g, carry):
        run_interleaved(rec(g - 1), prep(g))
        return carry

    lax.fori_loop(1, n_groups, group_body, 0)
    run_interleaved(rec(n_groups - 1))


def _gated_deltanet(proj, gates, conv_w, alog8, dtb8, out_gain):
    b, t, _ = proj.shape
    assert t % (PREP_CHUNKS * CHUNK) == 0 and t % DN_ROWS == 0
    return pl.pallas_call(
        _gdn_kernel,
        grid=(b,),
        in_specs=[
            pl.BlockSpec((None, t, QKV_COLS), lambda i: (i, 0, 0)),
            pl.BlockSpec((None, t, DN_WIDTH), lambda i: (i, 0, QKV_COLS // DN_WIDTH)),
            pl.BlockSpec((None, t, N_GATES), lambda i: (i, 0, 0)),
            pl.BlockSpec((CONV_WIDTH, QKV_COLS), lambda i: (0, 0)),
            pl.BlockSpec((1, N_GATES), lambda i: (0, 0)),
            pl.BlockSpec((1, N_GATES), lambda i: (0, 0)),
            pl.BlockSpec((1, DN_HEAD), lambda i: (0, 0)),
        ],
        out_specs=pl.BlockSpec((None, t, DN_WIDTH), lambda i: (i, 0, 0)),
        out_shape=jax.ShapeDtypeStruct((b, t, DN_WIDTH), BF16),
        scratch_shapes=[
            pltpu.VMEM((t, DN_WIDTH), F32),
            pltpu.VMEM((t, DN_WIDTH), F32),
            pltpu.VMEM((t, DN_WIDTH), F32),
            pltpu.VMEM((2 * t, DN_WIDTH), BF16),
            pltpu.VMEM((t, DN_WIDTH), BF16),
            pltpu.VMEM((2 * t, DN_WIDTH), BF16),
            pltpu.VMEM((t, N_GATES), F32),
            pltpu.VMEM((DN_HEADS, DN_HEAD, DN_HEAD), F32),
        ],
        compiler_params=pltpu.CompilerParams(
            dimension_semantics=("parallel",), vmem_limit_bytes=VMEM_LIMIT),
        name="gated_deltanet",
    )(proj, proj, gates, conv_w, alog8, dtb8, out_gain)


AT_BQ = 512
AT_BK = 256
AT_CW = 256
AT_NQ = 2 * AT_BQ
AT_CHAINS = AT_NQ // AT_CW
AT_CPM = AT_BQ // AT_CW


def _attention_tiles(n_qblocks):
    all_chains = tuple(range(AT_CHAINS))
    lo = tuple(c for c in all_chains if c % AT_CPM == 0)
    hi = tuple(c for c in all_chains if c % AT_CPM == 1)
    tiles = []
    for qi in range(n_qblocks):
        for kt in range(AT_CPM * qi + AT_CPM):
            d = kt - AT_CPM * qi
            chains, masked = (all_chains, ()) if d < 0 else (all_chains, lo) if d == 0 else (hi, hi)
            tiles.append(dict(qi=qi, kt=kt, chains=chains, masked=masked, first=kt == 0))
    return tiles


def _diff_attn_kernel(q_ref, k_ref, vt_ref, z_ref, lq1_ref, lk1_ref, lq2_ref, lk2_ref,
                      og_ref, out_ref, qq_s, m_s, l_s, acc_s, s_a, s_b, c_a, c_b, p_a, p_b):
    t = q_ref.shape[0]
    n_qblocks = t // AT_BQ
    all_chains = tuple(range(AT_CHAINS))

    def lanes(c):
        return slice(c * AT_CW, (c + 1) * AT_CW)

    first = lax.broadcasted_iota(jnp.int32, (AT_BQ, LANES), 1) < DF_HEAD_QK
    zero = jnp.zeros((AT_BQ, LANES), BF16)
    for qi in range(n_qblocks):
        blk = q_ref[qi * AT_BQ:(qi + 1) * AT_BQ, :]
        qq_s[qi, :AT_BQ, :] = jnp.where(first, blk, zero)
        qq_s[qi, AT_BQ:, :] = jnp.where(first, zero, blk)

    lam = (jnp.exp(jnp.sum(lq1_ref[...] * lk1_ref[...], axis=-1, keepdims=True))
           - jnp.exp(jnp.sum(lq2_ref[...] * lk2_ref[...], axis=-1, keepdims=True))
           + LAMBDA_INIT)

    krow = lax.broadcasted_iota(jnp.int32, (AT_BK, AT_CW), 0)
    qcol = lax.broadcasted_iota(jnp.int32, (AT_BK, AT_CW), 1)
    causal = krow <= qcol
    bufs = ((s_a, c_a, p_a), (s_b, c_b, p_b))

    def scores(tile, c, buf):
        rows = slice(tile["kt"] * AT_BK, (tile["kt"] + 1) * AT_BK)
        s = _dot_nt(k_ref[rows, :], qq_s[tile["qi"], lanes(c), :])
        if c in tile["masked"]:
            s = jnp.where(causal, s, -jnp.inf)
        buf[0][:, lanes(c)] = s
        buf[1][:, lanes(c)] = jnp.max(s, axis=0, keepdims=True)

    def values(tile, c, buf):
        rows = slice(tile["kt"] * AT_BK, (tile["kt"] + 1) * AT_BK)
        return jnp.dot(vt_ref[:, rows], buf[2][:, lanes(c)], preferred_element_type=F32)

    def finish(qi):
        rows = slice(qi * AT_BQ, (qi + 1) * AT_BQ)
        o_all = acc_s[qi] * (1.0 / l_s[qi])
        o = o_all[:, :AT_BQ] - lam * o_all[:, AT_BQ:]
        y = (o * lax.rsqrt(jnp.mean(o * o, axis=0, keepdims=True) + EPS)).T
        y = y * og_ref[...] * (1.0 - LAMBDA_INIT)
        out_ref[rows, :] = (y * _silu(z_ref[rows, :].astype(F32))).astype(out_ref.dtype)

    tiles = _attention_tiles(n_qblocks)
    for c in tiles[0]["chains"]:
        scores(tiles[0], c, bufs[0])
    for n, cur in enumerate(tiles):
        nxt = tiles[n + 1] if n + 1 < len(tiles) else None
        prv = tiles[n - 1] if n > 0 else None
        s_cur, c_cur, p_cur = bufs[n % 2]
        qi = cur["qi"]
        for c in all_chains:
            alpha = None
            if c in cur["chains"]:
                if cur["first"]:
                    m_new = c_cur[:, lanes(c)]
                else:
                    m_old = m_s[qi, :, lanes(c)]
                    m_new = jnp.maximum(m_old, c_cur[:, lanes(c)])
                    alpha = jnp.exp2(m_old - m_new)
                p = jnp.exp2(s_cur[:, lanes(c)] - m_new)
                l_new = jnp.sum(p, axis=0, keepdims=True)
                if alpha is not None:
                    l_new = alpha * l_s[qi, :, lanes(c)] + l_new
                l_s[qi, :, lanes(c)] = l_new
                m_s[qi, :, lanes(c)] = m_new
                p_cur[:, lanes(c)] = p.astype(BF16)
            if nxt is not None and c in nxt["chains"]:
                scores(nxt, c, bufs[(n + 1) % 2])
            if prv is not None and c in prv["chains"]:
                pv = values(prv, c, bufs[(n - 1) % 2])
                pq = prv["qi"]
                acc = pv if prv["first"] else acc_s[pq, :, lanes(c)] + pv
                if pq == qi and alpha is not None:
                    acc = alpha * acc
                acc_s[pq, :, lanes(c)] = acc
        if prv is not None and prv["qi"] != qi:
            finish(prv["qi"])
    last = tiles[-1]
    for c in last["chains"]:
        acc_s[last["qi"], :, lanes(c)] = (acc_s[last["qi"], :, lanes(c)]
                                          + values(last, c, bufs[(len(tiles) - 1) % 2]))
    finish(last["qi"])


def _diff_attention(proj, vt, lq1, lk1, lq2, lk2, out_gain):
    b, t, _ = proj.shape
    assert AT_BQ == AT_CPM * AT_BK and AT_CW == AT_BK and AT_CPM == 2 and t % AT_BQ == 0
    n_qblocks = t // AT_BQ
    small = lambda n: pl.BlockSpec((1, n), lambda i, h: (0, 0))
    return pl.pallas_call(
        _diff_attn_kernel,
        grid=(b, DF_HEADS),
        in_specs=[
            pl.BlockSpec((None, t, LANES), lambda i, h: (i, 0, _BLK_DF_Q + h)),
            pl.BlockSpec((None, t, LANES), lambda i, h: (i, 0, _BLK_DF_K + h)),
            pl.BlockSpec((None, DF_HEAD_V, t), lambda i, h: (i, h, 0)),
            pl.BlockSpec((None, t, LANES), lambda i, h: (i, 0, _BLK_DF_Z + h)),
            small(DF_HEAD_QK), small(DF_HEAD_QK), small(DF_HEAD_QK), small(DF_HEAD_QK),
            small(DF_HEAD_V),
        ],
        out_specs=pl.BlockSpec((None, t, LANES), lambda i, h: (i, 0, h)),
        out_shape=jax.ShapeDtypeStruct((b, t, DF_WIDTH), BF16),
        scratch_shapes=[
            pltpu.VMEM((n_qblocks, AT_NQ, LANES), BF16),
            pltpu.VMEM((n_qblocks, 1, AT_NQ), F32),
            pltpu.VMEM((n_qblocks, 1, AT_NQ), F32),
            pltpu.VMEM((n_qblocks, DF_HEAD_V, AT_NQ), F32),
            pltpu.VMEM((AT_BK, AT_NQ), F32),
            pltpu.VMEM((AT_BK, AT_NQ), F32),
            pltpu.VMEM((1, AT_NQ), F32),
            pltpu.VMEM((1, AT_NQ), F32),
            pltpu.VMEM((AT_BK, AT_NQ), BF16),
            pltpu.VMEM((AT_BK, AT_NQ), BF16),
        ],
        compiler_params=pltpu.CompilerParams(
            dimension_semantics=("parallel", "parallel"), vmem_limit_bytes=VMEM_LIMIT),
        name="diff_attention",
    )(proj, proj, vt, proj, lq1, lk1, lq2, lk2, out_gain)


OUT_TM = 512


def _outproj_kernel(x_ref, dn_ref, df_ref, w_ref, out_ref):
    acc = jnp.dot(dn_ref[...], w_ref[:DN_WIDTH, :], preferred_element_type=F32)
    acc = acc + jnp.dot(df_ref[...], w_ref[DN_WIDTH:, :], preferred_element_type=F32)
    out_ref[...] = x_ref[...] + acc


def _out_projection(x2d, mixed_dn, mixed_df, w_out):
    m = x2d.shape[0]
    return pl.pallas_call(
        _outproj_kernel,
        grid=(m // OUT_TM,),
        in_specs=[
            pl.BlockSpec((OUT_TM, D_MODEL), lambda i: (i, 0)),
            pl.BlockSpec((OUT_TM, DN_WIDTH), lambda i: (i, 0)),
            pl.BlockSpec((OUT_TM, DF_WIDTH), lambda i: (i, 0)),
            pl.BlockSpec((DN_WIDTH + DF_WIDTH, D_MODEL), lambda i: (0, 0)),
        ],
        out_specs=pl.BlockSpec((OUT_TM, D_MODEL), lambda i: (i, 0)),
        out_shape=jax.ShapeDtypeStruct((m, D_MODEL), F32),
        compiler_params=pltpu.CompilerParams(
            dimension_semantics=("parallel",), vmem_limit_bytes=VMEM_LIMIT),
        name="out_projection",
    )(x2d, mixed_dn, mixed_df, w_out)


def kernel(x, norm_gain, w_in, conv_w, a_log, dt_bias, dn_out_gain, q_gain, k_gain,
           lambda_q1, lambda_k1, lambda_q2, lambda_k2, df_out_gain, w_out):
    b, t, d = x.shape
    assert d == D_MODEL and norm_gain.shape[0] == 1
    x2d = x.reshape(b * t, d)

    w = w_in[0]
    gate_lo = 4 * DN_WIDTH
    df_lo = gate_lo + N_GATES
    w_main = jnp.concatenate([w[:, :gate_lo], w[:, df_lo:df_lo + 2 * DF_WIDTH],
                              w[:, df_lo + 3 * DF_WIDTH:]], axis=1).astype(BF16)
    w_vt = w[:, df_lo + 2 * DF_WIDTH:df_lo + 3 * DF_WIDTH].T.astype(BF16)
    w_gate = jnp.pad(w[:, gate_lo:df_lo], ((0, 0), (0, LANES - N_GATES))).astype(BF16)
    half = jnp.arange(SEG_W, dtype=jnp.int32) // DF_HEAD_QK
    seg = (half[:, None] == half[None, :]).astype(BF16)
    reps = DF_WIDTH // DF_HEAD_QK
    qk_gain = jnp.concatenate([jnp.tile(q_gain, (1, reps)) * (DF_HEAD_QK ** -0.5 * LOG2_E),
                               jnp.tile(k_gain, (1, reps))], axis=0)

    proj, vt, gates = _in_projection(x2d, t, norm_gain, w_main, w_vt, w_gate, seg, qk_gain)
    proj = proj.reshape(b, t, PROJ_COLS)
    gates = gates.reshape(b, t, N_GATES)

    zeros4 = jnp.zeros((1, DN_HEADS), F32)
    alog8 = jnp.concatenate([zeros4, a_log], axis=1)
    dtb8 = jnp.concatenate([zeros4, dt_bias], axis=1)
    mixed_dn = _gated_deltanet(proj, gates, conv_w[0], alog8, dtb8, dn_out_gain)

    mixed_df = _diff_attention(proj, vt, lambda_q1, lambda_k1, lambda_q2, lambda_k2,
                               df_out_gain)

    out = _out_projection(x2d, mixed_dn.reshape(b * t, DN_WIDTH),
                          mixed_df.reshape(b * t, DF_WIDTH), w_out[0].astype(BF16))
    return out.reshape(b, t, d)
```

```python
import math

import jax
import jax.numpy as jnp
from jax import lax
from jax.experimental import pallas as pl
from jax.experimental.pallas import tpu as pltpu

F32 = jnp.float32
BF16 = jnp.bfloat16

D_MODEL = 1024
DN_HEADS = 4
DN_HEAD = 128
DN_WIDTH = DN_HEADS * DN_HEAD
CONV_WIDTH = 4
CHUNK = 64
DF_HEADS = 4
DF_HEAD_QK = 64
DF_HEAD_V = 128
DF_WIDTH = DF_HEADS * DF_HEAD_V
EPS = 1e-6
LAMBDA_INIT = 0.8 - 0.6 * math.exp(-0.3 * 0)
LOG2_E = math.log2(math.e)

N_GATES = 2 * DN_HEADS
PROJ_COLS = 4 * DN_WIDTH + 3 * DF_WIDTH
LANES = 128
VMEM_LIMIT = 56 * 1024 * 1024

_BLK_DN_Z = 3 * DN_HEADS
_BLK_DF_Q = 4 * DN_HEADS
_BLK_DF_K = _BLK_DF_Q + DF_HEADS
_BLK_DF_Z = _BLK_DF_K + DF_HEADS


def _silu(x):
    return x * jax.nn.sigmoid(x)


def _dot(a, b):
    return jnp.dot(a.astype(BF16), b.astype(BF16), preferred_element_type=F32)


def _dot_nt(a, b):
    return lax.dot_general(a.astype(BF16), b.astype(BF16), (((1,), (1,)), ((), ())),
                           preferred_element_type=F32)


IN_TM = 512
IN_TN = 512
SEG_W = 256


def _inproj_kernel(x_ref, gain_ref, w_ref, wvt_ref, wg_ref, seg_ref, qkg_ref,
                   proj_ref, vt_ref, gate_ref):
    x = x_ref[...]
    ms = jnp.mean(x * x, axis=-1, keepdims=True)
    h = (x * lax.rsqrt(ms + EPS) * gain_ref[...]).astype(BF16)
    for j in range(PROJ_COLS // IN_TN):
        cols = slice(j * IN_TN, (j + 1) * IN_TN)
        y = jnp.dot(h, w_ref[:, cols], preferred_element_type=F32)
        if j in (_BLK_DF_Q * LANES // IN_TN, _BLK_DF_K * LANES // IN_TN):
            which = 0 if j == _BLK_DF_Q * LANES // IN_TN else 1
            sq = (y * y).astype(BF16)
            ssq = jnp.concatenate(
                [jnp.dot(sq[:, i * SEG_W:(i + 1) * SEG_W], seg_ref[...],
                         preferred_element_type=F32) for i in range(IN_TN // SEG_W)], axis=1)
            y = y * lax.rsqrt(ssq * (1.0 / DF_HEAD_QK) + EPS) * qkg_ref[which:which + 1, :]
        proj_ref[:, cols] = y.astype(BF16)
    vt_ref[...] = _dot_nt(wvt_ref[...], h).astype(BF16)
    gate_ref[...] = jnp.dot(h, wg_ref[...], preferred_element_type=F32)[:, :N_GATES]


def _in_projection(x2d, t, gain, w_main, w_vt, w_gate, seg, qk_gain):
    m = x2d.shape[0]
    assert DF_WIDTH == IN_TN and t % IN_TM == 0
    tiles_per_seq = t // IN_TM
    const = lambda shape: pl.BlockSpec(shape, lambda i: (0, 0))
    return pl.pallas_call(
        _inproj_kernel,
        grid=(m // IN_TM,),
        in_specs=[
            pl.BlockSpec((IN_TM, D_MODEL), lambda i: (i, 0)),
            const((1, D_MODEL)),
            const((D_MODEL, PROJ_COLS)),
            const((DF_WIDTH, D_MODEL)),
            const((D_MODEL, LANES)),
            const((SEG_W, SEG_W)),
            const((2, DF_WIDTH)),
        ],
        out_specs=[
            pl.BlockSpec((IN_TM, PROJ_COLS), lambda i: (i, 0)),
            pl.BlockSpec((None, DF_WIDTH, IN_TM),
                         lambda i: (i // tiles_per_seq, 0, i % tiles_per_seq)),
            pl.BlockSpec((IN_TM, N_GATES), lambda i: (i, 0)),
        ],
        out_shape=[
            jax.ShapeDtypeStruct((m, PROJ_COLS), BF16),
            jax.ShapeDtypeStruct((m // t, DF_WIDTH, t), BF16),
            jax.ShapeDtypeStruct((m, N_GATES), F32),
        ],
        compiler_params=pltpu.CompilerParams(
            dimension_semantics=("parallel",), vmem_limit_bytes=VMEM_LIMIT),
        name="in_projection",
    )(x2d, gain, w_main, w_vt, w_gate, seg, qk_gain)


DN_ROWS = 256
QKV_COLS = 3 * DN_WIDTH
PREP_CHUNKS = 4


def _gdn_kernel(qkv_ref, z_ref, gate_ref, convw_ref, alog_ref, dtb_ref, gain_ref,
                out_ref, q_s, k_s, v_s, wq_s, aqk_s, kdt_s, gate_s, state_s, stage_s):
    t = qkv_ref.shape[0]
    n_chunks = t // CHUNK

    group_rows = PREP_CHUNKS * CHUNK

    def group_base(g):
        return g * group_rows if isinstance(g, int) else pl.multiple_of(g * group_rows, group_rows)

    def conv(g):
        r0 = group_base(g)
        for cg in range(QKV_COLS // LANES):
            cols = slice(cg * LANES, (cg + 1) * LANES)
            cur = qkv_ref[pl.ds(r0, group_rows), cols].astype(F32)
            if isinstance(g, int) and g == 0:
                prev = jnp.zeros((8, LANES), F32)
            else:
                prev = qkv_ref[pl.ds(pl.multiple_of(r0 - 16, 16), 16), cols].astype(F32)[8:]
            stage_s[cg, :8, :] = prev
            stage_s[cg, 8:, :] = cur
            w = convw_ref[:, cols]
            acc = cur * w[CONV_WIDTH - 1:CONV_WIDTH]
            for j in range(1, CONV_WIDTH):
                acc = acc + (stage_s[cg, 8 - j:8 - j + group_rows, :]
                             * w[CONV_WIDTH - 1 - j:CONV_WIDTH - j])
            y = _silu(acc)
            hcols = slice((cg % DN_HEADS) * LANES, (cg % DN_HEADS + 1) * LANES)
            if cg < 2 * DN_HEADS:
                inv_norm = lax.rsqrt(jnp.sum(y * y, axis=-1, keepdims=True) + EPS)
                if cg < DN_HEADS:
                    q_s[pl.ds(r0, group_rows), hcols] = y * (inv_norm * (DN_HEAD ** -0.5))
                else:
                    k_s[pl.ds(r0, group_rows), hcols] = y * inv_norm
            else:
                v_s[pl.ds(r0, group_rows), hcols] = y
            yield

    gate = gate_ref[...]
    beta = jax.nn.sigmoid(gate)
    g = -jnp.exp(alog_ref[...]) * jax.nn.softplus(gate + dtb_ref[...])
    pos = lax.broadcasted_iota(jnp.int32, g.shape, 0) % CHUNK
    s = 1
    while s < CHUNK:
        g = g + jnp.where(pos >= s, pltpu.roll(g, s, 0), 0.0)
        s *= 2
    lane = lax.broadcasted_iota(jnp.int32, g.shape, 1)
    gate_s[...] = jnp.where(lane < DN_HEADS, beta, g)

    row = lax.broadcasted_iota(jnp.int32, (CHUNK, 2 * CHUNK), 0)
    col = lax.broadcasted_iota(jnp.int32, (CHUNK, 2 * CHUNK), 1)
    eye = row == col
    lower = row >= col
    strict = row > col
    upper_half = col >= CHUNK
    eye_hi = (col - CHUNK == row).astype(F32)
    zero_rows = jnp.zeros((CHUNK, DN_HEAD), BF16)

    items = [(cc, h) for cc in range(PREP_CHUNKS) for h in range(DN_HEADS)]
    heads = range(DN_HEADS)
    gain = gain_ref[...]

    def halves(fn, seq):
        out = []
        mid = len(seq) // 2
        for part in (seq[:mid], seq[mid:]):
            out.extend(fn(x) for x in part)
            yield out

    def prep(g):
        base = group_base(g)
        gates = [gate_s[pl.ds(base + cc * CHUNK, CHUNK), :] for cc in range(PREP_CHUNKS)]
        ks, lhs, decay, rhs, qe, kdt = [], [], [], [], [], []
        for cc, h in items:
            rows = pl.ds(base + cc * CHUNK, CHUNK)
            cols = slice(h * LANES, (h + 1) * LANES)
            q = q_s[rows, cols]
            k = k_s[rows, cols]
            v = v_s[rows, cols]
            beta_b = jnp.broadcast_to(gates[cc][:, h:h + 1], (CHUNK, DN_HEAD))
            gcb = jnp.broadcast_to(gates[cc][:, DN_HEADS + h:DN_HEADS + h + 1], (CHUNK, DN_HEAD))
            gr = jnp.sum(jnp.where(eye, gcb, 0.0), axis=0, keepdims=True)
            decay.append(jnp.where(lower, jnp.exp(jnp.where(lower, gcb - gr, 0.0)), 0.0))
            g_last = gcb[CHUNK - 1:CHUNK, :]
            eg = jnp.exp(gcb)
            kb = k * beta_b
            ks.append(jnp.concatenate([k.astype(BF16), zero_rows], axis=0))
            lhs.append(jnp.concatenate([kb, q], axis=0).astype(BF16))
            rhs.append(jnp.concatenate([v * beta_b, kb * eg], axis=1).astype(BF16))
            qe.append((q * eg).astype(BF16))
            kdt.append((k * jnp.exp(g_last - gcb)).T.astype(BF16))
        yield

        kq = None
        for kq in halves(lambda ab: _dot_nt(*ab), list(zip(lhs, ks))):
            yield
        m = [jnp.where(strict, x[:CHUNK] * d, 0.0) for x, d in zip(kq, decay)]
        a_qk = [(x[CHUNK:, :CHUNK] * d[:, :CHUNK]).astype(BF16) for x, d in zip(kq, decay)]

        w = None
        for w in halves(lambda x: _dot(x[:, :CHUNK], x - eye_hi) + eye_hi, m):
            yield
        for _ in range(5):
            prev = w
            for w in halves(lambda x: _dot(x[:, :CHUNK], x) + jnp.where(upper_half, x, 0.0), prev):
                yield
        sol = None
        for sol in halves(lambda xb: _dot(xb[0][:, CHUNK:], xb[1]), list(zip(w, rhs))):
            yield

        for n, (cc, h) in enumerate(items):
            rows = pl.ds(base + cc * CHUNK, CHUNK)
            rows_w = pl.ds(2 * base + cc * 2 * CHUNK, CHUNK)
            rows_q = pl.ds(2 * base + cc * 2 * CHUNK + CHUNK, CHUNK)
            rows_t = pl.ds(2 * base + cc * 2 * CHUNK, 2 * CHUNK)
            cols = slice(h * LANES, (h + 1) * LANES)
            cols_c = slice(h * LANES, h * LANES + CHUNK)
            v_s[rows, cols] = sol[n][:, :DN_HEAD]
            wq_s[rows_w, cols] = sol[n][:, DN_HEAD:].astype(BF16)
            wq_s[rows_q, cols] = qe[n]
            aqk_s[rows, cols_c] = a_qk[n]
            kdt_s[rows_t, cols_c] = kdt[n]
        yield

    def rec(g):
        base = group_base(g)
        cols = [slice(h * LANES, (h + 1) * LANES) for h in heads]
        cols_c = [slice(h * LANES, h * LANES + CHUNK) for h in heads]
        ld = []
        for cc in range(PREP_CHUNKS):
            rows = pl.ds(base + cc * CHUNK, CHUNK)
            rows2 = pl.ds(2 * base + cc * 2 * CHUNK, 2 * CHUNK)
            ld.append(dict(
                rows=rows,
                gates=gate_s[rows, :],
                wq=[wq_s[rows2, cols[h]] for h in heads],
                u=[v_s[rows, cols[h]] for h in heads],
                kdt=[kdt_s[rows2, cols_c[h]] for h in heads],
                aqk=[aqk_s[rows, cols_c[h]] for h in heads]))
        yield
        for c in ld:
            st = [state_s[h] for h in heads]
            r = [jnp.dot(c["wq"][h], st[h].astype(BF16), preferred_element_type=F32)
                 for h in heads]
            yield
            v_new = [(c["u"][h] - r[h][:CHUNK]).astype(BF16) for h in heads]
            upd = [jnp.dot(c["kdt"][h], v_new[h], preferred_element_type=F32) for h in heads]
            for h in heads:
                g_last = c["gates"][CHUNK - 1:CHUNK, DN_HEADS + h:DN_HEADS + h + 1]
                state_s[h] = st[h] * jnp.exp(g_last) + upd[h]
            yield
            o = [r[h][CHUNK:] + jnp.dot(c["aqk"][h], v_new[h], preferred_element_type=F32)
                 for h in heads]
            for h in heads:
                y = o[h] * lax.rsqrt(jnp.mean(o[h] * o[h], axis=-1, keepdims=True) + EPS) * gain
                zf = z_ref[c["rows"], cols[h]].astype(F32)
                out_ref[c["rows"], cols[h]] = (y * _silu(zf)).astype(out_ref.dtype)
            yield

    def run_interleaved(*gens):
        gens = list(gens)
        while gens:
            for gen in list(gens):
                if next(gen, StopIteration) is StopIteration:
                    gens.remove(gen)

    n_groups = n_chunks // PREP_CHUNKS
    state_s[...] = jnp.zeros_like(state_s)
    run_interleaved(conv(0))
    run_interleaved(prep(0), conv(1))

    def group_body(g, carry):
        run_interleaved(rec(g - 1), prep(g), conv(g + 1))
        return carry

    lax.fori_loop(1, n_groups - 1, group_body, 0)
    run_interleaved(rec(n_groups - 2), prep(n_groups - 1))
    run_interleaved(rec(n_groups - 1))


def _gated_deltanet(proj, gates, conv_w, alog8, dtb8, out_gain):
    b, t, _ = proj.shape
    assert t % (PREP_CHUNKS * CHUNK) == 0 and t % DN_ROWS == 0
    return pl.pallas_call(
        _gdn_kernel,
        grid=(b,),
        in_specs=[
            pl.BlockSpec((None, t, QKV_COLS), lambda i: (i, 0, 0)),
            pl.BlockSpec((None, t, DN_WIDTH), lambda i: (i, 0, QKV_COLS // DN_WIDTH)),
            pl.BlockSpec((None, t, N_GATES), lambda i: (i, 0, 0)),
            pl.BlockSpec((CONV_WIDTH, QKV_COLS), lambda i: (0, 0)),
            pl.BlockSpec((1, N_GATES), lambda i: (0, 0)),
            pl.BlockSpec((1, N_GATES), lambda i: (0, 0)),
            pl.BlockSpec((1, DN_HEAD), lambda i: (0, 0)),
        ],
        out_specs=pl.BlockSpec((None, t, DN_WIDTH), lambda i: (i, 0, 0)),
        out_shape=jax.ShapeDtypeStruct((b, t, DN_WIDTH), BF16),
        scratch_shapes=[
            pltpu.VMEM((t, DN_WIDTH), F32),
            pltpu.VMEM((t, DN_WIDTH), F32),
            pltpu.VMEM((t, DN_WIDTH), F32),
            pltpu.VMEM((2 * t, DN_WIDTH), BF16),
            pltpu.VMEM((t, DN_WIDTH), BF16),
            pltpu.VMEM((2 * t, DN_WIDTH), BF16),
            pltpu.VMEM((t, N_GATES), F32),
            pltpu.VMEM((DN_HEADS, DN_HEAD, DN_HEAD), F32),
            pltpu.VMEM((QKV_COLS // LANES, PREP_CHUNKS * CHUNK + 8, LANES), F32),
        ],
        compiler_params=pltpu.CompilerParams(
            dimension_semantics=("parallel",), vmem_limit_bytes=VMEM_LIMIT),
        name="gated_deltanet",
    )(proj, proj, gates, conv_w, alog8, dtb8, out_gain)


AT_BQ = 512
AT_BK = 256
AT_CW = 256
AT_NQ = 2 * AT_BQ
AT_CHAINS = AT_NQ // AT_CW
AT_CPM = AT_BQ // AT_CW


def _attention_tiles(n_qblocks):
    all_chains = tuple(range(AT_CHAINS))
    lo = tuple(c for c in all_chains if c % AT_CPM == 0)
    hi = tuple(c for c in all_chains if c % AT_CPM == 1)
    tiles = []
    for qi in range(n_qblocks):
        for kt in range(AT_CPM * qi + AT_CPM):
            d = kt - AT_CPM * qi
            chains, masked = (all_chains, ()) if d < 0 else (all_chains, lo) if d == 0 else (hi, hi)
            tiles.append(dict(qi=qi, kt=kt, chains=chains, masked=masked, first=kt == 0))
    return tiles


def _diff_attn_kernel(q_ref, k_ref, vt_ref, z_ref, lq1_ref, lk1_ref, lq2_ref, lk2_ref,
                      og_ref, out_ref, qq_s, m_s, l_s, acc_s, s_a, s_b, c_a, c_b, p_a, p_b):
    t = q_ref.shape[0]
    n_qblocks = t // AT_BQ
    all_chains = tuple(range(AT_CHAINS))

    def lanes(c):
        return slice(c * AT_CW, (c + 1) * AT_CW)

    first = lax.broadcasted_iota(jnp.int32, (AT_BQ, LANES), 1) < DF_HEAD_QK
    zero = jnp.zeros((AT_BQ, LANES), BF16)
    for qi in range(n_qblocks):
        blk = q_ref[qi * AT_BQ:(qi + 1) * AT_BQ, :]
        qq_s[qi, :AT_BQ, :] = jnp.where(first, blk, zero)
        qq_s[qi, AT_BQ:, :] = jnp.where(first, zero, blk)

    lam = (jnp.exp(jnp.sum(lq1_ref[...] * lk1_ref[...], axis=-1, keepdims=True))
           - jnp.exp(jnp.sum(lq2_ref[...] * lk2_ref[...], axis=-1, keepdims=True))
           + LAMBDA_INIT)

    krow = lax.broadcasted_iota(jnp.int32, (AT_BK, AT_CW), 0)
    qcol = lax.broadcasted_iota(jnp.int32, (AT_BK, AT_CW), 1)
    causal = krow <= qcol
    bufs = ((s_a, c_a, p_a), (s_b, c_b, p_b))

    def scores(tile, c, buf):
        rows = slice(tile["kt"] * AT_BK, (tile["kt"] + 1) * AT_BK)
        s = _dot_nt(k_ref[rows, :], qq_s[tile["qi"], lanes(c), :])
        if c in tile["masked"]:
            s = jnp.where(causal, s, -jnp.inf)
        buf[0][:, lanes(c)] = s
        buf[1][:, lanes(c)] = jnp.max(s, axis=0, keepdims=True)

    def values(tile, c, buf):
        rows = slice(tile["kt"] * AT_BK, (tile["kt"] + 1) * AT_BK)
        return jnp.dot(vt_ref[:, rows], buf[2][:, lanes(c)], preferred_element_type=F32)

    def finish(qi):
        rows = slice(qi * AT_BQ, (qi + 1) * AT_BQ)
        o_all = acc_s[qi] * (1.0 / l_s[qi])
        o = o_all[:, :AT_BQ] - lam * o_all[:, AT_BQ:]
        y = (o * lax.rsqrt(jnp.mean(o * o, axis=0, keepdims=True) + EPS)).T
        y = y * og_ref[...] * (1.0 - LAMBDA_INIT)
        out_ref[rows, :] = (y * _silu(z_ref[rows, :].astype(F32))).astype(out_ref.dtype)

    tiles = _attention_tiles(n_qblocks)
    for c in tiles[0]["chains"]:
        scores(tiles[0], c, bufs[0])
    for n, cur in enumerate(tiles):
        nxt = tiles[n + 1] if n + 1 < len(tiles) else None
        prv = tiles[n - 1] if n > 0 else None
        s_cur, c_cur, p_cur = bufs[n % 2]
        qi = cur["qi"]
        for c in all_chains:
            alpha = None
            if c in cur["chains"]:
                if cur["first"]:
                    m_new = c_cur[:, lanes(c)]
                else:
                    m_old = m_s[qi, :, lanes(c)]
                    m_new = jnp.maximum(m_old, c_cur[:, lanes(c)])
                    alpha = jnp.exp2(m_old - m_new)
                p = jnp.exp2(s_cur[:, lanes(c)] - m_new)
                l_new = jnp.sum(p, axis=0, keepdims=True)
                if alpha is not None:
                    l_new = alpha * l_s[qi, :, lanes(c)] + l_new
                l_s[qi, :, lanes(c)] = l_new
                m_s[qi, :, lanes(c)] = m_new
                p_cur[:, lanes(c)] = p.astype(BF16)
            if nxt is not None and c in nxt["chains"]:
                scores(nxt, c, bufs[(n + 1) % 2])
            if prv is not None and c in prv["chains"]:
                pv = values(prv, c, bufs[(n - 1) % 2])
                pq = prv["qi"]
                acc = pv if prv["first"] else acc_s[pq, :, lanes(c)] + pv
                if pq == qi and alpha is not None:
                    acc = alpha * acc
                acc_s[pq, :, lanes(c)] = acc
        if prv is not None and prv["qi"] != qi:
            finish(prv["qi"])
    last = tiles[-1]
    for c in last["chains"]:
        acc_s[last["qi"], :, lanes(c)] = (acc_s[last["qi"], :, lanes(c)]
                                          + values(last, c, bufs[(len(tiles) - 1) % 2]))
    finish(last["qi"])


def _diff_attention(proj, vt, lq1, lk1, lq2, lk2, out_gain):
    b, t, _ = proj.shape
    assert AT_BQ == AT_CPM * AT_BK and AT_CW == AT_BK and AT_CPM == 2 and t % AT_BQ == 0
    n_qblocks = t // AT_BQ
    small = lambda n: pl.BlockSpec((1, n), lambda i, h: (0, 0))
    return pl.pallas_call(
        _diff_attn_kernel,
        grid=(b, DF_HEADS),
        in_specs=[
            pl.BlockSpec((None, t, LANES), lambda i, h: (i, 0, _BLK_DF_Q + h)),
            pl.BlockSpec((None, t, LANES), lambda i, h: (i, 0, _BLK_DF_K + h)),
            pl.BlockSpec((None, DF_HEAD_V, t), lambda i, h: (i, h, 0)),
            pl.BlockSpec((None, t, LANES), lambda i, h: (i, 0, _BLK_DF_Z + h)),
            small(DF_HEAD_QK), small(DF_HEAD_QK), small(DF_HEAD_QK), small(DF_HEAD_QK),
            small(DF_HEAD_V),
        ],
        out_specs=pl.BlockSpec((None, t, LANES), lambda i, h: (i, 0, h)),
        out_shape=jax.ShapeDtypeStruct((b, t, DF_WIDTH), BF16),
        scratch_shapes=[
            pltpu.VMEM((n_qblocks, AT_NQ, LANES), BF16),
            pltpu.VMEM((n_qblocks, 1, AT_NQ), F32),
            pltpu.VMEM((n_qblocks, 1, AT_NQ), F32),
            pltpu.VMEM((n_qblocks, DF_HEAD_V, AT_NQ), F32),
            pltpu.VMEM((AT_BK, AT_NQ), F32),
            pltpu.VMEM((AT_BK, AT_NQ), F32),
            pltpu.VMEM((1, AT_NQ), F32),
            pltpu.VMEM((1, AT_NQ), F32),
            pltpu.VMEM((AT_BK, AT_NQ), BF16),
            pltpu.VMEM((AT_BK, AT_NQ), BF16),
        ],
        compiler_params=pltpu.CompilerParams(
            dimension_semantics=("parallel", "parallel"), vmem_limit_bytes=VMEM_LIMIT),
        name="diff_attention",
    )(proj, proj, vt, proj, lq1, lk1, lq2, lk2, out_gain)


OUT_TM = 512


def _outproj_kernel(x_ref, dn_ref, df_ref, w_ref, out_ref):
    acc = jnp.dot(dn_ref[...], w_ref[:DN_WIDTH, :], preferred_element_type=F32)
    acc = acc + jnp.dot(df_ref[...], w_ref[DN_WIDTH:, :], preferred_element_type=F32)
    out_ref[...] = x_ref[...] + acc


def _out_projection(x2d, mixed_dn, mixed_df, w_out):
    m = x2d.shape[0]
    return pl.pallas_call(
        _outproj_kernel,
        grid=(m // OUT_TM,),
        in_specs=[
            pl.BlockSpec((OUT_TM, D_MODEL), lambda i: (i, 0)),
            pl.BlockSpec((OUT_TM, DN_WIDTH), lambda i: (i, 0)),
            pl.BlockSpec((OUT_TM, DF_WIDTH), lambda i: (i, 0)),
            pl.BlockSpec((DN_WIDTH + DF_WIDTH, D_MODEL), lambda i: (0, 0)),
        ],
        out_specs=pl.BlockSpec((OUT_TM, D_MODEL), lambda i: (i, 0)),
        out_shape=jax.ShapeDtypeStruct((m, D_MODEL), F32),
        compiler_params=pltpu.CompilerParams(
            dimension_semantics=("parallel",), vmem_limit_bytes=VMEM_LIMIT),
        name="out_projection",
    )(x2d, mixed_dn, mixed_df, w_out)


def kernel(x, norm_gain, w_in, conv_w, a_log, dt_bias, dn_out_gain, q_gain, k_gain,
           lambda_q1, lambda_k1, lambda_q2, lambda_k2, df_out_gain, w_out):
    b, t, d = x.shape
    assert d == D_MODEL and norm_gain.shape[0] == 1
    x2d = x.reshape(b * t, d)

    w = w_in[0]
    gate_lo = 4 * DN_WIDTH
    df_lo = gate_lo + N_GATES
    w_main = jnp.concatenate([w[:, :gate_lo], w[:, df_lo:df_lo + 2 * DF_WIDTH],
                              w[:, df_lo + 3 * DF_WIDTH:]], axis=1).astype(BF16)
    w_vt = w[:, df_lo + 2 * DF_WIDTH:df_lo + 3 * DF_WIDTH].T.astype(BF16)
    w_gate = jnp.pad(w[:, gate_lo:df_lo], ((0, 0), (0, LANES - N_GATES))).astype(BF16)
    half = jnp.arange(SEG_W, dtype=jnp.int32) // DF_HEAD_QK
    seg = (half[:, None] == half[None, :]).astype(BF16)
    reps = DF_WIDTH // DF_HEAD_QK
    qk_gain = jnp.concatenate([jnp.tile(q_gain, (1, reps)) * (DF_HEAD_QK ** -0.5 * LOG2_E),
                               jnp.tile(k_gain, (1, reps))], axis=0)

    proj, vt, gates = _in_projection(x2d, t, norm_gain, w_main, w_vt, w_gate, seg, qk_gain)
    proj = proj.reshape(b, t, PROJ_COLS)
    gates = gates.reshape(b, t, N_GATES)

    zeros4 = jnp.zeros((1, DN_HEADS), F32)
    alog8 = jnp.concatenate([zeros4, a_log], axis=1)
    dtb8 = jnp.concatenate([zeros4, dt_bias], axis=1)
    mixed_dn = _gated_deltanet(proj, gates, conv_w[0], alog8, dtb8, dn_out_gain)

    mixed_df = _diff_attention(proj, vt, lambda_q1, lambda_k1, lambda_q2, lambda_k2,
                               df_out_gain)

    out = _out_projection(x2d, mixed_dn.reshape(b * t, DN_WIDTH),
                          mixed_df.reshape(b * t, DF_WIDTH), w_out[0].astype(BF16))
    return out.reshape(b, t, d)
```

```python
import math

import jax
import jax.numpy as jnp
from jax import lax
from jax.experimental import pallas as pl
from jax.experimental.pallas import tpu as pltpu

F32 = jnp.float32
BF16 = jnp.bfloat16

D_MODEL = 1024
DN_HEADS = 4
DN_HEAD = 128
DN_WIDTH = DN_HEADS * DN_HEAD
CONV_WIDTH = 4
CHUNK = 64
DF_HEADS = 4
DF_HEAD_QK = 64
DF_HEAD_V = 128
DF_WIDTH = DF_HEADS * DF_HEAD_V
EPS = 1e-6
LAMBDA_INIT = 0.8 - 0.6 * math.exp(-0.3 * 0)
LOG2_E = math.log2(math.e)

N_GATES = 2 * DN_HEADS
PROJ_COLS = 4 * DN_WIDTH + 3 * DF_WIDTH
LANES = 128
VMEM_LIMIT = 56 * 1024 * 1024

_BLK_DN_Z = 3 * DN_HEADS
_BLK_DF_Q = 4 * DN_HEADS
_BLK_DF_K = _BLK_DF_Q + DF_HEADS
_BLK_DF_Z = _BLK_DF_K + DF_HEADS


def _silu(x):
    return x * jax.nn.sigmoid(x)


def _dot(a, b):
    return jnp.dot(a.astype(BF16), b.astype(BF16), preferred_element_type=F32)


def _dot_nt(a, b):
    return lax.dot_general(a.astype(BF16), b.astype(BF16), (((1,), (1,)), ((), ())),
                           preferred_element_type=F32)


IN_TM = 512
IN_TN = 512
SEG_W = 256


def _inproj_kernel(x_ref, gain_ref, w_ref, wvt_ref, wg_ref, seg_ref, qkg_ref,
                   proj_ref, vt_ref, gate_ref):
    x = x_ref[...]
    ms = jnp.mean(x * x, axis=-1, keepdims=True)
    h = (x * lax.rsqrt(ms + EPS) * gain_ref[...]).astype(BF16)
    for j in range(PROJ_COLS // IN_TN):
        cols = slice(j * IN_TN, (j + 1) * IN_TN)
        y = jnp.dot(h, w_ref[:, cols], preferred_element_type=F32)
        if j in (_BLK_DF_Q * LANES // IN_TN, _BLK_DF_K * LANES // IN_TN):
            which = 0 if j == _BLK_DF_Q * LANES // IN_TN else 1
            sq = (y * y).astype(BF16)
            ssq = jnp.concatenate(
                [jnp.dot(sq[:, i * SEG_W:(i + 1) * SEG_W], seg_ref[...],
                         preferred_element_type=F32) for i in range(IN_TN // SEG_W)], axis=1)
            y = y * lax.rsqrt(ssq * (1.0 / DF_HEAD_QK) + EPS) * qkg_ref[which:which + 1, :]
        elif j in (_BLK_DN_Z * LANES // IN_TN, _BLK_DF_Z * LANES // IN_TN):
            y = _silu(y)
        proj_ref[:, cols] = y.astype(BF16)
    vt_ref[...] = _dot_nt(wvt_ref[...], h).astype(BF16)
    gate_ref[...] = jnp.dot(h, wg_ref[...], preferred_element_type=F32)[:, :N_GATES]


def _in_projection(x2d, t, gain, w_main, w_vt, w_gate, seg, qk_gain):
    m = x2d.shape[0]
    assert DF_WIDTH == IN_TN and t % IN_TM == 0
    tiles_per_seq = t // IN_TM
    const = lambda shape: pl.BlockSpec(shape, lambda i: (0, 0))
    return pl.pallas_call(
        _inproj_kernel,
        grid=(m // IN_TM,),
        in_specs=[
            pl.BlockSpec((IN_TM, D_MODEL), lambda i: (i, 0)),
            const((1, D_MODEL)),
            const((D_MODEL, PROJ_COLS)),
            const((DF_WIDTH, D_MODEL)),
            const((D_MODEL, LANES)),
            const((SEG_W, SEG_W)),
            const((2, DF_WIDTH)),
        ],
        out_specs=[
            pl.BlockSpec((IN_TM, PROJ_COLS), lambda i: (i, 0)),
            pl.BlockSpec((None, DF_WIDTH, IN_TM),
                         lambda i: (i // tiles_per_seq, 0, i % tiles_per_seq)),
            pl.BlockSpec((IN_TM, N_GATES), lambda i: (i, 0)),
        ],
        out_shape=[
            jax.ShapeDtypeStruct((m, PROJ_COLS), BF16),
            jax.ShapeDtypeStruct((m // t, DF_WIDTH, t), BF16),
            jax.ShapeDtypeStruct((m, N_GATES), F32),
        ],
        compiler_params=pltpu.CompilerParams(
            dimension_semantics=("parallel",), vmem_limit_bytes=VMEM_LIMIT),
        name="in_projection",
    )(x2d, gain, w_main, w_vt, w_gate, seg, qk_gain)


DN_ROWS = 256
QKV_COLS = 3 * DN_WIDTH
PREP_CHUNKS = 4


def _gdn_kernel(qkv_ref, zg_ref, gate_ref, convw_ref, alog_ref, dtb_ref, gain_ref,
                out_ref, q_s, k_s, v_s, wq_s, aqk_s, kdt_s, gate_s, state_s, stage_s):
    t = qkv_ref.shape[0]
    n_chunks = t // CHUNK

    group_rows = PREP_CHUNKS * CHUNK

    def group_base(g):
        return g * group_rows if isinstance(g, int) else pl.multiple_of(g * group_rows, group_rows)

    def conv(g):
        r0 = group_base(g)
        for cg in range(QKV_COLS // LANES):
            cols = slice(cg * LANES, (cg + 1) * LANES)
            cur = qkv_ref[pl.ds(r0, group_rows), cols].astype(F32)
            if isinstance(g, int) and g == 0:
                prev = jnp.zeros((8, LANES), F32)
            else:
                prev = qkv_ref[pl.ds(pl.multiple_of(r0 - 16, 16), 16), cols].astype(F32)[8:]
            stage_s[cg, :8, :] = prev
            stage_s[cg, 8:, :] = cur
            w = convw_ref[:, cols]
            acc = cur * w[CONV_WIDTH - 1:CONV_WIDTH]
            for j in range(1, CONV_WIDTH):
                acc = acc + (stage_s[cg, 8 - j:8 - j + group_rows, :]
                             * w[CONV_WIDTH - 1 - j:CONV_WIDTH - j])
            y = _silu(acc)
            hcols = slice((cg % DN_HEADS) * LANES, (cg % DN_HEADS + 1) * LANES)
            if cg < 2 * DN_HEADS:
                inv_norm = lax.rsqrt(jnp.sum(y * y, axis=-1, keepdims=True) + EPS)
                if cg < DN_HEADS:
                    q_s[pl.ds(r0, group_rows), hcols] = y * (inv_norm * (DN_HEAD ** -0.5))
                else:
                    k_s[pl.ds(r0, group_rows), hcols] = y * inv_norm
            else:
                v_s[pl.ds(r0, group_rows), hcols] = y
            yield

    gate = gate_ref[...]
    beta = jax.nn.sigmoid(gate)
    g = -jnp.exp(alog_ref[...]) * jax.nn.softplus(gate + dtb_ref[...])
    pos = lax.broadcasted_iota(jnp.int32, g.shape, 0) % CHUNK
    s = 1
    while s < CHUNK:
        g = g + jnp.where(pos >= s, pltpu.roll(g, s, 0), 0.0)
        s *= 2
    lane = lax.broadcasted_iota(jnp.int32, g.shape, 1)
    gate_s[...] = jnp.where(lane < DN_HEADS, beta, g)

    row = lax.broadcasted_iota(jnp.int32, (CHUNK, 2 * CHUNK), 0)
    col = lax.broadcasted_iota(jnp.int32, (CHUNK, 2 * CHUNK), 1)
    eye = row == col
    lower = row >= col
    strict = row > col
    upper_half = col >= CHUNK
    eye_hi = (col - CHUNK == row).astype(F32)
    zero_rows = jnp.zeros((CHUNK, DN_HEAD), BF16)

    items = [(cc, h) for cc in range(PREP_CHUNKS) for h in range(DN_HEADS)]
    heads = range(DN_HEADS)
    gain = gain_ref[...]

    def halves(fn, seq):
        out = []
        mid = len(seq) // 2
        for part in (seq[:mid], seq[mid:]):
            out.extend(fn(x) for x in part)
            yield out

    def prep(g):
        base = group_base(g)
        gates = [gate_s[pl.ds(base + cc * CHUNK, CHUNK), :] for cc in range(PREP_CHUNKS)]
        ks, lhs, decay, rhs, qe, kdt = [], [], [], [], [], []
        for cc, h in items:
            rows = pl.ds(base + cc * CHUNK, CHUNK)
            cols = slice(h * LANES, (h + 1) * LANES)
            q = q_s[rows, cols]
            k = k_s[rows, cols]
            v = v_s[rows, cols]
            beta_b = jnp.broadcast_to(gates[cc][:, h:h + 1], (CHUNK, DN_HEAD))
            gcb = jnp.broadcast_to(gates[cc][:, DN_HEADS + h:DN_HEADS + h + 1], (CHUNK, DN_HEAD))
            gr = jnp.sum(jnp.where(eye, gcb, 0.0), axis=0, keepdims=True)
            decay.append(jnp.where(lower, jnp.exp(jnp.where(lower, gcb - gr, 0.0)), 0.0))
            g_last = gcb[CHUNK - 1:CHUNK, :]
            eg = jnp.exp(gcb)
            kb = k * beta_b
            ks.append(jnp.concatenate([k.astype(BF16), zero_rows], axis=0))
            lhs.append(jnp.concatenate([kb, q], axis=0).astype(BF16))
            rhs.append(jnp.concatenate([v * beta_b, kb * eg], axis=1).astype(BF16))
            qe.append((q * eg).astype(BF16))
            kdt.append((k * jnp.exp(g_last - gcb)).T.astype(BF16))
        yield

        kq = None
        for kq in halves(lambda ab: _dot_nt(*ab), list(zip(lhs, ks))):
            yield
        m = [jnp.where(strict, x[:CHUNK] * d, 0.0) for x, d in zip(kq, decay)]
        a_qk = [(x[CHUNK:, :CHUNK] * d[:, :CHUNK]).astype(BF16) for x, d in zip(kq, decay)]

        w = None
        for w in halves(lambda x: _dot(x[:, :CHUNK], x - eye_hi) + eye_hi, m):
            yield
        for _ in range(5):
            prev = w
            for w in halves(lambda x: _dot(x[:, :CHUNK], x) + jnp.where(upper_half, x, 0.0), prev):
                yield
        sol = None
        for sol in halves(lambda xb: _dot(xb[0][:, CHUNK:], xb[1]), list(zip(w, rhs))):
            yield

        for n, (cc, h) in enumerate(items):
            rows = pl.ds(base + cc * CHUNK, CHUNK)
            rows_w = pl.ds(2 * base + cc * 2 * CHUNK, CHUNK)
            rows_q = pl.ds(2 * base + cc * 2 * CHUNK + CHUNK, CHUNK)
            rows_t = pl.ds(2 * base + cc * 2 * CHUNK, 2 * CHUNK)
            cols = slice(h * LANES, (h + 1) * LANES)
            cols_c = slice(h * LANES, h * LANES + CHUNK)
            v_s[rows, cols] = sol[n][:, :DN_HEAD]
            wq_s[rows_w, cols] = sol[n][:, DN_HEAD:].astype(BF16)
            wq_s[rows_q, cols] = qe[n]
            aqk_s[rows, cols_c] = a_qk[n]
            kdt_s[rows_t, cols_c] = kdt[n]
        yield

    def rec(g):
        base = group_base(g)
        cols = [slice(h * LANES, (h + 1) * LANES) for h in heads]
        cols_c = [slice(h * LANES, h * LANES + CHUNK) for h in heads]
        ld = []
        for cc in range(PREP_CHUNKS):
            rows = pl.ds(base + cc * CHUNK, CHUNK)
            rows2 = pl.ds(2 * base + cc * 2 * CHUNK, 2 * CHUNK)
            ld.append(dict(
                rows=rows,
                gates=gate_s[rows, :],
                wq=[wq_s[rows2, cols[h]] for h in heads],
                u=[v_s[rows, cols[h]] for h in heads],
                kdt=[kdt_s[rows2, cols_c[h]] for h in heads],
                aqk=[aqk_s[rows, cols_c[h]] for h in heads]))
        yield
        for c in ld:
            st = [state_s[h] for h in heads]
            r = [jnp.dot(c["wq"][h], st[h].astype(BF16), preferred_element_type=F32)
                 for h in heads]
            yield
            v_new = [(c["u"][h] - r[h][:CHUNK]).astype(BF16) for h in heads]
            upd = [jnp.dot(c["kdt"][h], v_new[h], preferred_element_type=F32) for h in heads]
            for h in heads:
                g_last = c["gates"][CHUNK - 1:CHUNK, DN_HEADS + h:DN_HEADS + h + 1]
                state_s[h] = st[h] * jnp.exp(g_last) + upd[h]
            yield
            o = [r[h][CHUNK:] + jnp.dot(c["aqk"][h], v_new[h], preferred_element_type=F32)
                 for h in heads]
            for h in heads:
                y = o[h] * lax.rsqrt(jnp.mean(o[h] * o[h], axis=-1, keepdims=True) + EPS) * gain
                gate = zg_ref[c["rows"], cols[h]].astype(F32)
                out_ref[c["rows"], cols[h]] = (y * gate).astype(out_ref.dtype)
            yield

    def run_interleaved(*gens):
        gens = list(gens)
        while gens:
            for gen in list(gens):
                if next(gen, StopIteration) is StopIteration:
                    gens.remove(gen)

    n_groups = n_chunks // PREP_CHUNKS
    state_s[...] = jnp.zeros_like(state_s)
    run_interleaved(conv(0))
    run_interleaved(prep(0), conv(1))

    def group_body(g, carry):
        run_interleaved(rec(g - 1), prep(g), conv(g + 1))
        return carry

    lax.fori_loop(1, n_groups - 1, group_body, 0)
    run_interleaved(rec(n_groups - 2), prep(n_groups - 1))
    run_interleaved(rec(n_groups - 1))


def _gated_deltanet(proj, gates, conv_w, alog8, dtb8, out_gain):
    b, t, _ = proj.shape
    assert t % (PREP_CHUNKS * CHUNK) == 0 and t % DN_ROWS == 0
    return pl.pallas_call(
        _gdn_kernel,
        grid=(b,),
        in_specs=[
            pl.BlockSpec((None, t, QKV_COLS), lambda i: (i, 0, 0)),
            pl.BlockSpec((None, t, DN_WIDTH), lambda i: (i, 0, QKV_COLS // DN_WIDTH)),
            pl.BlockSpec((None, t, N_GATES), lambda i: (i, 0, 0)),
            pl.BlockSpec((CONV_WIDTH, QKV_COLS), lambda i: (0, 0)),
            pl.BlockSpec((1, N_GATES), lambda i: (0, 0)),
            pl.BlockSpec((1, N_GATES), lambda i: (0, 0)),
            pl.BlockSpec((1, DN_HEAD), lambda i: (0, 0)),
        ],
        out_specs=pl.BlockSpec((None, t, DN_WIDTH), lambda i: (i, 0, 0)),
        out_shape=jax.ShapeDtypeStruct((b, t, DN_WIDTH), BF16),
        scratch_shapes=[
            pltpu.VMEM((t, DN_WIDTH), F32),
            pltpu.VMEM((t, DN_WIDTH), F32),
            pltpu.VMEM((t, DN_WIDTH), F32),
            pltpu.VMEM((2 * t, DN_WIDTH), BF16),
            pltpu.VMEM((t, DN_WIDTH), BF16),
            pltpu.VMEM((2 * t, DN_WIDTH), BF16),
            pltpu.VMEM((t, N_GATES), F32),
            pltpu.VMEM((DN_HEADS, DN_HEAD, DN_HEAD), F32),
            pltpu.VMEM((QKV_COLS // LANES, PREP_CHUNKS * CHUNK + 8, LANES), F32),
        ],
        compiler_params=pltpu.CompilerParams(
            dimension_semantics=("parallel",), vmem_limit_bytes=VMEM_LIMIT),
        name="gated_deltanet",
    )(proj, proj, gates, conv_w, alog8, dtb8, out_gain)


AT_BQ = 512
AT_BK = 256
AT_CW = 256
AT_NQ = 2 * AT_BQ
AT_CHAINS = AT_NQ // AT_CW
AT_CPM = AT_BQ // AT_CW
AT_VROWS = DF_HEAD_V + 16


def _attention_tiles(n_qblocks):
    all_chains = tuple(range(AT_CHAINS))
    lo = tuple(c for c in all_chains if c % AT_CPM == 0)
    hi = tuple(c for c in all_chains if c % AT_CPM == 1)
    tiles = []
    for qi in range(n_qblocks):
        for kt in range(AT_CPM * qi + AT_CPM):
            d = kt - AT_CPM * qi
            chains, masked = (all_chains, ()) if d < 0 else (all_chains, lo) if d == 0 else (hi, hi)
            tiles.append(dict(qi=qi, kt=kt, chains=chains, masked=masked, first=kt == 0))
    return tiles


def _diff_attn_kernel(q_ref, k_ref, vt_ref, zg_ref, lq1_ref, lk1_ref, lq2_ref, lk2_ref,
                      og_ref, out_ref, qq_s, vta_s, m_s, acc_s, s_a, s_b, c_a, c_b, p_a, p_b):
    t = q_ref.shape[0]
    n_qblocks = t // AT_BQ
    all_chains = tuple(range(AT_CHAINS))

    def lanes(c):
        return slice(c * AT_CW, (c + 1) * AT_CW)

    first = lax.broadcasted_iota(jnp.int32, (AT_BQ, LANES), 1) < DF_HEAD_QK
    zero = jnp.zeros((AT_BQ, LANES), BF16)
    for qi in range(n_qblocks):
        blk = q_ref[qi * AT_BQ:(qi + 1) * AT_BQ, :]
        qq_s[qi, :AT_BQ, :] = jnp.where(first, blk, zero)
        qq_s[qi, AT_BQ:, :] = jnp.where(first, zero, blk)

    vta_s[:DF_HEAD_V, :] = vt_ref[...]
    vta_s[DF_HEAD_V:, :] = jnp.ones((AT_VROWS - DF_HEAD_V, t), BF16)

    lam = (jnp.exp(jnp.sum(lq1_ref[...] * lk1_ref[...], axis=-1, keepdims=True))
           - jnp.exp(jnp.sum(lq2_ref[...] * lk2_ref[...], axis=-1, keepdims=True))
           + LAMBDA_INIT)

    krow = lax.broadcasted_iota(jnp.int32, (AT_BK, AT_CW), 0)
    qcol = lax.broadcasted_iota(jnp.int32, (AT_BK, AT_CW), 1)
    causal = krow <= qcol
    bufs = ((s_a, c_a, p_a), (s_b, c_b, p_b))

    def scores(tile, c, buf):
        rows = slice(tile["kt"] * AT_BK, (tile["kt"] + 1) * AT_BK)
        s = _dot_nt(k_ref[rows, :], qq_s[tile["qi"], lanes(c), :])
        if c in tile["masked"]:
            s = jnp.where(causal, s, -jnp.inf)
        buf[0][:, lanes(c)] = s
        buf[1][:, lanes(c)] = jnp.max(s, axis=0, keepdims=True)

    def values(tile, c, buf):
        rows = slice(tile["kt"] * AT_BK, (tile["kt"] + 1) * AT_BK)
        return jnp.dot(vta_s[:, rows], buf[2][:, lanes(c)], preferred_element_type=F32)

    def finish(qi):
        rows = slice(qi * AT_BQ, (qi + 1) * AT_BQ)
        denom = acc_s[qi, DF_HEAD_V:DF_HEAD_V + 1, :]
        o_all = acc_s[qi, :DF_HEAD_V, :] * (1.0 / denom)
        o = o_all[:, :AT_BQ] - lam * o_all[:, AT_BQ:]
        y = (o * lax.rsqrt(jnp.mean(o * o, axis=0, keepdims=True) + EPS)).T
        y = y * og_ref[...] * (1.0 - LAMBDA_INIT)
        out_ref[rows, :] = (y * zg_ref[rows, :].astype(F32)).astype(out_ref.dtype)

    tiles = _attention_tiles(n_qblocks)
    for c in tiles[0]["chains"]:
        scores(tiles[0], c, bufs[0])
    for n, cur in enumerate(tiles):
        nxt = tiles[n + 1] if n + 1 < len(tiles) else None
        prv = tiles[n - 1] if n > 0 else None
        s_cur, c_cur, p_cur = bufs[n % 2]
        qi = cur["qi"]
        for c in all_chains:
            alpha = None
            if c in cur["chains"]:
                if cur["first"]:
                    m_new = c_cur[:, lanes(c)]
                else:
                    m_old = m_s[qi, :, lanes(c)]
                    m_new = jnp.maximum(m_old, c_cur[:, lanes(c)])
                    alpha = jnp.exp2(m_old - m_new)
                p = jnp.exp2(s_cur[:, lanes(c)] - m_new)
                m_s[qi, :, lanes(c)] = m_new
                p_cur[:, lanes(c)] = p.astype(BF16)
            if nxt is not None and c in nxt["chains"]:
                scores(nxt, c, bufs[(n + 1) % 2])
            if prv is not None and c in prv["chains"]:
                pv = values(prv, c, bufs[(n - 1) % 2])
                pq = prv["qi"]
                acc = pv if prv["first"] else acc_s[pq, :, lanes(c)] + pv
                if pq == qi and alpha is not None:
                    acc = alpha * acc
                acc_s[pq, :, lanes(c)] = acc
        if prv is not None and prv["qi"] != qi:
            finish(prv["qi"])
    last = tiles[-1]
    for c in last["chains"]:
        acc_s[last["qi"], :, lanes(c)] = (acc_s[last["qi"], :, lanes(c)]
                                          + values(last, c, bufs[(len(tiles) - 1) % 2]))
    finish(last["qi"])


def _diff_attention(proj, vt, lq1, lk1, lq2, lk2, out_gain):
    b, t, _ = proj.shape
    assert AT_BQ == AT_CPM * AT_BK and AT_CW == AT_BK and AT_CPM == 2 and t % AT_BQ == 0
    n_qblocks = t // AT_BQ
    small = lambda n: pl.BlockSpec((1, n), lambda i, h: (0, 0))
    return pl.pallas_call(
        _diff_attn_kernel,
        grid=(b, DF_HEADS),
        in_specs=[
            pl.BlockSpec((None, t, LANES), lambda i, h: (i, 0, _BLK_DF_Q + h)),
            pl.BlockSpec((None, t, LANES), lambda i, h: (i, 0, _BLK_DF_K + h)),
            pl.BlockSpec((None, DF_HEAD_V, t), lambda i, h: (i, h, 0)),
            pl.BlockSpec((None, t, LANES), lambda i, h: (i, 0, _BLK_DF_Z + h)),
            small(DF_HEAD_QK), small(DF_HEAD_QK), small(DF_HEAD_QK), small(DF_HEAD_QK),
            small(DF_HEAD_V),
        ],
        out_specs=pl.BlockSpec((None, t, LANES), lambda i, h: (i, 0, h)),
        out_shape=jax.ShapeDtypeStruct((b, t, DF_WIDTH), BF16),
        scratch_shapes=[
            pltpu.VMEM((n_qblocks, AT_NQ, LANES), BF16),
            pltpu.VMEM((AT_VROWS, t), BF16),
            pltpu.VMEM((n_qblocks, 1, AT_NQ), F32),
            pltpu.VMEM((n_qblocks, AT_VROWS, AT_NQ), F32),
            pltpu.VMEM((AT_BK, AT_NQ), F32),
            pltpu.VMEM((AT_BK, AT_NQ), F32),
            pltpu.VMEM((1, AT_NQ), F32),
            pltpu.VMEM((1, AT_NQ), F32),
            pltpu.VMEM((AT_BK, AT_NQ), BF16),
            pltpu.VMEM((AT_BK, AT_NQ), BF16),
        ],
        compiler_params=pltpu.CompilerParams(
            dimension_semantics=("parallel", "parallel"), vmem_limit_bytes=VMEM_LIMIT),
        name="diff_attention",
    )(proj, proj, vt, proj, lq1, lk1, lq2, lk2, out_gain)


OUT_TM = 512


def _outproj_kernel(x_ref, dn_ref, df_ref, w_ref, out_ref):
    acc = jnp.dot(dn_ref[...], w_ref[:DN_WIDTH, :], preferred_element_type=F32)
    acc = acc + jnp.dot(df_ref[...], w_ref[DN_WIDTH:, :], preferred_element_type=F32)
    out_ref[...] = x_ref[...] + acc


def _out_projection(x2d, mixed_dn, mixed_df, w_out):
    m = x2d.shape[0]
    return pl.pallas_call(
        _outproj_kernel,
        grid=(m // OUT_TM,),
        in_specs=[
            pl.BlockSpec((OUT_TM, D_MODEL), lambda i: (i, 0)),
            pl.BlockSpec((OUT_TM, DN_WIDTH), lambda i: (i, 0)),
            pl.BlockSpec((OUT_TM, DF_WIDTH), lambda i: (i, 0)),
            pl.BlockSpec((DN_WIDTH + DF_WIDTH, D_MODEL), lambda i: (0, 0)),
        ],
        out_specs=pl.BlockSpec((OUT_TM, D_MODEL), lambda i: (i, 0)),
        out_shape=jax.ShapeDtypeStruct((m, D_MODEL), F32),
        compiler_params=pltpu.CompilerParams(
            dimension_semantics=("parallel",), vmem_limit_bytes=VMEM_LIMIT),
        name="out_projection",
    )(x2d, mixed_dn, mixed_df, w_out)


def kernel(x, norm_gain, w_in, conv_w, a_log, dt_bias, dn_out_gain, q_gain, k_gain,
           lambda_q1, lambda_k1, lambda_q2, lambda_k2, df_out_gain, w_out):
    b, t, d = x.shape
    assert d == D_MODEL and norm_gain.shape[0] == 1
    x2d = x.reshape(b * t, d)

    w = w_in[0]
    gate_lo = 4 * DN_WIDTH
    df_lo = gate_lo + N_GATES
    w_main = jnp.concatenate([w[:, :gate_lo], w[:, df_lo:df_lo + 2 * DF_WIDTH],
                              w[:, df_lo + 3 * DF_WIDTH:]], axis=1).astype(BF16)
    w_vt = w[:, df_lo + 2 * DF_WIDTH:df_lo + 3 * DF_WIDTH].T.astype(BF16)
    w_gate = jnp.pad(w[:, gate_lo:df_lo], ((0, 0), (0, LANES - N_GATES))).astype(BF16)
    half = jnp.arange(SEG_W, dtype=jnp.int32) // DF_HEAD_QK
    seg = (half[:, None] == half[None, :]).astype(BF16)
    reps = DF_WIDTH // DF_HEAD_QK
    qk_gain = jnp.concatenate([jnp.tile(q_gain, (1, reps)) * (DF_HEAD_QK ** -0.5 * LOG2_E),
                               jnp.tile(k_gain, (1, reps))], axis=0)

    proj, vt, gates = _in_projection(x2d, t, norm_gain, w_main, w_vt, w_gate, seg, qk_gain)
    proj = proj.reshape(b, t, PROJ_COLS)
    gates = gates.reshape(b, t, N_GATES)

    zeros4 = jnp.zeros((1, DN_HEADS), F32)
    alog8 = jnp.concatenate([zeros4, a_log], axis=1)
    dtb8 = jnp.concatenate([zeros4, dt_bias], axis=1)
    mixed_dn = _gated_deltanet(proj, gates, conv_w[0], alog8, dtb8, dn_out_gain)

    mixed_df = _diff_attention(proj, vt, lambda_q1, lambda_k1, lambda_q2, lambda_k2,
                               df_out_gain)

    out = _out_projection(x2d, mixed_dn.reshape(b * t, DN_WIDTH),
                          mixed_df.reshape(b * t, DF_WIDTH), w_out[0].astype(BF16))
    return out.reshape(b, t, d)
```

```python
import math

import jax
import jax.numpy as jnp
from jax import lax
from jax.experimental import pallas as pl
from jax.experimental.pallas import tpu as pltpu

F32 = jnp.float32
BF16 = jnp.bfloat16

D_MODEL = 1024
DN_HEADS = 4
DN_HEAD = 128
DN_WIDTH = DN_HEADS * DN_HEAD
CONV_WIDTH = 4
CHUNK = 64
DF_HEADS = 4
DF_HEAD_QK = 64
DF_HEAD_V = 128
DF_WIDTH = DF_HEADS * DF_HEAD_V
EPS = 1e-6
LAMBDA_INIT = 0.8 - 0.6 * math.exp(-0.3 * 0)
LOG2_E = math.log2(math.e)

N_GATES = 2 * DN_HEADS
GATE_ROWS = 16
PROJ_COLS = 4 * DN_WIDTH + 3 * DF_WIDTH
LANES = 128
VMEM_LIMIT = 56 * 1024 * 1024

_BLK_DN_Z = 3 * DN_HEADS
_BLK_DF_Q = 4 * DN_HEADS
_BLK_DF_K = _BLK_DF_Q + DF_HEADS
_BLK_DF_Z = _BLK_DF_K + DF_HEADS


def _silu(x):
    return x * jax.nn.sigmoid(x)


def _dot(a, b):
    return jnp.dot(a.astype(BF16), b.astype(BF16), preferred_element_type=F32)


def _dot_nt(a, b):
    return lax.dot_general(a.astype(BF16), b.astype(BF16), (((1,), (1,)), ((), ())),
                           preferred_element_type=F32)


IN_TM = 512
IN_TN = 512
SEG_W = 256


def _inproj_kernel(x_ref, gain_ref, w_ref, wt_ref, seg_ref, qkg_ref,
                   proj_ref, vt_ref, gate_ref):
    x = x_ref[...]
    ms = jnp.mean(x * x, axis=-1, keepdims=True)
    h = (x * lax.rsqrt(ms + EPS) * gain_ref[...]).astype(BF16)
    for j in range(PROJ_COLS // IN_TN):
        cols = slice(j * IN_TN, (j + 1) * IN_TN)
        y = jnp.dot(h, w_ref[:, cols], preferred_element_type=F32)
        if j in (_BLK_DF_Q * LANES // IN_TN, _BLK_DF_K * LANES // IN_TN):
            which = 0 if j == _BLK_DF_Q * LANES // IN_TN else 1
            sq = (y * y).astype(BF16)
            ssq = jnp.concatenate(
                [jnp.dot(sq[:, i * SEG_W:(i + 1) * SEG_W], seg_ref[...],
                         preferred_element_type=F32) for i in range(IN_TN // SEG_W)], axis=1)
            y = y * lax.rsqrt(ssq * (1.0 / DF_HEAD_QK) + EPS) * qkg_ref[which:which + 1, :]
        elif j in (_BLK_DN_Z * LANES // IN_TN, _BLK_DF_Z * LANES // IN_TN):
            y = _silu(y)
        proj_ref[:, cols] = y.astype(BF16)
    yt = _dot_nt(wt_ref[...], h)
    vt_ref[...] = yt[:DF_WIDTH].astype(BF16)
    gate_ref[...] = yt[DF_WIDTH:]


def _in_projection(x2d, t, gain, w_main, w_t, seg, qk_gain):
    m = x2d.shape[0]
    assert DF_WIDTH == IN_TN and t % IN_TM == 0
    tiles_per_seq = t // IN_TM
    const = lambda shape: pl.BlockSpec(shape, lambda i: (0, 0))
    return pl.pallas_call(
        _inproj_kernel,
        grid=(m // IN_TM,),
        in_specs=[
            pl.BlockSpec((IN_TM, D_MODEL), lambda i: (i, 0)),
            const((1, D_MODEL)),
            const((D_MODEL, PROJ_COLS)),
            const((DF_WIDTH + GATE_ROWS, D_MODEL)),
            const((SEG_W, SEG_W)),
            const((2, DF_WIDTH)),
        ],
        out_specs=[
            pl.BlockSpec((IN_TM, PROJ_COLS), lambda i: (i, 0)),
            pl.BlockSpec((None, DF_WIDTH, IN_TM),
                         lambda i: (i // tiles_per_seq, 0, i % tiles_per_seq)),
            pl.BlockSpec((None, GATE_ROWS, IN_TM),
                         lambda i: (i // tiles_per_seq, 0, i % tiles_per_seq)),
        ],
        out_shape=[
            jax.ShapeDtypeStruct((m, PROJ_COLS), BF16),
            jax.ShapeDtypeStruct((m // t, DF_WIDTH, t), BF16),
            jax.ShapeDtypeStruct((m // t, GATE_ROWS, t), F32),
        ],
        compiler_params=pltpu.CompilerParams(
            dimension_semantics=("parallel",), vmem_limit_bytes=VMEM_LIMIT),
        name="in_projection",
    )(x2d, gain, w_main, w_t, seg, qk_gain)


DN_ROWS = 256
QKV_COLS = 3 * DN_WIDTH
PREP_CHUNKS = 4


def _gdn_kernel(qkv_ref, zg_ref, gate_ref, convw_ref, alog_ref, dtb_ref, gain_ref,
                out_ref, q_s, k_s, v_s, wq_s, aqk_s, kdt_s, gate_s, state_s, stage_s):
    t = qkv_ref.shape[0]
    n_chunks = t // CHUNK

    group_rows = PREP_CHUNKS * CHUNK

    def group_base(g):
        return g * group_rows if isinstance(g, int) else pl.multiple_of(g * group_rows, group_rows)

    def conv(g):
        r0 = group_base(g)
        for cg in range(QKV_COLS // LANES):
            cols = slice(cg * LANES, (cg + 1) * LANES)
            cur = qkv_ref[pl.ds(r0, group_rows), cols].astype(F32)
            if isinstance(g, int) and g == 0:
                prev = jnp.zeros((8, LANES), F32)
            else:
                prev = qkv_ref[pl.ds(pl.multiple_of(r0 - 16, 16), 16), cols].astype(F32)[8:]
            stage_s[cg, :8, :] = prev
            stage_s[cg, 8:, :] = cur
            w = convw_ref[:, cols]
            acc = cur * w[CONV_WIDTH - 1:CONV_WIDTH]
            for j in range(1, CONV_WIDTH):
                acc = acc + (stage_s[cg, 8 - j:8 - j + group_rows, :]
                             * w[CONV_WIDTH - 1 - j:CONV_WIDTH - j])
            y = _silu(acc)
            hcols = slice((cg % DN_HEADS) * LANES, (cg % DN_HEADS + 1) * LANES)
            if cg < 2 * DN_HEADS:
                inv_norm = lax.rsqrt(jnp.sum(y * y, axis=-1, keepdims=True) + EPS)
                if cg < DN_HEADS:
                    q_s[pl.ds(r0, group_rows), hcols] = y * (inv_norm * (DN_HEAD ** -0.5))
                else:
                    k_s[pl.ds(r0, group_rows), hcols] = y * inv_norm
            else:
                v_s[pl.ds(r0, group_rows), hcols] = y
            yield

    gate = gate_ref[...]
    beta = jax.nn.sigmoid(gate)
    g = -jnp.exp(alog_ref[...]) * jax.nn.softplus(gate + dtb_ref[...])
    pos = lax.broadcasted_iota(jnp.int32, g.shape, 0) % CHUNK
    s = 1
    while s < CHUNK:
        g = g + jnp.where(pos >= s, pltpu.roll(g, s, 0), 0.0)
        s *= 2
    lane = lax.broadcasted_iota(jnp.int32, g.shape, 1)
    gate_s[...] = jnp.where(lane < DN_HEADS, beta, g)

    row = lax.broadcasted_iota(jnp.int32, (CHUNK, 2 * CHUNK), 0)
    col = lax.broadcasted_iota(jnp.int32, (CHUNK, 2 * CHUNK), 1)
    eye_f = (row == col).astype(F32)
    lower_f = (row >= col).astype(F32)
    strict_f = (row > col).astype(F32)
    upper_f = (col >= CHUNK).astype(F32)
    eye_hi = (col - CHUNK == row).astype(F32)
    zero_rows = jnp.zeros((CHUNK, DN_HEAD), BF16)

    items = [(cc, h) for cc in range(PREP_CHUNKS) for h in range(DN_HEADS)]
    heads = range(DN_HEADS)
    gain = gain_ref[...]

    def halves(fn, seq):
        out = []
        mid = len(seq) // 2
        for part in (seq[:mid], seq[mid:]):
            out.extend(fn(x) for x in part)
            yield out

    def prep(g):
        base = group_base(g)
        gates = [gate_s[pl.ds(base + cc * CHUNK, CHUNK), :] for cc in range(PREP_CHUNKS)]
        ks, lhs, decay, rhs, qe, kdt = [], [], [], [], [], []
        for cc, h in items:
            rows = pl.ds(base + cc * CHUNK, CHUNK)
            cols = slice(h * LANES, (h + 1) * LANES)
            q = q_s[rows, cols]
            k = k_s[rows, cols]
            v = v_s[rows, cols]
            beta_b = jnp.broadcast_to(gates[cc][:, h:h + 1], (CHUNK, DN_HEAD))
            gcb = jnp.broadcast_to(gates[cc][:, DN_HEADS + h:DN_HEADS + h + 1], (CHUNK, DN_HEAD))
            gr = jnp.sum(gcb * eye_f, axis=0, keepdims=True)
            decay.append(jnp.exp(jnp.minimum(gcb - gr, 0.0)))
            g_last = gcb[CHUNK - 1:CHUNK, :]
            eg = jnp.exp(gcb)
            kb = k * beta_b
            ks.append(jnp.concatenate([k.astype(BF16), zero_rows], axis=0))
            lhs.append(jnp.concatenate([kb, q], axis=0).astype(BF16))
            rhs.append(jnp.concatenate([v * beta_b, kb * eg], axis=1).astype(BF16))
            qe.append((q * eg).astype(BF16))
            kdt.append((k * jnp.exp(g_last - gcb)).T.astype(BF16))
        yield

        kq = None
        for kq in halves(lambda ab: _dot_nt(*ab), list(zip(lhs, ks))):
            yield
        m = [x[:CHUNK] * (d * strict_f) for x, d in zip(kq, decay)]
        a_qk = [(x[CHUNK:, :CHUNK] * (d * lower_f)[:, :CHUNK]).astype(BF16)
                for x, d in zip(kq, decay)]

        w = None
        for w in halves(lambda x: _dot(x[:, :CHUNK], x - eye_hi) + eye_hi, m):
            yield
        for _ in range(5):
            prev = w
            for w in halves(lambda x: _dot(x[:, :CHUNK], x) + x * upper_f, prev):
                yield
        sol = None
        for sol in halves(lambda xb: _dot(xb[0][:, CHUNK:], xb[1]), list(zip(w, rhs))):
            yield

        for n, (cc, h) in enumerate(items):
            rows = pl.ds(base + cc * CHUNK, CHUNK)
            rows_w = pl.ds(2 * base + cc * 2 * CHUNK, CHUNK)
            rows_q = pl.ds(2 * base + cc * 2 * CHUNK + CHUNK, CHUNK)
            rows_t = pl.ds(2 * base + cc * 2 * CHUNK, 2 * CHUNK)
            cols = slice(h * LANES, (h + 1) * LANES)
            cols_c = slice(h * LANES, h * LANES + CHUNK)
            v_s[rows, cols] = sol[n][:, :DN_HEAD]
            wq_s[rows_w, cols] = sol[n][:, DN_HEAD:].astype(BF16)
            wq_s[rows_q, cols] = qe[n]
            aqk_s[rows, cols_c] = a_qk[n]
            kdt_s[rows_t, cols_c] = kdt[n]
        yield

    def rec(g):
        base = group_base(g)
        cols = [slice(h * LANES, (h + 1) * LANES) for h in heads]
        cols_c = [slice(h * LANES, h * LANES + CHUNK) for h in heads]
        ld = []
        for cc in range(PREP_CHUNKS):
            rows = pl.ds(base + cc * CHUNK, CHUNK)
            rows2 = pl.ds(2 * base + cc * 2 * CHUNK, 2 * CHUNK)
            ld.append(dict(
                rows=rows,
                gates=gate_s[rows, :],
                wq=[wq_s[rows2, cols[h]] for h in heads],
                u=[v_s[rows, cols[h]] for h in heads],
                kdt=[kdt_s[rows2, cols_c[h]] for h in heads],
                aqk=[aqk_s[rows, cols_c[h]] for h in heads]))
        yield
        for c in ld:
            st = [state_s[h] for h in heads]
            r = [jnp.dot(c["wq"][h], st[h].astype(BF16), preferred_element_type=F32)
                 for h in heads]
            yield
            v_new = [(c["u"][h] - r[h][:CHUNK]).astype(BF16) for h in heads]
            upd = [jnp.dot(c["kdt"][h], v_new[h], preferred_element_type=F32) for h in heads]
            for h in heads:
                g_last = c["gates"][CHUNK - 1:CHUNK, DN_HEADS + h:DN_HEADS + h + 1]
                state_s[h] = st[h] * jnp.exp(g_last) + upd[h]
            yield
            o = [r[h][CHUNK:] + jnp.dot(c["aqk"][h], v_new[h], preferred_element_type=F32)
                 for h in heads]
            for h in heads:
                y = o[h] * lax.rsqrt(jnp.mean(o[h] * o[h], axis=-1, keepdims=True) + EPS) * gain
                gate = zg_ref[c["rows"], cols[h]].astype(F32)
                out_ref[c["rows"], cols[h]] = (y * gate).astype(out_ref.dtype)
            yield

    def run_interleaved(*gens):
        gens = list(gens)
        while gens:
            for gen in list(gens):
                if next(gen, StopIteration) is StopIteration:
                    gens.remove(gen)

    n_groups = n_chunks // PREP_CHUNKS
    state_s[...] = jnp.zeros_like(state_s)
    run_interleaved(conv(0))
    run_interleaved(prep(0), conv(1))

    def group_body(g, carry):
        run_interleaved(rec(g - 1), prep(g), conv(g + 1))
        return carry

    lax.fori_loop(1, n_groups - 1, group_body, 0)
    run_interleaved(rec(n_groups - 2), prep(n_groups - 1))
    run_interleaved(rec(n_groups - 1))


def _gated_deltanet(proj, gates, conv_w, alog8, dtb8, out_gain):
    b, t, _ = proj.shape
    assert t % (PREP_CHUNKS * CHUNK) == 0 and t % DN_ROWS == 0
    return pl.pallas_call(
        _gdn_kernel,
        grid=(b,),
        in_specs=[
            pl.BlockSpec((None, t, QKV_COLS), lambda i: (i, 0, 0)),
            pl.BlockSpec((None, t, DN_WIDTH), lambda i: (i, 0, QKV_COLS // DN_WIDTH)),
            pl.BlockSpec((None, t, N_GATES), lambda i: (i, 0, 0)),
            pl.BlockSpec((CONV_WIDTH, QKV_COLS), lambda i: (0, 0)),
            pl.BlockSpec((1, N_GATES), lambda i: (0, 0)),
            pl.BlockSpec((1, N_GATES), lambda i: (0, 0)),
            pl.BlockSpec((1, DN_HEAD), lambda i: (0, 0)),
        ],
        out_specs=pl.BlockSpec((None, t, DN_WIDTH), lambda i: (i, 0, 0)),
        out_shape=jax.ShapeDtypeStruct((b, t, DN_WIDTH), BF16),
        scratch_shapes=[
            pltpu.VMEM((t, DN_WIDTH), F32),
            pltpu.VMEM((t, DN_WIDTH), F32),
            pltpu.VMEM((t, DN_WIDTH), F32),
            pltpu.VMEM((2 * t, DN_WIDTH), BF16),
            pltpu.VMEM((t, DN_WIDTH), BF16),
            pltpu.VMEM((2 * t, DN_WIDTH), BF16),
            pltpu.VMEM((t, N_GATES), F32),
            pltpu.VMEM((DN_HEADS, DN_HEAD, DN_HEAD), F32),
            pltpu.VMEM((QKV_COLS // LANES, PREP_CHUNKS * CHUNK + 8, LANES), F32),
        ],
        compiler_params=pltpu.CompilerParams(
            dimension_semantics=("parallel",), vmem_limit_bytes=VMEM_LIMIT),
        name="gated_deltanet",
    )(proj, proj, gates, conv_w, alog8, dtb8, out_gain)


AT_BQ = 512
AT_BK = 256
AT_CW = 256
AT_NQ = 2 * AT_BQ
AT_CHAINS = AT_NQ // AT_CW
AT_CPM = AT_BQ // AT_CW
AT_VROWS = DF_HEAD_V + 16


def _attention_tiles(n_qblocks):
    all_chains = tuple(range(AT_CHAINS))
    lo = tuple(c for c in all_chains if c % AT_CPM == 0)
    hi = tuple(c for c in all_chains if c % AT_CPM == 1)
    tiles = []
    for qi in range(n_qblocks):
        for kt in range(AT_CPM * qi + AT_CPM):
            d = kt - AT_CPM * qi
            chains, masked = (all_chains, ()) if d < 0 else (all_chains, lo) if d == 0 else (hi, hi)
            tiles.append(dict(qi=qi, kt=kt, chains=chains, masked=masked, first=kt == 0))
    return tiles


def _diff_attn_kernel(q_ref, k_ref, vt_ref, zg_ref, lq1_ref, lk1_ref, lq2_ref, lk2_ref,
                      og_ref, out_ref, qq_s, vta_s, m_s, acc_s, s_a, s_b, c_a, c_b, p_a, p_b):
    t = q_ref.shape[0]
    n_qblocks = t // AT_BQ
    all_chains = tuple(range(AT_CHAINS))

    def lanes(c):
        return slice(c * AT_CW, (c + 1) * AT_CW)

    first = lax.broadcasted_iota(jnp.int32, (AT_BQ, LANES), 1) < DF_HEAD_QK
    zero = jnp.zeros((AT_BQ, LANES), BF16)
    for qi in range(n_qblocks):
        blk = q_ref[qi * AT_BQ:(qi + 1) * AT_BQ, :]
        qq_s[qi, :AT_BQ, :] = jnp.where(first, blk, zero)
        qq_s[qi, AT_BQ:, :] = jnp.where(first, zero, blk)

    vta_s[:DF_HEAD_V, :] = vt_ref[...]
    vta_s[DF_HEAD_V:, :] = jnp.ones((AT_VROWS - DF_HEAD_V, t), BF16)

    lam = (jnp.exp(jnp.sum(lq1_ref[...] * lk1_ref[...], axis=-1, keepdims=True))
           - jnp.exp(jnp.sum(lq2_ref[...] * lk2_ref[...], axis=-1, keepdims=True))
           + LAMBDA_INIT)

    krow = lax.broadcasted_iota(jnp.int32, (AT_BK, AT_CW), 0)
    qcol = lax.broadcasted_iota(jnp.int32, (AT_BK, AT_CW), 1)
    causal = krow <= qcol
    bufs = ((s_a, c_a, p_a), (s_b, c_b, p_b))

    def scores(tile, c, buf):
        rows = slice(tile["kt"] * AT_BK, (tile["kt"] + 1) * AT_BK)
        s = _dot_nt(k_ref[rows, :], qq_s[tile["qi"], lanes(c), :])
        if c in tile["masked"]:
            s = jnp.where(causal, s, -jnp.inf)
        buf[0][:, lanes(c)] = s
        buf[1][:, lanes(c)] = jnp.max(s, axis=0, keepdims=True)

    def values(tile, c, buf):
        rows = slice(tile["kt"] * AT_BK, (tile["kt"] + 1) * AT_BK)
        return jnp.dot(vta_s[:, rows], buf[2][:, lanes(c)], preferred_element_type=F32)

    def finish(qi):
        rows = slice(qi * AT_BQ, (qi + 1) * AT_BQ)
        denom = acc_s[qi, DF_HEAD_V:DF_HEAD_V + 1, :]
        o_all = acc_s[qi, :DF_HEAD_V, :] * (1.0 / denom)
        o = o_all[:, :AT_BQ] - lam * o_all[:, AT_BQ:]
        y = (o * lax.rsqrt(jnp.mean(o * o, axis=0, keepdims=True) + EPS)).T
        y = y * og_ref[...] * (1.0 - LAMBDA_INIT)
        out_ref[rows, :] = (y * zg_ref[rows, :].astype(F32)).astype(out_ref.dtype)

    tiles = _attention_tiles(n_qblocks)
    for c in tiles[0]["chains"]:
        scores(tiles[0], c, bufs[0])
    for n, cur in enumerate(tiles):
        nxt = tiles[n + 1] if n + 1 < len(tiles) else None
        prv = tiles[n - 1] if n > 0 else None
        s_cur, c_cur, p_cur = bufs[n % 2]
        qi = cur["qi"]
        for c in all_chains:
            alpha = None
            if c in cur["chains"]:
                if cur["first"]:
                    m_new = c_cur[:, lanes(c)]
                else:
                    m_old = m_s[qi, :, lanes(c)]
                    m_new = jnp.maximum(m_old, c_cur[:, lanes(c)])
                    alpha = jnp.exp2(m_old - m_new)
                p = jnp.exp2(s_cur[:, lanes(c)] - m_new)
                m_s[qi, :, lanes(c)] = m_new
                p_cur[:, lanes(c)] = p.astype(BF16)
            if nxt is not None and c in nxt["chains"]:
                scores(nxt, c, bufs[(n + 1) % 2])
            if prv is not None and c in prv["chains"]:
                pv = values(prv, c, bufs[(n - 1) % 2])
                pq = prv["qi"]
                acc = pv if prv["first"] else acc_s[pq, :, lanes(c)] + pv
                if pq == qi and alpha is not None:
                    acc = alpha * acc
                acc_s[pq, :, lanes(c)] = acc
        if prv is not None and prv["qi"] != qi:
            finish(prv["qi"])
    last = tiles[-1]
    for c in last["chains"]:
        acc_s[last["qi"], :, lanes(c)] = (acc_s[last["qi"], :, lanes(c)]
                                          + values(last, c, bufs[(len(tiles) - 1) % 2]))
    finish(last["qi"])


def _diff_attention(proj, vt, lq1, lk1, lq2, lk2, out_gain):
    b, t, _ = proj.shape
    assert AT_BQ == AT_CPM * AT_BK and AT_CW == AT_BK and AT_CPM == 2 and t % AT_BQ == 0
    n_qblocks = t // AT_BQ
    small = lambda n: pl.BlockSpec((1, n), lambda i, h: (0, 0))
    return pl.pallas_call(
        _diff_attn_kernel,
        grid=(b, DF_HEADS),
        in_specs=[
            pl.BlockSpec((None, t, LANES), lambda i, h: (i, 0, _BLK_DF_Q + h)),
            pl.BlockSpec((None, t, LANES), lambda i, h: (i, 0, _BLK_DF_K + h)),
            pl.BlockSpec((None, DF_HEAD_V, t), lambda i, h: (i, h, 0)),
            pl.BlockSpec((None, t, LANES), lambda i, h: (i, 0, _BLK_DF_Z + h)),
            small(DF_HEAD_QK), small(DF_HEAD_QK), small(DF_HEAD_QK), small(DF_HEAD_QK),
            small(DF_HEAD_V),
        ],
        out_specs=pl.BlockSpec((None, t, LANES), lambda i, h: (i, 0, h)),
        out_shape=jax.ShapeDtypeStruct((b, t, DF_WIDTH), BF16),
        scratch_shapes=[
            pltpu.VMEM((n_qblocks, AT_NQ, LANES), BF16),
            pltpu.VMEM((AT_VROWS, t), BF16),
            pltpu.VMEM((n_qblocks, 1, AT_NQ), F32),
            pltpu.VMEM((n_qblocks, AT_VROWS, AT_NQ), F32),
            pltpu.VMEM((AT_BK, AT_NQ), F32),
            pltpu.VMEM((AT_BK, AT_NQ), F32),
            pltpu.VMEM((1, AT_NQ), F32),
            pltpu.VMEM((1, AT_NQ), F32),
            pltpu.VMEM((AT_BK, AT_NQ), BF16),
            pltpu.VMEM((AT_BK, AT_NQ), BF16),
        ],
        compiler_params=pltpu.CompilerParams(
            dimension_semantics=("parallel", "parallel"), vmem_limit_bytes=VMEM_LIMIT),
        name="diff_attention",
    )(proj, proj, vt, proj, lq1, lk1, lq2, lk2, out_gain)


OUT_TM = 1024


def _outproj_kernel(x_ref, dn_ref, df_ref, w_ref, out_ref):
    acc = jnp.dot(dn_ref[...], w_ref[:DN_WIDTH, :], preferred_element_type=F32)
    acc = acc + jnp.dot(df_ref[...], w_ref[DN_WIDTH:, :], preferred_element_type=F32)
    out_ref[...] = x_ref[...] + acc


def _out_projection(x2d, mixed_dn, mixed_df, w_out):
    m = x2d.shape[0]
    return pl.pallas_call(
        _outproj_kernel,
        grid=(m // OUT_TM,),
        in_specs=[
            pl.BlockSpec((OUT_TM, D_MODEL), lambda i: (i, 0)),
            pl.BlockSpec((OUT_TM, DN_WIDTH), lambda i: (i, 0)),
            pl.BlockSpec((OUT_TM, DF_WIDTH), lambda i: (i, 0)),
            pl.BlockSpec((DN_WIDTH + DF_WIDTH, D_MODEL), lambda i: (0, 0)),
        ],
        out_specs=pl.BlockSpec((OUT_TM, D_MODEL), lambda i: (i, 0)),
        out_shape=jax.ShapeDtypeStruct((m, D_MODEL), F32),
        compiler_params=pltpu.CompilerParams(
            dimension_semantics=("parallel",), vmem_limit_bytes=VMEM_LIMIT),
        name="out_projection",
    )(x2d, mixed_dn, mixed_df, w_out)


def kernel(x, norm_gain, w_in, conv_w, a_log, dt_bias, dn_out_gain, q_gain, k_gain,
           lambda_q1, lambda_k1, lambda_q2, lambda_k2, df_out_gain, w_out):
    b, t, d = x.shape
    assert d == D_MODEL and norm_gain.shape[0] == 1
    x2d = x.reshape(b * t, d)

    w = w_in[0]
    gate_lo = 4 * DN_WIDTH
    df_lo = gate_lo + N_GATES
    w_main = jnp.concatenate([w[:, :gate_lo], w[:, df_lo:df_lo + 2 * DF_WIDTH],
                              w[:, df_lo + 3 * DF_WIDTH:]], axis=1).astype(BF16)
    w_t = jnp.concatenate([w[:, df_lo + 2 * DF_WIDTH:df_lo + 3 * DF_WIDTH],
                           jnp.pad(w[:, gate_lo:df_lo], ((0, 0), (0, GATE_ROWS - N_GATES)))],
                          axis=1).T.astype(BF16)
    half = jnp.arange(SEG_W, dtype=jnp.int32) // DF_HEAD_QK
    seg = (half[:, None] == half[None, :]).astype(BF16)
    reps = DF_WIDTH // DF_HEAD_QK
    qk_gain = jnp.concatenate([jnp.tile(q_gain, (1, reps)) * (DF_HEAD_QK ** -0.5 * LOG2_E),
                               jnp.tile(k_gain, (1, reps))], axis=0)

    proj, vt, gates_t = _in_projection(x2d, t, norm_gain, w_main, w_t, seg, qk_gain)
    proj = proj.reshape(b, t, PROJ_COLS)
    gates = jnp.swapaxes(gates_t[:, :N_GATES, :], 1, 2)

    zeros4 = jnp.zeros((1, DN_HEADS), F32)
    alog8 = jnp.concatenate([zeros4, a_log], axis=1)
    dtb8 = jnp.concatenate([zeros4, dt_bias], axis=1)
    mixed_dn = _gated_deltanet(proj, gates, conv_w[0], alog8, dtb8, dn_out_gain)

    mixed_df = _diff_attention(proj, vt, lambda_q1, lambda_k1, lambda_q2, lambda_k2,
                               df_out_gain)

    out = _out_projection(x2d, mixed_dn.reshape(b * t, DN_WIDTH),
                          mixed_df.reshape(b * t, DF_WIDTH), w_out[0].astype(BF16))
    return out.reshape(b, t, d)
```

```python
import math

import jax
import jax.numpy as jnp
from jax import lax
from jax.experimental import pallas as pl
from jax.experimental.pallas import tpu as pltpu

F32 = jnp.float32
BF16 = jnp.bfloat16

D_MODEL = 1024
DN_HEADS = 4
DN_HEAD = 128
DN_WIDTH = DN_HEADS * DN_HEAD
CONV_WIDTH = 4
CHUNK = 64
DF_HEADS = 4
DF_HEAD_QK = 64
DF_HEAD_V = 128
DF_WIDTH = DF_HEADS * DF_HEAD_V
EPS = 1e-6
LAMBDA_INIT = 0.8 - 0.6 * math.exp(-0.3 * 0)
LOG2_E = math.log2(math.e)

N_GATES = 2 * DN_HEADS
GATE_ROWS = 16
PROJ_COLS = 4 * DN_WIDTH + 3 * DF_WIDTH
LANES = 128
VMEM_LIMIT = 56 * 1024 * 1024

_BLK_DN_Z = 3 * DN_HEADS
_BLK_DF_Q = 4 * DN_HEADS
_BLK_DF_K = _BLK_DF_Q + DF_HEADS
_BLK_DF_Z = _BLK_DF_K + DF_HEADS


def _silu(x):
    return x * jax.nn.sigmoid(x)


def _dot(a, b):
    return jnp.dot(a.astype(BF16), b.astype(BF16), preferred_element_type=F32)


def _dot_nt(a, b):
    return lax.dot_general(a.astype(BF16), b.astype(BF16), (((1,), (1,)), ((), ())),
                           preferred_element_type=F32)


IN_TM = 512
IN_TN = 512
SEG_W = 256


def _inproj_kernel(x_ref, gain_ref, w_ref, wt_ref, seg_ref, qkg_ref,
                   proj_ref, vt_ref, gate_ref):
    x = x_ref[...]
    ms = jnp.mean(x * x, axis=-1, keepdims=True)
    h = (x * lax.rsqrt(ms + EPS) * gain_ref[...]).astype(BF16)
    for j in range(PROJ_COLS // IN_TN):
        cols = slice(j * IN_TN, (j + 1) * IN_TN)
        y = jnp.dot(h, w_ref[:, cols], preferred_element_type=F32)
        if j in (_BLK_DF_Q * LANES // IN_TN, _BLK_DF_K * LANES // IN_TN):
            which = 0 if j == _BLK_DF_Q * LANES // IN_TN else 1
            sq = (y * y).astype(BF16)
            ssq = jnp.concatenate(
                [jnp.dot(sq[:, i * SEG_W:(i + 1) * SEG_W], seg_ref[...],
                         preferred_element_type=F32) for i in range(IN_TN // SEG_W)], axis=1)
            y = y * lax.rsqrt(ssq * (1.0 / DF_HEAD_QK) + EPS) * qkg_ref[which:which + 1, :]
        elif j in (_BLK_DN_Z * LANES // IN_TN, _BLK_DF_Z * LANES // IN_TN):
            y = _silu(y)
        proj_ref[:, cols] = y.astype(BF16)
    yt = _dot_nt(wt_ref[...], h)
    vt_ref[...] = yt[:DF_WIDTH].astype(BF16)
    gate_ref[...] = yt[DF_WIDTH:]


def _in_projection(x2d, t, gain, w_main, w_t, seg, qk_gain):
    m = x2d.shape[0]
    assert DF_WIDTH == IN_TN and t % IN_TM == 0
    tiles_per_seq = t // IN_TM
    const = lambda shape: pl.BlockSpec(shape, lambda i: (0, 0))
    return pl.pallas_call(
        _inproj_kernel,
        grid=(m // IN_TM,),
        in_specs=[
            pl.BlockSpec((IN_TM, D_MODEL), lambda i: (i, 0)),
            const((1, D_MODEL)),
            const((D_MODEL, PROJ_COLS)),
            const((DF_WIDTH + GATE_ROWS, D_MODEL)),
            const((SEG_W, SEG_W)),
            const((2, DF_WIDTH)),
        ],
        out_specs=[
            pl.BlockSpec((IN_TM, PROJ_COLS), lambda i: (i, 0)),
            pl.BlockSpec((None, DF_WIDTH, IN_TM),
                         lambda i: (i // tiles_per_seq, 0, i % tiles_per_seq)),
            pl.BlockSpec((None, GATE_ROWS, IN_TM),
                         lambda i: (i // tiles_per_seq, 0, i % tiles_per_seq)),
        ],
        out_shape=[
            jax.ShapeDtypeStruct((m, PROJ_COLS), BF16),
            jax.ShapeDtypeStruct((m // t, DF_WIDTH, t), BF16),
            jax.ShapeDtypeStruct((m // t, GATE_ROWS, t), F32),
        ],
        compiler_params=pltpu.CompilerParams(
            dimension_semantics=("parallel",), vmem_limit_bytes=VMEM_LIMIT),
        name="in_projection",
    )(x2d, gain, w_main, w_t, seg, qk_gain)


QKV_COLS = 3 * DN_WIDTH
PREP_CHUNKS = 4


def _gdn_kernel(qkv_ref, zg_ref, gate_ref, convw_ref, alog_ref, dtb_ref, gain_ref,
                out_ref, q_r, k_r, v_r, u_r, wq_r, aqk_r, kdt_r, gate_s, state_s, stage_s):
    t = qkv_ref.shape[0]
    n_chunks = t // CHUNK
    group_rows = PREP_CHUNKS * CHUNK

    def group_base(g):
        return g * group_rows if isinstance(g, int) else pl.multiple_of(g * group_rows, group_rows)

    def conv(g, slot):
        r0 = group_base(g)
        for cg in range(QKV_COLS // LANES):
            cols = slice(cg * LANES, (cg + 1) * LANES)
            cur = qkv_ref[pl.ds(r0, group_rows), cols].astype(F32)
            if isinstance(g, int) and g == 0:
                prev = jnp.zeros((8, LANES), F32)
            else:
                prev = qkv_ref[pl.ds(pl.multiple_of(r0 - 16, 16), 16), cols].astype(F32)[8:]
            stage_s[cg, :8, :] = prev
            stage_s[cg, 8:, :] = cur
            w = convw_ref[:, cols]
            acc = cur * w[CONV_WIDTH - 1:CONV_WIDTH]
            for j in range(1, CONV_WIDTH):
                acc = acc + (stage_s[cg, 8 - j:8 - j + group_rows, :]
                             * w[CONV_WIDTH - 1 - j:CONV_WIDTH - j])
            y = _silu(acc)
            hcols = slice((cg % DN_HEADS) * LANES, (cg % DN_HEADS + 1) * LANES)
            if cg < 2 * DN_HEADS:
                inv_norm = lax.rsqrt(jnp.sum(y * y, axis=-1, keepdims=True) + EPS)
                if cg < DN_HEADS:
                    q_r[slot, :, hcols] = y * (inv_norm * (DN_HEAD ** -0.5))
                else:
                    k_r[slot, :, hcols] = y * inv_norm
            else:
                v_r[slot, :, hcols] = y
            yield

    gate = gate_ref[...]
    beta = jax.nn.sigmoid(gate)
    g = -jnp.exp(alog_ref[...]) * jax.nn.softplus(gate + dtb_ref[...])
    pos = lax.broadcasted_iota(jnp.int32, g.shape, 0) % CHUNK
    s = 1
    while s < CHUNK:
        g = g + jnp.where(pos >= s, pltpu.roll(g, s, 0), 0.0)
        s *= 2
    lane = lax.broadcasted_iota(jnp.int32, g.shape, 1)
    gate_s[...] = jnp.where(lane < DN_HEADS, beta, g)

    row = lax.broadcasted_iota(jnp.int32, (CHUNK, 2 * CHUNK), 0)
    col = lax.broadcasted_iota(jnp.int32, (CHUNK, 2 * CHUNK), 1)
    eye_f = (row == col).astype(F32)
    lower_f = (row >= col).astype(F32)
    strict_f = (row > col).astype(F32)
    upper_f = (col >= CHUNK).astype(F32)
    eye_hi = (col - CHUNK == row).astype(F32)
    zero_rows = jnp.zeros((CHUNK, DN_HEAD), BF16)

    items = [(cc, h) for cc in range(PREP_CHUNKS) for h in range(DN_HEADS)]
    heads = range(DN_HEADS)
    gain = gain_ref[...]
    hcols = [slice(h * LANES, (h + 1) * LANES) for h in heads]
    hcols_c = [slice(h * LANES, h * LANES + CHUNK) for h in heads]

    def halves(fn, seq):
        out = []
        mid = len(seq) // 2
        for part in (seq[:mid], seq[mid:]):
            out.extend(fn(x) for x in part)
            yield out

    def prep(g, slot):
        base = group_base(g)
        gates = [gate_s[pl.ds(base + cc * CHUNK, CHUNK), :] for cc in range(PREP_CHUNKS)]
        ks, lhs, decay, rhs, qe, kdt = [], [], [], [], [], []
        for cc, h in items:
            rows = slice(cc * CHUNK, (cc + 1) * CHUNK)
            q = q_r[slot, rows, hcols[h]]
            k = k_r[slot, rows, hcols[h]]
            v = v_r[slot, rows, hcols[h]]
            beta_b = jnp.broadcast_to(gates[cc][:, h:h + 1], (CHUNK, DN_HEAD))
            gcb = jnp.broadcast_to(gates[cc][:, DN_HEADS + h:DN_HEADS + h + 1], (CHUNK, DN_HEAD))
            gr = jnp.sum(gcb * eye_f, axis=0, keepdims=True)
            decay.append(jnp.exp(jnp.minimum(gcb - gr, 0.0)))
            g_last = gcb[CHUNK - 1:CHUNK, :]
            eg = jnp.exp(gcb)
            kb = k * beta_b
            ks.append(jnp.concatenate([k.astype(BF16), zero_rows], axis=0))
            lhs.append(jnp.concatenate([kb, q], axis=0).astype(BF16))
            rhs.append(jnp.concatenate([v * beta_b, kb * eg], axis=1).astype(BF16))
            qe.append((q * eg).astype(BF16))
            kdt.append((k * jnp.exp(g_last - gcb)).T.astype(BF16))
        yield

        kq = None
        for kq in halves(lambda ab: _dot_nt(*ab), list(zip(lhs, ks))):
            yield
        m = [x[:CHUNK] * (d * strict_f) for x, d in zip(kq, decay)]
        a_qk = [(x[CHUNK:, :CHUNK] * (d * lower_f)[:, :CHUNK]).astype(BF16)
                for x, d in zip(kq, decay)]

        w = None
        for w in halves(lambda x: _dot(x[:, :CHUNK], x - eye_hi) + eye_hi, m):
            yield
        for _ in range(5):
            prev = w
            for w in halves(lambda x: _dot(x[:, :CHUNK], x) + x * upper_f, prev):
                yield
        sol = None
        for sol in halves(lambda xb: _dot(xb[0][:, CHUNK:], xb[1]), list(zip(w, rhs))):
            yield

        for n, (cc, h) in enumerate(items):
            rows = slice(cc * CHUNK, (cc + 1) * CHUNK)
            rows_w = slice(2 * cc * CHUNK, (2 * cc + 1) * CHUNK)
            rows_q = slice((2 * cc + 1) * CHUNK, (2 * cc + 2) * CHUNK)
            rows_t = slice(2 * cc * CHUNK, (2 * cc + 2) * CHUNK)
            u_r[slot, rows, hcols[h]] = sol[n][:, :DN_HEAD]
            wq_r[slot, rows_w, hcols[h]] = sol[n][:, DN_HEAD:].astype(BF16)
            wq_r[slot, rows_q, hcols[h]] = qe[n]
            aqk_r[slot, rows, hcols_c[h]] = a_qk[n]
            kdt_r[slot, rows_t, hcols_c[h]] = kdt[n]
        yield

    def rec(g, slot):
        base = group_base(g)
        for cc in range(PREP_CHUNKS):
            rows = slice(cc * CHUNK, (cc + 1) * CHUNK)
            rows2 = slice(2 * cc * CHUNK, (2 * cc + 2) * CHUNK)
            trows = pl.ds(base + cc * CHUNK, CHUNK)
            st = [state_s[h] for h in heads]
            r = [jnp.dot(wq_r[slot, rows2, hcols[h]], st[h].astype(BF16),
                         preferred_element_type=F32) for h in heads]
            yield
            v_new = [(u_r[slot, rows, hcols[h]] - r[h][:CHUNK]).astype(BF16) for h in heads]
            upd = [jnp.dot(kdt_r[slot, rows2, hcols_c[h]], v_new[h], preferred_element_type=F32)
                   for h in heads]
            tail = gate_s[pl.ds(base + (cc + 1) * CHUNK - 8, 8), :]
            decay_last = jnp.exp(tail[7:8, :])
            for h in heads:
                state_s[h] = st[h] * decay_last[:, DN_HEADS + h:DN_HEADS + h + 1] + upd[h]
            yield
            o = [r[h][CHUNK:] + jnp.dot(aqk_r[slot, rows, hcols_c[h]], v_new[h],
                                        preferred_element_type=F32) for h in heads]
            for h in heads:
                y = o[h] * lax.rsqrt(jnp.mean(o[h] * o[h], axis=-1, keepdims=True) + EPS) * gain
                gate = zg_ref[trows, hcols[h]].astype(F32)
                out_ref[trows, hcols[h]] = (y * gate).astype(out_ref.dtype)
            yield

    def run_interleaved(*gens):
        gens = list(gens)
        while gens:
            for gen in list(gens):
                if next(gen, StopIteration) is StopIteration:
                    gens.remove(gen)

    n_groups = n_chunks // PREP_CHUNKS
    state_s[...] = jnp.zeros_like(state_s)
    run_interleaved(conv(0, 0))
    run_interleaved(prep(0, 0), conv(1, 1))

    def pair_body(j, carry):
        run_interleaved(rec(2 * j, 0), prep(2 * j + 1, 1), conv(2 * j + 2, 0))
        run_interleaved(rec(2 * j + 1, 1), prep(2 * j + 2, 0), conv(2 * j + 3, 1))
        return carry

    lax.fori_loop(0, (n_groups - 2) // 2, pair_body, 0)
    run_interleaved(rec(n_groups - 2, 0), prep(n_groups - 1, 1))
    run_interleaved(rec(n_groups - 1, 1))


def _gated_deltanet(proj, gates, conv_w, alog8, dtb8, out_gain):
    b, t, _ = proj.shape
    group_rows = PREP_CHUNKS * CHUNK
    assert t % (2 * group_rows) == 0 and t // group_rows >= 4
    ring = lambda rows, dtype: pltpu.VMEM((2, rows, DN_WIDTH), dtype)
    return pl.pallas_call(
        _gdn_kernel,
        grid=(b,),
        in_specs=[
            pl.BlockSpec((None, t, QKV_COLS), lambda i: (i, 0, 0)),
            pl.BlockSpec((None, t, DN_WIDTH), lambda i: (i, 0, QKV_COLS // DN_WIDTH)),
            pl.BlockSpec((None, t, N_GATES), lambda i: (i, 0, 0)),
            pl.BlockSpec((CONV_WIDTH, QKV_COLS), lambda i: (0, 0)),
            pl.BlockSpec((1, N_GATES), lambda i: (0, 0)),
            pl.BlockSpec((1, N_GATES), lambda i: (0, 0)),
            pl.BlockSpec((1, DN_HEAD), lambda i: (0, 0)),
        ],
        out_specs=pl.BlockSpec((None, t, DN_WIDTH), lambda i: (i, 0, 0)),
        out_shape=jax.ShapeDtypeStruct((b, t, DN_WIDTH), BF16),
        scratch_shapes=[
            ring(group_rows, F32),
            ring(group_rows, F32),
            ring(group_rows, F32),
            ring(group_rows, F32),
            ring(2 * group_rows, BF16),
            ring(group_rows, BF16),
            ring(2 * group_rows, BF16),
            pltpu.VMEM((t, N_GATES), F32),
            pltpu.VMEM((DN_HEADS, DN_HEAD, DN_HEAD), F32),
            pltpu.VMEM((QKV_COLS // LANES, group_rows + 8, LANES), F32),
        ],
        compiler_params=pltpu.CompilerParams(
            dimension_semantics=("parallel",), vmem_limit_bytes=VMEM_LIMIT),
        name="gated_deltanet",
    )(proj, proj, gates, conv_w, alog8, dtb8, out_gain)


AT_BQ = 512
AT_BK = 256
AT_CW = 256
AT_NQ = 2 * AT_BQ
AT_CHAINS = AT_NQ // AT_CW
AT_CPM = AT_BQ // AT_CW
AT_VROWS = DF_HEAD_V + 16


def _attention_tiles(n_qblocks):
    all_chains = tuple(range(AT_CHAINS))
    lo = tuple(c for c in all_chains if c % AT_CPM == 0)
    hi = tuple(c for c in all_chains if c % AT_CPM == 1)
    tiles = []
    for qi in range(n_qblocks):
        for kt in range(AT_CPM * qi + AT_CPM):
            d = kt - AT_CPM * qi
            chains, masked = (all_chains, ()) if d < 0 else (all_chains, lo) if d == 0 else (hi, hi)
            tiles.append(dict(qi=qi, kt=kt, chains=chains, masked=masked, first=kt == 0))
    return tiles


def _diff_attn_kernel(q_ref, k_ref, vt_ref, zg_ref, lq1_ref, lk1_ref, lq2_ref, lk2_ref,
                      og_ref, out_ref, qq_s, vta_s, m_s, acc_s, s_a, s_b, c_a, c_b, p_a, p_b):
    t = q_ref.shape[0]
    n_qblocks = t // AT_BQ
    all_chains = tuple(range(AT_CHAINS))

    def lanes(c):
        return slice(c * AT_CW, (c + 1) * AT_CW)

    first = lax.broadcasted_iota(jnp.int32, (AT_BQ, LANES), 1) < DF_HEAD_QK
    zero = jnp.zeros((AT_BQ, LANES), BF16)
    for qi in range(n_qblocks):
        blk = q_ref[qi * AT_BQ:(qi + 1) * AT_BQ, :]
        qq_s[qi, :AT_BQ, :] = jnp.where(first, blk, zero)
        qq_s[qi, AT_BQ:, :] = jnp.where(first, zero, blk)

    vta_s[:DF_HEAD_V, :] = vt_ref[...]
    vta_s[DF_HEAD_V:, :] = jnp.ones((AT_VROWS - DF_HEAD_V, t), BF16)

    lam = (jnp.exp(jnp.sum(lq1_ref[...] * lk1_ref[...], axis=-1, keepdims=True))
           - jnp.exp(jnp.sum(lq2_ref[...] * lk2_ref[...], axis=-1, keepdims=True))
           + LAMBDA_INIT)

    krow = lax.broadcasted_iota(jnp.int32, (AT_BK, AT_CW), 0)
    qcol = lax.broadcasted_iota(jnp.int32, (AT_BK, AT_CW), 1)
    causal = krow <= qcol
    bufs = ((s_a, c_a, p_a), (s_b, c_b, p_b))

    def scores(tile, c, buf):
        rows = slice(tile["kt"] * AT_BK, (tile["kt"] + 1) * AT_BK)
        s = _dot_nt(k_ref[rows, :], qq_s[tile["qi"], lanes(c), :])
        if c in tile["masked"]:
            s = jnp.where(causal, s, -jnp.inf)
        buf[0][:, lanes(c)] = s
        buf[1][:, lanes(c)] = jnp.max(s, axis=0, keepdims=True)

    def values(tile, c, buf):
        rows = slice(tile["kt"] * AT_BK, (tile["kt"] + 1) * AT_BK)
        return jnp.dot(vta_s[:, rows], buf[2][:, lanes(c)], preferred_element_type=F32)

    def finish(qi):
        rows = slice(qi * AT_BQ, (qi + 1) * AT_BQ)
        denom = acc_s[qi, DF_HEAD_V:DF_HEAD_V + 1, :]
        o_all = acc_s[qi, :DF_HEAD_V, :] * (1.0 / denom)
        o = o_all[:, :AT_BQ] - lam * o_all[:, AT_BQ:]
        y = (o * lax.rsqrt(jnp.mean(o * o, axis=0, keepdims=True) + EPS)).T
        y = y * og_ref[...] * (1.0 - LAMBDA_INIT)
        out_ref[rows, :] = (y * zg_ref[rows, :].astype(F32)).astype(out_ref.dtype)

    tiles = _attention_tiles(n_qblocks)
    for c in tiles[0]["chains"]:
        scores(tiles[0], c, bufs[0])
    for n, cur in enumerate(tiles):
        nxt = tiles[n + 1] if n + 1 < len(tiles) else None
        prv = tiles[n - 1] if n > 0 else None
        s_cur, c_cur, p_cur = bufs[n % 2]
        qi = cur["qi"]
        for c in all_chains:
            alpha = None
            if c in cur["chains"]:
                if cur["first"]:
                    m_new = c_cur[:, lanes(c)]
                else:
                    m_old = m_s[qi, :, lanes(c)]
                    m_new = jnp.maximum(m_old, c_cur[:, lanes(c)])
                    alpha = jnp.exp2(m_old - m_new)
                p = jnp.exp2(s_cur[:, lanes(c)] - m_new)
                m_s[qi, :, lanes(c)] = m_new
                p_cur[:, lanes(c)] = p.astype(BF16)
            if nxt is not None and c in nxt["chains"]:
                scores(nxt, c, bufs[(n + 1) % 2])
            if prv is not None and c in prv["chains"]:
                pv = values(prv, c, bufs[(n - 1) % 2])
                pq = prv["qi"]
                acc = pv if prv["first"] else acc_s[pq, :, lanes(c)] + pv
                if pq == qi and alpha is not None:
                    acc = alpha * acc
                acc_s[pq, :, lanes(c)] = acc
        if prv is not None and prv["qi"] != qi:
            finish(prv["qi"])
    last = tiles[-1]
    for c in last["chains"]:
        acc_s[last["qi"], :, lanes(c)] = (acc_s[last["qi"], :, lanes(c)]
                                          + values(last, c, bufs[(len(tiles) - 1) % 2]))
    finish(last["qi"])


def _diff_attention(proj, vt, lq1, lk1, lq2, lk2, out_gain):
    b, t, _ = proj.shape
    assert AT_BQ == AT_CPM * AT_BK and AT_CW == AT_BK and AT_CPM == 2 and t % AT_BQ == 0
    n_qblocks = t // AT_BQ
    small = lambda n: pl.BlockSpec((1, n), lambda i, h: (0, 0))
    return pl.pallas_call(
        _diff_attn_kernel,
        grid=(b, DF_HEADS),
        in_specs=[
            pl.BlockSpec((None, t, LANES), lambda i, h: (i, 0, _BLK_DF_Q + h)),
            pl.BlockSpec((None, t, LANES), lambda i, h: (i, 0, _BLK_DF_K + h)),
            pl.BlockSpec((None, DF_HEAD_V, t), lambda i, h: (i, h, 0)),
            pl.BlockSpec((None, t, LANES), lambda i, h: (i, 0, _BLK_DF_Z + h)),
            small(DF_HEAD_QK), small(DF_HEAD_QK), small(DF_HEAD_QK), small(DF_HEAD_QK),
            small(DF_HEAD_V),
        ],
        out_specs=pl.BlockSpec((None, t, LANES), lambda i, h: (i, 0, h)),
        out_shape=jax.ShapeDtypeStruct((b, t, DF_WIDTH), BF16),
        scratch_shapes=[
            pltpu.VMEM((n_qblocks, AT_NQ, LANES), BF16),
            pltpu.VMEM((AT_VROWS, t), BF16),
            pltpu.VMEM((n_qblocks, 1, AT_NQ), F32),
            pltpu.VMEM((n_qblocks, AT_VROWS, AT_NQ), F32),
            pltpu.VMEM((AT_BK, AT_NQ), F32),
            pltpu.VMEM((AT_BK, AT_NQ), F32),
            pltpu.VMEM((1, AT_NQ), F32),
            pltpu.VMEM((1, AT_NQ), F32),
            pltpu.VMEM((AT_BK, AT_NQ), BF16),
            pltpu.VMEM((AT_BK, AT_NQ), BF16),
        ],
        compiler_params=pltpu.CompilerParams(
            dimension_semantics=("parallel", "parallel"), vmem_limit_bytes=VMEM_LIMIT),
        name="diff_attention",
    )(proj, proj, vt, proj, lq1, lk1, lq2, lk2, out_gain)


OUT_TM = 2048


def _outproj_kernel(x_ref, dn_ref, df_ref, w_ref, out_ref):
    acc = jnp.dot(dn_ref[...], w_ref[:DN_WIDTH, :], preferred_element_type=F32)
    acc = acc + jnp.dot(df_ref[...], w_ref[DN_WIDTH:, :], preferred_element_type=F32)
    out_ref[...] = x_ref[...] + acc


def _out_projection(x2d, mixed_dn, mixed_df, w_out):
    m = x2d.shape[0]
    return pl.pallas_call(
        _outproj_kernel,
        grid=(m // OUT_TM,),
        in_specs=[
            pl.BlockSpec((OUT_TM, D_MODEL), lambda i: (i, 0)),
            pl.BlockSpec((OUT_TM, DN_WIDTH), lambda i: (i, 0)),
            pl.BlockSpec((OUT_TM, DF_WIDTH), lambda i: (i, 0)),
            pl.BlockSpec((DN_WIDTH + DF_WIDTH, D_MODEL), lambda i: (0, 0)),
        ],
        out_specs=pl.BlockSpec((OUT_TM, D_MODEL), lambda i: (i, 0)),
        out_shape=jax.ShapeDtypeStruct((m, D_MODEL), F32),
        compiler_params=pltpu.CompilerParams(
            dimension_semantics=("parallel",), vmem_limit_bytes=VMEM_LIMIT),
        name="out_projection",
    )(x2d, mixed_dn, mixed_df, w_out)


def kernel(x, norm_gain, w_in, conv_w, a_log, dt_bias, dn_out_gain, q_gain, k_gain,
           lambda_q1, lambda_k1, lambda_q2, lambda_k2, df_out_gain, w_out):
    b, t, d = x.shape
    assert d == D_MODEL and norm_gain.shape[0] == 1
    x2d = x.reshape(b * t, d)

    w = w_in[0]
    gate_lo = 4 * DN_WIDTH
    df_lo = gate_lo + N_GATES
    w_main = jnp.concatenate([w[:, :gate_lo], w[:, df_lo:df_lo + 2 * DF_WIDTH],
                              w[:, df_lo + 3 * DF_WIDTH:]], axis=1).astype(BF16)
    w_t = jnp.concatenate([w[:, df_lo + 2 * DF_WIDTH:df_lo + 3 * DF_WIDTH],
                           jnp.pad(w[:, gate_lo:df_lo], ((0, 0), (0, GATE_ROWS - N_GATES)))],
                          axis=1).T.astype(BF16)
    half = jnp.arange(SEG_W, dtype=jnp.int32) // DF_HEAD_QK
    seg = (half[:, None] == half[None, :]).astype(BF16)
    reps = DF_WIDTH // DF_HEAD_QK
    qk_gain = jnp.concatenate([jnp.tile(q_gain, (1, reps)) * (DF_HEAD_QK ** -0.5 * LOG2_E),
                               jnp.tile(k_gain, (1, reps))], axis=0)

    proj, vt, gates_t = _in_projection(x2d, t, norm_gain, w_main, w_t, seg, qk_gain)
    proj = proj.reshape(b, t, PROJ_COLS)
    gates = jnp.swapaxes(gates_t[:, :N_GATES, :], 1, 2)

    zeros4 = jnp.zeros((1, DN_HEADS), F32)
    alog8 = jnp.concatenate([zeros4, a_log], axis=1)
    dtb8 = jnp.concatenate([zeros4, dt_bias], axis=1)
    mixed_dn = _gated_deltanet(proj, gates, conv_w[0], alog8, dtb8, dn_out_gain)

    mixed_df = _diff_attention(proj, vt, lambda_q1, lambda_k1, lambda_q2, lambda_k2,
                               df_out_gain)

    out = _out_projection(x2d, mixed_dn.reshape(b * t, DN_WIDTH),
                          mixed_df.reshape(b * t, DF_WIDTH), w_out[0].astype(BF16))
    return out.reshape(b, t, d)
```

```python
import math

import jax
import jax.numpy as jnp
from jax import lax
from jax.experimental import pallas as pl
from jax.experimental.pallas import tpu as pltpu

F32 = jnp.float32
BF16 = jnp.bfloat16

D_MODEL = 1024
DN_HEADS = 4
DN_HEAD = 128
DN_WIDTH = DN_HEADS * DN_HEAD
CONV_WIDTH = 4
CHUNK = 64
DF_HEADS = 4
DF_HEAD_QK = 64
DF_HEAD_V = 128
DF_WIDTH = DF_HEADS * DF_HEAD_V
EPS = 1e-6
LAMBDA_INIT = 0.8 - 0.6 * math.exp(-0.3 * 0)
LOG2_E = math.log2(math.e)

N_GATES = 2 * DN_HEADS
GATE_ROWS = 16
PROJ_COLS = 4 * DN_WIDTH + 3 * DF_WIDTH
LANES = 128
VMEM_LIMIT = 56 * 1024 * 1024

_BLK_DN_Z = 3 * DN_HEADS
_BLK_DF_Q = 4 * DN_HEADS
_BLK_DF_K = _BLK_DF_Q + DF_HEADS
_BLK_DF_Z = _BLK_DF_K + DF_HEADS


def _silu(x):
    return x * jax.nn.sigmoid(x)


def _dot(a, b):
    return jnp.dot(a.astype(BF16), b.astype(BF16), preferred_element_type=F32)


def _dot_nt(a, b):
    return lax.dot_general(a.astype(BF16), b.astype(BF16), (((1,), (1,)), ((), ())),
                           preferred_element_type=F32)


IN_TM = 512
IN_TN = 512
SEG_W = 256


def _inproj_kernel(x_ref, gain_ref, w_ref, wt_ref, seg_ref, qkg_ref,
                   proj_ref, vt_ref, gate_ref):
    x = x_ref[...]
    ms = jnp.mean(x * x, axis=-1, keepdims=True)
    h = (x * lax.rsqrt(ms + EPS) * gain_ref[...]).astype(BF16)
    for j in range(PROJ_COLS // IN_TN):
        cols = slice(j * IN_TN, (j + 1) * IN_TN)
        y = jnp.dot(h, w_ref[:, cols], preferred_element_type=F32)
        if j in (_BLK_DF_Q * LANES // IN_TN, _BLK_DF_K * LANES // IN_TN):
            which = 0 if j == _BLK_DF_Q * LANES // IN_TN else 1
            sq = (y * y).astype(BF16)
            ssq = jnp.concatenate(
                [jnp.dot(sq[:, i * SEG_W:(i + 1) * SEG_W], seg_ref[...],
                         preferred_element_type=F32) for i in range(IN_TN // SEG_W)], axis=1)
            y = y * lax.rsqrt(ssq * (1.0 / DF_HEAD_QK) + EPS) * qkg_ref[which:which + 1, :]
        elif j in (_BLK_DN_Z * LANES // IN_TN, _BLK_DF_Z * LANES // IN_TN):
            y = _silu(y)
        proj_ref[:, cols] = y.astype(BF16)
    yt = _dot_nt(wt_ref[...], h)
    vt_ref[...] = yt[:DF_WIDTH].astype(BF16)
    gate_ref[...] = yt[DF_WIDTH:]


def _in_projection(x2d, t, gain, w_main, w_t, seg, qk_gain):
    m = x2d.shape[0]
    assert DF_WIDTH == IN_TN and t % IN_TM == 0
    tiles_per_seq = t // IN_TM
    const = lambda shape: pl.BlockSpec(shape, lambda i: (0, 0))
    return pl.pallas_call(
        _inproj_kernel,
        grid=(m // IN_TM,),
        in_specs=[
            pl.BlockSpec((IN_TM, D_MODEL), lambda i: (i, 0)),
            const((1, D_MODEL)),
            const((D_MODEL, PROJ_COLS)),
            const((DF_WIDTH + GATE_ROWS, D_MODEL)),
            const((SEG_W, SEG_W)),
            const((2, DF_WIDTH)),
        ],
        out_specs=[
            pl.BlockSpec((IN_TM, PROJ_COLS), lambda i: (i, 0)),
            pl.BlockSpec((None, DF_WIDTH, IN_TM),
                         lambda i: (i // tiles_per_seq, 0, i % tiles_per_seq)),
            pl.BlockSpec((None, GATE_ROWS, IN_TM),
                         lambda i: (i // tiles_per_seq, 0, i % tiles_per_seq)),
        ],
        out_shape=[
            jax.ShapeDtypeStruct((m, PROJ_COLS), BF16),
            jax.ShapeDtypeStruct((m // t, DF_WIDTH, t), BF16),
            jax.ShapeDtypeStruct((m // t, GATE_ROWS, t), F32),
        ],
        compiler_params=pltpu.CompilerParams(
            dimension_semantics=("parallel",), vmem_limit_bytes=VMEM_LIMIT),
        name="in_projection",
    )(x2d, gain, w_main, w_t, seg, qk_gain)


QKV_COLS = 3 * DN_WIDTH
PREP_CHUNKS = 4


def _gdn_kernel(qkv_ref, zg_ref, gate_ref, convw_ref, alog_ref, dtb_ref, gain_ref,
                out_ref, q_r, k_r, v_r, u_r, wq_r, aqk_r, kdt_r, gate_s, state_s, stage_s):
    t = qkv_ref.shape[0]
    n_chunks = t // CHUNK
    group_rows = PREP_CHUNKS * CHUNK

    def group_base(g):
        return g * group_rows if isinstance(g, int) else pl.multiple_of(g * group_rows, group_rows)

    def conv(g, slot):
        r0 = group_base(g)
        for cg in range(QKV_COLS // LANES):
            cols = slice(cg * LANES, (cg + 1) * LANES)
            cur = qkv_ref[pl.ds(r0, group_rows), cols].astype(F32)
            if isinstance(g, int) and g == 0:
                prev = jnp.zeros((8, LANES), F32)
            else:
                prev = qkv_ref[pl.ds(pl.multiple_of(r0 - 16, 16), 16), cols].astype(F32)[8:]
            stage_s[cg, :8, :] = prev
            stage_s[cg, 8:, :] = cur
            w = convw_ref[:, cols]
            acc = cur * w[CONV_WIDTH - 1:CONV_WIDTH]
            for j in range(1, CONV_WIDTH):
                acc = acc + (stage_s[cg, 8 - j:8 - j + group_rows, :]
                             * w[CONV_WIDTH - 1 - j:CONV_WIDTH - j])
            y = _silu(acc)
            hcols = slice((cg % DN_HEADS) * LANES, (cg % DN_HEADS + 1) * LANES)
            if cg < 2 * DN_HEADS:
                inv_norm = lax.rsqrt(jnp.sum(y * y, axis=-1, keepdims=True) + EPS)
                if cg < DN_HEADS:
                    q_r[slot, :, hcols] = y * (inv_norm * (DN_HEAD ** -0.5))
                else:
                    k_r[slot, :, hcols] = y * inv_norm
            else:
                v_r[slot, :, hcols] = y
            yield

    gate = gate_ref[...]
    beta = jax.nn.sigmoid(gate)
    g = -jnp.exp(alog_ref[...]) * jax.nn.softplus(gate + dtb_ref[...])
    pos = lax.broadcasted_iota(jnp.int32, g.shape, 0) % CHUNK
    s = 1
    while s < CHUNK:
        g = g + jnp.where(pos >= s, pltpu.roll(g, s, 0), 0.0)
        s *= 2
    lane = lax.broadcasted_iota(jnp.int32, g.shape, 1)
    gate_s[...] = jnp.where(lane < DN_HEADS, beta, g)

    row = lax.broadcasted_iota(jnp.int32, (CHUNK, 2 * CHUNK), 0)
    col = lax.broadcasted_iota(jnp.int32, (CHUNK, 2 * CHUNK), 1)
    eye_f = (row == col).astype(F32)
    lower_f = (row >= col).astype(F32)
    strict_f = (row > col).astype(F32)
    upper_f = (col >= CHUNK).astype(F32)
    eye_hi = (col - CHUNK == row).astype(F32)
    zero_rows = jnp.zeros((CHUNK, DN_HEAD), BF16)

    items = [(cc, h) for cc in range(PREP_CHUNKS) for h in range(DN_HEADS)]
    heads = range(DN_HEADS)
    gain = gain_ref[...]
    hcols = [slice(h * LANES, (h + 1) * LANES) for h in heads]
    hcols_c = [slice(h * LANES, h * LANES + CHUNK) for h in heads]

    def halves(fn, seq):
        out = []
        mid = len(seq) // 2
        for part in (seq[:mid], seq[mid:]):
            out.extend(fn(x) for x in part)
            yield out

    def prep(g, slot):
        base = group_base(g)
        gates = [gate_s[pl.ds(base + cc * CHUNK, CHUNK), :] for cc in range(PREP_CHUNKS)]
        ks, lhs, decay, rhs, qe, kdt = [], [], [], [], [], []
        for cc, h in items:
            rows = slice(cc * CHUNK, (cc + 1) * CHUNK)
            q = q_r[slot, rows, hcols[h]]
            k = k_r[slot, rows, hcols[h]]
            v = v_r[slot, rows, hcols[h]]
            beta_b = jnp.broadcast_to(gates[cc][:, h:h + 1], (CHUNK, DN_HEAD))
            gcb = jnp.broadcast_to(gates[cc][:, DN_HEADS + h:DN_HEADS + h + 1], (CHUNK, DN_HEAD))
            gr = jnp.sum(gcb * eye_f, axis=0, keepdims=True)
            decay.append(jnp.exp(jnp.minimum(gcb - gr, 0.0)))
            g_last = gcb[CHUNK - 1:CHUNK, :]
            eg = jnp.exp(gcb)
            kb = k * beta_b
            ks.append(jnp.concatenate([k.astype(BF16), zero_rows], axis=0))
            lhs.append(jnp.concatenate([kb, q], axis=0).astype(BF16))
            rhs.append(jnp.concatenate([v * beta_b, kb * eg], axis=1).astype(BF16))
            qe.append((q * eg).astype(BF16))
            kdt.append((k * jnp.exp(g_last - gcb)).T.astype(BF16))
        yield

        kq = None
        for kq in halves(lambda ab: _dot_nt(*ab), list(zip(lhs, ks))):
            yield
        m = [x[:CHUNK] * (d * strict_f) for x, d in zip(kq, decay)]
        a_qk = [(x[CHUNK:, :CHUNK] * (d * lower_f)[:, :CHUNK]).astype(BF16)
                for x, d in zip(kq, decay)]

        w = None
        for w in halves(lambda x: _dot(x[:, :CHUNK], x - eye_hi) + eye_hi, m):
            yield
        for _ in range(5):
            prev = w
            for w in halves(lambda x: _dot(x[:, :CHUNK], x) + x * upper_f, prev):
                yield
        sol = None
        for sol in halves(lambda xb: _dot(xb[0][:, CHUNK:], xb[1]), list(zip(w, rhs))):
            yield

        for n, (cc, h) in enumerate(items):
            rows = slice(cc * CHUNK, (cc + 1) * CHUNK)
            rows_w = slice(2 * cc * CHUNK, (2 * cc + 1) * CHUNK)
            rows_q = slice((2 * cc + 1) * CHUNK, (2 * cc + 2) * CHUNK)
            rows_t = slice(2 * cc * CHUNK, (2 * cc + 2) * CHUNK)
            u_r[slot, rows, hcols[h]] = sol[n][:, :DN_HEAD]
            wq_r[slot, rows_w, hcols[h]] = sol[n][:, DN_HEAD:].astype(BF16)
            wq_r[slot, rows_q, hcols[h]] = qe[n]
            aqk_r[slot, rows, hcols_c[h]] = a_qk[n]
            kdt_r[slot, rows_t, hcols_c[h]] = kdt[n]
        yield

    def rec(g, slot):
        base = group_base(g)
        for cc in range(PREP_CHUNKS):
            rows = slice(cc * CHUNK, (cc + 1) * CHUNK)
            rows2 = slice(2 * cc * CHUNK, (2 * cc + 2) * CHUNK)
            trows = pl.ds(base + cc * CHUNK, CHUNK)
            st = [state_s[h] for h in heads]
            r = [jnp.dot(wq_r[slot, rows2, hcols[h]], st[h].astype(BF16),
                         preferred_element_type=F32) for h in heads]
            yield
            v_new = [(u_r[slot, rows, hcols[h]] - r[h][:CHUNK]).astype(BF16) for h in heads]
            upd = [jnp.dot(kdt_r[slot, rows2, hcols_c[h]], v_new[h], preferred_element_type=F32)
                   for h in heads]
            tail = gate_s[pl.ds(base + (cc + 1) * CHUNK - 8, 8), :]
            decay_last = jnp.exp(tail[7:8, :])
            for h in heads:
                state_s[h] = st[h] * decay_last[:, DN_HEADS + h:DN_HEADS + h + 1] + upd[h]
            yield
            o = [r[h][CHUNK:] + jnp.dot(aqk_r[slot, rows, hcols_c[h]], v_new[h],
                                        preferred_element_type=F32) for h in heads]
            for h in heads:
                y = o[h] * lax.rsqrt(jnp.mean(o[h] * o[h], axis=-1, keepdims=True) + EPS) * gain
                gate = zg_ref[trows, hcols[h]].astype(F32)
                out_ref[trows, hcols[h]] = (y * gate).astype(out_ref.dtype)
            yield

    def run_interleaved(*gens):
        gens = list(gens)
        while gens:
            for gen in list(gens):
                if next(gen, StopIteration) is StopIteration:
                    gens.remove(gen)

    n_groups = n_chunks // PREP_CHUNKS
    state_s[...] = jnp.zeros_like(state_s)
    run_interleaved(conv(0, 0))
    run_interleaved(prep(0, 0), conv(1, 1))

    def pair_body(j, carry):
        run_interleaved(rec(2 * j, 0), prep(2 * j + 1, 1), conv(2 * j + 2, 0))
        run_interleaved(rec(2 * j + 1, 1), prep(2 * j + 2, 0), conv(2 * j + 3, 1))
        return carry

    lax.fori_loop(0, (n_groups - 2) // 2, pair_body, 0)
    run_interleaved(rec(n_groups - 2, 0), prep(n_groups - 1, 1))
    run_interleaved(rec(n_groups - 1, 1))


def _gated_deltanet(proj, gates, conv_w, alog8, dtb8, out_gain):
    b, t, _ = proj.shape
    group_rows = PREP_CHUNKS * CHUNK
    assert t % (2 * group_rows) == 0 and t // group_rows >= 4
    ring = lambda rows, dtype: pltpu.VMEM((2, rows, DN_WIDTH), dtype)
    return pl.pallas_call(
        _gdn_kernel,
        grid=(b,),
        in_specs=[
            pl.BlockSpec((None, t, QKV_COLS), lambda i: (i, 0, 0)),
            pl.BlockSpec((None, t, DN_WIDTH), lambda i: (i, 0, QKV_COLS // DN_WIDTH)),
            pl.BlockSpec((None, t, N_GATES), lambda i: (i, 0, 0)),
            pl.BlockSpec((CONV_WIDTH, QKV_COLS), lambda i: (0, 0)),
            pl.BlockSpec((1, N_GATES), lambda i: (0, 0)),
            pl.BlockSpec((1, N_GATES), lambda i: (0, 0)),
            pl.BlockSpec((1, DN_HEAD), lambda i: (0, 0)),
        ],
        out_specs=pl.BlockSpec((None, t, DN_WIDTH), lambda i: (i, 0, 0)),
        out_shape=jax.ShapeDtypeStruct((b, t, DN_WIDTH), BF16),
        scratch_shapes=[
            ring(group_rows, F32),
            ring(group_rows, F32),
            ring(group_rows, F32),
            ring(group_rows, F32),
            ring(2 * group_rows, BF16),
            ring(group_rows, BF16),
            ring(2 * group_rows, BF16),
            pltpu.VMEM((t, N_GATES), F32),
            pltpu.VMEM((DN_HEADS, DN_HEAD, DN_HEAD), F32),
            pltpu.VMEM((QKV_COLS // LANES, group_rows + 8, LANES), F32),
        ],
        compiler_params=pltpu.CompilerParams(
            dimension_semantics=("parallel",), vmem_limit_bytes=VMEM_LIMIT),
        name="gated_deltanet",
    )(proj, proj, gates, conv_w, alog8, dtb8, out_gain)


AT_BQ = 512
AT_BK = 256
AT_CW = 256
AT_NQ = 2 * AT_BQ
AT_CHAINS = AT_NQ // AT_CW
AT_CPM = AT_BQ // AT_CW
AT_VROWS = DF_HEAD_V + 16
AT_PROJ_DELAY = 2
OUT_TN = 256


def _attention_tiles(n_qblocks):
    all_chains = tuple(range(AT_CHAINS))
    lo = tuple(c for c in all_chains if c % AT_CPM == 0)
    hi = tuple(c for c in all_chains if c % AT_CPM == 1)
    tiles = []
    for qi in range(n_qblocks):
        for kt in range(AT_CPM * qi + AT_CPM):
            d = kt - AT_CPM * qi
            chains, masked = (all_chains, ()) if d < 0 else (all_chains, lo) if d == 0 else (hi, hi)
            tiles.append(dict(qi=qi, kt=kt, chains=chains, masked=masked, first=kt == 0))
    return tiles


def _diff_attn_kernel(q_ref, k_ref, vt_ref, zg_ref, lq1_ref, lk1_ref, lq2_ref, lk2_ref,
                      og_ref, x_ref, dn_ref, wout_ref, out_ref,
                      qq_s, vta_s, m_s, acc_s, y_s, s_a, s_b, c_a, c_b, p_a, p_b):
    t = q_ref.shape[0]
    n_qblocks = t // AT_BQ
    all_chains = tuple(range(AT_CHAINS))

    @pl.when(pl.program_id(1) == 0)
    def _():
        out_ref[...] = x_ref[...]

    def lanes(c):
        return slice(c * AT_CW, (c + 1) * AT_CW)

    first = lax.broadcasted_iota(jnp.int32, (AT_BQ, LANES), 1) < DF_HEAD_QK
    zero = jnp.zeros((AT_BQ, LANES), BF16)
    for qi in range(n_qblocks):
        blk = q_ref[qi * AT_BQ:(qi + 1) * AT_BQ, :]
        qq_s[qi, :AT_BQ, :] = jnp.where(first, blk, zero)
        qq_s[qi, AT_BQ:, :] = jnp.where(first, zero, blk)

    vta_s[:DF_HEAD_V, :] = vt_ref[...]
    vta_s[DF_HEAD_V:, :] = jnp.ones((AT_VROWS - DF_HEAD_V, t), BF16)

    lam = (jnp.exp(jnp.sum(lq1_ref[...] * lk1_ref[...], axis=-1, keepdims=True))
           - jnp.exp(jnp.sum(lq2_ref[...] * lk2_ref[...], axis=-1, keepdims=True))
           + LAMBDA_INIT)

    krow = lax.broadcasted_iota(jnp.int32, (AT_BK, AT_CW), 0)
    qcol = lax.broadcasted_iota(jnp.int32, (AT_BK, AT_CW), 1)
    causal = krow <= qcol
    bufs = ((s_a, c_a, p_a), (s_b, c_b, p_b))

    def scores(tile, c, buf):
        rows = slice(tile["kt"] * AT_BK, (tile["kt"] + 1) * AT_BK)
        s = _dot_nt(k_ref[rows, :], qq_s[tile["qi"], lanes(c), :])
        if c in tile["masked"]:
            s = jnp.where(causal, s, -jnp.inf)
        buf[0][:, lanes(c)] = s
        buf[1][:, lanes(c)] = jnp.max(s, axis=0, keepdims=True)

    def values(tile, c, buf):
        rows = slice(tile["kt"] * AT_BK, (tile["kt"] + 1) * AT_BK)
        return jnp.dot(vta_s[:, rows], buf[2][:, lanes(c)], preferred_element_type=F32)

    def finish(qi):
        rows = slice(qi * AT_BQ, (qi + 1) * AT_BQ)
        denom = acc_s[qi, DF_HEAD_V:DF_HEAD_V + 1, :]
        o_all = acc_s[qi, :DF_HEAD_V, :] * (1.0 / denom)
        o = o_all[:, :AT_BQ] - lam * o_all[:, AT_BQ:]
        y = (o * lax.rsqrt(jnp.mean(o * o, axis=0, keepdims=True) + EPS)).T
        y = y * og_ref[...] * (1.0 - LAMBDA_INIT)
        y_s[qi] = (y * zg_ref[rows, :].astype(F32)).astype(BF16)

    def project(qi):
        rows = slice(qi * AT_BQ, (qi + 1) * AT_BQ)
        lhs = jnp.concatenate([y_s[qi], dn_ref[rows, :]], axis=1)
        for nc in range(D_MODEL // OUT_TN):
            cols = slice(nc * OUT_TN, (nc + 1) * OUT_TN)
            out_ref[rows, cols] = out_ref[rows, cols] + jnp.dot(
                lhs, wout_ref[:, cols], preferred_element_type=F32)

    tiles = _attention_tiles(n_qblocks)
    due = {}
    for c in tiles[0]["chains"]:
        scores(tiles[0], c, bufs[0])
    for n, cur in enumerate(tiles):
        nxt = tiles[n + 1] if n + 1 < len(tiles) else None
        prv = tiles[n - 1] if n > 0 else None
        s_cur, c_cur, p_cur = bufs[n % 2]
        qi = cur["qi"]
        for c in all_chains:
            alpha = None
            if c in cur["chains"]:
                if cur["first"]:
                    m_new = c_cur[:, lanes(c)]
                else:
                    m_old = m_s[qi, :, lanes(c)]
                    m_new = jnp.maximum(m_old, c_cur[:, lanes(c)])
                    alpha = jnp.exp2(m_old - m_new)
                p = jnp.exp2(s_cur[:, lanes(c)] - m_new)
                m_s[qi, :, lanes(c)] = m_new
                p_cur[:, lanes(c)] = p.astype(BF16)
            if nxt is not None and c in nxt["chains"]:
                scores(nxt, c, bufs[(n + 1) % 2])
            if prv is not None and c in prv["chains"]:
                pv = values(prv, c, bufs[(n - 1) % 2])
                pq = prv["qi"]
                acc = pv if prv["first"] else acc_s[pq, :, lanes(c)] + pv
                if pq == qi and alpha is not None:
                    acc = alpha * acc
                acc_s[pq, :, lanes(c)] = acc
        if prv is not None and prv["qi"] != qi:
            finish(prv["qi"])
            due[n + AT_PROJ_DELAY] = prv["qi"]
        if n in due:
            project(due.pop(n))
    last = tiles[-1]
    for c in last["chains"]:
        acc_s[last["qi"], :, lanes(c)] = (acc_s[last["qi"], :, lanes(c)]
                                          + values(last, c, bufs[(len(tiles) - 1) % 2]))
    finish(last["qi"])
    for qi in sorted(due.values()) + [last["qi"]]:
        project(qi)


def _diff_attention(proj, vt, lq1, lk1, lq2, lk2, out_gain, x, mixed_dn, w_heads):
    b, t, _ = proj.shape
    assert AT_BQ == AT_CPM * AT_BK and AT_CW == AT_BK and AT_CPM == 2 and t % AT_BQ == 0
    assert DF_HEADS == DN_HEADS and DF_HEAD_V == DN_HEAD and D_MODEL % OUT_TN == 0
    n_qblocks = t // AT_BQ
    small = lambda n: pl.BlockSpec((1, n), lambda i, h: (0, 0))
    return pl.pallas_call(
        _diff_attn_kernel,
        grid=(b, DF_HEADS),
        in_specs=[
            pl.BlockSpec((None, t, LANES), lambda i, h: (i, 0, _BLK_DF_Q + h)),
            pl.BlockSpec((None, t, LANES), lambda i, h: (i, 0, _BLK_DF_K + h)),
            pl.BlockSpec((None, DF_HEAD_V, t), lambda i, h: (i, h, 0)),
            pl.BlockSpec((None, t, LANES), lambda i, h: (i, 0, _BLK_DF_Z + h)),
            small(DF_HEAD_QK), small(DF_HEAD_QK), small(DF_HEAD_QK), small(DF_HEAD_QK),
            small(DF_HEAD_V),
            pl.BlockSpec((None, t, D_MODEL), lambda i, h: (i, 0, 0)),
            pl.BlockSpec((None, t, LANES), lambda i, h: (i, 0, h)),
            pl.BlockSpec((None, DF_HEAD_V + DN_HEAD, D_MODEL), lambda i, h: (h, 0, 0)),
        ],
        out_specs=pl.BlockSpec((None, t, D_MODEL), lambda i, h: (i, 0, 0)),
        out_shape=jax.ShapeDtypeStruct((b, t, D_MODEL), F32),
        scratch_shapes=[
            pltpu.VMEM((n_qblocks, AT_NQ, LANES), BF16),
            pltpu.VMEM((AT_VROWS, t), BF16),
            pltpu.VMEM((n_qblocks, 1, AT_NQ), F32),
            pltpu.VMEM((n_qblocks, AT_VROWS, AT_NQ), F32),
            pltpu.VMEM((n_qblocks, AT_BQ, DF_HEAD_V), BF16),
            pltpu.VMEM((AT_BK, AT_NQ), F32),
            pltpu.VMEM((AT_BK, AT_NQ), F32),
            pltpu.VMEM((1, AT_NQ), F32),
            pltpu.VMEM((1, AT_NQ), F32),
            pltpu.VMEM((AT_BK, AT_NQ), BF16),
            pltpu.VMEM((AT_BK, AT_NQ), BF16),
        ],
        compiler_params=pltpu.CompilerParams(
            dimension_semantics=("parallel", "arbitrary"), vmem_limit_bytes=VMEM_LIMIT),
        name="diff_attention",
    )(proj, proj, vt, proj, lq1, lk1, lq2, lk2, out_gain, x, mixed_dn, w_heads)


def kernel(x, norm_gain, w_in, conv_w, a_log, dt_bias, dn_out_gain, q_gain, k_gain,
           lambda_q1, lambda_k1, lambda_q2, lambda_k2, df_out_gain, w_out):
    b, t, d = x.shape
    assert d == D_MODEL and norm_gain.shape[0] == 1
    x2d = x.reshape(b * t, d)

    w = w_in[0]
    gate_lo = 4 * DN_WIDTH
    df_lo = gate_lo + N_GATES
    w_main = jnp.concatenate([w[:, :gate_lo], w[:, df_lo:df_lo + 2 * DF_WIDTH],
                              w[:, df_lo + 3 * DF_WIDTH:]], axis=1).astype(BF16)
    w_t = jnp.concatenate([w[:, df_lo + 2 * DF_WIDTH:df_lo + 3 * DF_WIDTH],
                           jnp.pad(w[:, gate_lo:df_lo], ((0, 0), (0, GATE_ROWS - N_GATES)))],
                          axis=1).T.astype(BF16)
    half = jnp.arange(SEG_W, dtype=jnp.int32) // DF_HEAD_QK
    seg = (half[:, None] == half[None, :]).astype(BF16)
    reps = DF_WIDTH // DF_HEAD_QK
    qk_gain = jnp.concatenate([jnp.tile(q_gain, (1, reps)) * (DF_HEAD_QK ** -0.5 * LOG2_E),
                               jnp.tile(k_gain, (1, reps))], axis=0)

    proj, vt, gates_t = _in_projection(x2d, t, norm_gain, w_main, w_t, seg, qk_gain)
    proj = proj.reshape(b, t, PROJ_COLS)
    gates = jnp.swapaxes(gates_t[:, :N_GATES, :], 1, 2)

    zeros4 = jnp.zeros((1, DN_HEADS), F32)
    alog8 = jnp.concatenate([zeros4, a_log], axis=1)
    dtb8 = jnp.concatenate([zeros4, dt_bias], axis=1)
    mixed_dn = _gated_deltanet(proj, gates, conv_w[0], alog8, dtb8, dn_out_gain)

    wo = w_out[0].astype(BF16)
    w_heads = jnp.concatenate([wo[DN_WIDTH:].reshape(DF_HEADS, DF_HEAD_V, d),
                               wo[:DN_WIDTH].reshape(DN_HEADS, DN_HEAD, d)], axis=1)
    return _diff_attention(proj, vt, lambda_q1, lambda_k1, lambda_q2, lambda_k2,
                           df_out_gain, x, mixed_dn, w_heads)
```

```python
import math

import jax
import jax.numpy as jnp
from jax import lax
from jax.experimental import pallas as pl
from jax.experimental.pallas import tpu as pltpu

F32 = jnp.float32
BF16 = jnp.bfloat16

D_MODEL = 1024
DN_HEADS = 4
DN_HEAD = 128
DN_WIDTH = DN_HEADS * DN_HEAD
CONV_WIDTH = 4
CHUNK = 64
DF_HEADS = 4
DF_HEAD_QK = 64
DF_HEAD_V = 128
DF_WIDTH = DF_HEADS * DF_HEAD_V
EPS = 1e-6
LAMBDA_INIT = 0.8 - 0.6 * math.exp(-0.3 * 0)
LOG2_E = math.log2(math.e)

N_GATES = 2 * DN_HEADS
GATE_ROWS = 16
PROJ_COLS = 4 * DN_WIDTH + 3 * DF_WIDTH
LANES = 128
VMEM_LIMIT = 56 * 1024 * 1024

_BLK_DN_Z = 3 * DN_HEADS
_BLK_DF_Q = 4 * DN_HEADS
_BLK_DF_K = _BLK_DF_Q + DF_HEADS
_BLK_DF_Z = _BLK_DF_K + DF_HEADS


def _silu(x):
    return x * jax.nn.sigmoid(x)


def _dot(a, b):
    return jnp.dot(a.astype(BF16), b.astype(BF16), preferred_element_type=F32)


def _dot_nt(a, b):
    return lax.dot_general(a.astype(BF16), b.astype(BF16), (((1,), (1,)), ((), ())),
                           preferred_element_type=F32)


WP_ROWS = 128
_GATE_LO = 4 * DN_WIDTH
_DF_LO = _GATE_LO + N_GATES


def _weight_prep_kernel(w_ref, main_ref, t_ref):
    main_ref[:, :_GATE_LO] = w_ref[:, :_GATE_LO].astype(BF16)
    main_ref[:, _GATE_LO:_GATE_LO + 2 * DF_WIDTH] = (
        w_ref[:, _DF_LO:_DF_LO + 2 * DF_WIDTH].astype(BF16))
    main_ref[:, _GATE_LO + 2 * DF_WIDTH:] = w_ref[:, _DF_LO + 3 * DF_WIDTH:].astype(BF16)
    t_ref[:DF_WIDTH, :] = w_ref[:, _DF_LO + 2 * DF_WIDTH:_DF_LO + 3 * DF_WIDTH].T.astype(BF16)
    t_ref[DF_WIDTH:, :] = w_ref[:, _GATE_LO:_GATE_LO + LANES].T[:GATE_ROWS].astype(BF16)


def _weight_prep(w_in):
    _, d, n_cols = w_in.shape
    assert d == D_MODEL and n_cols == _DF_LO + 4 * DF_WIDTH and d % WP_ROWS == 0
    return pl.pallas_call(
        _weight_prep_kernel,
        grid=(d // WP_ROWS,),
        in_specs=[pl.BlockSpec((None, WP_ROWS, n_cols), lambda i: (0, i, 0))],
        out_specs=[
            pl.BlockSpec((WP_ROWS, PROJ_COLS), lambda i: (i, 0)),
            pl.BlockSpec((DF_WIDTH + GATE_ROWS, WP_ROWS), lambda i: (0, i)),
        ],
        out_shape=[
            jax.ShapeDtypeStruct((d, PROJ_COLS), BF16),
            jax.ShapeDtypeStruct((DF_WIDTH + GATE_ROWS, d), BF16),
        ],
        compiler_params=pltpu.CompilerParams(
            dimension_semantics=("parallel",), vmem_limit_bytes=VMEM_LIMIT),
        name="weight_prep",
    )(w_in)


IN_TM = 512
IN_TN = 512
SEG_W = 256


def _inproj_kernel(x_ref, gain_ref, w_ref, wt_ref, seg_ref, qkg_ref,
                   proj_ref, vt_ref, gate_ref):
    x = x_ref[...]
    ms = jnp.mean(x * x, axis=-1, keepdims=True)
    h = (x * lax.rsqrt(ms + EPS) * gain_ref[...]).astype(BF16)
    for j in range(PROJ_COLS // IN_TN):
        cols = slice(j * IN_TN, (j + 1) * IN_TN)
        y = jnp.dot(h, w_ref[:, cols], preferred_element_type=F32)
        if j in (_BLK_DF_Q * LANES // IN_TN, _BLK_DF_K * LANES // IN_TN):
            which = 0 if j == _BLK_DF_Q * LANES // IN_TN else 1
            sq = (y * y).astype(BF16)
            ssq = jnp.concatenate(
                [jnp.dot(sq[:, i * SEG_W:(i + 1) * SEG_W], seg_ref[...],
                         preferred_element_type=F32) for i in range(IN_TN // SEG_W)], axis=1)
            y = y * lax.rsqrt(ssq * (1.0 / DF_HEAD_QK) + EPS) * qkg_ref[which:which + 1, :]
        elif j in (_BLK_DN_Z * LANES // IN_TN, _BLK_DF_Z * LANES // IN_TN):
            y = _silu(y)
        proj_ref[:, cols] = y.astype(BF16)
    yt = _dot_nt(wt_ref[...], h)
    vt_ref[...] = yt[:DF_WIDTH].astype(BF16)
    gate_ref[...] = yt[DF_WIDTH:]


def _in_projection(x2d, t, gain, w_main, w_t, seg, qk_gain):
    m = x2d.shape[0]
    assert DF_WIDTH == IN_TN and t % IN_TM == 0
    tiles_per_seq = t // IN_TM
    const = lambda shape: pl.BlockSpec(shape, lambda i: (0, 0))
    return pl.pallas_call(
        _inproj_kernel,
        grid=(m // IN_TM,),
        in_specs=[
            pl.BlockSpec((IN_TM, D_MODEL), lambda i: (i, 0)),
            const((1, D_MODEL)),
            const((D_MODEL, PROJ_COLS)),
            const((DF_WIDTH + GATE_ROWS, D_MODEL)),
            const((SEG_W, SEG_W)),
            const((2, DF_WIDTH)),
        ],
        out_specs=[
            pl.BlockSpec((IN_TM, PROJ_COLS), lambda i: (i, 0)),
            pl.BlockSpec((None, DF_WIDTH, IN_TM),
                         lambda i: (i // tiles_per_seq, 0, i % tiles_per_seq)),
            pl.BlockSpec((None, GATE_ROWS, IN_TM),
                         lambda i: (i // tiles_per_seq, 0, i % tiles_per_seq)),
        ],
        out_shape=[
            jax.ShapeDtypeStruct((m, PROJ_COLS), BF16),
            jax.ShapeDtypeStruct((m // t, DF_WIDTH, t), BF16),
            jax.ShapeDtypeStruct((m // t, GATE_ROWS, t), F32),
        ],
        compiler_params=pltpu.CompilerParams(
            dimension_semantics=("parallel",), vmem_limit_bytes=VMEM_LIMIT),
        name="in_projection",
    )(x2d, gain, w_main, w_t, seg, qk_gain)


QKV_COLS = 3 * DN_WIDTH
PREP_CHUNKS = 4


def _gdn_kernel(qkv_ref, zg_ref, gate_ref, convw_ref, alog_ref, dtb_ref, gain_ref,
                out_ref, q_r, k_r, v_r, u_r, wq_r, aqk_r, kdt_r, gate_s, state_s, stage_s):
    t = qkv_ref.shape[0]
    n_chunks = t // CHUNK
    group_rows = PREP_CHUNKS * CHUNK

    def group_base(g):
        return g * group_rows if isinstance(g, int) else pl.multiple_of(g * group_rows, group_rows)

    def conv(g, slot):
        r0 = group_base(g)
        for cg in range(QKV_COLS // LANES):
            cols = slice(cg * LANES, (cg + 1) * LANES)
            cur = qkv_ref[pl.ds(r0, group_rows), cols].astype(F32)
            if isinstance(g, int) and g == 0:
                prev = jnp.zeros((8, LANES), F32)
            else:
                prev = qkv_ref[pl.ds(pl.multiple_of(r0 - 16, 16), 16), cols].astype(F32)[8:]
            stage_s[cg, :8, :] = prev
            stage_s[cg, 8:, :] = cur
            w = convw_ref[:, cols]
            acc = cur * w[CONV_WIDTH - 1:CONV_WIDTH]
            for j in range(1, CONV_WIDTH):
                acc = acc + (stage_s[cg, 8 - j:8 - j + group_rows, :]
                             * w[CONV_WIDTH - 1 - j:CONV_WIDTH - j])
            y = _silu(acc)
            hcols = slice((cg % DN_HEADS) * LANES, (cg % DN_HEADS + 1) * LANES)
            if cg < 2 * DN_HEADS:
                inv_norm = lax.rsqrt(jnp.sum(y * y, axis=-1, keepdims=True) + EPS)
                if cg < DN_HEADS:
                    q_r[slot, :, hcols] = y * (inv_norm * (DN_HEAD ** -0.5))
                else:
                    k_r[slot, :, hcols] = y * inv_norm
            else:
                v_r[slot, :, hcols] = y
            yield

    gate = gate_ref[...]
    beta = jax.nn.sigmoid(gate)
    g = -jnp.exp(alog_ref[...]) * jax.nn.softplus(gate + dtb_ref[...])
    pos = lax.broadcasted_iota(jnp.int32, g.shape, 0) % CHUNK
    s = 1
    while s < CHUNK:
        g = g + jnp.where(pos >= s, pltpu.roll(g, s, 0), 0.0)
        s *= 2
    lane = lax.broadcasted_iota(jnp.int32, g.shape, 1)
    gate_s[...] = jnp.where(lane < DN_HEADS, beta, g)

    row = lax.broadcasted_iota(jnp.int32, (CHUNK, 2 * CHUNK), 0)
    col = lax.broadcasted_iota(jnp.int32, (CHUNK, 2 * CHUNK), 1)
    eye_f = (row == col).astype(F32)
    lower_f = (row >= col).astype(F32)
    strict_f = (row > col).astype(F32)
    upper_f = (col >= CHUNK).astype(F32)
    eye_hi = (col - CHUNK == row).astype(F32)
    zero_rows = jnp.zeros((CHUNK, DN_HEAD), BF16)

    items = [(cc, h) for cc in range(PREP_CHUNKS) for h in range(DN_HEADS)]
    heads = range(DN_HEADS)
    gain = gain_ref[...]
    hcols = [slice(h * LANES, (h + 1) * LANES) for h in heads]
    hcols_c = [slice(h * LANES, h * LANES + CHUNK) for h in heads]

    def halves(fn, seq):
        out = []
        mid = len(seq) // 2
        for part in (seq[:mid], seq[mid:]):
            out.extend(fn(x) for x in part)
            yield out

    def prep(g, slot):
        base = group_base(g)
        gates = [gate_s[pl.ds(base + cc * CHUNK, CHUNK), :] for cc in range(PREP_CHUNKS)]
        ks, lhs, decay, rhs, qe, kdt = [], [], [], [], [], []
        for cc, h in items:
            rows = slice(cc * CHUNK, (cc + 1) * CHUNK)
            q = q_r[slot, rows, hcols[h]]
            k = k_r[slot, rows, hcols[h]]
            v = v_r[slot, rows, hcols[h]]
            beta_b = jnp.broadcast_to(gates[cc][:, h:h + 1], (CHUNK, DN_HEAD))
            gcb = jnp.broadcast_to(gates[cc][:, DN_HEADS + h:DN_HEADS + h + 1], (CHUNK, DN_HEAD))
            gr = jnp.sum(gcb * eye_f, axis=0, keepdims=True)
            decay.append(jnp.exp(jnp.minimum(gcb - gr, 0.0)))
            g_last = gcb[CHUNK - 1:CHUNK, :]
            eg = jnp.exp(gcb)
            kb = k * beta_b
            ks.append(jnp.concatenate([k.astype(BF16), zero_rows], axis=0))
            lhs.append(jnp.concatenate([kb, q], axis=0).astype(BF16))
            rhs.append(jnp.concatenate([v * beta_b, kb * eg], axis=1).astype(BF16))
            qe.append((q * eg).astype(BF16))
            kdt.append((k * jnp.exp(g_last - gcb)).T.astype(BF16))
        yield

        kq = None
        for kq in halves(lambda ab: _dot_nt(*ab), list(zip(lhs, ks))):
            yield
        m = [x[:CHUNK] * (d * strict_f) for x, d in zip(kq, decay)]
        a_qk = [(x[CHUNK:, :CHUNK] * (d * lower_f)[:, :CHUNK]).astype(BF16)
                for x, d in zip(kq, decay)]

        w = None
        for w in halves(lambda x: _dot(x[:, :CHUNK], x - eye_hi) + eye_hi, m):
            yield
        for _ in range(5):
            prev = w
            for w in halves(lambda x: _dot(x[:, :CHUNK], x) + x * upper_f, prev):
                yield
        sol = None
        for sol in halves(lambda xb: _dot(xb[0][:, CHUNK:], xb[1]), list(zip(w, rhs))):
            yield

        for n, (cc, h) in enumerate(items):
            rows = slice(cc * CHUNK, (cc + 1) * CHUNK)
            rows_w = slice(2 * cc * CHUNK, (2 * cc + 1) * CHUNK)
            rows_q = slice((2 * cc + 1) * CHUNK, (2 * cc + 2) * CHUNK)
            rows_t = slice(2 * cc * CHUNK, (2 * cc + 2) * CHUNK)
            u_r[slot, rows, hcols[h]] = sol[n][:, :DN_HEAD]
            wq_r[slot, rows_w, hcols[h]] = sol[n][:, DN_HEAD:].astype(BF16)
            wq_r[slot, rows_q, hcols[h]] = qe[n]
            aqk_r[slot, rows, hcols_c[h]] = a_qk[n]
            kdt_r[slot, rows_t, hcols_c[h]] = kdt[n]
        yield

    def rec(g, slot):
        base = group_base(g)
        for cc in range(PREP_CHUNKS):
            rows = slice(cc * CHUNK, (cc + 1) * CHUNK)
            rows2 = slice(2 * cc * CHUNK, (2 * cc + 2) * CHUNK)
            trows = pl.ds(base + cc * CHUNK, CHUNK)
            st = [state_s[h] for h in heads]
            r = [jnp.dot(wq_r[slot, rows2, hcols[h]], st[h].astype(BF16),
                         preferred_element_type=F32) for h in heads]
            yield
            v_new = [(u_r[slot, rows, hcols[h]] - r[h][:CHUNK]).astype(BF16) for h in heads]
            upd = [jnp.dot(kdt_r[slot, rows2, hcols_c[h]], v_new[h], preferred_element_type=F32)
                   for h in heads]
            tail = gate_s[pl.ds(base + (cc + 1) * CHUNK - 8, 8), :]
            decay_last = jnp.exp(tail[7:8, :])
            for h in heads:
                state_s[h] = st[h] * decay_last[:, DN_HEADS + h:DN_HEADS + h + 1] + upd[h]
            yield
            o = [r[h][CHUNK:] + jnp.dot(aqk_r[slot, rows, hcols_c[h]], v_new[h],
                                        preferred_element_type=F32) for h in heads]
            for h in heads:
                y = o[h] * lax.rsqrt(jnp.mean(o[h] * o[h], axis=-1, keepdims=True) + EPS) * gain
                gate = zg_ref[trows, hcols[h]].astype(F32)
                out_ref[trows, hcols[h]] = (y * gate).astype(out_ref.dtype)
            yield

    def run_interleaved(*gens):
        gens = list(gens)
        while gens:
            for gen in list(gens):
                if next(gen, StopIteration) is StopIteration:
                    gens.remove(gen)

    n_groups = n_chunks // PREP_CHUNKS
    state_s[...] = jnp.zeros_like(state_s)
    run_interleaved(conv(0, 0))
    run_interleaved(prep(0, 0), conv(1, 1))

    def pair_body(j, carry):
        run_interleaved(rec(2 * j, 0), prep(2 * j + 1, 1), conv(2 * j + 2, 0))
        run_interleaved(rec(2 * j + 1, 1), prep(2 * j + 2, 0), conv(2 * j + 3, 1))
        return carry

    lax.fori_loop(0, (n_groups - 2) // 2, pair_body, 0)
    run_interleaved(rec(n_groups - 2, 0), prep(n_groups - 1, 1))
    run_interleaved(rec(n_groups - 1, 1))


def _gated_deltanet(proj, gates, conv_w, alog8, dtb8, out_gain):
    b, t, _ = proj.shape
    group_rows = PREP_CHUNKS * CHUNK
    assert t % (2 * group_rows) == 0 and t // group_rows >= 4
    ring = lambda rows, dtype: pltpu.VMEM((2, rows, DN_WIDTH), dtype)
    return pl.pallas_call(
        _gdn_kernel,
        grid=(b,),
        in_specs=[
            pl.BlockSpec((None, t, QKV_COLS), lambda i: (i, 0, 0)),
            pl.BlockSpec((None, t, DN_WIDTH), lambda i: (i, 0, QKV_COLS // DN_WIDTH)),
            pl.BlockSpec((None, t, N_GATES), lambda i: (i, 0, 0)),
            pl.BlockSpec((CONV_WIDTH, QKV_COLS), lambda i: (0, 0)),
            pl.BlockSpec((1, N_GATES), lambda i: (0, 0)),
            pl.BlockSpec((1, N_GATES), lambda i: (0, 0)),
            pl.BlockSpec((1, DN_HEAD), lambda i: (0, 0)),
        ],
        out_specs=pl.BlockSpec((None, t, DN_WIDTH), lambda i: (i, 0, 0)),
        out_shape=jax.ShapeDtypeStruct((b, t, DN_WIDTH), BF16),
        scratch_shapes=[
            ring(group_rows, F32),
            ring(group_rows, F32),
            ring(group_rows, F32),
            ring(group_rows, F32),
            ring(2 * group_rows, BF16),
            ring(group_rows, BF16),
            ring(2 * group_rows, BF16),
            pltpu.VMEM((t, N_GATES), F32),
            pltpu.VMEM((DN_HEADS, DN_HEAD, DN_HEAD), F32),
            pltpu.VMEM((QKV_COLS // LANES, group_rows + 8, LANES), F32),
        ],
        compiler_params=pltpu.CompilerParams(
            dimension_semantics=("parallel",), vmem_limit_bytes=VMEM_LIMIT),
        name="gated_deltanet",
    )(proj, proj, gates, conv_w, alog8, dtb8, out_gain)


AT_BQ = 512
AT_BK = 256
AT_CW = 256
AT_NQ = 2 * AT_BQ
AT_CHAINS = AT_NQ // AT_CW
AT_CPM = AT_BQ // AT_CW
AT_VROWS = DF_HEAD_V + 16
AT_PROJ_DELAY = 2
OUT_TN = 256


def _attention_tiles(n_qblocks):
    all_chains = tuple(range(AT_CHAINS))
    lo = tuple(c for c in all_chains if c % AT_CPM == 0)
    hi = tuple(c for c in all_chains if c % AT_CPM == 1)
    tiles = []
    for qi in range(n_qblocks):
        for kt in range(AT_CPM * qi + AT_CPM):
            d = kt - AT_CPM * qi
            chains, masked = (all_chains, ()) if d < 0 else (all_chains, lo) if d == 0 else (hi, hi)
            tiles.append(dict(qi=qi, kt=kt, chains=chains, masked=masked, first=kt == 0))
    return tiles


def _diff_attn_kernel(q_ref, k_ref, vt_ref, zg_ref, lq1_ref, lk1_ref, lq2_ref, lk2_ref,
                      og_ref, x_ref, dn_ref, wout_ref, out_ref,
                      qq_s, vta_s, m_s, acc_s, y_s, s_a, s_b, c_a, c_b, p_a, p_b):
    t = q_ref.shape[0]
    n_qblocks = t // AT_BQ
    all_chains = tuple(range(AT_CHAINS))

    @pl.when(pl.program_id(1) == 0)
    def _():
        out_ref[...] = x_ref[...]

    def lanes(c):
        return slice(c * AT_CW, (c + 1) * AT_CW)

    first = lax.broadcasted_iota(jnp.int32, (AT_BQ, LANES), 1) < DF_HEAD_QK
    zero = jnp.zeros((AT_BQ, LANES), BF16)
    for qi in range(n_qblocks):
        blk = q_ref[qi * AT_BQ:(qi + 1) * AT_BQ, :]
        qq_s[qi, :AT_BQ, :] = jnp.where(first, blk, zero)
        qq_s[qi, AT_BQ:, :] = jnp.where(first, zero, blk)

    vta_s[:DF_HEAD_V, :] = vt_ref[...]
    vta_s[DF_HEAD_V:, :] = jnp.ones((AT_VROWS - DF_HEAD_V, t), BF16)

    lam = (jnp.exp(jnp.sum(lq1_ref[...] * lk1_ref[...], axis=-1, keepdims=True))
           - jnp.exp(jnp.sum(lq2_ref[...] * lk2_ref[...], axis=-1, keepdims=True))
           + LAMBDA_INIT)

    krow = lax.broadcasted_iota(jnp.int32, (AT_BK, AT_CW), 0)
    qcol = lax.broadcasted_iota(jnp.int32, (AT_BK, AT_CW), 1)
    causal = krow <= qcol
    bufs = ((s_a, c_a, p_a), (s_b, c_b, p_b))

    def scores(tile, c, buf):
        rows = slice(tile["kt"] * AT_BK, (tile["kt"] + 1) * AT_BK)
        s = _dot_nt(k_ref[rows, :], qq_s[tile["qi"], lanes(c), :])
        if c in tile["masked"]:
            s = jnp.where(causal, s, -jnp.inf)
        buf[0][:, lanes(c)] = s
        buf[1][:, lanes(c)] = jnp.max(s, axis=0, keepdims=True)

    def values(tile, c, buf):
        rows = slice(tile["kt"] * AT_BK, (tile["kt"] + 1) * AT_BK)
        return jnp.dot(vta_s[:, rows], buf[2][:, lanes(c)], preferred_element_type=F32)

    def finish(qi):
        rows = slice(qi * AT_BQ, (qi + 1) * AT_BQ)
        denom = acc_s[qi, DF_HEAD_V:DF_HEAD_V + 1, :]
        o_all = acc_s[qi, :DF_HEAD_V, :] * (1.0 / denom)
        o = o_all[:, :AT_BQ] - lam * o_all[:, AT_BQ:]
        y = (o * lax.rsqrt(jnp.mean(o * o, axis=0, keepdims=True) + EPS)).T
        y = y * og_ref[...] * (1.0 - LAMBDA_INIT)
        y_s[qi] = (y * zg_ref[rows, :].astype(F32)).astype(BF16)

    def project(qi, nc):
        rows = slice(qi * AT_BQ, (qi + 1) * AT_BQ)
        cols = slice(nc * OUT_TN, (nc + 1) * OUT_TN)
        lhs = jnp.concatenate([y_s[qi], dn_ref[rows, :]], axis=1)
        out_ref[rows, cols] = out_ref[rows, cols] + jnp.dot(
            lhs, wout_ref[:, cols], preferred_element_type=F32)

    tiles = _attention_tiles(n_qblocks)
    due = {}
    for c in tiles[0]["chains"]:
        scores(tiles[0], c, bufs[0])
    for n, cur in enumerate(tiles):
        nxt = tiles[n + 1] if n + 1 < len(tiles) else None
        prv = tiles[n - 1] if n > 0 else None
        s_cur, c_cur, p_cur = bufs[n % 2]
        qi = cur["qi"]
        for c in all_chains:
            alpha = None
            if c in cur["chains"]:
                if cur["first"]:
                    m_new = c_cur[:, lanes(c)]
                else:
                    m_old = m_s[qi, :, lanes(c)]
                    m_new = jnp.maximum(m_old, c_cur[:, lanes(c)])
                    alpha = jnp.exp2(m_old - m_new)
                p = jnp.exp2(s_cur[:, lanes(c)] - m_new)
                m_s[qi, :, lanes(c)] = m_new
                p_cur[:, lanes(c)] = p.astype(BF16)
            if nxt is not None and c in nxt["chains"]:
                scores(nxt, c, bufs[(n + 1) % 2])
            if prv is not None and c in prv["chains"]:
                pv = values(prv, c, bufs[(n - 1) % 2])
                pq = prv["qi"]
                acc = pv if prv["first"] else acc_s[pq, :, lanes(c)] + pv
                if pq == qi and alpha is not None:
                    acc = alpha * acc
                acc_s[pq, :, lanes(c)] = acc
        if prv is not None and prv["qi"] != qi:
            finish(prv["qi"])
            for nc in range(D_MODEL // OUT_TN):
                due[n + AT_PROJ_DELAY + nc] = (prv["qi"], nc)
        if n in due:
            project(*due.pop(n))
    last = tiles[-1]
    for c in last["chains"]:
        acc_s[last["qi"], :, lanes(c)] = (acc_s[last["qi"], :, lanes(c)]
                                          + values(last, c, bufs[(len(tiles) - 1) % 2]))
    finish(last["qi"])
    for qi, nc in sorted(due.values()) + [(last["qi"], nc) for nc in range(D_MODEL // OUT_TN)]:
        project(qi, nc)


def _diff_attention(proj, vt, lq1, lk1, lq2, lk2, out_gain, x, mixed_dn, w_heads):
    b, t, _ = proj.shape
    assert AT_BQ == AT_CPM * AT_BK and AT_CW == AT_BK and AT_CPM == 2 and t % AT_BQ == 0
    assert DF_HEADS == DN_HEADS and DF_HEAD_V == DN_HEAD and D_MODEL % OUT_TN == 0
    n_qblocks = t // AT_BQ
    small = lambda n: pl.BlockSpec((1, n), lambda i, h: (0, 0))
    return pl.pallas_call(
        _diff_attn_kernel,
        grid=(b, DF_HEADS),
        in_specs=[
            pl.BlockSpec((None, t, LANES), lambda i, h: (i, 0, _BLK_DF_Q + h)),
            pl.BlockSpec((None, t, LANES), lambda i, h: (i, 0, _BLK_DF_K + h)),
            pl.BlockSpec((None, DF_HEAD_V, t), lambda i, h: (i, h, 0)),
            pl.BlockSpec((None, t, LANES), lambda i, h: (i, 0, _BLK_DF_Z + h)),
            small(DF_HEAD_QK), small(DF_HEAD_QK), small(DF_HEAD_QK), small(DF_HEAD_QK),
            small(DF_HEAD_V),
            pl.BlockSpec((None, t, D_MODEL), lambda i, h: (i, 0, 0)),
            pl.BlockSpec((None, t, LANES), lambda i, h: (i, 0, h)),
            pl.BlockSpec((None, DF_HEAD_V + DN_HEAD, D_MODEL), lambda i, h: (h, 0, 0)),
        ],
        out_specs=pl.BlockSpec((None, t, D_MODEL), lambda i, h: (i, 0, 0)),
        out_shape=jax.ShapeDtypeStruct((b, t, D_MODEL), F32),
        scratch_shapes=[
            pltpu.VMEM((n_qblocks, AT_NQ, LANES), BF16),
            pltpu.VMEM((AT_VROWS, t), BF16),
            pltpu.VMEM((n_qblocks, 1, AT_NQ), F32),
            pltpu.VMEM((n_qblocks, AT_VROWS, AT_NQ), F32),
            pltpu.VMEM((n_qblocks, AT_BQ, DF_HEAD_V), BF16),
            pltpu.VMEM((AT_BK, AT_NQ), F32),
            pltpu.VMEM((AT_BK, AT_NQ), F32),
            pltpu.VMEM((1, AT_NQ), F32),
            pltpu.VMEM((1, AT_NQ), F32),
            pltpu.VMEM((AT_BK, AT_NQ), BF16),
            pltpu.VMEM((AT_BK, AT_NQ), BF16),
        ],
        compiler_params=pltpu.CompilerParams(
            dimension_semantics=("parallel", "arbitrary"), vmem_limit_bytes=VMEM_LIMIT),
        name="diff_attention",
    )(proj, proj, vt, proj, lq1, lk1, lq2, lk2, out_gain, x, mixed_dn, w_heads)


def kernel(x, norm_gain, w_in, conv_w, a_log, dt_bias, dn_out_gain, q_gain, k_gain,
           lambda_q1, lambda_k1, lambda_q2, lambda_k2, df_out_gain, w_out):
    b, t, d = x.shape
    assert d == D_MODEL and norm_gain.shape[0] == 1
    x2d = x.reshape(b * t, d)

    w_main, w_t = _weight_prep(w_in)
    half = jnp.arange(SEG_W, dtype=jnp.int32) // DF_HEAD_QK
    seg = (half[:, None] == half[None, :]).astype(BF16)
    reps = DF_WIDTH // DF_HEAD_QK
    qk_gain = jnp.concatenate([jnp.tile(q_gain, (1, reps)) * (DF_HEAD_QK ** -0.5 * LOG2_E),
                               jnp.tile(k_gain, (1, reps))], axis=0)

    proj, vt, gates_t = _in_projection(x2d, t, norm_gain, w_main, w_t, seg, qk_gain)
    proj = proj.reshape(b, t, PROJ_COLS)
    gates = jnp.swapaxes(gates_t[:, :N_GATES, :], 1, 2)

    zeros4 = jnp.zeros((1, DN_HEADS), F32)
    alog8 = jnp.concatenate([zeros4, a_log], axis=1)
    dtb8 = jnp.concatenate([zeros4, dt_bias], axis=1)
    mixed_dn = _gated_deltanet(proj, gates, conv_w[0], alog8, dtb8, dn_out_gain)

    wo = w_out[0].astype(BF16)
    w_heads = jnp.concatenate([wo[DN_WIDTH:].reshape(DF_HEADS, DF_HEAD_V, d),
                               wo[:DN_WIDTH].reshape(DN_HEADS, DN_HEAD, d)], axis=1)
    return _diff_attention(proj, vt, lambda_q1, lambda_k1, lambda_q2, lambda_k2,
                           df_out_gain, x, mixed_dn, w_heads)
```

```python
import math

import jax
import jax.numpy as jnp
from jax import lax
from jax.experimental import pallas as pl
from jax.experimental.pallas import tpu as pltpu

F32 = jnp.float32
BF16 = jnp.bfloat16

D_MODEL = 1024
DN_HEADS = 4
DN_HEAD = 128
DN_WIDTH = DN_HEADS * DN_HEAD
CONV_WIDTH = 4
CHUNK = 64
DF_HEADS = 4
DF_HEAD_QK = 64
DF_HEAD_V = 128
DF_WIDTH = DF_HEADS * DF_HEAD_V
EPS = 1e-6
LAMBDA_INIT = 0.8 - 0.6 * math.exp(-0.3 * 0)
LOG2_E = math.log2(math.e)

N_GATES = 2 * DN_HEADS
GATE_ROWS = 16
PROJ_COLS = 4 * DN_WIDTH + 3 * DF_WIDTH
LANES = 128
VMEM_LIMIT = 56 * 1024 * 1024

_BLK_DN_Z = 3 * DN_HEADS
_BLK_DF_Q = 4 * DN_HEADS
_BLK_DF_K = _BLK_DF_Q + DF_HEADS
_BLK_DF_Z = _BLK_DF_K + DF_HEADS


def _silu(x):
    return x * jax.nn.sigmoid(x)


def _dot(a, b):
    return jnp.dot(a.astype(BF16), b.astype(BF16), preferred_element_type=F32)


def _dot_nt(a, b):
    return lax.dot_general(a.astype(BF16), b.astype(BF16), (((1,), (1,)), ((), ())),
                           preferred_element_type=F32)


IN_TM = 512
IN_TN = 512
SEG_W = 256


def _inproj_kernel(x_ref, gain_ref, wa_ref, wb_ref, wc_ref, wv_ref, wg_ref, seg_ref, qkg_ref,
                   proj_ref, vt_ref, gate_ref, wt_s):
    @pl.when(pl.program_id(0) == 0)
    def _():
        wt_s[:DF_WIDTH, :] = wv_ref[...].T.astype(BF16)
        wt_s[DF_WIDTH:, :] = wg_ref[...].T[:GATE_ROWS].astype(BF16)

    def w_chunk(j):
        lo = j * IN_TN
        for ref in (wa_ref, wb_ref, wc_ref):
            if lo < ref.shape[1]:
                return ref[:, lo:lo + IN_TN]
            lo -= ref.shape[1]

    x = x_ref[...]
    ms = jnp.mean(x * x, axis=-1, keepdims=True)
    h = (x * lax.rsqrt(ms + EPS) * gain_ref[...]).astype(BF16)
    for j in range(PROJ_COLS // IN_TN):
        cols = slice(j * IN_TN, (j + 1) * IN_TN)
        y = jnp.dot(h, w_chunk(j), preferred_element_type=F32)
        if j in (_BLK_DF_Q * LANES // IN_TN, _BLK_DF_K * LANES // IN_TN):
            which = 0 if j == _BLK_DF_Q * LANES // IN_TN else 1
            sq = (y * y).astype(BF16)
            ssq = jnp.concatenate(
                [jnp.dot(sq[:, i * SEG_W:(i + 1) * SEG_W], seg_ref[...],
                         preferred_element_type=F32) for i in range(IN_TN // SEG_W)], axis=1)
            y = y * lax.rsqrt(ssq * (1.0 / DF_HEAD_QK) + EPS) * qkg_ref[which:which + 1, :]
        elif j in (_BLK_DN_Z * LANES // IN_TN, _BLK_DF_Z * LANES // IN_TN):
            y = _silu(y)
        proj_ref[:, cols] = y.astype(BF16)
    yt = _dot_nt(wt_s[...], h)
    vt_ref[...] = yt[:DF_WIDTH].astype(BF16)
    gate_ref[...] = yt[DF_WIDTH:]


def _in_projection(x2d, t, gain, w_a, w_b, w_c, w_v, w_g, seg, qk_gain):
    m = x2d.shape[0]
    assert DF_WIDTH == IN_TN and t % IN_TM == 0
    assert w_a.shape[1] + w_b.shape[1] + w_c.shape[1] == PROJ_COLS
    tiles_per_seq = t // IN_TM
    const = lambda shape: pl.BlockSpec(shape, lambda i: (0, 0))
    return pl.pallas_call(
        _inproj_kernel,
        grid=(m // IN_TM,),
        in_specs=[
            pl.BlockSpec((IN_TM, D_MODEL), lambda i: (i, 0)),
            const((1, D_MODEL)),
            const(w_a.shape), const(w_b.shape), const(w_c.shape),
            const((D_MODEL, DF_WIDTH)),
            const((D_MODEL, LANES)),
            const((SEG_W, SEG_W)),
            const((2, DF_WIDTH)),
        ],
        out_specs=[
            pl.BlockSpec((IN_TM, PROJ_COLS), lambda i: (i, 0)),
            pl.BlockSpec((None, DF_WIDTH, IN_TM),
                         lambda i: (i // tiles_per_seq, 0, i % tiles_per_seq)),
            pl.BlockSpec((None, GATE_ROWS, IN_TM),
                         lambda i: (i // tiles_per_seq, 0, i % tiles_per_seq)),
        ],
        out_shape=[
            jax.ShapeDtypeStruct((m, PROJ_COLS), BF16),
            jax.ShapeDtypeStruct((m // t, DF_WIDTH, t), BF16),
            jax.ShapeDtypeStruct((m // t, GATE_ROWS, t), F32),
        ],
        scratch_shapes=[pltpu.VMEM((DF_WIDTH + GATE_ROWS, D_MODEL), BF16)],
        compiler_params=pltpu.CompilerParams(
            dimension_semantics=("arbitrary",), vmem_limit_bytes=VMEM_LIMIT),
        name="in_projection",
    )(x2d, gain, w_a, w_b, w_c, w_v, w_g, seg, qk_gain)


QKV_COLS = 3 * DN_WIDTH
PREP_CHUNKS = 4


def _gdn_kernel(qkv_ref, zg_ref, gate_ref, convw_ref, alog_ref, dtb_ref, gain_ref,
                out_ref, q_r, k_r, v_r, u_r, wq_r, aqk_r, kdt_r, gate_s, state_s, stage_s):
    t = qkv_ref.shape[0]
    n_chunks = t // CHUNK
    group_rows = PREP_CHUNKS * CHUNK

    def group_base(g):
        return g * group_rows if isinstance(g, int) else pl.multiple_of(g * group_rows, group_rows)

    def conv(g, slot):
        r0 = group_base(g)
        for cg in range(QKV_COLS // LANES):
            cols = slice(cg * LANES, (cg + 1) * LANES)
            cur = qkv_ref[pl.ds(r0, group_rows), cols].astype(F32)
            if isinstance(g, int) and g == 0:
                prev = jnp.zeros((8, LANES), F32)
            else:
                prev = qkv_ref[pl.ds(pl.multiple_of(r0 - 16, 16), 16), cols].astype(F32)[8:]
            stage_s[cg, :8, :] = prev
            stage_s[cg, 8:, :] = cur
            w = convw_ref[:, cols]
            acc = cur * w[CONV_WIDTH - 1:CONV_WIDTH]
            for j in range(1, CONV_WIDTH):
                acc = acc + (stage_s[cg, 8 - j:8 - j + group_rows, :]
                             * w[CONV_WIDTH - 1 - j:CONV_WIDTH - j])
            y = _silu(acc)
            hcols = slice((cg % DN_HEADS) * LANES, (cg % DN_HEADS + 1) * LANES)
            if cg < 2 * DN_HEADS:
                inv_norm = lax.rsqrt(jnp.sum(y * y, axis=-1, keepdims=True) + EPS)
                if cg < DN_HEADS:
                    q_r[slot, :, hcols] = y * (inv_norm * (DN_HEAD ** -0.5))
                else:
                    k_r[slot, :, hcols] = y * inv_norm
            else:
                v_r[slot, :, hcols] = y
            yield

    gate = gate_ref[...]
    beta = jax.nn.sigmoid(gate)
    g = -jnp.exp(alog_ref[...]) * jax.nn.softplus(gate + dtb_ref[...])
    pos = lax.broadcasted_iota(jnp.int32, g.shape, 0) % CHUNK
    s = 1
    while s < CHUNK:
        g = g + jnp.where(pos >= s, pltpu.roll(g, s, 0), 0.0)
        s *= 2
    lane = lax.broadcasted_iota(jnp.int32, g.shape, 1)
    gate_s[...] = jnp.where(lane < DN_HEADS, beta, g)

    row = lax.broadcasted_iota(jnp.int32, (CHUNK, 2 * CHUNK), 0)
    col = lax.broadcasted_iota(jnp.int32, (CHUNK, 2 * CHUNK), 1)
    eye_f = (row == col).astype(F32)
    lower_f = (row >= col).astype(F32)
    strict_f = (row > col).astype(F32)
    upper_f = (col >= CHUNK).astype(F32)
    eye_hi = (col - CHUNK == row).astype(F32)
    zero_rows = jnp.zeros((CHUNK, DN_HEAD), BF16)

    items = [(cc, h) for cc in range(PREP_CHUNKS) for h in range(DN_HEADS)]
    heads = range(DN_HEADS)
    gain = gain_ref[...]
    hcols = [slice(h * LANES, (h + 1) * LANES) for h in heads]
    hcols_c = [slice(h * LANES, h * LANES + CHUNK) for h in heads]

    def halves(fn, seq):
        out = []
        mid = len(seq) // 2
        for part in (seq[:mid], seq[mid:]):
            out.extend(fn(x) for x in part)
            yield out

    def prep(g, slot):
        base = group_base(g)
        gates = [gate_s[pl.ds(base + cc * CHUNK, CHUNK), :] for cc in range(PREP_CHUNKS)]
        ks, lhs, decay, rhs, qe, kdt = [], [], [], [], [], []
        for cc, h in items:
            rows = slice(cc * CHUNK, (cc + 1) * CHUNK)
            q = q_r[slot, rows, hcols[h]]
            k = k_r[slot, rows, hcols[h]]
            v = v_r[slot, rows, hcols[h]]
            beta_b = jnp.broadcast_to(gates[cc][:, h:h + 1], (CHUNK, DN_HEAD))
            gcb = jnp.broadcast_to(gates[cc][:, DN_HEADS + h:DN_HEADS + h + 1], (CHUNK, DN_HEAD))
            gr = jnp.sum(gcb * eye_f, axis=0, keepdims=True)
            decay.append(jnp.exp(jnp.minimum(gcb - gr, 0.0)))
            g_last = gcb[CHUNK - 1:CHUNK, :]
            eg = jnp.exp(gcb)
            kb = k * beta_b
            ks.append(jnp.concatenate([k.astype(BF16), zero_rows], axis=0))
            lhs.append(jnp.concatenate([kb, q], axis=0).astype(BF16))
            rhs.append(jnp.concatenate([v * beta_b, kb * eg], axis=1).astype(BF16))
            qe.append((q * eg).astype(BF16))
            kdt.append((k * jnp.exp(g_last - gcb)).T.astype(BF16))
        yield

        kq = None
        for kq in halves(lambda ab: _dot_nt(*ab), list(zip(lhs, ks))):
            yield
        m = [x[:CHUNK] * (d * strict_f) for x, d in zip(kq, decay)]
        a_qk = [(x[CHUNK:, :CHUNK] * (d * lower_f)[:, :CHUNK]).astype(BF16)
                for x, d in zip(kq, decay)]

        w = None
        for w in halves(lambda x: _dot(x[:, :CHUNK], x - eye_hi) + eye_hi, m):
            yield
        for _ in range(5):
            prev = w
            for w in halves(lambda x: _dot(x[:, :CHUNK], x) + x * upper_f, prev):
                yield
        sol = None
        for sol in halves(lambda xb: _dot(xb[0][:, CHUNK:], xb[1]), list(zip(w, rhs))):
            yield

        for n, (cc, h) in enumerate(items):
            rows = slice(cc * CHUNK, (cc + 1) * CHUNK)
            rows_w = slice(2 * cc * CHUNK, (2 * cc + 1) * CHUNK)
            rows_q = slice((2 * cc + 1) * CHUNK, (2 * cc + 2) * CHUNK)
            rows_t = slice(2 * cc * CHUNK, (2 * cc + 2) * CHUNK)
            u_r[slot, rows, hcols[h]] = sol[n][:, :DN_HEAD]
            wq_r[slot, rows_w, hcols[h]] = sol[n][:, DN_HEAD:].astype(BF16)
            wq_r[slot, rows_q, hcols[h]] = qe[n]
            aqk_r[slot, rows, hcols_c[h]] = a_qk[n]
            kdt_r[slot, rows_t, hcols_c[h]] = kdt[n]
        yield

    def rec(g, slot):
        base = group_base(g)
        for cc in range(PREP_CHUNKS):
            rows = slice(cc * CHUNK, (cc + 1) * CHUNK)
            rows2 = slice(2 * cc * CHUNK, (2 * cc + 2) * CHUNK)
            trows = pl.ds(base + cc * CHUNK, CHUNK)
            st = [state_s[h] for h in heads]
            r = [jnp.dot(wq_r[slot, rows2, hcols[h]], st[h].astype(BF16),
                         preferred_element_type=F32) for h in heads]
            yield
            v_new = [(u_r[slot, rows, hcols[h]] - r[h][:CHUNK]).astype(BF16) for h in heads]
            upd = [jnp.dot(kdt_r[slot, rows2, hcols_c[h]], v_new[h], preferred_element_type=F32)
                   for h in heads]
            tail = gate_s[pl.ds(base + (cc + 1) * CHUNK - 8, 8), :]
            decay_last = jnp.exp(tail[7:8, :])
            for h in heads:
                state_s[h] = st[h] * decay_last[:, DN_HEADS + h:DN_HEADS + h + 1] + upd[h]
            yield
            o = [r[h][CHUNK:] + jnp.dot(aqk_r[slot, rows, hcols_c[h]], v_new[h],
                                        preferred_element_type=F32) for h in heads]
            for h in heads:
                y = o[h] * lax.rsqrt(jnp.mean(o[h] * o[h], axis=-1, keepdims=True) + EPS) * gain
                gate = zg_ref[trows, hcols[h]].astype(F32)
                out_ref[trows, hcols[h]] = (y * gate).astype(out_ref.dtype)
            yield

    def run_interleaved(*gens):
        gens = list(gens)
        while gens:
            for gen in list(gens):
                if next(gen, StopIteration) is StopIteration:
                    gens.remove(gen)

    n_groups = n_chunks // PREP_CHUNKS
    state_s[...] = jnp.zeros_like(state_s)
    run_interleaved(conv(0, 0))
    run_interleaved(prep(0, 0), conv(1, 1))

    def pair_body(j, carry):
        run_interleaved(rec(2 * j, 0), prep(2 * j + 1, 1), conv(2 * j + 2, 0))
        run_interleaved(rec(2 * j + 1, 1), prep(2 * j + 2, 0), conv(2 * j + 3, 1))
        return carry

    lax.fori_loop(0, (n_groups - 2) // 2, pair_body, 0)
    run_interleaved(rec(n_groups - 2, 0), prep(n_groups - 1, 1))
    run_interleaved(rec(n_groups - 1, 1))


def _gated_deltanet(proj, gates, conv_w, alog8, dtb8, out_gain):
    b, t, _ = proj.shape
    group_rows = PREP_CHUNKS * CHUNK
    assert t % (2 * group_rows) == 0 and t // group_rows >= 4
    ring = lambda rows, dtype: pltpu.VMEM((2, rows, DN_WIDTH), dtype)
    return pl.pallas_call(
        _gdn_kernel,
        grid=(b,),
        in_specs=[
            pl.BlockSpec((None, t, QKV_COLS), lambda i: (i, 0, 0)),
            pl.BlockSpec((None, t, DN_WIDTH), lambda i: (i, 0, QKV_COLS // DN_WIDTH)),
            pl.BlockSpec((None, t, N_GATES), lambda i: (i, 0, 0)),
            pl.BlockSpec((CONV_WIDTH, QKV_COLS), lambda i: (0, 0)),
            pl.BlockSpec((1, N_GATES), lambda i: (0, 0)),
            pl.BlockSpec((1, N_GATES), lambda i: (0, 0)),
            pl.BlockSpec((1, DN_HEAD), lambda i: (0, 0)),
        ],
        out_specs=pl.BlockSpec((None, t, DN_WIDTH), lambda i: (i, 0, 0)),
        out_shape=jax.ShapeDtypeStruct((b, t, DN_WIDTH), BF16),
        scratch_shapes=[
            ring(group_rows, F32),
            ring(group_rows, F32),
            ring(group_rows, F32),
            ring(group_rows, F32),
            ring(2 * group_rows, BF16),
            ring(group_rows, BF16),
            ring(2 * group_rows, BF16),
            pltpu.VMEM((t, N_GATES), F32),
            pltpu.VMEM((DN_HEADS, DN_HEAD, DN_HEAD), F32),
            pltpu.VMEM((QKV_COLS // LANES, group_rows + 8, LANES), F32),
        ],
        compiler_params=pltpu.CompilerParams(
            dimension_semantics=("parallel",), vmem_limit_bytes=VMEM_LIMIT),
        name="gated_deltanet",
    )(proj, proj, gates, conv_w, alog8, dtb8, out_gain)


AT_BQ = 512
AT_BK = 256
AT_CW = 256
AT_NQ = 2 * AT_BQ
AT_CHAINS = AT_NQ // AT_CW
AT_CPM = AT_BQ // AT_CW
AT_VROWS = DF_HEAD_V + 16
AT_PROJ_DELAY = 2
OUT_TN = 256


def _attention_tiles(n_qblocks):
    all_chains = tuple(range(AT_CHAINS))
    lo = tuple(c for c in all_chains if c % AT_CPM == 0)
    hi = tuple(c for c in all_chains if c % AT_CPM == 1)
    tiles = []
    for qi in range(n_qblocks):
        for kt in range(AT_CPM * qi + AT_CPM):
            d = kt - AT_CPM * qi
            chains, masked = (all_chains, ()) if d < 0 else (all_chains, lo) if d == 0 else (hi, hi)
            tiles.append(dict(qi=qi, kt=kt, chains=chains, masked=masked, first=kt == 0))
    return tiles


def _diff_attn_kernel(q_ref, k_ref, vt_ref, zg_ref, lq1_ref, lk1_ref, lq2_ref, lk2_ref,
                      og_ref, x_ref, dn_ref, wodf_ref, wodn_ref, out_ref,
                      qq_s, vta_s, m_s, acc_s, y_s, wo_s, s_a, s_b, c_a, c_b, p_a, p_b):
    t = q_ref.shape[0]
    n_qblocks = t // AT_BQ
    all_chains = tuple(range(AT_CHAINS))

    @pl.when(pl.program_id(1) == 0)
    def _():
        out_ref[...] = x_ref[...]

    def lanes(c):
        return slice(c * AT_CW, (c + 1) * AT_CW)

    first = lax.broadcasted_iota(jnp.int32, (AT_BQ, LANES), 1) < DF_HEAD_QK
    zero = jnp.zeros((AT_BQ, LANES), BF16)
    for qi in range(n_qblocks):
        blk = q_ref[qi * AT_BQ:(qi + 1) * AT_BQ, :]
        qq_s[qi, :AT_BQ, :] = jnp.where(first, blk, zero)
        qq_s[qi, AT_BQ:, :] = jnp.where(first, zero, blk)

    wo_s[:DF_HEAD_V, :] = wodf_ref[...].astype(BF16)
    wo_s[DF_HEAD_V:, :] = wodn_ref[...].astype(BF16)

    vta_s[:DF_HEAD_V, :] = vt_ref[...]
    vta_s[DF_HEAD_V:, :] = jnp.ones((AT_VROWS - DF_HEAD_V, t), BF16)

    lam = (jnp.exp(jnp.sum(lq1_ref[...] * lk1_ref[...], axis=-1, keepdims=True))
           - jnp.exp(jnp.sum(lq2_ref[...] * lk2_ref[...], axis=-1, keepdims=True))
           + LAMBDA_INIT)

    krow = lax.broadcasted_iota(jnp.int32, (AT_BK, AT_CW), 0)
    qcol = lax.broadcasted_iota(jnp.int32, (AT_BK, AT_CW), 1)
    causal = krow <= qcol
    bufs = ((s_a, c_a, p_a), (s_b, c_b, p_b))

    def scores(tile, c, buf):
        rows = slice(tile["kt"] * AT_BK, (tile["kt"] + 1) * AT_BK)
        s = _dot_nt(k_ref[rows, :], qq_s[tile["qi"], lanes(c), :])
        if c in tile["masked"]:
            s = jnp.where(causal, s, -jnp.inf)
        buf[0][:, lanes(c)] = s
        buf[1][:, lanes(c)] = jnp.max(s, axis=0, keepdims=True)

    def values(tile, c, buf):
        rows = slice(tile["kt"] * AT_BK, (tile["kt"] + 1) * AT_BK)
        return jnp.dot(vta_s[:, rows], buf[2][:, lanes(c)], preferred_element_type=F32)

    def finish(qi):
        rows = slice(qi * AT_BQ, (qi + 1) * AT_BQ)
        denom = acc_s[qi, DF_HEAD_V:DF_HEAD_V + 1, :]
        o_all = acc_s[qi, :DF_HEAD_V, :] * (1.0 / denom)
        o = o_all[:, :AT_BQ] - lam * o_all[:, AT_BQ:]
        y = (o * lax.rsqrt(jnp.mean(o * o, axis=0, keepdims=True) + EPS)).T
        y = y * og_ref[...] * (1.0 - LAMBDA_INIT)
        y_s[qi] = (y * zg_ref[rows, :].astype(F32)).astype(BF16)

    def project(qi):
        rows = slice(qi * AT_BQ, (qi + 1) * AT_BQ)
        lhs = jnp.concatenate([y_s[qi], dn_ref[rows, :]], axis=1)
        for nc in range(D_MODEL // OUT_TN):
            cols = slice(nc * OUT_TN, (nc + 1) * OUT_TN)
            out_ref[rows, cols] = out_ref[rows, cols] + jnp.dot(
                lhs, wo_s[:, cols], preferred_element_type=F32)

    tiles = _attention_tiles(n_qblocks)
    due = {}
    for c in tiles[0]["chains"]:
        scores(tiles[0], c, bufs[0])
    for n, cur in enumerate(tiles):
        nxt = tiles[n + 1] if n + 1 < len(tiles) else None
        prv = tiles[n - 1] if n > 0 else None
        s_cur, c_cur, p_cur = bufs[n % 2]
        qi = cur["qi"]
        for c in all_chains:
            alpha = None
            if c in cur["chains"]:
                if cur["first"]:
                    m_new = c_cur[:, lanes(c)]
                else:
                    m_old = m_s[qi, :, lanes(c)]
                    m_new = jnp.maximum(m_old, c_cur[:, lanes(c)])
                    alpha = jnp.exp2(m_old - m_new)
                p = jnp.exp2(s_cur[:, lanes(c)] - m_new)
                m_s[qi, :, lanes(c)] = m_new
                p_cur[:, lanes(c)] = p.astype(BF16)
            if nxt is not None and c in nxt["chains"]:
                scores(nxt, c, bufs[(n + 1) % 2])
            if prv is not None and c in prv["chains"]:
                pv = values(prv, c, bufs[(n - 1) % 2])
                pq = prv["qi"]
                acc = pv if prv["first"] else acc_s[pq, :, lanes(c)] + pv
                if pq == qi and alpha is not None:
                    acc = alpha * acc
                acc_s[pq, :, lanes(c)] = acc
        if prv is not None and prv["qi"] != qi:
            finish(prv["qi"])
            due[n + AT_PROJ_DELAY] = prv["qi"]
        if n in due:
            project(due.pop(n))
    last = tiles[-1]
    for c in last["chains"]:
        acc_s[last["qi"], :, lanes(c)] = (acc_s[last["qi"], :, lanes(c)]
                                          + values(last, c, bufs[(len(tiles) - 1) % 2]))
    finish(last["qi"])
    for qi in sorted(due.values()) + [last["qi"]]:
        project(qi)


def _diff_attention(proj, vt, lq1, lk1, lq2, lk2, out_gain, x, mixed_dn, w_out):
    b, t, _ = proj.shape
    assert AT_BQ == AT_CPM * AT_BK and AT_CW == AT_BK and AT_CPM == 2 and t % AT_BQ == 0
    assert DF_HEADS == DN_HEADS and DF_HEAD_V == DN_HEAD and D_MODEL % OUT_TN == 0
    n_qblocks = t // AT_BQ
    small = lambda n: pl.BlockSpec((1, n), lambda i, h: (0, 0))
    return pl.pallas_call(
        _diff_attn_kernel,
        grid=(b, DF_HEADS),
        in_specs=[
            pl.BlockSpec((None, t, LANES), lambda i, h: (i, 0, _BLK_DF_Q + h)),
            pl.BlockSpec((None, t, LANES), lambda i, h: (i, 0, _BLK_DF_K + h)),
            pl.BlockSpec((None, DF_HEAD_V, t), lambda i, h: (i, h, 0)),
            pl.BlockSpec((None, t, LANES), lambda i, h: (i, 0, _BLK_DF_Z + h)),
            small(DF_HEAD_QK), small(DF_HEAD_QK), small(DF_HEAD_QK), small(DF_HEAD_QK),
            small(DF_HEAD_V),
            pl.BlockSpec((None, t, D_MODEL), lambda i, h: (i, 0, 0)),
            pl.BlockSpec((None, t, LANES), lambda i, h: (i, 0, h)),
            pl.BlockSpec((None, DF_HEAD_V, D_MODEL), lambda i, h: (0, DN_HEADS + h, 0)),
            pl.BlockSpec((None, DN_HEAD, D_MODEL), lambda i, h: (0, h, 0)),
        ],
        out_specs=pl.BlockSpec((None, t, D_MODEL), lambda i, h: (i, 0, 0)),
        out_shape=jax.ShapeDtypeStruct((b, t, D_MODEL), F32),
        scratch_shapes=[
            pltpu.VMEM((n_qblocks, AT_NQ, LANES), BF16),
            pltpu.VMEM((AT_VROWS, t), BF16),
            pltpu.VMEM((n_qblocks, 1, AT_NQ), F32),
            pltpu.VMEM((n_qblocks, AT_VROWS, AT_NQ), F32),
            pltpu.VMEM((n_qblocks, AT_BQ, DF_HEAD_V), BF16),
            pltpu.VMEM((DF_HEAD_V + DN_HEAD, D_MODEL), BF16),
            pltpu.VMEM((AT_BK, AT_NQ), F32),
            pltpu.VMEM((AT_BK, AT_NQ), F32),
            pltpu.VMEM((1, AT_NQ), F32),
            pltpu.VMEM((1, AT_NQ), F32),
            pltpu.VMEM((AT_BK, AT_NQ), BF16),
            pltpu.VMEM((AT_BK, AT_NQ), BF16),
        ],
        compiler_params=pltpu.CompilerParams(
            dimension_semantics=("parallel", "arbitrary"), vmem_limit_bytes=VMEM_LIMIT),
        name="diff_attention",
    )(proj, proj, vt, proj, lq1, lk1, lq2, lk2, out_gain, x, mixed_dn, w_out, w_out)


def kernel(x, norm_gain, w_in, conv_w, a_log, dt_bias, dn_out_gain, q_gain, k_gain,
           lambda_q1, lambda_k1, lambda_q2, lambda_k2, df_out_gain, w_out):
    b, t, d = x.shape
    assert d == D_MODEL and norm_gain.shape[0] == 1
    x2d = x.reshape(b * t, d)

    w = w_in[0]
    gate_lo = 4 * DN_WIDTH
    df_lo = gate_lo + N_GATES
    w_a = w[:, :gate_lo].astype(BF16)
    w_b = w[:, df_lo:df_lo + 2 * DF_WIDTH].astype(BF16)
    w_c = w[:, df_lo + 3 * DF_WIDTH:].astype(BF16)
    w_v = w[:, df_lo + 2 * DF_WIDTH:df_lo + 3 * DF_WIDTH]
    w_g = w[:, gate_lo:gate_lo + LANES]
    half = jnp.arange(SEG_W, dtype=jnp.int32) // DF_HEAD_QK
    seg = (half[:, None] == half[None, :]).astype(BF16)
    reps = DF_WIDTH // DF_HEAD_QK
    qk_gain = jnp.concatenate([jnp.tile(q_gain, (1, reps)) * (DF_HEAD_QK ** -0.5 * LOG2_E),
                               jnp.tile(k_gain, (1, reps))], axis=0)

    proj, vt, gates_t = _in_projection(x2d, t, norm_gain, w_a, w_b, w_c, w_v, w_g, seg, qk_gain)
    proj = proj.reshape(b, t, PROJ_COLS)
    gates = jnp.swapaxes(gates_t[:, :N_GATES, :], 1, 2)

    zeros4 = jnp.zeros((1, DN_HEADS), F32)
    alog8 = jnp.concatenate([zeros4, a_log], axis=1)
    dtb8 = jnp.concatenate([zeros4, dt_bias], axis=1)
    mixed_dn = _gated_deltanet(proj, gates, conv_w[0], alog8, dtb8, dn_out_gain)

    return _diff_attention(proj, vt, lambda_q1, lambda_k1, lambda_q2, lambda_k2,
                           df_out_gain, x, mixed_dn, w_out)
```

```python
import math

import jax
import jax.numpy as jnp
from jax import lax
from jax.experimental import pallas as pl
from jax.experimental.pallas import tpu as pltpu

F32 = jnp.float32
BF16 = jnp.bfloat16

D_MODEL = 1024
DN_HEADS = 4
DN_HEAD = 128
DN_WIDTH = DN_HEADS * DN_HEAD
CONV_WIDTH = 4
CHUNK = 64
DF_HEADS = 4
DF_HEAD_QK = 64
DF_HEAD_V = 128
DF_WIDTH = DF_HEADS * DF_HEAD_V
EPS = 1e-6
LAMBDA_INIT = 0.8 - 0.6 * math.exp(-0.3 * 0)
LOG2_E = math.log2(math.e)

N_GATES = 2 * DN_HEADS
GATE_ROWS = 16
PROJ_COLS = 4 * DN_WIDTH + 3 * DF_WIDTH
LANES = 128
VMEM_LIMIT = 56 * 1024 * 1024

_BLK_DN_Z = 3 * DN_HEADS
_BLK_DF_Q = 4 * DN_HEADS
_BLK_DF_K = _BLK_DF_Q + DF_HEADS
_BLK_DF_Z = _BLK_DF_K + DF_HEADS


def _silu(x):
    return x * jax.nn.sigmoid(x)


def _dot(a, b):
    return jnp.dot(a.astype(BF16), b.astype(BF16), preferred_element_type=F32)


def _dot_nt(a, b):
    return lax.dot_general(a.astype(BF16), b.astype(BF16), (((1,), (1,)), ((), ())),
                           preferred_element_type=F32)


IN_TM = 512
IN_TN = 512
SEG_W = 256


def _inproj_kernel(x_ref, gain_ref, w_ref, wt_ref, seg_ref, qkg_ref, alog_ref, dtb_ref,
                   proj_ref, vt_ref, gate_ref):
    x = x_ref[...]
    ms = jnp.mean(x * x, axis=-1, keepdims=True)
    h = (x * lax.rsqrt(ms + EPS) * gain_ref[...]).astype(BF16)

    yt = _dot_nt(wt_ref[...], h)
    vt_ref[...] = yt[:DF_WIDTH].astype(BF16)
    pre = yt[DF_WIDTH:]
    g = -jnp.exp(alog_ref[...]) * jax.nn.softplus(pre + dtb_ref[...])
    pos = lax.broadcasted_iota(jnp.int32, g.shape, 1) % CHUNK
    s = 1
    while s < CHUNK:
        g = g + jnp.where(pos >= s, pltpu.roll(g, s, 1), 0.0)
        s *= 2
    row = lax.broadcasted_iota(jnp.int32, g.shape, 0)
    gate_ref[...] = jnp.where(row < DN_HEADS, jax.nn.sigmoid(pre), g)

    for j in range(PROJ_COLS // IN_TN):
        cols = slice(j * IN_TN, (j + 1) * IN_TN)
        y = jnp.dot(h, w_ref[:, cols], preferred_element_type=F32)
        if j in (_BLK_DF_Q * LANES // IN_TN, _BLK_DF_K * LANES // IN_TN):
            which = 0 if j == _BLK_DF_Q * LANES // IN_TN else 1
            sq = (y * y).astype(BF16)
            ssq = jnp.concatenate(
                [jnp.dot(sq[:, i * SEG_W:(i + 1) * SEG_W], seg_ref[...],
                         preferred_element_type=F32) for i in range(IN_TN // SEG_W)], axis=1)
            y = y * lax.rsqrt(ssq * (1.0 / DF_HEAD_QK) + EPS) * qkg_ref[which:which + 1, :]
        elif j in (_BLK_DN_Z * LANES // IN_TN, _BLK_DF_Z * LANES // IN_TN):
            y = _silu(y)
        proj_ref[:, cols] = y.astype(BF16)


def _in_projection(x2d, t, gain, w_main, w_t, seg, qk_gain, alog_col, dtb_col):
    m = x2d.shape[0]
    assert DF_WIDTH == IN_TN and t % IN_TM == 0 and IN_TM % CHUNK == 0
    tiles_per_seq = t // IN_TM
    const = lambda shape: pl.BlockSpec(shape, lambda i: (0, 0))
    return pl.pallas_call(
        _inproj_kernel,
        grid=(m // IN_TM,),
        in_specs=[
            pl.BlockSpec((IN_TM, D_MODEL), lambda i: (i, 0)),
            const((1, D_MODEL)),
            const((D_MODEL, PROJ_COLS)),
            const((DF_WIDTH + GATE_ROWS, D_MODEL)),
            const((SEG_W, SEG_W)),
            const((2, DF_WIDTH)),
            const((GATE_ROWS, 1)),
            const((GATE_ROWS, 1)),
        ],
        out_specs=[
            pl.BlockSpec((IN_TM, PROJ_COLS), lambda i: (i, 0)),
            pl.BlockSpec((None, DF_WIDTH, IN_TM),
                         lambda i: (i // tiles_per_seq, 0, i % tiles_per_seq)),
            pl.BlockSpec((None, GATE_ROWS, IN_TM),
                         lambda i: (i // tiles_per_seq, 0, i % tiles_per_seq)),
        ],
        out_shape=[
            jax.ShapeDtypeStruct((m, PROJ_COLS), BF16),
            jax.ShapeDtypeStruct((m // t, DF_WIDTH, t), BF16),
            jax.ShapeDtypeStruct((m // t, GATE_ROWS, t), F32),
        ],
        compiler_params=pltpu.CompilerParams(
            dimension_semantics=("parallel",), vmem_limit_bytes=VMEM_LIMIT),
        name="in_projection",
    )(x2d, gain, w_main, w_t, seg, qk_gain, alog_col, dtb_col)


QKV_COLS = 3 * DN_WIDTH
PREP_CHUNKS = 4


def _gdn_kernel(qkv_ref, zg_ref, gate_ref, convw_ref, gain_ref,
                out_ref, q_r, k_r, v_r, u_r, wq_r, aqk_r, kdt_r, state_s, stage_s):
    t = qkv_ref.shape[0]
    n_chunks = t // CHUNK
    group_rows = PREP_CHUNKS * CHUNK

    def group_base(g):
        return g * group_rows if isinstance(g, int) else pl.multiple_of(g * group_rows, group_rows)

    def conv(g, slot):
        r0 = group_base(g)
        for cg in range(QKV_COLS // LANES):
            cols = slice(cg * LANES, (cg + 1) * LANES)
            cur = qkv_ref[pl.ds(r0, group_rows), cols].astype(F32)
            if isinstance(g, int) and g == 0:
                prev = jnp.zeros((8, LANES), F32)
            else:
                prev = qkv_ref[pl.ds(pl.multiple_of(r0 - 16, 16), 16), cols].astype(F32)[8:]
            stage_s[cg, :8, :] = prev
            stage_s[cg, 8:, :] = cur
            w = convw_ref[:, cols]
            acc = cur * w[CONV_WIDTH - 1:CONV_WIDTH]
            for j in range(1, CONV_WIDTH):
                acc = acc + (stage_s[cg, 8 - j:8 - j + group_rows, :]
                             * w[CONV_WIDTH - 1 - j:CONV_WIDTH - j])
            y = _silu(acc)
            hcols = slice((cg % DN_HEADS) * LANES, (cg % DN_HEADS + 1) * LANES)
            if cg < 2 * DN_HEADS:
                inv_norm = lax.rsqrt(jnp.sum(y * y, axis=-1, keepdims=True) + EPS)
                if cg < DN_HEADS:
                    q_r[slot, :, hcols] = y * (inv_norm * (DN_HEAD ** -0.5))
                else:
                    k_r[slot, :, hcols] = y * inv_norm
            else:
                v_r[slot, :, hcols] = y
            yield

    row = lax.broadcasted_iota(jnp.int32, (CHUNK, 2 * CHUNK), 0)
    col = lax.broadcasted_iota(jnp.int32, (CHUNK, 2 * CHUNK), 1)
    eye_f = (row == col).astype(F32)
    lower_f = (row >= col).astype(F32)
    strict_f = (row > col).astype(F32)
    upper_f = (col >= CHUNK).astype(F32)
    eye_hi = (col - CHUNK == row).astype(F32)
    zero_rows = jnp.zeros((CHUNK, DN_HEAD), BF16)

    items = [(cc, h) for cc in range(PREP_CHUNKS) for h in range(DN_HEADS)]
    heads = range(DN_HEADS)
    gain = gain_ref[...]
    hcols = [slice(h * LANES, (h + 1) * LANES) for h in heads]
    hcols_c = [slice(h * LANES, h * LANES + CHUNK) for h in heads]

    def halves(fn, seq):
        out = []
        mid = len(seq) // 2
        for part in (seq[:mid], seq[mid:]):
            out.extend(fn(x) for x in part)
            yield out

    def prep(g, slot):
        base = group_base(g)
        gates = [gate_ref[pl.ds(base + cc * CHUNK, CHUNK), :] for cc in range(PREP_CHUNKS)]
        ks, lhs, decay, rhs, qe, kdt = [], [], [], [], [], []
        for cc, h in items:
            rows = slice(cc * CHUNK, (cc + 1) * CHUNK)
            q = q_r[slot, rows, hcols[h]]
            k = k_r[slot, rows, hcols[h]]
            v = v_r[slot, rows, hcols[h]]
            beta_b = jnp.broadcast_to(gates[cc][:, h:h + 1], (CHUNK, DN_HEAD))
            gcb = jnp.broadcast_to(gates[cc][:, DN_HEADS + h:DN_HEADS + h + 1], (CHUNK, DN_HEAD))
            gr = jnp.sum(gcb * eye_f, axis=0, keepdims=True)
            decay.append(jnp.exp(jnp.minimum(gcb - gr, 0.0)))
            g_last = gcb[CHUNK - 1:CHUNK, :]
            eg = jnp.exp(gcb)
            kb = k * beta_b
            ks.append(jnp.concatenate([k.astype(BF16), zero_rows], axis=0))
            lhs.append(jnp.concatenate([kb, q], axis=0).astype(BF16))
            rhs.append(jnp.concatenate([v * beta_b, kb * eg], axis=1).astype(BF16))
            qe.append((q * eg).astype(BF16))
            kdt.append((k * jnp.exp(g_last - gcb)).T.astype(BF16))
        yield

        kq = None
        for kq in halves(lambda ab: _dot_nt(*ab), list(zip(lhs, ks))):
            yield
        m = [x[:CHUNK] * (d * strict_f) for x, d in zip(kq, decay)]
        a_qk = [(x[CHUNK:, :CHUNK] * (d * lower_f)[:, :CHUNK]).astype(BF16)
                for x, d in zip(kq, decay)]

        w = None
        for w in halves(lambda x: _dot(x[:, :CHUNK], x - eye_hi) + eye_hi, m):
            yield
        for _ in range(5):
            prev = w
            for w in halves(lambda x: _dot(x[:, :CHUNK], x) + x * upper_f, prev):
                yield
        sol = None
        for sol in halves(lambda xb: _dot(xb[0][:, CHUNK:], xb[1]), list(zip(w, rhs))):
            yield

        for n, (cc, h) in enumerate(items):
            rows = slice(cc * CHUNK, (cc + 1) * CHUNK)
            rows_w = slice(2 * cc * CHUNK, (2 * cc + 1) * CHUNK)
            rows_q = slice((2 * cc + 1) * CHUNK, (2 * cc + 2) * CHUNK)
            rows_t = slice(2 * cc * CHUNK, (2 * cc + 2) * CHUNK)
            u_r[slot, rows, hcols[h]] = sol[n][:, :DN_HEAD]
            wq_r[slot, rows_w, hcols[h]] = sol[n][:, DN_HEAD:].astype(BF16)
            wq_r[slot, rows_q, hcols[h]] = qe[n]
            aqk_r[slot, rows, hcols_c[h]] = a_qk[n]
            kdt_r[slot, rows_t, hcols_c[h]] = kdt[n]
        yield

    def rec(g, slot):
        base = group_base(g)
        for cc in range(PREP_CHUNKS):
            rows = slice(cc * CHUNK, (cc + 1) * CHUNK)
            rows2 = slice(2 * cc * CHUNK, (2 * cc + 2) * CHUNK)
            trows = pl.ds(base + cc * CHUNK, CHUNK)
            st = [state_s[h] for h in heads]
            r = [jnp.dot(wq_r[slot, rows2, hcols[h]], st[h].astype(BF16),
                         preferred_element_type=F32) for h in heads]
            yield
            v_new = [(u_r[slot, rows, hcols[h]] - r[h][:CHUNK]).astype(BF16) for h in heads]
            upd = [jnp.dot(kdt_r[slot, rows2, hcols_c[h]], v_new[h], preferred_element_type=F32)
                   for h in heads]
            tail = gate_ref[pl.ds(base + (cc + 1) * CHUNK - 8, 8), :]
            decay_last = jnp.exp(tail[7:8, :])
            for h in heads:
                state_s[h] = st[h] * decay_last[:, DN_HEADS + h:DN_HEADS + h + 1] + upd[h]
            yield
            o = [r[h][CHUNK:] + jnp.dot(aqk_r[slot, rows, hcols_c[h]], v_new[h],
                                        preferred_element_type=F32) for h in heads]
            for h in heads:
                y = o[h] * lax.rsqrt(jnp.mean(o[h] * o[h], axis=-1, keepdims=True) + EPS) * gain
                gate = zg_ref[trows, hcols[h]].astype(F32)
                out_ref[trows, hcols[h]] = (y * gate).astype(out_ref.dtype)
            yield

    def run_interleaved(*gens):
        gens = list(gens)
        while gens:
            for gen in list(gens):
                if next(gen, StopIteration) is StopIteration:
                    gens.remove(gen)

    n_groups = n_chunks // PREP_CHUNKS
    state_s[...] = jnp.zeros_like(state_s)
    run_interleaved(conv(0, 0))
    run_interleaved(prep(0, 0), conv(1, 1))

    def pair_body(j, carry):
        run_interleaved(rec(2 * j, 0), prep(2 * j + 1, 1), conv(2 * j + 2, 0))
        run_interleaved(rec(2 * j + 1, 1), prep(2 * j + 2, 0), conv(2 * j + 3, 1))
        return carry

    lax.fori_loop(0, (n_groups - 2) // 2, pair_body, 0)
    run_interleaved(rec(n_groups - 2, 0), prep(n_groups - 1, 1))
    run_interleaved(rec(n_groups - 1, 1))


def _gated_deltanet(proj, gates, conv_w, out_gain):
    b, t, _ = proj.shape
    group_rows = PREP_CHUNKS * CHUNK
    assert t % (2 * group_rows) == 0 and t // group_rows >= 4
    ring = lambda rows, dtype: pltpu.VMEM((2, rows, DN_WIDTH), dtype)
    return pl.pallas_call(
        _gdn_kernel,
        grid=(b,),
        in_specs=[
            pl.BlockSpec((None, t, QKV_COLS), lambda i: (i, 0, 0)),
            pl.BlockSpec((None, t, DN_WIDTH), lambda i: (i, 0, QKV_COLS // DN_WIDTH)),
            pl.BlockSpec((None, t, N_GATES), lambda i: (i, 0, 0)),
            pl.BlockSpec((CONV_WIDTH, QKV_COLS), lambda i: (0, 0)),
            pl.BlockSpec((1, DN_HEAD), lambda i: (0, 0)),
        ],
        out_specs=pl.BlockSpec((None, t, DN_WIDTH), lambda i: (i, 0, 0)),
        out_shape=jax.ShapeDtypeStruct((b, t, DN_WIDTH), BF16),
        scratch_shapes=[
            ring(group_rows, F32),
            ring(group_rows, F32),
            ring(group_rows, F32),
            ring(group_rows, F32),
            ring(2 * group_rows, BF16),
            ring(group_rows, BF16),
            ring(2 * group_rows, BF16),
            pltpu.VMEM((DN_HEADS, DN_HEAD, DN_HEAD), F32),
            pltpu.VMEM((QKV_COLS // LANES, group_rows + 8, LANES), F32),
        ],
        compiler_params=pltpu.CompilerParams(
            dimension_semantics=("parallel",), vmem_limit_bytes=VMEM_LIMIT),
        name="gated_deltanet",
    )(proj, proj, gates, conv_w, out_gain)


AT_BQ = 512
AT_BK = 256
AT_CW = 256
AT_NQ = 2 * AT_BQ
AT_CHAINS = AT_NQ // AT_CW
AT_CPM = AT_BQ // AT_CW
AT_VROWS = DF_HEAD_V + 16
AT_PROJ_DELAY = 2
OUT_TN = 256


def _attention_tiles(n_qblocks):
    all_chains = tuple(range(AT_CHAINS))
    lo = tuple(c for c in all_chains if c % AT_CPM == 0)
    hi = tuple(c for c in all_chains if c % AT_CPM == 1)
    tiles = []
    for qi in range(n_qblocks):
        for kt in range(AT_CPM * qi + AT_CPM):
            d = kt - AT_CPM * qi
            chains, masked = (all_chains, ()) if d < 0 else (all_chains, lo) if d == 0 else (hi, hi)
            tiles.append(dict(qi=qi, kt=kt, chains=chains, masked=masked, first=kt == 0))
    return tiles


def _diff_attn_kernel(q_ref, k_ref, vt_ref, zg_ref, lq1_ref, lk1_ref, lq2_ref, lk2_ref,
                      og_ref, x_ref, dn_ref, wout_ref, out_ref,
                      qq_s, vta_s, m_s, acc_s, y_s, s_a, s_b, c_a, c_b, p_a, p_b):
    t = q_ref.shape[0]
    n_qblocks = t // AT_BQ
    all_chains = tuple(range(AT_CHAINS))

    @pl.when(pl.program_id(1) == 0)
    def _():
        out_ref[...] = x_ref[...]

    def lanes(c):
        return slice(c * AT_CW, (c + 1) * AT_CW)

    first = lax.broadcasted_iota(jnp.int32, (AT_BQ, LANES), 1) < DF_HEAD_QK
    zero = jnp.zeros((AT_BQ, LANES), BF16)
    for qi in range(n_qblocks):
        blk = q_ref[qi * AT_BQ:(qi + 1) * AT_BQ, :]
        qq_s[qi, :AT_BQ, :] = jnp.where(first, blk, zero)
        qq_s[qi, AT_BQ:, :] = jnp.where(first, zero, blk)

    vta_s[:DF_HEAD_V, :] = vt_ref[...]
    vta_s[DF_HEAD_V:, :] = jnp.ones((AT_VROWS - DF_HEAD_V, t), BF16)

    lam = (jnp.exp(jnp.sum(lq1_ref[...] * lk1_ref[...], axis=-1, keepdims=True))
           - jnp.exp(jnp.sum(lq2_ref[...] * lk2_ref[...], axis=-1, keepdims=True))
           + LAMBDA_INIT)

    krow = lax.broadcasted_iota(jnp.int32, (AT_BK, AT_CW), 0)
    qcol = lax.broadcasted_iota(jnp.int32, (AT_BK, AT_CW), 1)
    causal = krow <= qcol
    bufs = ((s_a, c_a, p_a), (s_b, c_b, p_b))

    def scores(tile, c, buf):
        rows = slice(tile["kt"] * AT_BK, (tile["kt"] + 1) * AT_BK)
        s = _dot_nt(k_ref[rows, :], qq_s[tile["qi"], lanes(c), :])
        if c in tile["masked"]:
            s = jnp.where(causal, s, -jnp.inf)
        buf[0][:, lanes(c)] = s
        buf[1][:, lanes(c)] = jnp.max(s, axis=0, keepdims=True)

    def values(tile, c, buf):
        rows = slice(tile["kt"] * AT_BK, (tile["kt"] + 1) * AT_BK)
        return jnp.dot(vta_s[:, rows], buf[2][:, lanes(c)], preferred_element_type=F32)

    def finish(qi):
        rows = slice(qi * AT_BQ, (qi + 1) * AT_BQ)
        denom = acc_s[qi, DF_HEAD_V:DF_HEAD_V + 1, :]
        o_all = acc_s[qi, :DF_HEAD_V, :] * (1.0 / denom)
        o = o_all[:, :AT_BQ] - lam * o_all[:, AT_BQ:]
        y = (o * lax.rsqrt(jnp.mean(o * o, axis=0, keepdims=True) + EPS)).T
        y = y * og_ref[...] * (1.0 - LAMBDA_INIT)
        y_s[qi] = (y * zg_ref[rows, :].astype(F32)).astype(BF16)

    def project(qi):
        rows = slice(qi * AT_BQ, (qi + 1) * AT_BQ)
        lhs = jnp.concatenate([y_s[qi], dn_ref[rows, :]], axis=1)
        for nc in range(D_MODEL // OUT_TN):
            cols = slice(nc * OUT_TN, (nc + 1) * OUT_TN)
            out_ref[rows, cols] = out_ref[rows, cols] + jnp.dot(
                lhs, wout_ref[:, cols], preferred_element_type=F32)

    tiles = _attention_tiles(n_qblocks)
    due = {}
    for c in tiles[0]["chains"]:
        scores(tiles[0], c, bufs[0])
    for n, cur in enumerate(tiles):
        nxt = tiles[n + 1] if n + 1 < len(tiles) else None
        prv = tiles[n - 1] if n > 0 else None
        s_cur, c_cur, p_cur = bufs[n % 2]
        qi = cur["qi"]
        for c in all_chains:
            alpha = None
            if c in cur["chains"]:
                if cur["first"]:
                    m_new = c_cur[:, lanes(c)]
                else:
                    m_old = m_s[qi, :, lanes(c)]
                    m_new = jnp.maximum(m_old, c_cur[:, lanes(c)])
                    alpha = jnp.exp2(m_old - m_new)
                p = jnp.exp2(s_cur[:, lanes(c)] - m_new)
                m_s[qi, :, lanes(c)] = m_new
                p_cur[:, lanes(c)] = p.astype(BF16)
            if nxt is not None and c in nxt["chains"]:
                scores(nxt, c, bufs[(n + 1) % 2])
            if prv is not None and c in prv["chains"]:
                pv = values(prv, c, bufs[(n - 1) % 2])
                pq = prv["qi"]
                acc = pv if prv["first"] else acc_s[pq, :, lanes(c)] + pv
                if pq == qi and alpha is not None:
                    acc = alpha * acc
                acc_s[pq, :, lanes(c)] = acc
        if prv is not None and prv["qi"] != qi:
            finish(prv["qi"])
            due[n + AT_PROJ_DELAY] = prv["qi"]
        if n in due:
            project(due.pop(n))
    last = tiles[-1]
    for c in last["chains"]:
        acc_s[last["qi"], :, lanes(c)] = (acc_s[last["qi"], :, lanes(c)]
                                          + values(last, c, bufs[(len(tiles) - 1) % 2]))
    finish(last["qi"])
    for qi in sorted(due.values()) + [last["qi"]]:
        project(qi)


def _diff_attention(proj, vt, lq1, lk1, lq2, lk2, out_gain, x, mixed_dn, w_heads):
    b, t, _ = proj.shape
    assert AT_BQ == AT_CPM * AT_BK and AT_CW == AT_BK and AT_CPM == 2 and t % AT_BQ == 0
    assert DF_HEADS == DN_HEADS and DF_HEAD_V == DN_HEAD and D_MODEL % OUT_TN == 0
    n_qblocks = t // AT_BQ
    small = lambda n: pl.BlockSpec((1, n), lambda i, h: (0, 0))
    return pl.pallas_call(
        _diff_attn_kernel,
        grid=(b, DF_HEADS),
        in_specs=[
            pl.BlockSpec((None, t, LANES), lambda i, h: (i, 0, _BLK_DF_Q + h)),
            pl.BlockSpec((None, t, LANES), lambda i, h: (i, 0, _BLK_DF_K + h)),
            pl.BlockSpec((None, DF_HEAD_V, t), lambda i, h: (i, h, 0)),
            pl.BlockSpec((None, t, LANES), lambda i, h: (i, 0, _BLK_DF_Z + h)),
            small(DF_HEAD_QK), small(DF_HEAD_QK), small(DF_HEAD_QK), small(DF_HEAD_QK),
            small(DF_HEAD_V),
            pl.BlockSpec((None, t, D_MODEL), lambda i, h: (i, 0, 0)),
            pl.BlockSpec((None, t, LANES), lambda i, h: (i, 0, h)),
            pl.BlockSpec((None, DF_HEAD_V + DN_HEAD, D_MODEL), lambda i, h: (h, 0, 0)),
        ],
        out_specs=pl.BlockSpec((None, t, D_MODEL), lambda i, h: (i, 0, 0)),
        out_shape=jax.ShapeDtypeStruct((b, t, D_MODEL), F32),
        scratch_shapes=[
            pltpu.VMEM((n_qblocks, AT_NQ, LANES), BF16),
            pltpu.VMEM((AT_VROWS, t), BF16),
            pltpu.VMEM((n_qblocks, 1, AT_NQ), F32),
            pltpu.VMEM((n_qblocks, AT_VROWS, AT_NQ), F32),
            pltpu.VMEM((n_qblocks, AT_BQ, DF_HEAD_V), BF16),
            pltpu.VMEM((AT_BK, AT_NQ), F32),
            pltpu.VMEM((AT_BK, AT_NQ), F32),
            pltpu.VMEM((1, AT_NQ), F32),
            pltpu.VMEM((1, AT_NQ), F32),
            pltpu.VMEM((AT_BK, AT_NQ), BF16),
            pltpu.VMEM((AT_BK, AT_NQ), BF16),
        ],
        compiler_params=pltpu.CompilerParams(
            dimension_semantics=("parallel", "arbitrary"), vmem_limit_bytes=VMEM_LIMIT),
        name="diff_attention",
    )(proj, proj, vt, proj, lq1, lk1, lq2, lk2, out_gain, x, mixed_dn, w_heads)


def kernel(x, norm_gain, w_in, conv_w, a_log, dt_bias, dn_out_gain, q_gain, k_gain,
           lambda_q1, lambda_k1, lambda_q2, lambda_k2, df_out_gain, w_out):
    b, t, d = x.shape
    assert d == D_MODEL and norm_gain.shape[0] == 1
    x2d = x.reshape(b * t, d)

    w = w_in[0]
    gate_lo = 4 * DN_WIDTH
    df_lo = gate_lo + N_GATES
    w_main = jnp.concatenate([w[:, :gate_lo], w[:, df_lo:df_lo + 2 * DF_WIDTH],
                              w[:, df_lo + 3 * DF_WIDTH:]], axis=1).astype(BF16)
    w_t = jnp.concatenate([w[:, df_lo + 2 * DF_WIDTH:df_lo + 3 * DF_WIDTH],
                           jnp.pad(w[:, gate_lo:df_lo], ((0, 0), (0, GATE_ROWS - N_GATES)))],
                          axis=1).T.astype(BF16)
    half = jnp.arange(SEG_W, dtype=jnp.int32) // DF_HEAD_QK
    seg = (half[:, None] == half[None, :]).astype(BF16)
    reps = DF_WIDTH // DF_HEAD_QK
    qk_gain = jnp.concatenate([jnp.tile(q_gain, (1, reps)) * (DF_HEAD_QK ** -0.5 * LOG2_E),
                               jnp.tile(k_gain, (1, reps))], axis=0)

    pad_rows = lambda v: jnp.pad(v.reshape(DN_HEADS, 1),
                                 ((DN_HEADS, GATE_ROWS - 2 * DN_HEADS), (0, 0)))
    proj, vt, gates_t = _in_projection(x2d, t, norm_gain, w_main, w_t, seg, qk_gain,
                                       pad_rows(a_log), pad_rows(dt_bias))
    proj = proj.reshape(b, t, PROJ_COLS)
    gates = jnp.swapaxes(gates_t[:, :N_GATES, :], 1, 2)

    mixed_dn = _gated_deltanet(proj, gates, conv_w[0], dn_out_gain)

    wo = w_out[0].astype(BF16)
    w_heads = jnp.concatenate([wo[DN_WIDTH:].reshape(DF_HEADS, DF_HEAD_V, d),
                               wo[:DN_WIDTH].reshape(DN_HEADS, DN_HEAD, d)], axis=1)
    return _diff_attention(proj, vt, lambda_q1, lambda_k1, lambda_q2, lambda_k2,
                           df_out_gain, x, mixed_dn, w_heads)
```

```python
import math

import jax
import jax.numpy as jnp
from jax import lax
from jax.experimental import pallas as pl
from jax.experimental.pallas import tpu as pltpu

F32 = jnp.float32
BF16 = jnp.bfloat16

D_MODEL = 1024
DN_HEADS = 4
DN_HEAD = 128
DN_WIDTH = DN_HEADS * DN_HEAD
CONV_WIDTH = 4
CHUNK = 64
DF_HEADS = 4
DF_HEAD_QK = 64
DF_HEAD_V = 128
DF_WIDTH = DF_HEADS * DF_HEAD_V
EPS = 1e-6
LAMBDA_INIT = 0.8 - 0.6 * math.exp(-0.3 * 0)
LOG2_E = math.log2(math.e)

N_GATES = 2 * DN_HEADS
GATE_ROWS = 16
PROJ_COLS = 4 * DN_WIDTH + 3 * DF_WIDTH
LANES = 128
VMEM_LIMIT = 56 * 1024 * 1024

_BLK_DN_Z = 3 * DN_HEADS
_BLK_DF_Q = 4 * DN_HEADS
_BLK_DF_K = _BLK_DF_Q + DF_HEADS
_BLK_DF_Z = _BLK_DF_K + DF_HEADS


def _silu(x):
    return x * jax.nn.sigmoid(x)


def _dot(a, b):
    return jnp.dot(a.astype(BF16), b.astype(BF16), preferred_element_type=F32)


def _dot_nt(a, b):
    return lax.dot_general(a.astype(BF16), b.astype(BF16), (((1,), (1,)), ((), ())),
                           preferred_element_type=F32)


IN_TM = 512
IN_TN = 512
SEG_W = 256


def _inproj_kernel(x_ref, gain_ref, w_ref, wt_ref, seg_ref, qkg_ref, alog_ref, dtb_ref,
                   proj_ref, vt_ref, gate_ref):
    x = x_ref[...]
    ms = jnp.mean(x * x, axis=-1, keepdims=True)
    h = (x * lax.rsqrt(ms + EPS) * gain_ref[...]).astype(BF16)

    yt = _dot_nt(wt_ref[...], h)
    vt_ref[...] = yt[:DF_WIDTH].astype(BF16)
    pre = yt[DF_WIDTH:]
    g = -jnp.exp(alog_ref[...]) * jax.nn.softplus(pre + dtb_ref[...])
    pos = lax.broadcasted_iota(jnp.int32, g.shape, 1) % CHUNK
    s = 1
    while s < CHUNK:
        g = g + jnp.where(pos >= s, pltpu.roll(g, s, 1), 0.0)
        s *= 2
    row = lax.broadcasted_iota(jnp.int32, g.shape, 0)
    gate_ref[...] = jnp.where(row < DN_HEADS, jax.nn.sigmoid(pre), g)

    for j in range(PROJ_COLS // IN_TN):
        cols = slice(j * IN_TN, (j + 1) * IN_TN)
        y = jnp.dot(h, w_ref[:, cols], preferred_element_type=F32)
        if j in (_BLK_DF_Q * LANES // IN_TN, _BLK_DF_K * LANES // IN_TN):
            which = 0 if j == _BLK_DF_Q * LANES // IN_TN else 1
            sq = (y * y).astype(BF16)
            ssq = jnp.concatenate(
                [jnp.dot(sq[:, i * SEG_W:(i + 1) * SEG_W], seg_ref[...],
                         preferred_element_type=F32) for i in range(IN_TN // SEG_W)], axis=1)
            y = y * lax.rsqrt(ssq * (1.0 / DF_HEAD_QK) + EPS) * qkg_ref[which:which + 1, :]
        elif j in (_BLK_DN_Z * LANES // IN_TN, _BLK_DF_Z * LANES // IN_TN):
            y = _silu(y)
        proj_ref[:, cols] = y.astype(BF16)


def _in_projection(x2d, t, gain, w_main, w_t, seg, qk_gain, alog_col, dtb_col):
    m = x2d.shape[0]
    assert DF_WIDTH == IN_TN and t % IN_TM == 0 and IN_TM % CHUNK == 0
    tiles_per_seq = t // IN_TM
    const = lambda shape: pl.BlockSpec(shape, lambda i: (0, 0))
    return pl.pallas_call(
        _inproj_kernel,
        grid=(m // IN_TM,),
        in_specs=[
            pl.BlockSpec((IN_TM, D_MODEL), lambda i: (i, 0)),
            const((1, D_MODEL)),
            const((D_MODEL, PROJ_COLS)),
            const((DF_WIDTH + GATE_ROWS, D_MODEL)),
            const((SEG_W, SEG_W)),
            const((2, DF_WIDTH)),
            const((GATE_ROWS, 1)),
            const((GATE_ROWS, 1)),
        ],
        out_specs=[
            pl.BlockSpec((IN_TM, PROJ_COLS), lambda i: (i, 0)),
            pl.BlockSpec((None, DF_WIDTH, IN_TM),
                         lambda i: (i // tiles_per_seq, 0, i % tiles_per_seq)),
            pl.BlockSpec((None, GATE_ROWS, IN_TM),
                         lambda i: (i // tiles_per_seq, 0, i % tiles_per_seq)),
        ],
        out_shape=[
            jax.ShapeDtypeStruct((m, PROJ_COLS), BF16),
            jax.ShapeDtypeStruct((m // t, DF_WIDTH, t), BF16),
            jax.ShapeDtypeStruct((m // t, GATE_ROWS, t), F32),
        ],
        compiler_params=pltpu.CompilerParams(
            dimension_semantics=("parallel",), vmem_limit_bytes=VMEM_LIMIT),
        name="in_projection",
    )(x2d, gain, w_main, w_t, seg, qk_gain, alog_col, dtb_col)


QKV_COLS = 3 * DN_WIDTH
PREP_CHUNKS = 4


def _gdn_kernel(qkv_ref, zg_ref, gate_ref, convw_ref, gain_ref,
                out_ref, q_r, k_r, v_r, u_r, wq_r, aqk_r, kdt_r, state_s, stage_s):
    t = qkv_ref.shape[0]
    n_chunks = t // CHUNK
    group_rows = PREP_CHUNKS * CHUNK

    def group_base(g):
        return g * group_rows if isinstance(g, int) else pl.multiple_of(g * group_rows, group_rows)

    def conv(g, slot):
        r0 = group_base(g)
        for cg in range(QKV_COLS // LANES):
            cols = slice(cg * LANES, (cg + 1) * LANES)
            cur = qkv_ref[pl.ds(r0, group_rows), cols].astype(F32)
            if isinstance(g, int) and g == 0:
                prev = jnp.zeros((8, LANES), F32)
            else:
                prev = qkv_ref[pl.ds(pl.multiple_of(r0 - 16, 16), 16), cols].astype(F32)[8:]
            stage_s[cg, :8, :] = prev
            stage_s[cg, 8:, :] = cur
            w = convw_ref[:, cols]
            acc = cur * w[CONV_WIDTH - 1:CONV_WIDTH]
            for j in range(1, CONV_WIDTH):
                acc = acc + (stage_s[cg, 8 - j:8 - j + group_rows, :]
                             * w[CONV_WIDTH - 1 - j:CONV_WIDTH - j])
            y = _silu(acc)
            hcols = slice((cg % DN_HEADS) * LANES, (cg % DN_HEADS + 1) * LANES)
            if cg < 2 * DN_HEADS:
                inv_norm = lax.rsqrt(jnp.sum(y * y, axis=-1, keepdims=True) + EPS)
                if cg < DN_HEADS:
                    q_r[slot, :, hcols] = y * (inv_norm * (DN_HEAD ** -0.5))
                else:
                    k_r[slot, :, hcols] = y * inv_norm
            else:
                v_r[slot, :, hcols] = y
            yield

    row = lax.broadcasted_iota(jnp.int32, (CHUNK, 2 * CHUNK), 0)
    col = lax.broadcasted_iota(jnp.int32, (CHUNK, 2 * CHUNK), 1)
    eye_f = (row == col).astype(F32)
    lower_f = (row >= col).astype(F32)
    strict_f = (row > col).astype(F32)
    upper_f = (col >= CHUNK).astype(F32)
    eye_hi = (col - CHUNK == row).astype(F32)
    zero_rows = jnp.zeros((CHUNK, DN_HEAD), BF16)

    items = [(cc, h) for cc in range(PREP_CHUNKS) for h in range(DN_HEADS)]
    heads = range(DN_HEADS)
    gain = gain_ref[...]
    hcols = [slice(h * LANES, (h + 1) * LANES) for h in heads]
    hcols_c = [slice(h * LANES, h * LANES + CHUNK) for h in heads]

    def halves(fn, seq):
        out = []
        mid = len(seq) // 2
        for part in (seq[:mid], seq[mid:]):
            out.extend(fn(x) for x in part)
            yield out

    def prep(g, slot):
        base = group_base(g)
        gates = [gate_ref[pl.ds(base + cc * CHUNK, CHUNK), :] for cc in range(PREP_CHUNKS)]
        ks, lhs, decay, rhs, qe, kdt = [], [], [], [], [], []
        for cc, h in items:
            rows = slice(cc * CHUNK, (cc + 1) * CHUNK)
            q = q_r[slot, rows, hcols[h]]
            k = k_r[slot, rows, hcols[h]]
            v = v_r[slot, rows, hcols[h]]
            beta_b = jnp.broadcast_to(gates[cc][:, h:h + 1], (CHUNK, DN_HEAD))
            gcb = jnp.broadcast_to(gates[cc][:, DN_HEADS + h:DN_HEADS + h + 1], (CHUNK, DN_HEAD))
            gr = jnp.sum(gcb * eye_f, axis=0, keepdims=True)
            decay.append(jnp.exp(jnp.minimum(gcb - gr, 0.0)))
            g_last = gcb[CHUNK - 1:CHUNK, :]
            eg = jnp.exp(gcb)
            kb = k * beta_b
            ks.append(jnp.concatenate([k.astype(BF16), zero_rows], axis=0))
            lhs.append(jnp.concatenate([kb, q], axis=0).astype(BF16))
            rhs.append(jnp.concatenate([v * beta_b, kb * eg], axis=1).astype(BF16))
            qe.append((q * eg).astype(BF16))
            kdt.append((k * jnp.exp(g_last - gcb)).T.astype(BF16))
        yield

        kq = None
        for kq in halves(lambda ab: _dot_nt(*ab), list(zip(lhs, ks))):
            yield
        m = [x[:CHUNK] * (d * strict_f) for x, d in zip(kq, decay)]
        a_qk = [(x[CHUNK:, :CHUNK] * (d * lower_f)[:, :CHUNK]).astype(BF16)
                for x, d in zip(kq, decay)]

        w = None
        for w in halves(lambda x: _dot(x[:, :CHUNK], x - eye_hi) + eye_hi, m):
            yield
        for _ in range(5):
            prev = w
            for w in halves(lambda x: _dot(x[:, :CHUNK], x) + x * upper_f, prev):
                yield
        sol = None
        for sol in halves(lambda xb: _dot(xb[0][:, CHUNK:], xb[1]), list(zip(w, rhs))):
            yield

        for n, (cc, h) in enumerate(items):
            rows = slice(cc * CHUNK, (cc + 1) * CHUNK)
            rows_w = slice(2 * cc * CHUNK, (2 * cc + 1) * CHUNK)
            rows_q = slice((2 * cc + 1) * CHUNK, (2 * cc + 2) * CHUNK)
            rows_t = slice(2 * cc * CHUNK, (2 * cc + 2) * CHUNK)
            u_r[slot, rows, hcols[h]] = sol[n][:, :DN_HEAD]
            wq_r[slot, rows_w, hcols[h]] = sol[n][:, DN_HEAD:].astype(BF16)
            wq_r[slot, rows_q, hcols[h]] = qe[n]
            aqk_r[slot, rows, hcols_c[h]] = a_qk[n]
            kdt_r[slot, rows_t, hcols_c[h]] = kdt[n]
        yield

    def rec(g, slot):
        base = group_base(g)
        for cc in range(PREP_CHUNKS):
            rows = slice(cc * CHUNK, (cc + 1) * CHUNK)
            rows2 = slice(2 * cc * CHUNK, (2 * cc + 2) * CHUNK)
            trows = pl.ds(base + cc * CHUNK, CHUNK)
            st = [state_s[h] for h in heads]
            r = [jnp.dot(wq_r[slot, rows2, hcols[h]], st[h].astype(BF16),
                         preferred_element_type=F32) for h in heads]
            yield
            v_new = [(u_r[slot, rows, hcols[h]] - r[h][:CHUNK]).astype(BF16) for h in heads]
            upd = [jnp.dot(kdt_r[slot, rows2, hcols_c[h]], v_new[h], preferred_element_type=F32)
                   for h in heads]
            tail = gate_ref[pl.ds(base + (cc + 1) * CHUNK - 8, 8), :]
            decay_last = jnp.exp(tail[7:8, :])
            for h in heads:
                state_s[h] = st[h] * decay_last[:, DN_HEADS + h:DN_HEADS + h + 1] + upd[h]
            yield
            o = [r[h][CHUNK:] + jnp.dot(aqk_r[slot, rows, hcols_c[h]], v_new[h],
                                        preferred_element_type=F32) for h in heads]
            for h in heads:
                y = o[h] * lax.rsqrt(jnp.mean(o[h] * o[h], axis=-1, keepdims=True) + EPS) * gain
                gate = zg_ref[trows, hcols[h]].astype(F32)
                out_ref[trows, hcols[h]] = (y * gate).astype(out_ref.dtype)
            yield

    def run_interleaved(*gens):
        gens = list(gens)
        while gens:
            for gen in list(gens):
                if next(gen, StopIteration) is StopIteration:
                    gens.remove(gen)

    n_groups = n_chunks // PREP_CHUNKS
    state_s[...] = jnp.zeros_like(state_s)
    run_interleaved(conv(0, 0))
    run_interleaved(prep(0, 0), conv(1, 1))

    def pair_body(j, carry):
        run_interleaved(rec(2 * j, 0), prep(2 * j + 1, 1), conv(2 * j + 2, 0))
        run_interleaved(rec(2 * j + 1, 1), prep(2 * j + 2, 0), conv(2 * j + 3, 1))
        return carry

    lax.fori_loop(0, (n_groups - 2) // 2, pair_body, 0)
    run_interleaved(rec(n_groups - 2, 0), prep(n_groups - 1, 1))
    run_interleaved(rec(n_groups - 1, 1))


def _gated_deltanet(proj, gates, conv_w, out_gain):
    b, t, _ = proj.shape
    group_rows = PREP_CHUNKS * CHUNK
    assert t % (2 * group_rows) == 0 and t // group_rows >= 4
    ring = lambda rows, dtype: pltpu.VMEM((2, rows, DN_WIDTH), dtype)
    return pl.pallas_call(
        _gdn_kernel,
        grid=(b,),
        in_specs=[
            pl.BlockSpec((None, t, QKV_COLS), lambda i: (i, 0, 0)),
            pl.BlockSpec((None, t, DN_WIDTH), lambda i: (i, 0, QKV_COLS // DN_WIDTH)),
            pl.BlockSpec((None, t, N_GATES), lambda i: (i, 0, 0)),
            pl.BlockSpec((CONV_WIDTH, QKV_COLS), lambda i: (0, 0)),
            pl.BlockSpec((1, DN_HEAD), lambda i: (0, 0)),
        ],
        out_specs=pl.BlockSpec((None, t, DN_WIDTH), lambda i: (i, 0, 0)),
        out_shape=jax.ShapeDtypeStruct((b, t, DN_WIDTH), BF16),
        scratch_shapes=[
            ring(group_rows, F32),
            ring(group_rows, F32),
            ring(group_rows, F32),
            ring(group_rows, F32),
            ring(2 * group_rows, BF16),
            ring(group_rows, BF16),
            ring(2 * group_rows, BF16),
            pltpu.VMEM((DN_HEADS, DN_HEAD, DN_HEAD), F32),
            pltpu.VMEM((QKV_COLS // LANES, group_rows + 8, LANES), F32),
        ],
        compiler_params=pltpu.CompilerParams(
            dimension_semantics=("parallel",), vmem_limit_bytes=VMEM_LIMIT),
        name="gated_deltanet",
    )(proj, proj, gates, conv_w, out_gain)


AT_BQ = 512
AT_BK = 256
AT_CW = 256
AT_NQ = 2 * AT_BQ
AT_CHAINS = AT_NQ // AT_CW
AT_CPM = AT_BQ // AT_CW
AT_VROWS = DF_HEAD_V + 16
AT_PROJ_DELAY = 2
AT_HPS = 2
OUT_TN = 256


def _attention_tiles(n_heads, n_qblocks):
    all_chains = tuple(range(AT_CHAINS))
    lo = tuple(c for c in all_chains if c % AT_CPM == 0)
    hi = tuple(c for c in all_chains if c % AT_CPM == 1)
    tiles = []
    for hd in range(n_heads):
        for qi in range(n_qblocks):
            for kt in range(AT_CPM * qi + AT_CPM):
                d = kt - AT_CPM * qi
                chains, masked = ((all_chains, ()) if d < 0 else
                                  (all_chains, lo) if d == 0 else (hi, hi))
                tiles.append(dict(hd=hd, qi=qi, qb=hd * n_qblocks + qi, kt=kt, chains=chains,
                                  masked=masked, first=kt == 0))
    return tiles


def _diff_attn_kernel(q_ref, k_ref, vt_ref, zg_ref, lq1_ref, lk1_ref, lq2_ref, lk2_ref,
                      og_ref, x_ref, dn_ref, wout_ref, out_ref,
                      qq_s, vta_s, m_s, acc_s, y_s, s_a, s_b, c_a, c_b, p_a, p_b):
    t = q_ref.shape[0]
    n_qblocks = t // AT_BQ
    all_chains = tuple(range(AT_CHAINS))

    @pl.when(pl.program_id(1) == 0)
    def _():
        out_ref[...] = x_ref[...]

    def lanes(c):
        return slice(c * AT_CW, (c + 1) * AT_CW)

    first = lax.broadcasted_iota(jnp.int32, (AT_BQ, LANES), 1) < DF_HEAD_QK
    zero = jnp.zeros((AT_BQ, LANES), BF16)
    hcols = [slice(hd * LANES, (hd + 1) * LANES) for hd in range(AT_HPS)]
    for hd in range(AT_HPS):
        for qi in range(n_qblocks):
            blk = q_ref[qi * AT_BQ:(qi + 1) * AT_BQ, hcols[hd]]
            qq_s[hd * n_qblocks + qi, :AT_BQ, :] = jnp.where(first, blk, zero)
            qq_s[hd * n_qblocks + qi, AT_BQ:, :] = jnp.where(first, zero, blk)
        vta_s[hd, :DF_HEAD_V, :] = vt_ref[hcols[hd], :]
        vta_s[hd, DF_HEAD_V:, :] = jnp.ones((AT_VROWS - DF_HEAD_V, t), BF16)

    lam = (jnp.exp(jnp.sum(lq1_ref[...] * lk1_ref[...], axis=-1, keepdims=True))
           - jnp.exp(jnp.sum(lq2_ref[...] * lk2_ref[...], axis=-1, keepdims=True))
           + LAMBDA_INIT)

    krow = lax.broadcasted_iota(jnp.int32, (AT_BK, AT_CW), 0)
    qcol = lax.broadcasted_iota(jnp.int32, (AT_BK, AT_CW), 1)
    causal = krow <= qcol
    bufs = ((s_a, c_a, p_a), (s_b, c_b, p_b))

    def scores(tile, c, buf):
        rows = slice(tile["kt"] * AT_BK, (tile["kt"] + 1) * AT_BK)
        s = _dot_nt(k_ref[rows, hcols[tile["hd"]]], qq_s[tile["qb"], lanes(c), :])
        if c in tile["masked"]:
            s = jnp.where(causal, s, -jnp.inf)
        buf[0][:, lanes(c)] = s
        buf[1][:, lanes(c)] = jnp.max(s, axis=0, keepdims=True)

    def values(tile, c, buf):
        rows = slice(tile["kt"] * AT_BK, (tile["kt"] + 1) * AT_BK)
        return jnp.dot(vta_s[tile["hd"], :, rows], buf[2][:, lanes(c)],
                       preferred_element_type=F32)

    def finish(tile):
        qb = tile["qb"]
        rows = slice(tile["qi"] * AT_BQ, (tile["qi"] + 1) * AT_BQ)
        denom = acc_s[qb, DF_HEAD_V:DF_HEAD_V + 1, :]
        o_all = acc_s[qb, :DF_HEAD_V, :] * (1.0 / denom)
        o = o_all[:, :AT_BQ] - lam * o_all[:, AT_BQ:]
        y = (o * lax.rsqrt(jnp.mean(o * o, axis=0, keepdims=True) + EPS)).T
        y = y * og_ref[...] * (1.0 - LAMBDA_INIT)
        y_s[qb] = (y * zg_ref[rows, hcols[tile["hd"]]].astype(F32)).astype(BF16)

    def project(tile):
        rows = slice(tile["qi"] * AT_BQ, (tile["qi"] + 1) * AT_BQ)
        hd = tile["hd"]
        lhs = jnp.concatenate([y_s[tile["qb"]], dn_ref[rows, hcols[hd]]], axis=1)
        for nc in range(D_MODEL // OUT_TN):
            cols = slice(nc * OUT_TN, (nc + 1) * OUT_TN)
            out_ref[rows, cols] = out_ref[rows, cols] + jnp.dot(
                lhs, wout_ref[hd, :, cols], preferred_element_type=F32)

    tiles = _attention_tiles(AT_HPS, n_qblocks)
    due = {}
    for c in tiles[0]["chains"]:
        scores(tiles[0], c, bufs[0])
    for n, cur in enumerate(tiles):
        nxt = tiles[n + 1] if n + 1 < len(tiles) else None
        prv = tiles[n - 1] if n > 0 else None
        s_cur, c_cur, p_cur = bufs[n % 2]
        qb = cur["qb"]
        for c in all_chains:
            alpha = None
            if c in cur["chains"]:
                if cur["first"]:
                    m_new = c_cur[:, lanes(c)]
                else:
                    m_old = m_s[qb, :, lanes(c)]
                    m_new = jnp.maximum(m_old, c_cur[:, lanes(c)])
                    alpha = jnp.exp2(m_old - m_new)
                p = jnp.exp2(s_cur[:, lanes(c)] - m_new)
                m_s[qb, :, lanes(c)] = m_new
                p_cur[:, lanes(c)] = p.astype(BF16)
            if nxt is not None and c in nxt["chains"]:
                scores(nxt, c, bufs[(n + 1) % 2])
            if prv is not None and c in prv["chains"]:
                pv = values(prv, c, bufs[(n - 1) % 2])
                pq = prv["qb"]
                acc = pv if prv["first"] else acc_s[pq, :, lanes(c)] + pv
                if pq == qb and alpha is not None:
                    acc = alpha * acc
                acc_s[pq, :, lanes(c)] = acc
        if prv is not None and prv["qb"] != qb:
            finish(prv)
            due[n + AT_PROJ_DELAY] = prv
        if n in due:
            project(due.pop(n))
    last = tiles[-1]
    for c in last["chains"]:
        acc_s[last["qb"], :, lanes(c)] = (acc_s[last["qb"], :, lanes(c)]
                                          + values(last, c, bufs[(len(tiles) - 1) % 2]))
    finish(last)
    for tile in sorted(due.values(), key=lambda tl: tl["qb"]) + [last]:
        project(tile)


def _diff_attention(proj, vt, lq1, lk1, lq2, lk2, out_gain, x, mixed_dn, w_heads):
    b, t, _ = proj.shape
    assert AT_BQ == AT_CPM * AT_BK and AT_CW == AT_BK and AT_CPM == 2 and t % AT_BQ == 0
    assert DF_HEADS == DN_HEADS and DF_HEAD_V == DN_HEAD and D_MODEL % OUT_TN == 0
    assert DF_HEADS % AT_HPS == 0 and all(
        blk % AT_HPS == 0 for blk in (_BLK_DF_Q, _BLK_DF_K, _BLK_DF_Z))
    n_qblocks = t // AT_BQ
    n_blocks = AT_HPS * n_qblocks
    wide = AT_HPS * LANES
    small = lambda n: pl.BlockSpec((1, n), lambda i, h: (0, 0))
    return pl.pallas_call(
        _diff_attn_kernel,
        grid=(b, DF_HEADS // AT_HPS),
        in_specs=[
            pl.BlockSpec((None, t, wide), lambda i, h: (i, 0, _BLK_DF_Q // AT_HPS + h)),
            pl.BlockSpec((None, t, wide), lambda i, h: (i, 0, _BLK_DF_K // AT_HPS + h)),
            pl.BlockSpec((None, wide, t), lambda i, h: (i, h, 0)),
            pl.BlockSpec((None, t, wide), lambda i, h: (i, 0, _BLK_DF_Z // AT_HPS + h)),
            small(DF_HEAD_QK), small(DF_HEAD_QK), small(DF_HEAD_QK), small(DF_HEAD_QK),
            small(DF_HEAD_V),
            pl.BlockSpec((None, t, D_MODEL), lambda i, h: (i, 0, 0)),
            pl.BlockSpec((None, t, wide), lambda i, h: (i, 0, h)),
            pl.BlockSpec((AT_HPS, DF_HEAD_V + DN_HEAD, D_MODEL), lambda i, h: (h, 0, 0)),
        ],
        out_specs=pl.BlockSpec((None, t, D_MODEL), lambda i, h: (i, 0, 0)),
        out_shape=jax.ShapeDtypeStruct((b, t, D_MODEL), F32),
        scratch_shapes=[
            pltpu.VMEM((n_blocks, AT_NQ, LANES), BF16),
            pltpu.VMEM((AT_HPS, AT_VROWS, t), BF16),
            pltpu.VMEM((n_blocks, 1, AT_NQ), F32),
            pltpu.VMEM((n_blocks, AT_VROWS, AT_NQ), F32),
            pltpu.VMEM((n_blocks, AT_BQ, DF_HEAD_V), BF16),
            pltpu.VMEM((AT_BK, AT_NQ), F32),
            pltpu.VMEM((AT_BK, AT_NQ), F32),
            pltpu.VMEM((1, AT_NQ), F32),
            pltpu.VMEM((1, AT_NQ), F32),
            pltpu.VMEM((AT_BK, AT_NQ), BF16),
            pltpu.VMEM((AT_BK, AT_NQ), BF16),
        ],
        compiler_params=pltpu.CompilerParams(
            dimension_semantics=("parallel", "arbitrary"), vmem_limit_bytes=VMEM_LIMIT),
        name="diff_attention",
    )(proj, proj, vt, proj, lq1, lk1, lq2, lk2, out_gain, x, mixed_dn, w_heads)


def kernel(x, norm_gain, w_in, conv_w, a_log, dt_bias, dn_out_gain, q_gain, k_gain,
           lambda_q1, lambda_k1, lambda_q2, lambda_k2, df_out_gain, w_out):
    b, t, d = x.shape
    assert d == D_MODEL and norm_gain.shape[0] == 1
    x2d = x.reshape(b * t, d)

    w = w_in[0]
    gate_lo = 4 * DN_WIDTH
    df_lo = gate_lo + N_GATES
    w_main = jnp.concatenate([w[:, :gate_lo], w[:, df_lo:df_lo + 2 * DF_WIDTH],
                              w[:, df_lo + 3 * DF_WIDTH:]], axis=1).astype(BF16)
    w_t = jnp.concatenate([w[:, df_lo + 2 * DF_WIDTH:df_lo + 3 * DF_WIDTH],
                           jnp.pad(w[:, gate_lo:df_lo], ((0, 0), (0, GATE_ROWS - N_GATES)))],
                          axis=1).T.astype(BF16)
    half = jnp.arange(SEG_W, dtype=jnp.int32) // DF_HEAD_QK
    seg = (half[:, None] == half[None, :]).astype(BF16)
    reps = DF_WIDTH // DF_HEAD_QK
    qk_gain = jnp.concatenate([jnp.tile(q_gain, (1, reps)) * (DF_HEAD_QK ** -0.5 * LOG2_E),
                               jnp.tile(k_gain, (1, reps))], axis=0)

    pad_rows = lambda v: jnp.pad(v.reshape(DN_HEADS, 1),
                                 ((DN_HEADS, GATE_ROWS - 2 * DN_HEADS), (0, 0)))
    proj, vt, gates_t = _in_projection(x2d, t, norm_gain, w_main, w_t, seg, qk_gain,
                                       pad_rows(a_log), pad_rows(dt_bias))
    proj = proj.reshape(b, t, PROJ_COLS)
    gates = jnp.swapaxes(gates_t[:, :N_GATES, :], 1, 2)

    mixed_dn = _gated_deltanet(proj, gates, conv_w[0], dn_out_gain)

    wo = w_out[0].astype(BF16)
    w_heads = jnp.concatenate([wo[DN_WIDTH:].reshape(DF_HEADS, DF_HEAD_V, d),
                               wo[:DN_WIDTH].reshape(DN_HEADS, DN_HEAD, d)], axis=1)
    return _diff_attention(proj, vt, lambda_q1, lambda_k1, lambda_q2, lambda_k2,
                           df_out_gain, x, mixed_dn, w_heads)
```

```python
import math

import jax
import jax.numpy as jnp
from jax import lax
from jax.experimental import pallas as pl
from jax.experimental.pallas import tpu as pltpu

F32 = jnp.float32
BF16 = jnp.bfloat16

D_MODEL = 1024
DN_HEADS = 4
DN_HEAD = 128
DN_WIDTH = DN_HEADS * DN_HEAD
CONV_WIDTH = 4
CHUNK = 64
DF_HEADS = 4
DF_HEAD_QK = 64
DF_HEAD_V = 128
DF_WIDTH = DF_HEADS * DF_HEAD_V
EPS = 1e-6
LAMBDA_INIT = 0.8 - 0.6 * math.exp(-0.3 * 0)
LOG2_E = math.log2(math.e)

N_GATES = 2 * DN_HEADS
GATE_ROWS = 16
PROJ_COLS = 4 * DN_WIDTH + 3 * DF_WIDTH
LANES = 128
VMEM_LIMIT = 56 * 1024 * 1024

_BLK_DN_Z = 3 * DN_HEADS
_BLK_DF_Q = 4 * DN_HEADS
_BLK_DF_K = _BLK_DF_Q + DF_HEADS
_BLK_DF_Z = _BLK_DF_K + DF_HEADS


def _silu(x):
    return x * jax.nn.sigmoid(x)


def _dot(a, b):
    return jnp.dot(a.astype(BF16), b.astype(BF16), preferred_element_type=F32)


def _dot_nt(a, b):
    return lax.dot_general(a.astype(BF16), b.astype(BF16), (((1,), (1,)), ((), ())),
                           preferred_element_type=F32)


IN_TM = 1024
IN_TN = 512
SEG_W = 256


def _inproj_kernel(x_ref, gain_ref, w_ref, wt_ref, seg_ref, qkg_ref, alog_ref, dtb_ref,
                   proj_ref, vt_ref, gate_ref):
    x = x_ref[...]
    ms = jnp.mean(x * x, axis=-1, keepdims=True)
    h = (x * lax.rsqrt(ms + EPS) * gain_ref[...]).astype(BF16)

    yt = _dot_nt(wt_ref[...], h)
    vt_ref[...] = yt[:DF_WIDTH].astype(BF16)
    pre = yt[DF_WIDTH:]
    g = -jnp.exp(alog_ref[...]) * jax.nn.softplus(pre + dtb_ref[...])
    pos = lax.broadcasted_iota(jnp.int32, g.shape, 1) % CHUNK
    s = 1
    while s < CHUNK:
        g = g + jnp.where(pos >= s, pltpu.roll(g, s, 1), 0.0)
        s *= 2
    row = lax.broadcasted_iota(jnp.int32, g.shape, 0)
    gate_ref[...] = jnp.where(row < DN_HEADS, jax.nn.sigmoid(pre), g)

    for j in range(PROJ_COLS // IN_TN):
        cols = slice(j * IN_TN, (j + 1) * IN_TN)
        y = jnp.dot(h, w_ref[:, cols], preferred_element_type=F32)
        if j in (_BLK_DF_Q * LANES // IN_TN, _BLK_DF_K * LANES // IN_TN):
            which = 0 if j == _BLK_DF_Q * LANES // IN_TN else 1
            sq = (y * y).astype(BF16)
            ssq = jnp.concatenate(
                [jnp.dot(sq[:, i * SEG_W:(i + 1) * SEG_W], seg_ref[...],
                         preferred_element_type=F32) for i in range(IN_TN // SEG_W)], axis=1)
            y = y * lax.rsqrt(ssq * (1.0 / DF_HEAD_QK) + EPS) * qkg_ref[which:which + 1, :]
        elif j in (_BLK_DN_Z * LANES // IN_TN, _BLK_DF_Z * LANES // IN_TN):
            y = _silu(y)
        proj_ref[:, cols] = y.astype(BF16)


def _in_projection(x2d, t, gain, w_main, w_t, seg, qk_gain, alog_col, dtb_col):
    m = x2d.shape[0]
    assert DF_WIDTH == IN_TN and t % IN_TM == 0 and IN_TM % CHUNK == 0
    tiles_per_seq = t // IN_TM
    const = lambda shape: pl.BlockSpec(shape, lambda i: (0, 0))
    return pl.pallas_call(
        _inproj_kernel,
        grid=(m // IN_TM,),
        in_specs=[
            pl.BlockSpec((IN_TM, D_MODEL), lambda i: (i, 0)),
            const((1, D_MODEL)),
            const((D_MODEL, PROJ_COLS)),
            const((DF_WIDTH + GATE_ROWS, D_MODEL)),
            const((SEG_W, SEG_W)),
            const((2, DF_WIDTH)),
            const((GATE_ROWS, 1)),
            const((GATE_ROWS, 1)),
        ],
        out_specs=[
            pl.BlockSpec((IN_TM, PROJ_COLS), lambda i: (i, 0)),
            pl.BlockSpec((None, DF_WIDTH, IN_TM),
                         lambda i: (i // tiles_per_seq, 0, i % tiles_per_seq)),
            pl.BlockSpec((None, GATE_ROWS, IN_TM),
                         lambda i: (i // tiles_per_seq, 0, i % tiles_per_seq)),
        ],
        out_shape=[
            jax.ShapeDtypeStruct((m, PROJ_COLS), BF16),
            jax.ShapeDtypeStruct((m // t, DF_WIDTH, t), BF16),
            jax.ShapeDtypeStruct((m // t, GATE_ROWS, t), F32),
        ],
        compiler_params=pltpu.CompilerParams(
            dimension_semantics=("parallel",), vmem_limit_bytes=VMEM_LIMIT),
        name="in_projection",
    )(x2d, gain, w_main, w_t, seg, qk_gain, alog_col, dtb_col)


QKV_COLS = 3 * DN_WIDTH
PREP_CHUNKS = 4


def _gdn_kernel(qkv_ref, zg_ref, gate_ref, convw_ref, gain_ref,
                out_ref, q_r, k_r, v_r, u_r, wq_r, aqk_r, kdt_r, state_s, stage_s):
    t = qkv_ref.shape[0]
    n_chunks = t // CHUNK
    group_rows = PREP_CHUNKS * CHUNK

    def group_base(g):
        return g * group_rows if isinstance(g, int) else pl.multiple_of(g * group_rows, group_rows)

    def conv(g, slot):
        r0 = group_base(g)
        for cg in range(QKV_COLS // LANES):
            cols = slice(cg * LANES, (cg + 1) * LANES)
            cur = qkv_ref[pl.ds(r0, group_rows), cols].astype(F32)
            if isinstance(g, int) and g == 0:
                prev = jnp.zeros((8, LANES), F32)
            else:
                prev = qkv_ref[pl.ds(pl.multiple_of(r0 - 16, 16), 16), cols].astype(F32)[8:]
            stage_s[cg, :8, :] = prev
            stage_s[cg, 8:, :] = cur
            w = convw_ref[:, cols]
            acc = cur * w[CONV_WIDTH - 1:CONV_WIDTH]
            for j in range(1, CONV_WIDTH):
                acc = acc + (stage_s[cg, 8 - j:8 - j + group_rows, :]
                             * w[CONV_WIDTH - 1 - j:CONV_WIDTH - j])
            y = _silu(acc)
            hcols = slice((cg % DN_HEADS) * LANES, (cg % DN_HEADS + 1) * LANES)
            if cg < 2 * DN_HEADS:
                inv_norm = lax.rsqrt(jnp.sum(y * y, axis=-1, keepdims=True) + EPS)
                if cg < DN_HEADS:
                    q_r[slot, :, hcols] = y * (inv_norm * (DN_HEAD ** -0.5))
                else:
                    k_r[slot, :, hcols] = y * inv_norm
            else:
                v_r[slot, :, hcols] = y
            yield

    row = lax.broadcasted_iota(jnp.int32, (CHUNK, 2 * CHUNK), 0)
    col = lax.broadcasted_iota(jnp.int32, (CHUNK, 2 * CHUNK), 1)
    eye_f = (row == col).astype(F32)
    lower_f = (row >= col).astype(F32)
    strict_f = (row > col).astype(F32)
    upper_f = (col >= CHUNK).astype(F32)
    eye_hi = (col - CHUNK == row).astype(F32)
    zero_rows = jnp.zeros((CHUNK, DN_HEAD), BF16)

    items = [(cc, h) for cc in range(PREP_CHUNKS) for h in range(DN_HEADS)]
    heads = range(DN_HEADS)
    gain = gain_ref[...]
    hcols = [slice(h * LANES, (h + 1) * LANES) for h in heads]
    hcols_c = [slice(h * LANES, h * LANES + CHUNK) for h in heads]

    def halves(fn, seq):
        out = []
        mid = len(seq) // 2
        for part in (seq[:mid], seq[mid:]):
            out.extend(fn(x) for x in part)
            yield out

    def prep(g, slot):
        base = group_base(g)
        gates = [gate_ref[pl.ds(base + cc * CHUNK, CHUNK), :] for cc in range(PREP_CHUNKS)]
        ks, lhs, decay, rhs, qe, kdt = [], [], [], [], [], []
        for cc, h in items:
            rows = slice(cc * CHUNK, (cc + 1) * CHUNK)
            q = q_r[slot, rows, hcols[h]]
            k = k_r[slot, rows, hcols[h]]
            v = v_r[slot, rows, hcols[h]]
            beta_b = jnp.broadcast_to(gates[cc][:, h:h + 1], (CHUNK, DN_HEAD))
            gcb = jnp.broadcast_to(gates[cc][:, DN_HEADS + h:DN_HEADS + h + 1], (CHUNK, DN_HEAD))
            gr = jnp.sum(gcb * eye_f, axis=0, keepdims=True)
            decay.append(jnp.exp(jnp.minimum(gcb - gr, 0.0)))
            g_last = gcb[CHUNK - 1:CHUNK, :]
            eg = jnp.exp(gcb)
            kb = k * beta_b
            ks.append(jnp.concatenate([k.astype(BF16), zero_rows], axis=0))
            lhs.append(jnp.concatenate([kb, q], axis=0).astype(BF16))
            rhs.append(jnp.concatenate([v * beta_b, kb * eg], axis=1).astype(BF16))
            qe.append((q * eg).astype(BF16))
            kdt.append((k * jnp.exp(g_last - gcb)).T.astype(BF16))
        yield

        kq = None
        for kq in halves(lambda ab: _dot_nt(*ab), list(zip(lhs, ks))):
            yield
        m = [x[:CHUNK] * (d * strict_f) for x, d in zip(kq, decay)]
        a_qk = [(x[CHUNK:, :CHUNK] * (d * lower_f)[:, :CHUNK]).astype(BF16)
                for x, d in zip(kq, decay)]

        w = None
        for w in halves(lambda x: _dot(x[:, :CHUNK], x - eye_hi) + eye_hi, m):
            yield
        for _ in range(5):
            prev = w
            for w in halves(lambda x: _dot(x[:, :CHUNK], x) + x * upper_f, prev):
                yield
        sol = None
        for sol in halves(lambda xb: _dot(xb[0][:, CHUNK:], xb[1]), list(zip(w, rhs))):
            yield

        for n, (cc, h) in enumerate(items):
            rows = slice(cc * CHUNK, (cc + 1) * CHUNK)
            rows_w = slice(2 * cc * CHUNK, (2 * cc + 1) * CHUNK)
            rows_q = slice((2 * cc + 1) * CHUNK, (2 * cc + 2) * CHUNK)
            rows_t = slice(2 * cc * CHUNK, (2 * cc + 2) * CHUNK)
            u_r[slot, rows, hcols[h]] = sol[n][:, :DN_HEAD]
            wq_r[slot, rows_w, hcols[h]] = sol[n][:, DN_HEAD:].astype(BF16)
            wq_r[slot, rows_q, hcols[h]] = qe[n]
            aqk_r[slot, rows, hcols_c[h]] = a_qk[n]
            kdt_r[slot, rows_t, hcols_c[h]] = kdt[n]
        yield

    def rec(g, slot):
        base = group_base(g)
        for cc in range(PREP_CHUNKS):
            rows = slice(cc * CHUNK, (cc + 1) * CHUNK)
            rows2 = slice(2 * cc * CHUNK, (2 * cc + 2) * CHUNK)
            trows = pl.ds(base + cc * CHUNK, CHUNK)
            st = [state_s[h] for h in heads]
            r = [jnp.dot(wq_r[slot, rows2, hcols[h]], st[h].astype(BF16),
                         preferred_element_type=F32) for h in heads]
            yield
            v_new = [(u_r[slot, rows, hcols[h]] - r[h][:CHUNK]).astype(BF16) for h in heads]
            upd = [jnp.dot(kdt_r[slot, rows2, hcols_c[h]], v_new[h], preferred_element_type=F32)
                   for h in heads]
            tail = gate_ref[pl.ds(base + (cc + 1) * CHUNK - 8, 8), :]
            decay_last = jnp.exp(tail[7:8, :])
            for h in heads:
                state_s[h] = st[h] * decay_last[:, DN_HEADS + h:DN_HEADS + h + 1] + upd[h]
            yield
            o = [r[h][CHUNK:] + jnp.dot(aqk_r[slot, rows, hcols_c[h]], v_new[h],
                                        preferred_element_type=F32) for h in heads]
            for h in heads:
                y = o[h] * lax.rsqrt(jnp.mean(o[h] * o[h], axis=-1, keepdims=True) + EPS) * gain
                gate = zg_ref[trows, hcols[h]].astype(F32)
                out_ref[trows, hcols[h]] = (y * gate).astype(out_ref.dtype)
            yield

    def run_interleaved(*gens):
        gens = list(gens)
        while gens:
            for gen in list(gens):
                if next(gen, StopIteration) is StopIteration:
                    gens.remove(gen)

    n_groups = n_chunks // PREP_CHUNKS
    state_s[...] = jnp.zeros_like(state_s)
    run_interleaved(conv(0, 0))
    run_interleaved(prep(0, 0), conv(1, 1))

    def pair_body(j, carry):
        run_interleaved(rec(2 * j, 0), prep(2 * j + 1, 1), conv(2 * j + 2, 0))
        run_interleaved(rec(2 * j + 1, 1), prep(2 * j + 2, 0), conv(2 * j + 3, 1))
        return carry

    lax.fori_loop(0, (n_groups - 2) // 2, pair_body, 0)
    run_interleaved(rec(n_groups - 2, 0), prep(n_groups - 1, 1))
    run_interleaved(rec(n_groups - 1, 1))


def _gated_deltanet(proj, gates, conv_w, out_gain):
    b, t, _ = proj.shape
    group_rows = PREP_CHUNKS * CHUNK
    assert t % (2 * group_rows) == 0 and t // group_rows >= 4
    ring = lambda rows, dtype: pltpu.VMEM((2, rows, DN_WIDTH), dtype)
    return pl.pallas_call(
        _gdn_kernel,
        grid=(b,),
        in_specs=[
            pl.BlockSpec((None, t, QKV_COLS), lambda i: (i, 0, 0)),
            pl.BlockSpec((None, t, DN_WIDTH), lambda i: (i, 0, QKV_COLS // DN_WIDTH)),
            pl.BlockSpec((None, t, N_GATES), lambda i: (i, 0, 0)),
            pl.BlockSpec((CONV_WIDTH, QKV_COLS), lambda i: (0, 0)),
            pl.BlockSpec((1, DN_HEAD), lambda i: (0, 0)),
        ],
        out_specs=pl.BlockSpec((None, t, DN_WIDTH), lambda i: (i, 0, 0)),
        out_shape=jax.ShapeDtypeStruct((b, t, DN_WIDTH), BF16),
        scratch_shapes=[
            ring(group_rows, F32),
            ring(group_rows, F32),
            ring(group_rows, F32),
            ring(group_rows, F32),
            ring(2 * group_rows, BF16),
            ring(group_rows, BF16),
            ring(2 * group_rows, BF16),
            pltpu.VMEM((DN_HEADS, DN_HEAD, DN_HEAD), F32),
            pltpu.VMEM((QKV_COLS // LANES, group_rows + 8, LANES), F32),
        ],
        compiler_params=pltpu.CompilerParams(
            dimension_semantics=("parallel",), vmem_limit_bytes=VMEM_LIMIT),
        name="gated_deltanet",
    )(proj, proj, gates, conv_w, out_gain)


AT_BQ = 512
AT_BK = 256
AT_CW = 256
AT_NQ = 2 * AT_BQ
AT_CHAINS = AT_NQ // AT_CW
AT_CPM = AT_BQ // AT_CW
AT_VROWS = DF_HEAD_V + 16
AT_PROJ_DELAY = 2
AT_HPS = 2
OUT_TN = 256


def _attention_tiles(n_heads, n_qblocks):
    all_chains = tuple(range(AT_CHAINS))
    lo = tuple(c for c in all_chains if c % AT_CPM == 0)
    hi = tuple(c for c in all_chains if c % AT_CPM == 1)
    tiles = []
    for hd in range(n_heads):
        for qi in range(n_qblocks):
            for kt in range(AT_CPM * qi + AT_CPM):
                d = kt - AT_CPM * qi
                chains, masked = ((all_chains, ()) if d < 0 else
                                  (all_chains, lo) if d == 0 else (hi, hi))
                tiles.append(dict(hd=hd, qi=qi, qb=hd * n_qblocks + qi, kt=kt, chains=chains,
                                  masked=masked, first=kt == 0))
    return tiles


def _diff_attn_kernel(q_ref, k_ref, vt_ref, zg_ref, lq1_ref, lk1_ref, lq2_ref, lk2_ref,
                      og_ref, x_ref, dn_ref, wout_ref, out_ref,
                      qq_s, vta_s, m_s, acc_s, y_s, s_a, s_b, c_a, c_b, p_a, p_b):
    t = q_ref.shape[0]
    n_qblocks = t // AT_BQ
    all_chains = tuple(range(AT_CHAINS))

    @pl.when(pl.program_id(1) == 0)
    def _():
        out_ref[...] = x_ref[...]

    def lanes(c):
        return slice(c * AT_CW, (c + 1) * AT_CW)

    first = lax.broadcasted_iota(jnp.int32, (AT_BQ, LANES), 1) < DF_HEAD_QK
    zero = jnp.zeros((AT_BQ, LANES), BF16)
    hcols = [slice(hd * LANES, (hd + 1) * LANES) for hd in range(AT_HPS)]
    for hd in range(AT_HPS):
        for qi in range(n_qblocks):
            blk = q_ref[qi * AT_BQ:(qi + 1) * AT_BQ, hcols[hd]]
            qq_s[hd * n_qblocks + qi, :AT_BQ, :] = jnp.where(first, blk, zero)
            qq_s[hd * n_qblocks + qi, AT_BQ:, :] = jnp.where(first, zero, blk)
        vta_s[hd, :DF_HEAD_V, :] = vt_ref[hcols[hd], :]
        vta_s[hd, DF_HEAD_V:, :] = jnp.ones((AT_VROWS - DF_HEAD_V, t), BF16)

    lam = (jnp.exp(jnp.sum(lq1_ref[...] * lk1_ref[...], axis=-1, keepdims=True))
           - jnp.exp(jnp.sum(lq2_ref[...] * lk2_ref[...], axis=-1, keepdims=True))
           + LAMBDA_INIT)

    krow = lax.broadcasted_iota(jnp.int32, (AT_BK, AT_CW), 0)
    qcol = lax.broadcasted_iota(jnp.int32, (AT_BK, AT_CW), 1)
    causal = krow <= qcol
    bufs = ((s_a, c_a, p_a), (s_b, c_b, p_b))

    def scores(tile, c, buf):
        rows = slice(tile["kt"] * AT_BK, (tile["kt"] + 1) * AT_BK)
        s = _dot_nt(k_ref[rows, hcols[tile["hd"]]], qq_s[tile["qb"], lanes(c), :])
        if c in tile["masked"]:
            s = jnp.where(causal, s, -jnp.inf)
        buf[0][:, lanes(c)] = s
        buf[1][:, lanes(c)] = jnp.max(s, axis=0, keepdims=True)

    def values(tile, c, buf):
        rows = slice(tile["kt"] * AT_BK, (tile["kt"] + 1) * AT_BK)
        return jnp.dot(vta_s[tile["hd"], :, rows], buf[2][:, lanes(c)],
                       preferred_element_type=F32)

    def finish(tile):
        qb = tile["qb"]
        rows = slice(tile["qi"] * AT_BQ, (tile["qi"] + 1) * AT_BQ)
        denom = acc_s[qb, DF_HEAD_V:DF_HEAD_V + 1, :]
        o_all = acc_s[qb, :DF_HEAD_V, :] * (1.0 / denom)
        o = o_all[:, :AT_BQ] - lam * o_all[:, AT_BQ:]
        y = (o * lax.rsqrt(jnp.mean(o * o, axis=0, keepdims=True) + EPS)).T
        y = y * og_ref[...] * (1.0 - LAMBDA_INIT)
        y_s[qb] = (y * zg_ref[rows, hcols[tile["hd"]]].astype(F32)).astype(BF16)

    def project(tile):
        rows = slice(tile["qi"] * AT_BQ, (tile["qi"] + 1) * AT_BQ)
        hd = tile["hd"]
        lhs = jnp.concatenate([y_s[tile["qb"]], dn_ref[rows, hcols[hd]]], axis=1)
        for nc in range(D_MODEL // OUT_TN):
            cols = slice(nc * OUT_TN, (nc + 1) * OUT_TN)
            out_ref[rows, cols] = out_ref[rows, cols] + jnp.dot(
                lhs, wout_ref[hd, :, cols], preferred_element_type=F32)

    tiles = _attention_tiles(AT_HPS, n_qblocks)
    due = {}
    for c in tiles[0]["chains"]:
        scores(tiles[0], c, bufs[0])
    for n, cur in enumerate(tiles):
        nxt = tiles[n + 1] if n + 1 < len(tiles) else None
        prv = tiles[n - 1] if n > 0 else None
        s_cur, c_cur, p_cur = bufs[n % 2]
        qb = cur["qb"]
        for c in all_chains:
            alpha = None
            if c in cur["chains"]:
                if cur["first"]:
                    m_new = c_cur[:, lanes(c)]
                else:
                    m_old = m_s[qb, :, lanes(c)]
                    m_new = jnp.maximum(m_old, c_cur[:, lanes(c)])
                    alpha = jnp.exp2(m_old - m_new)
                p = jnp.exp2(s_cur[:, lanes(c)] - m_new)
                m_s[qb, :, lanes(c)] = m_new
                p_cur[:, lanes(c)] = p.astype(BF16)
            if nxt is not None and c in nxt["chains"]:
                scores(nxt, c, bufs[(n + 1) % 2])
            if prv is not None and c in prv["chains"]:
                pv = values(prv, c, bufs[(n - 1) % 2])
                pq = prv["qb"]
                acc = pv if prv["first"] else acc_s[pq, :, lanes(c)] + pv
                if pq == qb and alpha is not None:
                    acc = alpha * acc
                acc_s[pq, :, lanes(c)] = acc
        if prv is not None and prv["qb"] != qb:
            finish(prv)
            due[n + AT_PROJ_DELAY] = prv
        if n in due:
            project(due.pop(n))
    last = tiles[-1]
    for c in last["chains"]:
        acc_s[last["qb"], :, lanes(c)] = (acc_s[last["qb"], :, lanes(c)]
                                          + values(last, c, bufs[(len(tiles) - 1) % 2]))
    finish(last)
    for tile in sorted(due.values(), key=lambda tl: tl["qb"]) + [last]:
        project(tile)


def _diff_attention(proj, vt, lq1, lk1, lq2, lk2, out_gain, x, mixed_dn, w_heads):
    b, t, _ = proj.shape
    assert AT_BQ == AT_CPM * AT_BK and AT_CW == AT_BK and AT_CPM == 2 and t % AT_BQ == 0
    assert DF_HEADS == DN_HEADS and DF_HEAD_V == DN_HEAD and D_MODEL % OUT_TN == 0
    assert DF_HEADS % AT_HPS == 0 and all(
        blk % AT_HPS == 0 for blk in (_BLK_DF_Q, _BLK_DF_K, _BLK_DF_Z))
    n_qblocks = t // AT_BQ
    n_blocks = AT_HPS * n_qblocks
    wide = AT_HPS * LANES
    small = lambda n: pl.BlockSpec((1, n), lambda i, h: (0, 0))
    return pl.pallas_call(
        _diff_attn_kernel,
        grid=(b, DF_HEADS // AT_HPS),
        in_specs=[
            pl.BlockSpec((None, t, wide), lambda i, h: (i, 0, _BLK_DF_Q // AT_HPS + h)),
            pl.BlockSpec((None, t, wide), lambda i, h: (i, 0, _BLK_DF_K // AT_HPS + h)),
            pl.BlockSpec((None, wide, t), lambda i, h: (i, h, 0)),
            pl.BlockSpec((None, t, wide), lambda i, h: (i, 0, _BLK_DF_Z // AT_HPS + h)),
            small(DF_HEAD_QK), small(DF_HEAD_QK), small(DF_HEAD_QK), small(DF_HEAD_QK),
            small(DF_HEAD_V),
            pl.BlockSpec((None, t, D_MODEL), lambda i, h: (i, 0, 0)),
            pl.BlockSpec((None, t, wide), lambda i, h: (i, 0, h)),
            pl.BlockSpec((AT_HPS, DF_HEAD_V + DN_HEAD, D_MODEL), lambda i, h: (h, 0, 0)),
        ],
        out_specs=pl.BlockSpec((None, t, D_MODEL), lambda i, h: (i, 0, 0)),
        out_shape=jax.ShapeDtypeStruct((b, t, D_MODEL), F32),
        scratch_shapes=[
            pltpu.VMEM((n_blocks, AT_NQ, LANES), BF16),
            pltpu.VMEM((AT_HPS, AT_VROWS, t), BF16),
            pltpu.VMEM((n_blocks, 1, AT_NQ), F32),
            pltpu.VMEM((n_blocks, AT_VROWS, AT_NQ), F32),
            pltpu.VMEM((n_blocks, AT_BQ, DF_HEAD_V), BF16),
            pltpu.VMEM((AT_BK, AT_NQ), F32),
            pltpu.VMEM((AT_BK, AT_NQ), F32),
            pltpu.VMEM((1, AT_NQ), F32),
            pltpu.VMEM((1, AT_NQ), F32),
            pltpu.VMEM((AT_BK, AT_NQ), BF16),
            pltpu.VMEM((AT_BK, AT_NQ), BF16),
        ],
        compiler_params=pltpu.CompilerParams(
            dimension_semantics=("parallel", "arbitrary"), vmem_limit_bytes=VMEM_LIMIT),
        name="diff_attention",
    )(proj, proj, vt, proj, lq1, lk1, lq2, lk2, out_gain, x, mixed_dn, w_heads)


def kernel(x, norm_gain, w_in, conv_w, a_log, dt_bias, dn_out_gain, q_gain, k_gain,
           lambda_q1, lambda_k1, lambda_q2, lambda_k2, df_out_gain, w_out):
    b, t, d = x.shape
    assert d == D_MODEL and norm_gain.shape[0] == 1
    x2d = x.reshape(b * t, d)

    w = w_in[0]
    gate_lo = 4 * DN_WIDTH
    df_lo = gate_lo + N_GATES
    w_main = jnp.concatenate([w[:, :gate_lo], w[:, df_lo:df_lo + 2 * DF_WIDTH],
                              w[:, df_lo + 3 * DF_WIDTH:]], axis=1).astype(BF16)
    w_t = jnp.concatenate([w[:, df_lo + 2 * DF_WIDTH:df_lo + 3 * DF_WIDTH],
                           jnp.pad(w[:, gate_lo:df_lo], ((0, 0), (0, GATE_ROWS - N_GATES)))],
                          axis=1).T.astype(BF16)
    half = jnp.arange(SEG_W, dtype=jnp.int32) // DF_HEAD_QK
    seg = (half[:, None] == half[None, :]).astype(BF16)
    reps = DF_WIDTH // DF_HEAD_QK
    qk_gain = jnp.concatenate([jnp.tile(q_gain, (1, reps)) * (DF_HEAD_QK ** -0.5 * LOG2_E),
                               jnp.tile(k_gain, (1, reps))], axis=0)

    pad_rows = lambda v: jnp.pad(v.reshape(DN_HEADS, 1),
                                 ((DN_HEADS, GATE_ROWS - 2 * DN_HEADS), (0, 0)))
    proj, vt, gates_t = _in_projection(x2d, t, norm_gain, w_main, w_t, seg, qk_gain,
                                       pad_rows(a_log), pad_rows(dt_bias))
    proj = proj.reshape(b, t, PROJ_COLS)
    gates = jnp.swapaxes(gates_t[:, :N_GATES, :], 1, 2)

    mixed_dn = _gated_deltanet(proj, gates, conv_w[0], dn_out_gain)

    wo = w_out[0].astype(BF16)
    w_heads = jnp.concatenate([wo[DN_WIDTH:].reshape(DF_HEADS, DF_HEAD_V, d),
                               wo[:DN_WIDTH].reshape(DN_HEADS, DN_HEAD, d)], axis=1)
    return _diff_attention(proj, vt, lambda_q1, lambda_k1, lambda_q2, lambda_k2,
                           df_out_gain, x, mixed_dn, w_heads)
```

```python
import math

import jax
import jax.numpy as jnp
from jax import lax
from jax.experimental import pallas as pl
from jax.experimental.pallas import tpu as pltpu

F32 = jnp.float32
BF16 = jnp.bfloat16

D_MODEL = 1024
DN_HEADS = 4
DN_HEAD = 128
DN_WIDTH = DN_HEADS * DN_HEAD
CONV_WIDTH = 4
CHUNK = 64
DF_HEADS = 4
DF_HEAD_QK = 64
DF_HEAD_V = 128
DF_WIDTH = DF_HEADS * DF_HEAD_V
EPS = 1e-6
LAMBDA_INIT = 0.8 - 0.6 * math.exp(-0.3 * 0)
LOG2_E = math.log2(math.e)

N_GATES = 2 * DN_HEADS
GATE_ROWS = 16
PROJ_COLS = 4 * DN_WIDTH + 3 * DF_WIDTH
LANES = 128
VMEM_LIMIT = 56 * 1024 * 1024

_BLK_DN_Z = 3 * DN_HEADS
_BLK_DF_Q = 4 * DN_HEADS
_BLK_DF_K = _BLK_DF_Q + DF_HEADS
_BLK_DF_Z = _BLK_DF_K + DF_HEADS


def _silu(x):
    return x * jax.nn.sigmoid(x)


def _dot(a, b):
    return jnp.dot(a.astype(BF16), b.astype(BF16), preferred_element_type=F32)


def _dot_nt(a, b):
    return lax.dot_general(a.astype(BF16), b.astype(BF16), (((1,), (1,)), ((), ())),
                           preferred_element_type=F32)


IN_TM = 1024
IN_TN = 512
SEG_W = 256


def _inproj_kernel(x_ref, gain_ref, w_ref, wt_ref, seg_ref, qkg_ref, alog_ref, dtb_ref,
                   proj_ref, vt_ref, gate_ref):
    x = x_ref[...]
    ms = jnp.mean(x * x, axis=-1, keepdims=True)
    h = (x * lax.rsqrt(ms + EPS) * gain_ref[...]).astype(BF16)

    yt = _dot_nt(wt_ref[...], h)
    vt_ref[...] = yt[:DF_WIDTH].astype(BF16)
    pre = yt[DF_WIDTH:]
    g = -jnp.exp(alog_ref[...]) * jax.nn.softplus(pre + dtb_ref[...])
    pos = lax.broadcasted_iota(jnp.int32, g.shape, 1) % CHUNK
    s = 1
    while s < CHUNK:
        g = g + jnp.where(pos >= s, pltpu.roll(g, s, 1), 0.0)
        s *= 2
    row = lax.broadcasted_iota(jnp.int32, g.shape, 0)
    gate_ref[...] = jnp.where(row < DN_HEADS, jax.nn.sigmoid(pre), g)

    for j in range(PROJ_COLS // IN_TN):
        cols = slice(j * IN_TN, (j + 1) * IN_TN)
        y = jnp.dot(h, w_ref[:, cols], preferred_element_type=F32)
        if j in (_BLK_DF_Q * LANES // IN_TN, _BLK_DF_K * LANES // IN_TN):
            which = 0 if j == _BLK_DF_Q * LANES // IN_TN else 1
            sq = (y * y).astype(BF16)
            ssq = jnp.concatenate(
                [jnp.dot(sq[:, i * SEG_W:(i + 1) * SEG_W], seg_ref[...],
                         preferred_element_type=F32) for i in range(IN_TN // SEG_W)], axis=1)
            y = y * lax.rsqrt(ssq * (1.0 / DF_HEAD_QK) + EPS) * qkg_ref[which:which + 1, :]
        elif j in (_BLK_DN_Z * LANES // IN_TN, _BLK_DF_Z * LANES // IN_TN):
            y = _silu(y)
        proj_ref[:, cols] = y.astype(BF16)


def _in_projection(x2d, t, gain, w_main, w_t, seg, qk_gain, alog_col, dtb_col):
    m = x2d.shape[0]
    assert DF_WIDTH == IN_TN and t % IN_TM == 0 and IN_TM % CHUNK == 0
    tiles_per_seq = t // IN_TM
    const = lambda shape: pl.BlockSpec(shape, lambda i: (0, 0))
    return pl.pallas_call(
        _inproj_kernel,
        grid=(m // IN_TM,),
        in_specs=[
            pl.BlockSpec((IN_TM, D_MODEL), lambda i: (i, 0)),
            const((1, D_MODEL)),
            const((D_MODEL, PROJ_COLS)),
            const((DF_WIDTH + GATE_ROWS, D_MODEL)),
            const((SEG_W, SEG_W)),
            const((2, DF_WIDTH)),
            const((GATE_ROWS, 1)),
            const((GATE_ROWS, 1)),
        ],
        out_specs=[
            pl.BlockSpec((IN_TM, PROJ_COLS), lambda i: (i, 0)),
            pl.BlockSpec((None, DF_WIDTH, IN_TM),
                         lambda i: (i // tiles_per_seq, 0, i % tiles_per_seq)),
            pl.BlockSpec((None, GATE_ROWS, IN_TM),
                         lambda i: (i // tiles_per_seq, 0, i % tiles_per_seq)),
        ],
        out_shape=[
            jax.ShapeDtypeStruct((m, PROJ_COLS), BF16),
            jax.ShapeDtypeStruct((m // t, DF_WIDTH, t), BF16),
            jax.ShapeDtypeStruct((m // t, GATE_ROWS, t), F32),
        ],
        compiler_params=pltpu.CompilerParams(
            dimension_semantics=("parallel",), vmem_limit_bytes=VMEM_LIMIT),
        name="in_projection",
    )(x2d, gain, w_main, w_t, seg, qk_gain, alog_col, dtb_col)


QKV_COLS = 3 * DN_WIDTH
PREP_CHUNKS = 4


def _gdn_kernel(qkv_ref, zg_ref, gate_ref, convw_ref, gain_ref,
                out_ref, q_r, k_r, v_r, u_r, wq_r, aqk_r, kdt_r, state_s, stage_s):
    t = qkv_ref.shape[0]
    n_chunks = t // CHUNK
    group_rows = PREP_CHUNKS * CHUNK

    def group_base(g):
        return g * group_rows if isinstance(g, int) else pl.multiple_of(g * group_rows, group_rows)

    def conv(g, slot):
        r0 = group_base(g)
        for cg in range(QKV_COLS // LANES):
            cols = slice(cg * LANES, (cg + 1) * LANES)
            cur = qkv_ref[pl.ds(r0, group_rows), cols].astype(F32)
            if isinstance(g, int) and g == 0:
                prev = jnp.zeros((8, LANES), F32)
            else:
                prev = qkv_ref[pl.ds(pl.multiple_of(r0 - 16, 16), 16), cols].astype(F32)[8:]
            stage_s[cg, :8, :] = prev
            stage_s[cg, 8:, :] = cur
            w = convw_ref[:, cols]
            acc = cur * w[CONV_WIDTH - 1:CONV_WIDTH]
            for j in range(1, CONV_WIDTH):
                acc = acc + (stage_s[cg, 8 - j:8 - j + group_rows, :]
                             * w[CONV_WIDTH - 1 - j:CONV_WIDTH - j])
            y = _silu(acc)
            hcols = slice((cg % DN_HEADS) * LANES, (cg % DN_HEADS + 1) * LANES)
            if cg < 2 * DN_HEADS:
                inv_norm = lax.rsqrt(jnp.sum(y * y, axis=-1, keepdims=True) + EPS)
                if cg < DN_HEADS:
                    q_r[slot, :, hcols] = y * (inv_norm * (DN_HEAD ** -0.5))
                else:
                    k_r[slot, :, hcols] = y * inv_norm
            else:
                v_r[slot, :, hcols] = y
            yield

    row = lax.broadcasted_iota(jnp.int32, (CHUNK, 2 * CHUNK), 0)
    col = lax.broadcasted_iota(jnp.int32, (CHUNK, 2 * CHUNK), 1)
    eye_f = (row == col).astype(F32)
    lower_f = (row >= col).astype(F32)
    strict_f = (row > col).astype(F32)
    upper_f = (col >= CHUNK).astype(F32)
    eye_hi = (col - CHUNK == row).astype(F32)
    zero_rows = jnp.zeros((CHUNK, DN_HEAD), BF16)

    items = [(cc, h) for cc in range(PREP_CHUNKS) for h in range(DN_HEADS)]
    heads = range(DN_HEADS)
    gain = gain_ref[...]
    hcols = [slice(h * LANES, (h + 1) * LANES) for h in heads]
    hcols_c = [slice(h * LANES, h * LANES + CHUNK) for h in heads]

    def halves(fn, seq):
        out = []
        mid = len(seq) // 2
        for part in (seq[:mid], seq[mid:]):
            out.extend(fn(x) for x in part)
            yield out

    def prep(g, slot):
        base = group_base(g)
        gates = [gate_ref[pl.ds(base + cc * CHUNK, CHUNK), :] for cc in range(PREP_CHUNKS)]
        ks, lhs, decay, rhs, qe, kdt = [], [], [], [], [], []
        for cc, h in items:
            rows = slice(cc * CHUNK, (cc + 1) * CHUNK)
            q = q_r[slot, rows, hcols[h]]
            k = k_r[slot, rows, hcols[h]]
            v = v_r[slot, rows, hcols[h]]
            beta_b = jnp.broadcast_to(gates[cc][:, h:h + 1], (CHUNK, DN_HEAD))
            gcb = jnp.broadcast_to(gates[cc][:, DN_HEADS + h:DN_HEADS + h + 1], (CHUNK, DN_HEAD))
            gr = jnp.sum(gcb * eye_f, axis=0, keepdims=True)
            decay.append(jnp.exp(jnp.minimum(gcb - gr, 0.0)))
            g_last = gcb[CHUNK - 1:CHUNK, :]
            eg = jnp.exp(gcb)
            kb = k * beta_b
            ks.append(jnp.concatenate([k.astype(BF16), zero_rows], axis=0))
            lhs.append(jnp.concatenate([kb, q], axis=0).astype(BF16))
            rhs.append(jnp.concatenate([v * beta_b, kb * eg], axis=1).astype(BF16))
            qe.append((q * eg).astype(BF16))
            kdt.append((k * jnp.exp(g_last - gcb)).T.astype(BF16))
        yield

        kq = None
        for kq in halves(lambda ab: _dot_nt(*ab), list(zip(lhs, ks))):
            yield
        m = [x[:CHUNK] * (d * strict_f) for x, d in zip(kq, decay)]
        a_qk = [(x[CHUNK:, :CHUNK] * (d * lower_f)[:, :CHUNK]).astype(BF16)
                for x, d in zip(kq, decay)]

        w = None
        for w in halves(lambda x: _dot(x[:, :CHUNK], x - eye_hi) + eye_hi, m):
            yield
        for _ in range(5):
            prev = w
            for w in halves(lambda x: _dot(x[:, :CHUNK], x) + x * upper_f, prev):
                yield
        sol = None
        for sol in halves(lambda xb: _dot(xb[0][:, CHUNK:], xb[1]), list(zip(w, rhs))):
            yield

        for n, (cc, h) in enumerate(items):
            rows = slice(cc * CHUNK, (cc + 1) * CHUNK)
            rows_w = slice(2 * cc * CHUNK, (2 * cc + 1) * CHUNK)
            rows_q = slice((2 * cc + 1) * CHUNK, (2 * cc + 2) * CHUNK)
            rows_t = slice(2 * cc * CHUNK, (2 * cc + 2) * CHUNK)
            u_r[slot, rows, hcols[h]] = sol[n][:, :DN_HEAD]
            wq_r[slot, rows_w, hcols[h]] = sol[n][:, DN_HEAD:].astype(BF16)
            wq_r[slot, rows_q, hcols[h]] = qe[n]
            aqk_r[slot, rows, hcols_c[h]] = a_qk[n]
            kdt_r[slot, rows_t, hcols_c[h]] = kdt[n]
        yield

    def rec(g, slot):
        base = group_base(g)
        for cc in range(PREP_CHUNKS):
            rows = slice(cc * CHUNK, (cc + 1) * CHUNK)
            rows2 = slice(2 * cc * CHUNK, (2 * cc + 2) * CHUNK)
            trows = pl.ds(base + cc * CHUNK, CHUNK)
            st = [state_s[h] for h in heads]
            r = [jnp.dot(wq_r[slot, rows2, hcols[h]], st[h].astype(BF16),
                         preferred_element_type=F32) for h in heads]
            yield
            v_new = [(u_r[slot, rows, hcols[h]] - r[h][:CHUNK]).astype(BF16) for h in heads]
            upd = [jnp.dot(kdt_r[slot, rows2, hcols_c[h]], v_new[h], preferred_element_type=F32)
                   for h in heads]
            tail = gate_ref[pl.ds(base + (cc + 1) * CHUNK - 8, 8), :]
            decay_last = jnp.exp(tail[7:8, :])
            for h in heads:
                state_s[h] = st[h] * decay_last[:, DN_HEADS + h:DN_HEADS + h + 1] + upd[h]
            yield
            o = [r[h][CHUNK:] + jnp.dot(aqk_r[slot, rows, hcols_c[h]], v_new[h],
                                        preferred_element_type=F32) for h in heads]
            for h in heads:
                y = o[h] * lax.rsqrt(jnp.mean(o[h] * o[h], axis=-1, keepdims=True) + EPS) * gain
                gate = zg_ref[trows, hcols[h]].astype(F32)
                out_ref[trows, hcols[h]] = (y * gate).astype(out_ref.dtype)
            yield

    def run_interleaved(*gens):
        gens = list(gens)
        while gens:
            for gen in list(gens):
                if next(gen, StopIteration) is StopIteration:
                    gens.remove(gen)

    n_groups = n_chunks // PREP_CHUNKS
    state_s[...] = jnp.zeros_like(state_s)
    run_interleaved(conv(0, 0))
    run_interleaved(prep(0, 0), conv(1, 1))

    def pair_body(j, carry):
        run_interleaved(rec(2 * j, 0), prep(2 * j + 1, 1), conv(2 * j + 2, 0))
        run_interleaved(rec(2 * j + 1, 1), prep(2 * j + 2, 0), conv(2 * j + 3, 1))
        return carry

    lax.fori_loop(0, (n_groups - 2) // 2, pair_body, 0)
    run_interleaved(rec(n_groups - 2, 0), prep(n_groups - 1, 1))
    run_interleaved(rec(n_groups - 1, 1))


def _gated_deltanet(proj, gates, conv_w, out_gain):
    b, t, _ = proj.shape
    group_rows = PREP_CHUNKS * CHUNK
    assert t % (2 * group_rows) == 0 and t // group_rows >= 4
    ring = lambda rows, dtype: pltpu.VMEM((2, rows, DN_WIDTH), dtype)
    return pl.pallas_call(
        _gdn_kernel,
        grid=(b,),
        in_specs=[
            pl.BlockSpec((None, t, QKV_COLS), lambda i: (i, 0, 0)),
            pl.BlockSpec((None, t, DN_WIDTH), lambda i: (i, 0, QKV_COLS // DN_WIDTH)),
            pl.BlockSpec((None, t, N_GATES), lambda i: (i, 0, 0)),
            pl.BlockSpec((CONV_WIDTH, QKV_COLS), lambda i: (0, 0)),
            pl.BlockSpec((1, DN_HEAD), lambda i: (0, 0)),
        ],
        out_specs=pl.BlockSpec((None, t, DN_WIDTH), lambda i: (i, 0, 0)),
        out_shape=jax.ShapeDtypeStruct((b, t, DN_WIDTH), BF16),
        scratch_shapes=[
            ring(group_rows, F32),
            ring(group_rows, F32),
            ring(group_rows, F32),
            ring(group_rows, F32),
            ring(2 * group_rows, BF16),
            ring(group_rows, BF16),
            ring(2 * group_rows, BF16),
            pltpu.VMEM((DN_HEADS, DN_HEAD, DN_HEAD), F32),
            pltpu.VMEM((QKV_COLS // LANES, group_rows + 8, LANES), F32),
        ],
        compiler_params=pltpu.CompilerParams(
            dimension_semantics=("parallel",), vmem_limit_bytes=VMEM_LIMIT),
        name="gated_deltanet",
    )(proj, proj, gates, conv_w, out_gain)


AT_BQ = 512
AT_BK = 256
AT_CW = 256
AT_NQ = 2 * AT_BQ
AT_CHAINS = AT_NQ // AT_CW
AT_CPM = AT_BQ // AT_CW
AT_VROWS = DF_HEAD_V + 16
AT_PROJ_DELAY = 2
AT_HPS = 2
OUT_TN = 256


def _attention_tiles(n_heads, n_qblocks):
    all_chains = tuple(range(AT_CHAINS))
    lo = tuple(c for c in all_chains if c % AT_CPM == 0)
    hi = tuple(c for c in all_chains if c % AT_CPM == 1)
    tiles = []
    for hd in range(n_heads):
        for qi in range(n_qblocks):
            for kt in range(AT_CPM * qi + AT_CPM):
                d = kt - AT_CPM * qi
                chains, masked = ((all_chains, ()) if d < 0 else
                                  (all_chains, lo) if d == 0 else (hi, hi))
                tiles.append(dict(hd=hd, qi=qi, qb=hd * n_qblocks + qi, kt=kt, chains=chains,
                                  masked=masked, first=kt == 0))
    return tiles


def _diff_attn_kernel(q_ref, k_ref, vt_ref, zg_ref, lq1_ref, lk1_ref, lq2_ref, lk2_ref,
                      og_ref, x_ref, dn_ref, wout_ref, out_ref,
                      qq_s, vta_s, m_s, acc_s, y_s, s_a, s_b, c_a, c_b, p_a, p_b):
    t = q_ref.shape[0]
    n_qblocks = t // AT_BQ
    all_chains = tuple(range(AT_CHAINS))

    @pl.when(pl.program_id(1) == 0)
    def _():
        out_ref[...] = x_ref[...]

    def lanes(c):
        return slice(c * AT_CW, (c + 1) * AT_CW)

    first = lax.broadcasted_iota(jnp.int32, (AT_BQ, LANES), 1) < DF_HEAD_QK
    zero = jnp.zeros((AT_BQ, LANES), BF16)
    hcols = [slice(hd * LANES, (hd + 1) * LANES) for hd in range(AT_HPS)]
    for hd in range(AT_HPS):
        for qi in range(n_qblocks):
            blk = q_ref[qi * AT_BQ:(qi + 1) * AT_BQ, hcols[hd]]
            qq_s[hd * n_qblocks + qi, :AT_BQ, :] = jnp.where(first, blk, zero)
            qq_s[hd * n_qblocks + qi, AT_BQ:, :] = jnp.where(first, zero, blk)
        vta_s[hd, :DF_HEAD_V, :] = vt_ref[hcols[hd], :]
        vta_s[hd, DF_HEAD_V:, :] = jnp.ones((AT_VROWS - DF_HEAD_V, t), BF16)

    lam = (jnp.exp(jnp.sum(lq1_ref[...] * lk1_ref[...], axis=-1, keepdims=True))
           - jnp.exp(jnp.sum(lq2_ref[...] * lk2_ref[...], axis=-1, keepdims=True))
           + LAMBDA_INIT)

    krow = lax.broadcasted_iota(jnp.int32, (AT_BK, AT_CW), 0)
    qcol = lax.broadcasted_iota(jnp.int32, (AT_BK, AT_CW), 1)
    causal = krow <= qcol
    bufs = ((s_a, c_a, p_a), (s_b, c_b, p_b))

    def scores(tile, c, buf):
        rows = slice(tile["kt"] * AT_BK, (tile["kt"] + 1) * AT_BK)
        s = _dot_nt(k_ref[rows, hcols[tile["hd"]]], qq_s[tile["qb"], lanes(c), :])
        if c in tile["masked"]:
            s = jnp.where(causal, s, -jnp.inf)
        buf[0][:, lanes(c)] = s
        buf[1][:, lanes(c)] = jnp.max(s, axis=0, keepdims=True)

    def values(tile, c, buf):
        rows = slice(tile["kt"] * AT_BK, (tile["kt"] + 1) * AT_BK)
        return jnp.dot(vta_s[tile["hd"], :, rows], buf[2][:, lanes(c)],
                       preferred_element_type=F32)

    def finish(tile):
        qb = tile["qb"]
        rows = slice(tile["qi"] * AT_BQ, (tile["qi"] + 1) * AT_BQ)
        denom = acc_s[qb, DF_HEAD_V:DF_HEAD_V + 1, :]
        o_all = acc_s[qb, :DF_HEAD_V, :] * (1.0 / denom)
        o = o_all[:, :AT_BQ] - lam * o_all[:, AT_BQ:]
        y = (o * lax.rsqrt(jnp.mean(o * o, axis=0, keepdims=True) + EPS)).T
        y = y * og_ref[...] * (1.0 - LAMBDA_INIT)
        y_s[qb] = (y * zg_ref[rows, hcols[tile["hd"]]].astype(F32)).astype(BF16)

    def project(tile):
        rows = slice(tile["qi"] * AT_BQ, (tile["qi"] + 1) * AT_BQ)
        hd = tile["hd"]
        lhs = jnp.concatenate([y_s[tile["qb"]], dn_ref[rows, hcols[hd]]], axis=1)
        for nc in range(D_MODEL // OUT_TN):
            cols = slice(nc * OUT_TN, (nc + 1) * OUT_TN)
            out_ref[rows, cols] = out_ref[rows, cols] + jnp.dot(
                lhs, wout_ref[hd, :, cols], preferred_element_type=F32)

    tiles = _attention_tiles(AT_HPS, n_qblocks)
    due = {}
    for c in tiles[0]["chains"]:
        scores(tiles[0], c, bufs[0])
    for n, cur in enumerate(tiles):
        nxt = tiles[n + 1] if n + 1 < len(tiles) else None
        prv = tiles[n - 1] if n > 0 else None
        s_cur, c_cur, p_cur = bufs[n % 2]
        qb = cur["qb"]
        for c in all_chains:
            alpha = None
            if c in cur["chains"]:
                if cur["first"]:
                    m_new = c_cur[:, lanes(c)]
                else:
                    m_old = m_s[qb, :, lanes(c)]
                    m_new = jnp.maximum(m_old, c_cur[:, lanes(c)])
                    alpha = jnp.exp2(m_old - m_new)
                p = jnp.exp2(s_cur[:, lanes(c)] - m_new)
                m_s[qb, :, lanes(c)] = m_new
                p_cur[:, lanes(c)] = p.astype(BF16)
            if nxt is not None and c in nxt["chains"]:
                scores(nxt, c, bufs[(n + 1) % 2])
            if prv is not None and c in prv["chains"]:
                pv = values(prv, c, bufs[(n - 1) % 2])
                pq = prv["qb"]
                acc = pv if prv["first"] else acc_s[pq, :, lanes(c)] + pv
                if pq == qb and alpha is not None:
                    acc = alpha * acc
                acc_s[pq, :, lanes(c)] = acc
        if prv is not None and prv["qb"] != qb:
            finish(prv)
            due[n + AT_PROJ_DELAY] = prv
        if n in due:
            project(due.pop(n))
    last = tiles[-1]
    for c in last["chains"]:
        acc_s[last["qb"], :, lanes(c)] = (acc_s[last["qb"], :, lanes(c)]
                                          + values(last, c, bufs[(len(tiles) - 1) % 2]))
    finish(last)
    for tile in sorted(due.values(), key=lambda tl: tl["qb"]) + [last]:
        project(tile)


def _diff_attention(proj, vt, lq1, lk1, lq2, lk2, out_gain, x, mixed_dn, w_heads):
    b, t, _ = proj.shape
    assert AT_BQ == AT_CPM * AT_BK and AT_CW == AT_BK and AT_CPM == 2 and t % AT_BQ == 0
    assert DF_HEADS == DN_HEADS and DF_HEAD_V == DN_HEAD and D_MODEL % OUT_TN == 0
    assert DF_HEADS % AT_HPS == 0 and all(
        blk % AT_HPS == 0 for blk in (_BLK_DF_Q, _BLK_DF_K, _BLK_DF_Z))
    n_qblocks = t // AT_BQ
    n_blocks = AT_HPS * n_qblocks
    wide = AT_HPS * LANES
    small = lambda n: pl.BlockSpec((1, n), lambda i, h: (0, 0))
    return pl.pallas_call(
        _diff_attn_kernel,
        grid=(b, DF_HEADS // AT_HPS),
        in_specs=[
            pl.BlockSpec((None, t, wide), lambda i, h: (i, 0, _BLK_DF_Q // AT_HPS + h)),
            pl.BlockSpec((None, t, wide), lambda i, h: (i, 0, _BLK_DF_K // AT_HPS + h)),
            pl.BlockSpec((None, wide, t), lambda i, h: (i, h, 0)),
            pl.BlockSpec((None, t, wide), lambda i, h: (i, 0, _BLK_DF_Z // AT_HPS + h)),
            small(DF_HEAD_QK), small(DF_HEAD_QK), small(DF_HEAD_QK), small(DF_HEAD_QK),
            small(DF_HEAD_V),
            pl.BlockSpec((None, t, D_MODEL), lambda i, h: (i, 0, 0)),
            pl.BlockSpec((None, t, wide), lambda i, h: (i, 0, h)),
            pl.BlockSpec((AT_HPS, DF_HEAD_V + DN_HEAD, D_MODEL), lambda i, h: (h, 0, 0)),
        ],
        out_specs=pl.BlockSpec((None, t, D_MODEL), lambda i, h: (i, 0, 0)),
        out_shape=jax.ShapeDtypeStruct((b, t, D_MODEL), F32),
        scratch_shapes=[
            pltpu.VMEM((n_blocks, AT_NQ, LANES), BF16),
            pltpu.VMEM((AT_HPS, AT_VROWS, t), BF16),
            pltpu.VMEM((n_blocks, 1, AT_NQ), F32),
            pltpu.VMEM((n_blocks, AT_VROWS, AT_NQ), F32),
            pltpu.VMEM((n_blocks, AT_BQ, DF_HEAD_V), BF16),
            pltpu.VMEM((AT_BK, AT_NQ), F32),
            pltpu.VMEM((AT_BK, AT_NQ), F32),
            pltpu.VMEM((1, AT_NQ), F32),
            pltpu.VMEM((1, AT_NQ), F32),
            pltpu.VMEM((AT_BK, AT_NQ), BF16),
            pltpu.VMEM((AT_BK, AT_NQ), BF16),
        ],
        compiler_params=pltpu.CompilerParams(
            dimension_semantics=("parallel", "arbitrary"), vmem_limit_bytes=VMEM_LIMIT),
        name="diff_attention",
    )(proj, proj, vt, proj, lq1, lk1, lq2, lk2, out_gain, x, mixed_dn, w_heads)


def kernel(x, norm_gain, w_in, conv_w, a_log, dt_bias, dn_out_gain, q_gain, k_gain,
           lambda_q1, lambda_k1, lambda_q2, lambda_k2, df_out_gain, w_out):
    b, t, d = x.shape
    assert d == D_MODEL and norm_gain.shape[0] == 1
    x2d = x.reshape(b * t, d)

    w = w_in[0]
    gate_lo = 4 * DN_WIDTH
    df_lo = gate_lo + N_GATES
    w_main = jnp.concatenate([w[:, :gate_lo], w[:, df_lo:df_lo + 2 * DF_WIDTH],
                              w[:, df_lo + 3 * DF_WIDTH:]], axis=1).astype(BF16)
    w_t = jnp.concatenate([w[:, df_lo + 2 * DF_WIDTH:df_lo + 3 * DF_WIDTH],
                           jnp.pad(w[:, gate_lo:df_lo], ((0, 0), (0, GATE_ROWS - N_GATES)))],
                          axis=1).astype(BF16).T
    half = jnp.arange(SEG_W, dtype=jnp.int32) // DF_HEAD_QK
    seg = (half[:, None] == half[None, :]).astype(BF16)
    reps = DF_WIDTH // DF_HEAD_QK
    qk_gain = jnp.concatenate([jnp.tile(q_gain, (1, reps)) * (DF_HEAD_QK ** -0.5 * LOG2_E),
                               jnp.tile(k_gain, (1, reps))], axis=0)

    pad_rows = lambda v: jnp.pad(v.reshape(DN_HEADS, 1),
                                 ((DN_HEADS, GATE_ROWS - 2 * DN_HEADS), (0, 0)))
    proj, vt, gates_t = _in_projection(x2d, t, norm_gain, w_main, w_t, seg, qk_gain,
                                       pad_rows(a_log), pad_rows(dt_bias))
    proj = proj.reshape(b, t, PROJ_COLS)
    gates = jnp.swapaxes(gates_t[:, :N_GATES, :], 1, 2)

    mixed_dn = _gated_deltanet(proj, gates, conv_w[0], dn_out_gain)

    wo = w_out[0].astype(BF16)
    w_heads = jnp.concatenate([wo[DN_WIDTH:].reshape(DF_HEADS, DF_HEAD_V, d),
                               wo[:DN_WIDTH].reshape(DN_HEADS, DN_HEAD, d)], axis=1)
    return _diff_attention(proj, vt, lambda_q1, lambda_k1, lambda_q2, lambda_k2,
                           df_out_gain, x, mixed_dn, w_heads)
```

```python
import functools
import math

import jax
import jax.numpy as jnp
from jax import lax
from jax.experimental import pallas as pl
from jax.experimental.pallas import tpu as pltpu

F32 = jnp.float32
BF16 = jnp.bfloat16

D_MODEL = 1024
DN_HEADS = 4
DN_HEAD = 128
DN_WIDTH = DN_HEADS * DN_HEAD
CONV_WIDTH = 4
CHUNK = 64
DF_HEADS = 4
DF_HEAD_QK = 64
DF_HEAD_V = 128
DF_WIDTH = DF_HEADS * DF_HEAD_V
EPS = 1e-6
LAMBDA_INIT = 0.8 - 0.6 * math.exp(-0.3 * 0)
LOG2_E = math.log2(math.e)

N_GATES = 2 * DN_HEADS
GATE_ROWS = 16
PROJ_COLS = 4 * DN_WIDTH + 3 * DF_WIDTH
LANES = 128
SUBLANES = 8
QKV_COLS = 3 * DN_WIDTH
VMEM_LIMIT = 56 * 1024 * 1024

_BLK_DN_Z = 3 * DN_HEADS
_BLK_DF_Q = 4 * DN_HEADS
_BLK_DF_K = _BLK_DF_Q + DF_HEADS
_BLK_DF_Z = _BLK_DF_K + DF_HEADS


def _silu(x):
    return x * jax.nn.sigmoid(x)


def _dot(a, b):
    return jnp.dot(a.astype(BF16), b.astype(BF16), preferred_element_type=F32)


def _dot_nt(a, b):
    return lax.dot_general(a.astype(BF16), b.astype(BF16), (((1,), (1,)), ((), ())),
                           preferred_element_type=F32)


IN_TM = 1024
IN_TN = 512
SEG_W = 256
CONV_ROWS = 512


def _inproj_kernel(x_ref, gain_ref, w_ref, wt_ref, seg_ref, qkg_ref, alog_ref, dtb_ref, convw_ref,
                   proj_ref, vt_ref, gate_ref, cstage_s, tail_s, *, tiles_per_seq):
    @pl.when(pl.program_id(0) % tiles_per_seq == 0)
    def _():
        tail_s[...] = jnp.zeros_like(tail_s)

    def stage(y, j):
        cstage_s[j, :SUBLANES, :] = tail_s[j]
        cstage_s[j, SUBLANES:, :] = y
        tail_s[j] = y[IN_TM - SUBLANES:, :]

    def conv_silu_norm(j, part):
        cols = slice(j * IN_TN, (j + 1) * IN_TN)
        r0 = SUBLANES + part * CONV_ROWS
        w = convw_ref[:, cols]
        acc = cstage_s[j, r0:r0 + CONV_ROWS, :] * w[CONV_WIDTH - 1:CONV_WIDTH]
        for d in range(1, CONV_WIDTH):
            acc = acc + (cstage_s[j, r0 - d:r0 - d + CONV_ROWS, :]
                         * w[CONV_WIDTH - 1 - d:CONV_WIDTH - d])
        a = _silu(acc)
        if j < 2:
            heads = []
            for hd in range(DN_HEADS):
                ah = a[:, hd * DN_HEAD:(hd + 1) * DN_HEAD]
                inv_norm = lax.rsqrt(jnp.sum(ah * ah, axis=-1, keepdims=True) + EPS)
                heads.append(ah * (inv_norm * (DN_HEAD ** -0.5) if j == 0 else inv_norm))
            a = jnp.concatenate(heads, axis=1)
        proj_ref[part * CONV_ROWS:(part + 1) * CONV_ROWS, cols] = a.astype(BF16)

    x = x_ref[...]
    ms = jnp.mean(x * x, axis=-1, keepdims=True)
    h = (x * lax.rsqrt(ms + EPS) * gain_ref[...]).astype(BF16)

    yt = _dot_nt(wt_ref[...], h)
    vt_ref[...] = yt[:DF_WIDTH].astype(BF16)
    pre = yt[DF_WIDTH:]
    g = -jnp.exp(alog_ref[...]) * jax.nn.softplus(pre + dtb_ref[...])
    pos = lax.broadcasted_iota(jnp.int32, g.shape, 1) % CHUNK
    s = 1
    while s < CHUNK:
        g = g + jnp.where(pos >= s, pltpu.roll(g, s, 1), 0.0)
        s *= 2
    row = lax.broadcasted_iota(jnp.int32, g.shape, 0)
    gate_ref[...] = jnp.where(row < DN_HEADS, jax.nn.sigmoid(pre), g)

    n_conv = QKV_COLS // IN_TN
    pending = []
    for j in range(PROJ_COLS // IN_TN):
        cols = slice(j * IN_TN, (j + 1) * IN_TN)
        y = jnp.dot(h, w_ref[:, cols], preferred_element_type=F32)
        if j < n_conv:
            stage(y, j)
            pending += [(j, part) for part in range(IN_TM // CONV_ROWS)]
            if j > 0:
                conv_silu_norm(*pending.pop(0))
            continue
        if pending:
            conv_silu_norm(*pending.pop(0))
        if j in (_BLK_DF_Q * LANES // IN_TN, _BLK_DF_K * LANES // IN_TN):
            which = 0 if j == _BLK_DF_Q * LANES // IN_TN else 1
            sq = (y * y).astype(BF16)
            ssq = jnp.concatenate(
                [jnp.dot(sq[:, i * SEG_W:(i + 1) * SEG_W], seg_ref[...],
                         preferred_element_type=F32) for i in range(IN_TN // SEG_W)], axis=1)
            y = y * lax.rsqrt(ssq * (1.0 / DF_HEAD_QK) + EPS) * qkg_ref[which:which + 1, :]
        elif j in (_BLK_DN_Z * LANES // IN_TN, _BLK_DF_Z * LANES // IN_TN):
            y = _silu(y)
        proj_ref[:, cols] = y.astype(BF16)
    for item in pending:
        conv_silu_norm(*item)


def _in_projection(x2d, t, gain, w_main, w_t, seg, qk_gain, alog_col, dtb_col, conv_w):
    m = x2d.shape[0]
    assert DF_WIDTH == IN_TN and DN_WIDTH == IN_TN and t % IN_TM == 0 and IN_TM % CHUNK == 0
    tiles_per_seq = t // IN_TM
    const = lambda shape: pl.BlockSpec(shape, lambda i: (0, 0))
    return pl.pallas_call(
        functools.partial(_inproj_kernel, tiles_per_seq=tiles_per_seq),
        grid=(m // IN_TM,),
        in_specs=[
            pl.BlockSpec((IN_TM, D_MODEL), lambda i: (i, 0)),
            const((1, D_MODEL)),
            const((D_MODEL, PROJ_COLS)),
            const((DF_WIDTH + GATE_ROWS, D_MODEL)),
            const((SEG_W, SEG_W)),
            const((2, DF_WIDTH)),
            const((GATE_ROWS, 1)),
            const((GATE_ROWS, 1)),
            const((CONV_WIDTH, QKV_COLS)),
        ],
        out_specs=[
            pl.BlockSpec((IN_TM, PROJ_COLS), lambda i: (i, 0)),
            pl.BlockSpec((None, DF_WIDTH, IN_TM),
                         lambda i: (i // tiles_per_seq, 0, i % tiles_per_seq)),
            pl.BlockSpec((None, GATE_ROWS, IN_TM),
                         lambda i: (i // tiles_per_seq, 0, i % tiles_per_seq)),
        ],
        out_shape=[
            jax.ShapeDtypeStruct((m, PROJ_COLS), BF16),
            jax.ShapeDtypeStruct((m // t, DF_WIDTH, t), BF16),
            jax.ShapeDtypeStruct((m // t, GATE_ROWS, t), F32),
        ],
        scratch_shapes=[
            pltpu.VMEM((QKV_COLS // IN_TN, IN_TM + SUBLANES, IN_TN), F32),
            pltpu.VMEM((QKV_COLS // IN_TN, SUBLANES, IN_TN), F32),
        ],
        compiler_params=pltpu.CompilerParams(
            dimension_semantics=("arbitrary",), vmem_limit_bytes=VMEM_LIMIT),
        name="in_projection",
    )(x2d, gain, w_main, w_t, seg, qk_gain, alog_col, dtb_col, conv_w)


PREP_CHUNKS = 4


def _gdn_kernel(qkv_ref, zg_ref, gate_ref, gain_ref,
                out_ref, u_r, wq_r, aqk_r, kdt_r, state_s):
    t = qkv_ref.shape[0]
    n_chunks = t // CHUNK
    group_rows = PREP_CHUNKS * CHUNK

    def group_base(g):
        return g * group_rows if isinstance(g, int) else pl.multiple_of(g * group_rows, group_rows)

    row = lax.broadcasted_iota(jnp.int32, (CHUNK, 2 * CHUNK), 0)
    col = lax.broadcasted_iota(jnp.int32, (CHUNK, 2 * CHUNK), 1)
    eye_f = (row == col).astype(F32)
    lower_f = (row >= col).astype(F32)
    strict_f = (row > col).astype(F32)
    upper_f = (col >= CHUNK).astype(F32)
    eye_hi = (col - CHUNK == row).astype(F32)
    zero_rows = jnp.zeros((CHUNK, DN_HEAD), BF16)

    items = [(cc, h) for cc in range(PREP_CHUNKS) for h in range(DN_HEADS)]
    heads = range(DN_HEADS)
    gain = gain_ref[...]
    hcols = [slice(h * LANES, (h + 1) * LANES) for h in heads]
    hcols_c = [slice(h * LANES, h * LANES + CHUNK) for h in heads]

    def halves(fn, seq):
        out = []
        mid = len(seq) // 2
        for part in (seq[:mid], seq[mid:]):
            out.extend(fn(x) for x in part)
            yield out

    def prep(g, slot):
        base = group_base(g)
        gates = [gate_ref[pl.ds(base + cc * CHUNK, CHUNK), :] for cc in range(PREP_CHUNKS)]
        ks, lhs, decay, rhs, qe, kdt = [], [], [], [], [], []
        for cc, h in items:
            trows = pl.ds(base + cc * CHUNK, CHUNK)
            q = qkv_ref[trows, hcols[h]].astype(F32)
            k = qkv_ref[trows, slice(DN_WIDTH + h * LANES, DN_WIDTH + (h + 1) * LANES)].astype(F32)
            v = qkv_ref[trows, slice(2 * DN_WIDTH + h * LANES,
                                     2 * DN_WIDTH + (h + 1) * LANES)].astype(F32)
            beta_b = jnp.broadcast_to(gates[cc][:, h:h + 1], (CHUNK, DN_HEAD))
            gcb = jnp.broadcast_to(gates[cc][:, DN_HEADS + h:DN_HEADS + h + 1], (CHUNK, DN_HEAD))
            gr = jnp.sum(gcb * eye_f, axis=0, keepdims=True)
            decay.append(jnp.exp(jnp.minimum(gcb - gr, 0.0)))
            g_last = gcb[CHUNK - 1:CHUNK, :]
            eg = jnp.exp(gcb)
            kb = k * beta_b
            ks.append(jnp.concatenate([k.astype(BF16), zero_rows], axis=0))
            lhs.append(jnp.concatenate([kb, q], axis=0).astype(BF16))
            rhs.append(jnp.concatenate([v * beta_b, kb * eg], axis=1).astype(BF16))
            qe.append((q * eg).astype(BF16))
            kdt.append((k * jnp.exp(g_last - gcb)).T.astype(BF16))
        yield

        kq = None
        for kq in halves(lambda ab: _dot_nt(*ab), list(zip(lhs, ks))):
            yield
        m = [x[:CHUNK] * (d * strict_f) for x, d in zip(kq, decay)]
        a_qk = [(x[CHUNK:, :CHUNK] * (d * lower_f)[:, :CHUNK]).astype(BF16)
                for x, d in zip(kq, decay)]

        w = None
        for w in halves(lambda x: _dot(x[:, :CHUNK], x - eye_hi) + eye_hi, m):
            yield
        for _ in range(5):
            prev = w
            for w in halves(lambda x: _dot(x[:, :CHUNK], x) + x * upper_f, prev):
                yield
        sol = None
        for sol in halves(lambda xb: _dot(xb[0][:, CHUNK:], xb[1]), list(zip(w, rhs))):
            yield

        for n, (cc, h) in enumerate(items):
            rows = slice(cc * CHUNK, (cc + 1) * CHUNK)
            rows_w = slice(2 * cc * CHUNK, (2 * cc + 1) * CHUNK)
            rows_q = slice((2 * cc + 1) * CHUNK, (2 * cc + 2) * CHUNK)
            rows_t = slice(2 * cc * CHUNK, (2 * cc + 2) * CHUNK)
            u_r[slot, rows, hcols[h]] = sol[n][:, :DN_HEAD]
            wq_r[slot, rows_w, hcols[h]] = sol[n][:, DN_HEAD:].astype(BF16)
            wq_r[slot, rows_q, hcols[h]] = qe[n]
            aqk_r[slot, rows, hcols_c[h]] = a_qk[n]
            kdt_r[slot, rows_t, hcols_c[h]] = kdt[n]
        yield

    def rec(g, slot):
        base = group_base(g)
        for cc in range(PREP_CHUNKS):
            rows = slice(cc * CHUNK, (cc + 1) * CHUNK)
            rows2 = slice(2 * cc * CHUNK, (2 * cc + 2) * CHUNK)
            trows = pl.ds(base + cc * CHUNK, CHUNK)
            st = [state_s[h] for h in heads]
            r = [jnp.dot(wq_r[slot, rows2, hcols[h]], st[h].astype(BF16),
                         preferred_element_type=F32) for h in heads]
            yield
            v_new = [(u_r[slot, rows, hcols[h]] - r[h][:CHUNK]).astype(BF16) for h in heads]
            upd = [jnp.dot(kdt_r[slot, rows2, hcols_c[h]], v_new[h], preferred_element_type=F32)
                   for h in heads]
            tail = gate_ref[pl.ds(base + (cc + 1) * CHUNK - 8, 8), :]
            decay_last = jnp.exp(tail[7:8, :])
            for h in heads:
                state_s[h] = st[h] * decay_last[:, DN_HEADS + h:DN_HEADS + h + 1] + upd[h]
            yield
            o = [r[h][CHUNK:] + jnp.dot(aqk_r[slot, rows, hcols_c[h]], v_new[h],
                                        preferred_element_type=F32) for h in heads]
            for h in heads:
                y = o[h] * lax.rsqrt(jnp.mean(o[h] * o[h], axis=-1, keepdims=True) + EPS) * gain
                gate = zg_ref[trows, hcols[h]].astype(F32)
                out_ref[trows, hcols[h]] = (y * gate).astype(out_ref.dtype)
            yield

    def run_interleaved(*gens):
        gens = list(gens)
        while gens:
            for gen in list(gens):
                if next(gen, StopIteration) is StopIteration:
                    gens.remove(gen)

    n_groups = n_chunks // PREP_CHUNKS
    state_s[...] = jnp.zeros_like(state_s)
    run_interleaved(prep(0, 0))

    def pair_body(j, carry):
        run_interleaved(rec(2 * j, 0), prep(2 * j + 1, 1))
        run_interleaved(rec(2 * j + 1, 1), prep(2 * j + 2, 0))
        return carry

    lax.fori_loop(0, (n_groups - 2) // 2, pair_body, 0)
    run_interleaved(rec(n_groups - 2, 0), prep(n_groups - 1, 1))
    run_interleaved(rec(n_groups - 1, 1))


def _gated_deltanet(proj, gates, out_gain):
    b, t, _ = proj.shape
    group_rows = PREP_CHUNKS * CHUNK
    assert t % (2 * group_rows) == 0 and t // group_rows >= 4
    ring = lambda rows, dtype: pltpu.VMEM((2, rows, DN_WIDTH), dtype)
    return pl.pallas_call(
        _gdn_kernel,
        grid=(b,),
        in_specs=[
            pl.BlockSpec((None, t, QKV_COLS), lambda i: (i, 0, 0)),
            pl.BlockSpec((None, t, DN_WIDTH), lambda i: (i, 0, QKV_COLS // DN_WIDTH)),
            pl.BlockSpec((None, t, N_GATES), lambda i: (i, 0, 0)),
            pl.BlockSpec((1, DN_HEAD), lambda i: (0, 0)),
        ],
        out_specs=pl.BlockSpec((None, t, DN_WIDTH), lambda i: (i, 0, 0)),
        out_shape=jax.ShapeDtypeStruct((b, t, DN_WIDTH), BF16),
        scratch_shapes=[
            ring(group_rows, F32),
            ring(2 * group_rows, BF16),
            ring(group_rows, BF16),
            ring(2 * group_rows, BF16),
            pltpu.VMEM((DN_HEADS, DN_HEAD, DN_HEAD), F32),
        ],
        compiler_params=pltpu.CompilerParams(
            dimension_semantics=("parallel",), vmem_limit_bytes=VMEM_LIMIT),
        name="gated_deltanet",
    )(proj, proj, gates, out_gain)


AT_BQ = 512
AT_BK = 256
AT_CW = 256
AT_NQ = 2 * AT_BQ
AT_CHAINS = AT_NQ // AT_CW
AT_CPM = AT_BQ // AT_CW
AT_VROWS = DF_HEAD_V + 16
AT_PROJ_DELAY = 2
AT_HPS = 2
OUT_TN = 256


def _attention_tiles(n_heads, n_qblocks):
    all_chains = tuple(range(AT_CHAINS))
    lo = tuple(c for c in all_chains if c % AT_CPM == 0)
    hi = tuple(c for c in all_chains if c % AT_CPM == 1)
    tiles = []
    for hd in range(n_heads):
        for qi in range(n_qblocks):
            for kt in range(AT_CPM * qi + AT_CPM):
                d = kt - AT_CPM * qi
                chains, masked = ((all_chains, ()) if d < 0 else
                                  (all_chains, lo) if d == 0 else (hi, hi))
                tiles.append(dict(hd=hd, qi=qi, qb=hd * n_qblocks + qi, kt=kt, chains=chains,
                                  masked=masked, first=kt == 0))
    return tiles


def _diff_attn_kernel(q_ref, k_ref, vt_ref, zg_ref, lq1_ref, lk1_ref, lq2_ref, lk2_ref,
                      og_ref, x_ref, dn_ref, wout_ref, out_ref,
                      qq_s, vta_s, m_s, acc_s, y_s, s_a, s_b, c_a, c_b, p_a, p_b):
    t = q_ref.shape[0]
    n_qblocks = t // AT_BQ
    all_chains = tuple(range(AT_CHAINS))

    @pl.when(pl.program_id(1) == 0)
    def _():
        out_ref[...] = x_ref[...]

    def lanes(c):
        return slice(c * AT_CW, (c + 1) * AT_CW)

    first = lax.broadcasted_iota(jnp.int32, (AT_BQ, LANES), 1) < DF_HEAD_QK
    zero = jnp.zeros((AT_BQ, LANES), BF16)
    hcols = [slice(hd * LANES, (hd + 1) * LANES) for hd in range(AT_HPS)]
    for hd in range(AT_HPS):
        for qi in range(n_qblocks):
            blk = q_ref[qi * AT_BQ:(qi + 1) * AT_BQ, hcols[hd]]
            qq_s[hd * n_qblocks + qi, :AT_BQ, :] = jnp.where(first, blk, zero)
            qq_s[hd * n_qblocks + qi, AT_BQ:, :] = jnp.where(first, zero, blk)
        vta_s[hd, :DF_HEAD_V, :] = vt_ref[hcols[hd], :]
        vta_s[hd, DF_HEAD_V:, :] = jnp.ones((AT_VROWS - DF_HEAD_V, t), BF16)

    lam = (jnp.exp(jnp.sum(lq1_ref[...] * lk1_ref[...], axis=-1, keepdims=True))
           - jnp.exp(jnp.sum(lq2_ref[...] * lk2_ref[...], axis=-1, keepdims=True))
           + LAMBDA_INIT)

    krow = lax.broadcasted_iota(jnp.int32, (AT_BK, AT_CW), 0)
    qcol = lax.broadcasted_iota(jnp.int32, (AT_BK, AT_CW), 1)
    causal = krow <= qcol
    bufs = ((s_a, c_a, p_a), (s_b, c_b, p_b))

    def scores(tile, c, buf):
        rows = slice(tile["kt"] * AT_BK, (tile["kt"] + 1) * AT_BK)
        s = _dot_nt(k_ref[rows, hcols[tile["hd"]]], qq_s[tile["qb"], lanes(c), :])
        if c in tile["masked"]:
            s = jnp.where(causal, s, -jnp.inf)
        buf[0][:, lanes(c)] = s
        buf[1][:, lanes(c)] = jnp.max(s, axis=0, keepdims=True)

    def values(tile, c, buf):
        rows = slice(tile["kt"] * AT_BK, (tile["kt"] + 1) * AT_BK)
        return jnp.dot(vta_s[tile["hd"], :, rows], buf[2][:, lanes(c)],
                       preferred_element_type=F32)

    def finish(tile):
        qb = tile["qb"]
        rows = slice(tile["qi"] * AT_BQ, (tile["qi"] + 1) * AT_BQ)
        denom = acc_s[qb, DF_HEAD_V:DF_HEAD_V + 1, :]
        o_all = acc_s[qb, :DF_HEAD_V, :] * (1.0 / denom)
        o = o_all[:, :AT_BQ] - lam * o_all[:, AT_BQ:]
        y = (o * lax.rsqrt(jnp.mean(o * o, axis=0, keepdims=True) + EPS)).T
        y = y * og_ref[...] * (1.0 - LAMBDA_INIT)
        y_s[qb] = (y * zg_ref[rows, hcols[tile["hd"]]].astype(F32)).astype(BF16)

    def project(tile):
        rows = slice(tile["qi"] * AT_BQ, (tile["qi"] + 1) * AT_BQ)
        hd = tile["hd"]
        lhs = jnp.concatenate([y_s[tile["qb"]], dn_ref[rows, hcols[hd]]], axis=1)
        for nc in range(D_MODEL // OUT_TN):
            cols = slice(nc * OUT_TN, (nc + 1) * OUT_TN)
            out_ref[rows, cols] = out_ref[rows, cols] + jnp.dot(
                lhs, wout_ref[hd, :, cols], preferred_element_type=F32)

    tiles = _attention_tiles(AT_HPS, n_qblocks)
    due = {}
    for c in tiles[0]["chains"]:
        scores(tiles[0], c, bufs[0])
    for n, cur in enumerate(tiles):
        nxt = tiles[n + 1] if n + 1 < len(tiles) else None
        prv = tiles[n - 1] if n > 0 else None
        s_cur, c_cur, p_cur = bufs[n % 2]
        qb = cur["qb"]
        for c in all_chains:
            alpha = None
            if c in cur["chains"]:
                if cur["first"]:
                    m_new = c_cur[:, lanes(c)]
                else:
                    m_old = m_s[qb, :, lanes(c)]
                    m_new = jnp.maximum(m_old, c_cur[:, lanes(c)])
                    alpha = jnp.exp2(m_old - m_new)
                p = jnp.exp2(s_cur[:, lanes(c)] - m_new)
                m_s[qb, :, lanes(c)] = m_new
                p_cur[:, lanes(c)] = p.astype(BF16)
            if nxt is not None and c in nxt["chains"]:
                scores(nxt, c, bufs[(n + 1) % 2])
            if prv is not None and c in prv["chains"]:
                pv = values(prv, c, bufs[(n - 1) % 2])
                pq = prv["qb"]
                acc = pv if prv["first"] else acc_s[pq, :, lanes(c)] + pv
                if pq == qb and alpha is not None:
                    acc = alpha * acc
                acc_s[pq, :, lanes(c)] = acc
        if prv is not None and prv["qb"] != qb:
            finish(prv)
            due[n + AT_PROJ_DELAY] = prv
        if n in due:
            project(due.pop(n))
    last = tiles[-1]
    for c in last["chains"]:
        acc_s[last["qb"], :, lanes(c)] = (acc_s[last["qb"], :, lanes(c)]
                                          + values(last, c, bufs[(len(tiles) - 1) % 2]))
    finish(last)
    for tile in sorted(due.values(), key=lambda tl: tl["qb"]) + [last]:
        project(tile)


def _diff_attention(proj, vt, lq1, lk1, lq2, lk2, out_gain, x, mixed_dn, w_heads):
    b, t, _ = proj.shape
    assert AT_BQ == AT_CPM * AT_BK and AT_CW == AT_BK and AT_CPM == 2 and t % AT_BQ == 0
    assert DF_HEADS == DN_HEADS and DF_HEAD_V == DN_HEAD and D_MODEL % OUT_TN == 0
    assert DF_HEADS % AT_HPS == 0 and all(
        blk % AT_HPS == 0 for blk in (_BLK_DF_Q, _BLK_DF_K, _BLK_DF_Z))
    n_qblocks = t // AT_BQ
    n_blocks = AT_HPS * n_qblocks
    wide = AT_HPS * LANES
    small = lambda n: pl.BlockSpec((1, n), lambda i, h: (0, 0))
    return pl.pallas_call(
        _diff_attn_kernel,
        grid=(b, DF_HEADS // AT_HPS),
        in_specs=[
            pl.BlockSpec((None, t, wide), lambda i, h: (i, 0, _BLK_DF_Q // AT_HPS + h)),
            pl.BlockSpec((None, t, wide), lambda i, h: (i, 0, _BLK_DF_K // AT_HPS + h)),
            pl.BlockSpec((None, wide, t), lambda i, h: (i, h, 0)),
            pl.BlockSpec((None, t, wide), lambda i, h: (i, 0, _BLK_DF_Z // AT_HPS + h)),
            small(DF_HEAD_QK), small(DF_HEAD_QK), small(DF_HEAD_QK), small(DF_HEAD_QK),
            small(DF_HEAD_V),
            pl.BlockSpec((None, t, D_MODEL), lambda i, h: (i, 0, 0)),
            pl.BlockSpec((None, t, wide), lambda i, h: (i, 0, h)),
            pl.BlockSpec((AT_HPS, DF_HEAD_V + DN_HEAD, D_MODEL), lambda i, h: (h, 0, 0)),
        ],
        out_specs=pl.BlockSpec((None, t, D_MODEL), lambda i, h: (i, 0, 0)),
        out_shape=jax.ShapeDtypeStruct((b, t, D_MODEL), F32),
        scratch_shapes=[
            pltpu.VMEM((n_blocks, AT_NQ, LANES), BF16),
            pltpu.VMEM((AT_HPS, AT_VROWS, t), BF16),
            pltpu.VMEM((n_blocks, 1, AT_NQ), F32),
            pltpu.VMEM((n_blocks, AT_VROWS, AT_NQ), F32),
            pltpu.VMEM((n_blocks, AT_BQ, DF_HEAD_V), BF16),
            pltpu.VMEM((AT_BK, AT_NQ), F32),
            pltpu.VMEM((AT_BK, AT_NQ), F32),
            pltpu.VMEM((1, AT_NQ), F32),
            pltpu.VMEM((1, AT_NQ), F32),
            pltpu.VMEM((AT_BK, AT_NQ), BF16),
            pltpu.VMEM((AT_BK, AT_NQ), BF16),
        ],
        compiler_params=pltpu.CompilerParams(
            dimension_semantics=("parallel", "arbitrary"), vmem_limit_bytes=VMEM_LIMIT),
        name="diff_attention",
    )(proj, proj, vt, proj, lq1, lk1, lq2, lk2, out_gain, x, mixed_dn, w_heads)


def kernel(x, norm_gain, w_in, conv_w, a_log, dt_bias, dn_out_gain, q_gain, k_gain,
           lambda_q1, lambda_k1, lambda_q2, lambda_k2, df_out_gain, w_out):
    b, t, d = x.shape
    assert d == D_MODEL and norm_gain.shape[0] == 1
    x2d = x.reshape(b * t, d)

    w = w_in[0]
    gate_lo = 4 * DN_WIDTH
    df_lo = gate_lo + N_GATES
    w_main = jnp.concatenate([w[:, :gate_lo], w[:, df_lo:df_lo + 2 * DF_WIDTH],
                              w[:, df_lo + 3 * DF_WIDTH:]], axis=1).astype(BF16)
    w_t = jnp.concatenate([w[:, df_lo + 2 * DF_WIDTH:df_lo + 3 * DF_WIDTH],
                           jnp.pad(w[:, gate_lo:df_lo], ((0, 0), (0, GATE_ROWS - N_GATES)))],
                          axis=1).astype(BF16).T
    half = jnp.arange(SEG_W, dtype=jnp.int32) // DF_HEAD_QK
    seg = (half[:, None] == half[None, :]).astype(BF16)
    reps = DF_WIDTH // DF_HEAD_QK
    qk_gain = jnp.concatenate([jnp.tile(q_gain, (1, reps)) * (DF_HEAD_QK ** -0.5 * LOG2_E),
                               jnp.tile(k_gain, (1, reps))], axis=0)

    pad_rows = lambda v: jnp.pad(v.reshape(DN_HEADS, 1),
                                 ((DN_HEADS, GATE_ROWS - 2 * DN_HEADS), (0, 0)))
    proj, vt, gates_t = _in_projection(x2d, t, norm_gain, w_main, w_t, seg, qk_gain,
                                       pad_rows(a_log), pad_rows(dt_bias), conv_w[0])
    proj = proj.reshape(b, t, PROJ_COLS)
    gates = jnp.swapaxes(gates_t[:, :N_GATES, :], 1, 2)

    mixed_dn = _gated_deltanet(proj, gates, dn_out_gain)

    wo = w_out[0].astype(BF16)
    w_heads = jnp.concatenate([wo[DN_WIDTH:].reshape(DF_HEADS, DF_HEAD_V, d),
                               wo[:DN_WIDTH].reshape(DN_HEADS, DN_HEAD, d)], axis=1)
    return _diff_attention(proj, vt, lambda_q1, lambda_k1, lambda_q2, lambda_k2,
                           df_out_gain, x, mixed_dn, w_heads)
```

```python
import functools
import math

import jax
import jax.numpy as jnp
from jax import lax
from jax.experimental import pallas as pl
from jax.experimental.pallas import tpu as pltpu

F32 = jnp.float32
BF16 = jnp.bfloat16

D_MODEL = 1024
DN_HEADS = 4
DN_HEAD = 128
DN_WIDTH = DN_HEADS * DN_HEAD
CONV_WIDTH = 4
CHUNK = 64
DF_HEADS = 4
DF_HEAD_QK = 64
DF_HEAD_V = 128
DF_WIDTH = DF_HEADS * DF_HEAD_V
EPS = 1e-6
LAMBDA_INIT = 0.8 - 0.6 * math.exp(-0.3 * 0)
LOG2_E = math.log2(math.e)

N_GATES = 2 * DN_HEADS
GATE_ROWS = 16
PROJ_COLS = 4 * DN_WIDTH + 3 * DF_WIDTH
LANES = 128
SUBLANES = 8
QKV_COLS = 3 * DN_WIDTH
VMEM_LIMIT = 56 * 1024 * 1024

_BLK_DN_Z = 3 * DN_HEADS
_BLK_DF_Q = 4 * DN_HEADS
_BLK_DF_K = _BLK_DF_Q + DF_HEADS
_BLK_DF_Z = _BLK_DF_K + DF_HEADS


def _silu(x):
    return x * jax.nn.sigmoid(x)


def _dot(a, b):
    return jnp.dot(a.astype(BF16), b.astype(BF16), preferred_element_type=F32)


def _dot_nt(a, b):
    return lax.dot_general(a.astype(BF16), b.astype(BF16), (((1,), (1,)), ((), ())),
                           preferred_element_type=F32)


IN_TM = 1024
IN_TN = 512
SEG_W = 256
CONV_ROWS = 512


def _inproj_kernel(x_ref, gain_ref, w_ref, wt_ref, seg_ref, qkg_ref, alog_ref, dtb_ref,
                   convw_ref, proj_ref, vt_ref, gate_ref, h_s, cstage_s, tail_s, *, tiles_per_seq):
    @pl.when(pl.program_id(0) % tiles_per_seq == 0)
    def _():
        tail_s[...] = jnp.zeros_like(tail_s)

    def stage(y, j):
        for hd in range(DN_HEADS):
            yh = y[:, hd * DN_HEAD:(hd + 1) * DN_HEAD]
            cstage_s[j * DN_HEADS + hd, :SUBLANES, :] = tail_s[j * DN_HEADS + hd]
            cstage_s[j * DN_HEADS + hd, SUBLANES:, :] = yh
            tail_s[j * DN_HEADS + hd] = yh[IN_TM - SUBLANES:, :]

    def conv_silu_norm(j, part):
        r0 = SUBLANES + part * CONV_ROWS
        for hd in range(DN_HEADS):
            cols = slice(j * IN_TN + hd * DN_HEAD, j * IN_TN + (hd + 1) * DN_HEAD)
            buf = cstage_s.at[j * DN_HEADS + hd]
            w = convw_ref[:, cols]
            acc = buf[r0:r0 + CONV_ROWS, :] * w[CONV_WIDTH - 1:CONV_WIDTH]
            for d in range(1, CONV_WIDTH):
                acc = acc + buf[r0 - d:r0 - d + CONV_ROWS, :] * w[CONV_WIDTH - 1 - d:CONV_WIDTH - d]
            a = _silu(acc)
            if j < 2:
                inv_norm = lax.rsqrt(jnp.sum(a * a, axis=-1, keepdims=True) + EPS)
                a = a * (inv_norm * (DN_HEAD ** -0.5) if j == 0 else inv_norm)
            proj_ref[part * CONV_ROWS:(part + 1) * CONV_ROWS, cols] = a.astype(BF16)

    def head_phase():
        x = x_ref[...]
        ms = jnp.mean(x * x, axis=-1, keepdims=True)
        h = (x * lax.rsqrt(ms + EPS) * gain_ref[...]).astype(BF16)
        h_s[...] = h
        yt = _dot_nt(wt_ref[...], h)
        vt_ref[...] = yt[:DF_WIDTH].astype(BF16)
        pre = yt[DF_WIDTH:]
        g = -jnp.exp(alog_ref[...]) * jax.nn.softplus(pre + dtb_ref[...])
        pos = lax.broadcasted_iota(jnp.int32, g.shape, 1) % CHUNK
        s = 1
        while s < CHUNK:
            g = g + jnp.where(pos >= s, pltpu.roll(g, s, 1), 0.0)
            s *= 2
        row = lax.broadcasted_iota(jnp.int32, g.shape, 0)
        gate_ref[...] = jnp.where(row < DN_HEADS, jax.nn.sigmoid(pre), g)

    def chunk_phase(j, conv_item):
        cols = slice(j * IN_TN, (j + 1) * IN_TN)
        y = jnp.dot(h_s[...], w_ref[:, cols], preferred_element_type=F32)
        if conv_item is not None:
            conv_silu_norm(*conv_item)
        if j < QKV_COLS // IN_TN:
            stage(y, j)
            return
        if j in (_BLK_DF_Q * LANES // IN_TN, _BLK_DF_K * LANES // IN_TN):
            which = 0 if j == _BLK_DF_Q * LANES // IN_TN else 1
            sq = (y * y).astype(BF16)
            ssq = jnp.concatenate(
                [jnp.dot(sq[:, i * SEG_W:(i + 1) * SEG_W], seg_ref[...],
                         preferred_element_type=F32) for i in range(IN_TN // SEG_W)], axis=1)
            y = y * lax.rsqrt(ssq * (1.0 / DF_HEAD_QK) + EPS) * qkg_ref[which:which + 1, :]
        elif j in (_BLK_DN_Z * LANES // IN_TN, _BLK_DF_Z * LANES // IN_TN):
            y = _silu(y)
        proj_ref[:, cols] = y.astype(BF16)

    head_phase()
    pending = []
    for j in range(PROJ_COLS // IN_TN):
        item = pending.pop(0) if pending else None
        chunk_phase(j, item)
        if j < QKV_COLS // IN_TN:
            pending += [(j, part) for part in range(IN_TM // CONV_ROWS)]
    for item in pending:
        conv_silu_norm(*item)


def _in_projection(x2d, t, gain, w_main, w_t, seg, qk_gain, alog_col, dtb_col, conv_w):
    m = x2d.shape[0]
    assert DF_WIDTH == IN_TN and DN_WIDTH == IN_TN and t % IN_TM == 0 and IN_TM % CHUNK == 0
    tiles_per_seq = t // IN_TM
    const = lambda shape: pl.BlockSpec(shape, lambda i: (0, 0))
    return pl.pallas_call(
        functools.partial(_inproj_kernel, tiles_per_seq=tiles_per_seq),
        grid=(m // IN_TM,),
        in_specs=[
            pl.BlockSpec((IN_TM, D_MODEL), lambda i: (i, 0)),
            const((1, D_MODEL)),
            const((D_MODEL, PROJ_COLS)),
            const((DF_WIDTH + GATE_ROWS, D_MODEL)),
            const((SEG_W, SEG_W)),
            const((2, DF_WIDTH)),
            const((GATE_ROWS, 1)),
            const((GATE_ROWS, 1)),
            const((CONV_WIDTH, QKV_COLS)),
        ],
        out_specs=[
            pl.BlockSpec((IN_TM, PROJ_COLS), lambda i: (i, 0)),
            pl.BlockSpec((None, DF_WIDTH, IN_TM),
                         lambda i: (i // tiles_per_seq, 0, i % tiles_per_seq)),
            pl.BlockSpec((None, GATE_ROWS, IN_TM),
                         lambda i: (i // tiles_per_seq, 0, i % tiles_per_seq)),
        ],
        out_shape=[
            jax.ShapeDtypeStruct((m, PROJ_COLS), BF16),
            jax.ShapeDtypeStruct((m // t, DF_WIDTH, t), BF16),
            jax.ShapeDtypeStruct((m // t, GATE_ROWS, t), F32),
        ],
        scratch_shapes=[
            pltpu.VMEM((IN_TM, D_MODEL), BF16),
            pltpu.VMEM((QKV_COLS // LANES, IN_TM + SUBLANES, LANES), F32),
            pltpu.VMEM((QKV_COLS // LANES, SUBLANES, LANES), F32),
        ],
        compiler_params=pltpu.CompilerParams(
            dimension_semantics=("arbitrary",), vmem_limit_bytes=VMEM_LIMIT),
        name="in_projection",
    )(x2d, gain, w_main, w_t, seg, qk_gain, alog_col, dtb_col, conv_w)


PREP_CHUNKS = 4


def _gdn_kernel(qkv_ref, zg_ref, gate_ref, gain_ref,
                out_ref, u_r, wq_r, aqk_r, kdt_r, state_s):
    t = qkv_ref.shape[0]
    n_chunks = t // CHUNK
    group_rows = PREP_CHUNKS * CHUNK

    def group_base(g):
        return g * group_rows if isinstance(g, int) else pl.multiple_of(g * group_rows, group_rows)

    row = lax.broadcasted_iota(jnp.int32, (CHUNK, 2 * CHUNK), 0)
    col = lax.broadcasted_iota(jnp.int32, (CHUNK, 2 * CHUNK), 1)
    eye_f = (row == col).astype(F32)
    lower_f = (row >= col).astype(F32)
    strict_f = (row > col).astype(F32)
    upper_f = (col >= CHUNK).astype(F32)
    eye_hi = (col - CHUNK == row).astype(F32)
    zero_rows = jnp.zeros((CHUNK, DN_HEAD), BF16)

    items = [(cc, h) for cc in range(PREP_CHUNKS) for h in range(DN_HEADS)]
    heads = range(DN_HEADS)
    gain = gain_ref[...]
    hcols = [slice(h * LANES, (h + 1) * LANES) for h in heads]
    hcols_c = [slice(h * LANES, h * LANES + CHUNK) for h in heads]

    def halves(fn, seq):
        out = []
        mid = len(seq) // 2
        for part in (seq[:mid], seq[mid:]):
            out.extend(fn(x) for x in part)
            yield out

    def prep(g, slot):
        base = group_base(g)
        gates = [gate_ref[pl.ds(base + cc * CHUNK, CHUNK), :] for cc in range(PREP_CHUNKS)]
        ks, lhs, decay, rhs, qe, kdt = [], [], [], [], [], []
        for cc, h in items:
            trows = pl.ds(base + cc * CHUNK, CHUNK)
            q = qkv_ref[trows, hcols[h]].astype(F32)
            k = qkv_ref[trows, slice(DN_WIDTH + h * LANES, DN_WIDTH + (h + 1) * LANES)].astype(F32)
            v = qkv_ref[trows, slice(2 * DN_WIDTH + h * LANES,
                                     2 * DN_WIDTH + (h + 1) * LANES)].astype(F32)
            beta_b = jnp.broadcast_to(gates[cc][:, h:h + 1], (CHUNK, DN_HEAD))
            gcb = jnp.broadcast_to(gates[cc][:, DN_HEADS + h:DN_HEADS + h + 1], (CHUNK, DN_HEAD))
            gr = jnp.sum(gcb * eye_f, axis=0, keepdims=True)
            decay.append(jnp.exp(jnp.minimum(gcb - gr, 0.0)))
            g_last = gcb[CHUNK - 1:CHUNK, :]
            eg = jnp.exp(gcb)
            kb = k * beta_b
            ks.append(jnp.concatenate([k.astype(BF16), zero_rows], axis=0))
            lhs.append(jnp.concatenate([kb, q], axis=0).astype(BF16))
            rhs.append(jnp.concatenate([v * beta_b, kb * eg], axis=1).astype(BF16))
            qe.append((q * eg).astype(BF16))
            kdt.append((k * jnp.exp(g_last - gcb)).T.astype(BF16))
        yield

        kq = None
        for kq in halves(lambda ab: _dot_nt(*ab), list(zip(lhs, ks))):
            yield
        m = [x[:CHUNK] * (d * strict_f) for x, d in zip(kq, decay)]
        a_qk = [(x[CHUNK:, :CHUNK] * (d * lower_f)[:, :CHUNK]).astype(BF16)
                for x, d in zip(kq, decay)]

        w = None
        for w in halves(lambda x: _dot(x[:, :CHUNK], x - eye_hi) + eye_hi, m):
            yield
        for _ in range(5):
            prev = w
            for w in halves(lambda x: _dot(x[:, :CHUNK], x) + x * upper_f, prev):
                yield
        sol = None
        for sol in halves(lambda xb: _dot(xb[0][:, CHUNK:], xb[1]), list(zip(w, rhs))):
            yield

        for n, (cc, h) in enumerate(items):
            rows = slice(cc * CHUNK, (cc + 1) * CHUNK)
            rows_w = slice(2 * cc * CHUNK, (2 * cc + 1) * CHUNK)
            rows_q = slice((2 * cc + 1) * CHUNK, (2 * cc + 2) * CHUNK)
            rows_t = slice(2 * cc * CHUNK, (2 * cc + 2) * CHUNK)
            u_r[slot, rows, hcols[h]] = sol[n][:, :DN_HEAD]
            wq_r[slot, rows_w, hcols[h]] = sol[n][:, DN_HEAD:].astype(BF16)
            wq_r[slot, rows_q, hcols[h]] = qe[n]
            aqk_r[slot, rows, hcols_c[h]] = a_qk[n]
            kdt_r[slot, rows_t, hcols_c[h]] = kdt[n]
        yield

    def rec(g, slot):
        base = group_base(g)
        for cc in range(PREP_CHUNKS):
            rows = slice(cc * CHUNK, (cc + 1) * CHUNK)
            rows2 = slice(2 * cc * CHUNK, (2 * cc + 2) * CHUNK)
            trows = pl.ds(base + cc * CHUNK, CHUNK)
            st = [state_s[h] for h in heads]
            r = [jnp.dot(wq_r[slot, rows2, hcols[h]], st[h].astype(BF16),
                         preferred_element_type=F32) for h in heads]
            yield
            v_new = [(u_r[slot, rows, hcols[h]] - r[h][:CHUNK]).astype(BF16) for h in heads]
            upd = [jnp.dot(kdt_r[slot, rows2, hcols_c[h]], v_new[h], preferred_element_type=F32)
                   for h in heads]
            tail = gate_ref[pl.ds(base + (cc + 1) * CHUNK - 8, 8), :]
            decay_last = jnp.exp(tail[7:8, :])
            for h in heads:
                state_s[h] = st[h] * decay_last[:, DN_HEADS + h:DN_HEADS + h + 1] + upd[h]
            yield
            o = [r[h][CHUNK:] + jnp.dot(aqk_r[slot, rows, hcols_c[h]], v_new[h],
                                        preferred_element_type=F32) for h in heads]
            for h in heads:
                y = o[h] * lax.rsqrt(jnp.mean(o[h] * o[h], axis=-1, keepdims=True) + EPS) * gain
                gate = zg_ref[trows, hcols[h]].astype(F32)
                out_ref[trows, hcols[h]] = (y * gate).astype(out_ref.dtype)
            yield

    def run_interleaved(*gens):
        gens = list(gens)
        while gens:
            for gen in list(gens):
                if next(gen, StopIteration) is StopIteration:
                    gens.remove(gen)

    n_groups = n_chunks // PREP_CHUNKS
    state_s[...] = jnp.zeros_like(state_s)
    run_interleaved(prep(0, 0))

    def pair_body(j, carry):
        run_interleaved(rec(2 * j, 0), prep(2 * j + 1, 1))
        run_interleaved(rec(2 * j + 1, 1), prep(2 * j + 2, 0))
        return carry

    lax.fori_loop(0, (n_groups - 2) // 2, pair_body, 0)
    run_interleaved(rec(n_groups - 2, 0), prep(n_groups - 1, 1))
    run_interleaved(rec(n_groups - 1, 1))


def _gated_deltanet(proj, gates, out_gain):
    b, t, _ = proj.shape
    group_rows = PREP_CHUNKS * CHUNK
    assert t % (2 * group_rows) == 0 and t // group_rows >= 4
    ring = lambda rows, dtype: pltpu.VMEM((2, rows, DN_WIDTH), dtype)
    return pl.pallas_call(
        _gdn_kernel,
        grid=(b,),
        in_specs=[
            pl.BlockSpec((None, t, QKV_COLS), lambda i: (i, 0, 0)),
            pl.BlockSpec((None, t, DN_WIDTH), lambda i: (i, 0, QKV_COLS // DN_WIDTH)),
            pl.BlockSpec((None, t, N_GATES), lambda i: (i, 0, 0)),
            pl.BlockSpec((1, DN_HEAD), lambda i: (0, 0)),
        ],
        out_specs=pl.BlockSpec((None, t, DN_WIDTH), lambda i: (i, 0, 0)),
        out_shape=jax.ShapeDtypeStruct((b, t, DN_WIDTH), BF16),
        scratch_shapes=[
            ring(group_rows, F32),
            ring(2 * group_rows, BF16),
            ring(group_rows, BF16),
            ring(2 * group_rows, BF16),
            pltpu.VMEM((DN_HEADS, DN_HEAD, DN_HEAD), F32),
        ],
        compiler_params=pltpu.CompilerParams(
            dimension_semantics=("parallel",), vmem_limit_bytes=VMEM_LIMIT),
        name="gated_deltanet",
    )(proj, proj, gates, out_gain)


AT_BQ = 512
AT_BK = 256
AT_CW = 256
AT_NQ = 2 * AT_BQ
AT_CHAINS = AT_NQ // AT_CW
AT_CPM = AT_BQ // AT_CW
AT_VROWS = DF_HEAD_V + 16
AT_PROJ_DELAY = 2
AT_HPS = 2
OUT_TN = 256


def _attention_tiles(n_heads, n_qblocks):
    all_chains = tuple(range(AT_CHAINS))
    lo = tuple(c for c in all_chains if c % AT_CPM == 0)
    hi = tuple(c for c in all_chains if c % AT_CPM == 1)
    tiles = []
    for hd in range(n_heads):
        for qi in range(n_qblocks):
            for kt in range(AT_CPM * qi + AT_CPM):
                d = kt - AT_CPM * qi
                chains, masked = ((all_chains, ()) if d < 0 else
                                  (all_chains, lo) if d == 0 else (hi, hi))
                tiles.append(dict(hd=hd, qi=qi, qb=hd * n_qblocks + qi, kt=kt, chains=chains,
                                  masked=masked, first=kt == 0))
    return tiles


def _diff_attn_kernel(q_ref, k_ref, vt_ref, zg_ref, lq1_ref, lk1_ref, lq2_ref, lk2_ref,
                      og_ref, x_ref, dn_ref, wout_ref, out_ref,
                      qq_s, vta_s, m_s, acc_s, y_s, s_a, s_b, c_a, c_b, p_a, p_b):
    t = q_ref.shape[0]
    n_qblocks = t // AT_BQ
    all_chains = tuple(range(AT_CHAINS))

    @pl.when(pl.program_id(1) == 0)
    def _():
        out_ref[...] = x_ref[...]

    def lanes(c):
        return slice(c * AT_CW, (c + 1) * AT_CW)

    first = lax.broadcasted_iota(jnp.int32, (AT_BQ, LANES), 1) < DF_HEAD_QK
    zero = jnp.zeros((AT_BQ, LANES), BF16)
    hcols = [slice(hd * LANES, (hd + 1) * LANES) for hd in range(AT_HPS)]
    for hd in range(AT_HPS):
        for qi in range(n_qblocks):
            blk = q_ref[qi * AT_BQ:(qi + 1) * AT_BQ, hcols[hd]]
            qq_s[hd * n_qblocks + qi, :AT_BQ, :] = jnp.where(first, blk, zero)
            qq_s[hd * n_qblocks + qi, AT_BQ:, :] = jnp.where(first, zero, blk)
        vta_s[hd, :DF_HEAD_V, :] = vt_ref[hcols[hd], :]
        vta_s[hd, DF_HEAD_V:, :] = jnp.ones((AT_VROWS - DF_HEAD_V, t), BF16)

    lam = (jnp.exp(jnp.sum(lq1_ref[...] * lk1_ref[...], axis=-1, keepdims=True))
           - jnp.exp(jnp.sum(lq2_ref[...] * lk2_ref[...], axis=-1, keepdims=True))
           + LAMBDA_INIT)

    krow = lax.broadcasted_iota(jnp.int32, (AT_BK, AT_CW), 0)
    qcol = lax.broadcasted_iota(jnp.int32, (AT_BK, AT_CW), 1)
    causal = krow <= qcol
    bufs = ((s_a, c_a, p_a), (s_b, c_b, p_b))

    def scores(tile, c, buf):
        rows = slice(tile["kt"] * AT_BK, (tile["kt"] + 1) * AT_BK)
        s = _dot_nt(k_ref[rows, hcols[tile["hd"]]], qq_s[tile["qb"], lanes(c), :])
        if c in tile["masked"]:
            s = jnp.where(causal, s, -jnp.inf)
        buf[0][:, lanes(c)] = s
        buf[1][:, lanes(c)] = jnp.max(s, axis=0, keepdims=True)

    def values(tile, c, buf):
        rows = slice(tile["kt"] * AT_BK, (tile["kt"] + 1) * AT_BK)
        return jnp.dot(vta_s[tile["hd"], :, rows], buf[2][:, lanes(c)],
                       preferred_element_type=F32)

    def finish(tile):
        qb = tile["qb"]
        rows = slice(tile["qi"] * AT_BQ, (tile["qi"] + 1) * AT_BQ)
        denom = acc_s[qb, DF_HEAD_V:DF_HEAD_V + 1, :]
        o_all = acc_s[qb, :DF_HEAD_V, :] * (1.0 / denom)
        o = o_all[:, :AT_BQ] - lam * o_all[:, AT_BQ:]
        y = (o * lax.rsqrt(jnp.mean(o * o, axis=0, keepdims=True) + EPS)).T
        y = y * og_ref[...] * (1.0 - LAMBDA_INIT)
        y_s[qb] = (y * zg_ref[rows, hcols[tile["hd"]]].astype(F32)).astype(BF16)

    def project(tile):
        rows = slice(tile["qi"] * AT_BQ, (tile["qi"] + 1) * AT_BQ)
        hd = tile["hd"]
        lhs = jnp.concatenate([y_s[tile["qb"]], dn_ref[rows, hcols[hd]]], axis=1)
        for nc in range(D_MODEL // OUT_TN):
            cols = slice(nc * OUT_TN, (nc + 1) * OUT_TN)
            out_ref[rows, cols] = out_ref[rows, cols] + jnp.dot(
                lhs, wout_ref[hd, :, cols], preferred_element_type=F32)

    tiles = _attention_tiles(AT_HPS, n_qblocks)
    due = {}
    for c in tiles[0]["chains"]:
        scores(tiles[0], c, bufs[0])
    for n, cur in enumerate(tiles):
        nxt = tiles[n + 1] if n + 1 < len(tiles) else None
        prv = tiles[n - 1] if n > 0 else None
        s_cur, c_cur, p_cur = bufs[n % 2]
        qb = cur["qb"]
        for c in all_chains:
            alpha = None
            if c in cur["chains"]:
                if cur["first"]:
                    m_new = c_cur[:, lanes(c)]
                else:
                    m_old = m_s[qb, :, lanes(c)]
                    m_new = jnp.maximum(m_old, c_cur[:, lanes(c)])
                    alpha = jnp.exp2(m_old - m_new)
                p = jnp.exp2(s_cur[:, lanes(c)] - m_new)
                m_s[qb, :, lanes(c)] = m_new
                p_cur[:, lanes(c)] = p.astype(BF16)
            if nxt is not None and c in nxt["chains"]:
                scores(nxt, c, bufs[(n + 1) % 2])
            if prv is not None and c in prv["chains"]:
                pv = values(prv, c, bufs[(n - 1) % 2])
                pq = prv["qb"]
                acc = pv if prv["first"] else acc_s[pq, :, lanes(c)] + pv
                if pq == qb and alpha is not None:
                    acc = alpha * acc
                acc_s[pq, :, lanes(c)] = acc
        if prv is not None and prv["qb"] != qb:
            finish(prv)
            due[n + AT_PROJ_DELAY] = prv
        if n in due:
            project(due.pop(n))
    last = tiles[-1]
    for c in last["chains"]:
        acc_s[last["qb"], :, lanes(c)] = (acc_s[last["qb"], :, lanes(c)]
                                          + values(last, c, bufs[(len(tiles) - 1) % 2]))
    finish(last)
    for tile in sorted(due.values(), key=lambda tl: tl["qb"]) + [last]:
        project(tile)


def _diff_attention(proj, vt, lq1, lk1, lq2, lk2, out_gain, x, mixed_dn, w_heads):
    b, t, _ = proj.shape
    assert AT_BQ == AT_CPM * AT_BK and AT_CW == AT_BK and AT_CPM == 2 and t % AT_BQ == 0
    assert DF_HEADS == DN_HEADS and DF_HEAD_V == DN_HEAD and D_MODEL % OUT_TN == 0
    assert DF_HEADS % AT_HPS == 0 and all(
        blk % AT_HPS == 0 for blk in (_BLK_DF_Q, _BLK_DF_K, _BLK_DF_Z))
    n_qblocks = t // AT_BQ
    n_blocks = AT_HPS * n_qblocks
    wide = AT_HPS * LANES
    small = lambda n: pl.BlockSpec((1, n), lambda i, h: (0, 0))
    return pl.pallas_call(
        _diff_attn_kernel,
        grid=(b, DF_HEADS // AT_HPS),
        in_specs=[
            pl.BlockSpec((None, t, wide), lambda i, h: (i, 0, _BLK_DF_Q // AT_HPS + h)),
            pl.BlockSpec((None, t, wide), lambda i, h: (i, 0, _BLK_DF_K // AT_HPS + h)),
            pl.BlockSpec((None, wide, t), lambda i, h: (i, h, 0)),
            pl.BlockSpec((None, t, wide), lambda i, h: (i, 0, _BLK_DF_Z // AT_HPS + h)),
            small(DF_HEAD_QK), small(DF_HEAD_QK), small(DF_HEAD_QK), small(DF_HEAD_QK),
            small(DF_HEAD_V),
            pl.BlockSpec((None, t, D_MODEL), lambda i, h: (i, 0, 0)),
            pl.BlockSpec((None, t, wide), lambda i, h: (i, 0, h)),
            pl.BlockSpec((AT_HPS, DF_HEAD_V + DN_HEAD, D_MODEL), lambda i, h: (h, 0, 0)),
        ],
        out_specs=pl.BlockSpec((None, t, D_MODEL), lambda i, h: (i, 0, 0)),
        out_shape=jax.ShapeDtypeStruct((b, t, D_MODEL), F32),
        scratch_shapes=[
            pltpu.VMEM((n_blocks, AT_NQ, LANES), BF16),
            pltpu.VMEM((AT_HPS, AT_VROWS, t), BF16),
            pltpu.VMEM((n_blocks, 1, AT_NQ), F32),
            pltpu.VMEM((n_blocks, AT_VROWS, AT_NQ), F32),
            pltpu.VMEM((n_blocks, AT_BQ, DF_HEAD_V), BF16),
            pltpu.VMEM((AT_BK, AT_NQ), F32),
            pltpu.VMEM((AT_BK, AT_NQ), F32),
            pltpu.VMEM((1, AT_NQ), F32),
            pltpu.VMEM((1, AT_NQ), F32),
            pltpu.VMEM((AT_BK, AT_NQ), BF16),
            pltpu.VMEM((AT_BK, AT_NQ), BF16),
        ],
        compiler_params=pltpu.CompilerParams(
            dimension_semantics=("parallel", "arbitrary"), vmem_limit_bytes=VMEM_LIMIT),
        name="diff_attention",
    )(proj, proj, vt, proj, lq1, lk1, lq2, lk2, out_gain, x, mixed_dn, w_heads)


def kernel(x, norm_gain, w_in, conv_w, a_log, dt_bias, dn_out_gain, q_gain, k_gain,
           lambda_q1, lambda_k1, lambda_q2, lambda_k2, df_out_gain, w_out):
    b, t, d = x.shape
    assert d == D_MODEL and norm_gain.shape[0] == 1
    x2d = x.reshape(b * t, d)

    w = w_in[0]
    gate_lo = 4 * DN_WIDTH
    df_lo = gate_lo + N_GATES
    w_main = jnp.concatenate([w[:, :gate_lo], w[:, df_lo:df_lo + 2 * DF_WIDTH],
                              w[:, df_lo + 3 * DF_WIDTH:]], axis=1).astype(BF16)
    w_t = jnp.concatenate([w[:, df_lo + 2 * DF_WIDTH:df_lo + 3 * DF_WIDTH],
                           jnp.pad(w[:, gate_lo:df_lo], ((0, 0), (0, GATE_ROWS - N_GATES)))],
                          axis=1).astype(BF16).T
    half = jnp.arange(SEG_W, dtype=jnp.int32) // DF_HEAD_QK
    seg = (half[:, None] == half[None, :]).astype(BF16)
    reps = DF_WIDTH // DF_HEAD_QK
    qk_gain = jnp.concatenate([jnp.tile(q_gain, (1, reps)) * (DF_HEAD_QK ** -0.5 * LOG2_E),
                               jnp.tile(k_gain, (1, reps))], axis=0)

    pad_rows = lambda v: jnp.pad(v.reshape(DN_HEADS, 1),
                                 ((DN_HEADS, GATE_ROWS - 2 * DN_HEADS), (0, 0)))
    proj, vt, gates_t = _in_projection(x2d, t, norm_gain, w_main, w_t, seg, qk_gain,
                                       pad_rows(a_log), pad_rows(dt_bias), conv_w[0])
    proj = proj.reshape(b, t, PROJ_COLS)
    gates = jnp.swapaxes(gates_t[:, :N_GATES, :], 1, 2)

    mixed_dn = _gated_deltanet(proj, gates, dn_out_gain)

    wo = w_out[0].astype(BF16)
    w_heads = jnp.concatenate([wo[DN_WIDTH:].reshape(DF_HEADS, DF_HEAD_V, d),
                               wo[:DN_WIDTH].reshape(DN_HEADS, DN_HEAD, d)], axis=1)
    return _diff_attention(proj, vt, lambda_q1, lambda_k1, lambda_q2, lambda_k2,
                           df_out_gain, x, mixed_dn, w_heads)
```

```python
import functools
import math

import jax
import jax.numpy as jnp
from jax import lax
from jax.experimental import pallas as pl
from jax.experimental.pallas import tpu as pltpu

F32 = jnp.float32
BF16 = jnp.bfloat16

D_MODEL = 1024
DN_HEADS = 4
DN_HEAD = 128
DN_WIDTH = DN_HEADS * DN_HEAD
CONV_WIDTH = 4
CHUNK = 64
DF_HEADS = 4
DF_HEAD_QK = 64
DF_HEAD_V = 128
DF_WIDTH = DF_HEADS * DF_HEAD_V
EPS = 1e-6
LAMBDA_INIT = 0.8 - 0.6 * math.exp(-0.3 * 0)
LOG2_E = math.log2(math.e)

N_GATES = 2 * DN_HEADS
GATE_ROWS = 16
PROJ_COLS = 4 * DN_WIDTH + 3 * DF_WIDTH
LANES = 128
SUBLANES = 8
QKV_COLS = 3 * DN_WIDTH
VMEM_LIMIT = 56 * 1024 * 1024

_BLK_DN_Z = 3 * DN_HEADS
_BLK_DF_Q = 4 * DN_HEADS
_BLK_DF_K = _BLK_DF_Q + DF_HEADS
_BLK_DF_Z = _BLK_DF_K + DF_HEADS


def _silu(x):
    half = 0.5 * x
    return half + half * jnp.tanh(half)


def _dot(a, b):
    return jnp.dot(a.astype(BF16), b.astype(BF16), preferred_element_type=F32)


def _dot_nt(a, b):
    return lax.dot_general(a.astype(BF16), b.astype(BF16), (((1,), (1,)), ((), ())),
                           preferred_element_type=F32)


IN_TM = 1024
IN_TN = 512
SEG_W = 256
CONV_ROWS = 512


def _inproj_kernel(x_ref, gain_ref, w_ref, wt_ref, seg_ref, qkg_ref, alog_ref, dtb_ref,
                   convw_ref, proj_ref, vt_ref, gate_ref, h_s, cstage_s, tail_s, *, tiles_per_seq):
    @pl.when(pl.program_id(0) % tiles_per_seq == 0)
    def _():
        tail_s[...] = jnp.zeros_like(tail_s)

    def stage(y, j):
        for hd in range(DN_HEADS):
            yh = y[:, hd * DN_HEAD:(hd + 1) * DN_HEAD]
            cstage_s[j * DN_HEADS + hd, :SUBLANES, :] = tail_s[j * DN_HEADS + hd]
            cstage_s[j * DN_HEADS + hd, SUBLANES:, :] = yh
            tail_s[j * DN_HEADS + hd] = yh[IN_TM - SUBLANES:, :]

    def conv_silu_norm(j, part):
        r0 = SUBLANES + part * CONV_ROWS
        for hd in range(DN_HEADS):
            cols = slice(j * IN_TN + hd * DN_HEAD, j * IN_TN + (hd + 1) * DN_HEAD)
            buf = cstage_s.at[j * DN_HEADS + hd]
            w = convw_ref[:, cols]
            acc = buf[r0:r0 + CONV_ROWS, :] * w[CONV_WIDTH - 1:CONV_WIDTH]
            for d in range(1, CONV_WIDTH):
                acc = acc + buf[r0 - d:r0 - d + CONV_ROWS, :] * w[CONV_WIDTH - 1 - d:CONV_WIDTH - d]
            a = _silu(acc)
            if j < 2:
                inv_norm = lax.rsqrt(jnp.sum(a * a, axis=-1, keepdims=True) + EPS)
                a = a * (inv_norm * (DN_HEAD ** -0.5) if j == 0 else inv_norm)
            proj_ref[part * CONV_ROWS:(part + 1) * CONV_ROWS, cols] = a.astype(BF16)

    def head_phase():
        x = x_ref[...]
        ms = jnp.mean(x * x, axis=-1, keepdims=True)
        h = (x * lax.rsqrt(ms + EPS) * gain_ref[...]).astype(BF16)
        h_s[...] = h
        yt = _dot_nt(wt_ref[...], h)
        vt_ref[...] = yt[:DF_WIDTH].astype(BF16)
        pre = yt[DF_WIDTH:]
        g = -jnp.exp(alog_ref[...]) * jax.nn.softplus(pre + dtb_ref[...])
        pos = lax.broadcasted_iota(jnp.int32, g.shape, 1) % CHUNK
        s = 1
        while s < CHUNK:
            g = g + jnp.where(pos >= s, pltpu.roll(g, s, 1), 0.0)
            s *= 2
        row = lax.broadcasted_iota(jnp.int32, g.shape, 0)
        gate_ref[...] = jnp.where(row < DN_HEADS, jax.nn.sigmoid(pre), g)

    def chunk_phase(j, conv_items):
        cols = slice(j * IN_TN, (j + 1) * IN_TN)
        y = jnp.dot(h_s[...], w_ref[:, cols], preferred_element_type=F32)
        for item in conv_items:
            conv_silu_norm(*item)
        if j < QKV_COLS // IN_TN:
            stage(y, j)
            return
        if j in (_BLK_DF_Q * LANES // IN_TN, _BLK_DF_K * LANES // IN_TN):
            which = 0 if j == _BLK_DF_Q * LANES // IN_TN else 1
            sq = (y * y).astype(BF16)
            ssq = jnp.concatenate(
                [jnp.dot(sq[:, i * SEG_W:(i + 1) * SEG_W], seg_ref[...],
                         preferred_element_type=F32) for i in range(IN_TN // SEG_W)], axis=1)
            y = y * lax.rsqrt(ssq * (1.0 / DF_HEAD_QK) + EPS) * qkg_ref[which:which + 1, :]
        elif j in (_BLK_DN_Z * LANES // IN_TN, _BLK_DF_Z * LANES // IN_TN):
            y = _silu(y)
        proj_ref[:, cols] = y.astype(BF16)

    head_phase()
    n_chunks = PROJ_COLS // IN_TN
    per_phase = -(-(QKV_COLS // IN_TN) * (IN_TM // CONV_ROWS) // (n_chunks - 1))
    pending = []
    for j in range(n_chunks):
        items = [pending.pop(0) for _ in range(min(per_phase, len(pending)))]
        chunk_phase(j, items)
        if j < QKV_COLS // IN_TN:
            pending += [(j, part) for part in range(IN_TM // CONV_ROWS)]
    for item in pending:
        conv_silu_norm(*item)


def _in_projection(x2d, t, gain, w_main, w_t, seg, qk_gain, alog_col, dtb_col, conv_w):
    m = x2d.shape[0]
    assert DF_WIDTH == IN_TN and DN_WIDTH == IN_TN and t % IN_TM == 0 and IN_TM % CHUNK == 0
    tiles_per_seq = t // IN_TM
    const = lambda shape: pl.BlockSpec(shape, lambda i: (0, 0))
    return pl.pallas_call(
        functools.partial(_inproj_kernel, tiles_per_seq=tiles_per_seq),
        grid=(m // IN_TM,),
        in_specs=[
            pl.BlockSpec((IN_TM, D_MODEL), lambda i: (i, 0)),
            const((1, D_MODEL)),
            const((D_MODEL, PROJ_COLS)),
            const((DF_WIDTH + GATE_ROWS, D_MODEL)),
            const((SEG_W, SEG_W)),
            const((2, DF_WIDTH)),
            const((GATE_ROWS, 1)),
            const((GATE_ROWS, 1)),
            const((CONV_WIDTH, QKV_COLS)),
        ],
        out_specs=[
            pl.BlockSpec((IN_TM, PROJ_COLS), lambda i: (i, 0)),
            pl.BlockSpec((None, DF_WIDTH, IN_TM),
                         lambda i: (i // tiles_per_seq, 0, i % tiles_per_seq)),
            pl.BlockSpec((None, GATE_ROWS, IN_TM),
                         lambda i: (i // tiles_per_seq, 0, i % tiles_per_seq)),
        ],
        out_shape=[
            jax.ShapeDtypeStruct((m, PROJ_COLS), BF16),
            jax.ShapeDtypeStruct((m // t, DF_WIDTH, t), BF16),
            jax.ShapeDtypeStruct((m // t, GATE_ROWS, t), F32),
        ],
        scratch_shapes=[
            pltpu.VMEM((IN_TM, D_MODEL), BF16),
            pltpu.VMEM((QKV_COLS // LANES, IN_TM + SUBLANES, LANES), F32),
            pltpu.VMEM((QKV_COLS // LANES, SUBLANES, LANES), F32),
        ],
        compiler_params=pltpu.CompilerParams(
            dimension_semantics=("arbitrary",), vmem_limit_bytes=VMEM_LIMIT),
        name="in_projection",
    )(x2d, gain, w_main, w_t, seg, qk_gain, alog_col, dtb_col, conv_w)


PREP_CHUNKS = 4


def _gdn_kernel(qkv_ref, zg_ref, gate_ref, gain_ref,
                out_ref, u_r, wq_r, aqk_r, kdt_r, state_s):
    t = qkv_ref.shape[0]
    n_chunks = t // CHUNK
    group_rows = PREP_CHUNKS * CHUNK

    def group_base(g):
        return g * group_rows if isinstance(g, int) else pl.multiple_of(g * group_rows, group_rows)

    row = lax.broadcasted_iota(jnp.int32, (CHUNK, 2 * CHUNK), 0)
    col = lax.broadcasted_iota(jnp.int32, (CHUNK, 2 * CHUNK), 1)
    eye_f = (row == col).astype(F32)
    lower_f = (row >= col).astype(F32)
    strict_f = (row > col).astype(F32)
    upper_f = (col >= CHUNK).astype(F32)
    eye_hi = (col - CHUNK == row).astype(F32)
    zero_rows = jnp.zeros((CHUNK, DN_HEAD), BF16)

    items = [(cc, h) for cc in range(PREP_CHUNKS) for h in range(DN_HEADS)]
    heads = range(DN_HEADS)
    gain = gain_ref[...]
    hcols = [slice(h * LANES, (h + 1) * LANES) for h in heads]
    hcols_c = [slice(h * LANES, h * LANES + CHUNK) for h in heads]

    def halves(fn, seq):
        out = []
        mid = len(seq) // 2
        for part in (seq[:mid], seq[mid:]):
            out.extend(fn(x) for x in part)
            yield out

    def prep(g, slot):
        base = group_base(g)
        gates = [gate_ref[pl.ds(base + cc * CHUNK, CHUNK), :] for cc in range(PREP_CHUNKS)]
        ks, lhs, decay, rhs, qe, kdt = [], [], [], [], [], []
        for cc, h in items:
            trows = pl.ds(base + cc * CHUNK, CHUNK)
            q = qkv_ref[trows, hcols[h]].astype(F32)
            k = qkv_ref[trows, slice(DN_WIDTH + h * LANES, DN_WIDTH + (h + 1) * LANES)].astype(F32)
            v = qkv_ref[trows, slice(2 * DN_WIDTH + h * LANES,
                                     2 * DN_WIDTH + (h + 1) * LANES)].astype(F32)
            beta_b = jnp.broadcast_to(gates[cc][:, h:h + 1], (CHUNK, DN_HEAD))
            gcb = jnp.broadcast_to(gates[cc][:, DN_HEADS + h:DN_HEADS + h + 1], (CHUNK, DN_HEAD))
            gr = jnp.sum(gcb * eye_f, axis=0, keepdims=True)
            decay.append(jnp.exp(jnp.minimum(gcb - gr, 0.0)))
            g_last = gcb[CHUNK - 1:CHUNK, :]
            eg = jnp.exp(gcb)
            kb = k * beta_b
            ks.append(jnp.concatenate([k.astype(BF16), zero_rows], axis=0))
            lhs.append(jnp.concatenate([kb, q], axis=0).astype(BF16))
            rhs.append(jnp.concatenate([v * beta_b, kb * eg], axis=1).astype(BF16))
            qe.append((q * eg).astype(BF16))
            kdt.append((k * jnp.exp(g_last - gcb)).T.astype(BF16))
        yield

        kq = None
        for kq in halves(lambda ab: _dot_nt(*ab), list(zip(lhs, ks))):
            yield
        m = [x[:CHUNK] * (d * strict_f) for x, d in zip(kq, decay)]
        a_qk = [(x[CHUNK:, :CHUNK] * (d * lower_f)[:, :CHUNK]).astype(BF16)
                for x, d in zip(kq, decay)]

        w = None
        for w in halves(lambda x: _dot(x[:, :CHUNK], x - eye_hi) + eye_hi, m):
            yield
        for _ in range(5):
            prev = w
            for w in halves(lambda x: _dot(x[:, :CHUNK], x) + x * upper_f, prev):
                yield
        sol = None
        for sol in halves(lambda xb: _dot(xb[0][:, CHUNK:], xb[1]), list(zip(w, rhs))):
            yield

        for n, (cc, h) in enumerate(items):
            rows = slice(cc * CHUNK, (cc + 1) * CHUNK)
            rows_w = slice(2 * cc * CHUNK, (2 * cc + 1) * CHUNK)
            rows_q = slice((2 * cc + 1) * CHUNK, (2 * cc + 2) * CHUNK)
            rows_t = slice(2 * cc * CHUNK, (2 * cc + 2) * CHUNK)
            u_r[slot, rows, hcols[h]] = sol[n][:, :DN_HEAD]
            wq_r[slot, rows_w, hcols[h]] = sol[n][:, DN_HEAD:].astype(BF16)
            wq_r[slot, rows_q, hcols[h]] = qe[n]
            aqk_r[slot, rows, hcols_c[h]] = a_qk[n]
            kdt_r[slot, rows_t, hcols_c[h]] = kdt[n]
        yield

    def rec(g, slot):
        base = group_base(g)
        for cc in range(PREP_CHUNKS):
            rows = slice(cc * CHUNK, (cc + 1) * CHUNK)
            rows2 = slice(2 * cc * CHUNK, (2 * cc + 2) * CHUNK)
            trows = pl.ds(base + cc * CHUNK, CHUNK)
            st = [state_s[h] for h in heads]
            r = [jnp.dot(wq_r[slot, rows2, hcols[h]], st[h].astype(BF16),
                         preferred_element_type=F32) for h in heads]
            yield
            v_new = [(u_r[slot, rows, hcols[h]] - r[h][:CHUNK]).astype(BF16) for h in heads]
            upd = [jnp.dot(kdt_r[slot, rows2, hcols_c[h]], v_new[h], preferred_element_type=F32)
                   for h in heads]
            tail = gate_ref[pl.ds(base + (cc + 1) * CHUNK - 8, 8), :]
            decay_last = jnp.exp(tail[7:8, :])
            for h in heads:
                state_s[h] = st[h] * decay_last[:, DN_HEADS + h:DN_HEADS + h + 1] + upd[h]
            yield
            o = [r[h][CHUNK:] + jnp.dot(aqk_r[slot, rows, hcols_c[h]], v_new[h],
                                        preferred_element_type=F32) for h in heads]
            for h in heads:
                y = o[h] * lax.rsqrt(jnp.mean(o[h] * o[h], axis=-1, keepdims=True) + EPS) * gain
                gate = zg_ref[trows, hcols[h]].astype(F32)
                out_ref[trows, hcols[h]] = (y * gate).astype(out_ref.dtype)
            yield

    def run_interleaved(*gens):
        gens = list(gens)
        while gens:
            for gen in list(gens):
                if next(gen, StopIteration) is StopIteration:
                    gens.remove(gen)

    n_groups = n_chunks // PREP_CHUNKS
    state_s[...] = jnp.zeros_like(state_s)
    run_interleaved(prep(0, 0))

    def pair_body(j, carry):
        run_interleaved(rec(2 * j, 0), prep(2 * j + 1, 1))
        run_interleaved(rec(2 * j + 1, 1), prep(2 * j + 2, 0))
        return carry

    lax.fori_loop(0, (n_groups - 2) // 2, pair_body, 0)
    run_interleaved(rec(n_groups - 2, 0), prep(n_groups - 1, 1))
    run_interleaved(rec(n_groups - 1, 1))


def _gated_deltanet(proj, gates, out_gain):
    b, t, _ = proj.shape
    group_rows = PREP_CHUNKS * CHUNK
    assert t % (2 * group_rows) == 0 and t // group_rows >= 4
    ring = lambda rows, dtype: pltpu.VMEM((2, rows, DN_WIDTH), dtype)
    return pl.pallas_call(
        _gdn_kernel,
        grid=(b,),
        in_specs=[
            pl.BlockSpec((None, t, QKV_COLS), lambda i: (i, 0, 0)),
            pl.BlockSpec((None, t, DN_WIDTH), lambda i: (i, 0, QKV_COLS // DN_WIDTH)),
            pl.BlockSpec((None, t, N_GATES), lambda i: (i, 0, 0)),
            pl.BlockSpec((1, DN_HEAD), lambda i: (0, 0)),
        ],
        out_specs=pl.BlockSpec((None, t, DN_WIDTH), lambda i: (i, 0, 0)),
        out_shape=jax.ShapeDtypeStruct((b, t, DN_WIDTH), BF16),
        scratch_shapes=[
            ring(group_rows, F32),
            ring(2 * group_rows, BF16),
            ring(group_rows, BF16),
            ring(2 * group_rows, BF16),
            pltpu.VMEM((DN_HEADS, DN_HEAD, DN_HEAD), F32),
        ],
        compiler_params=pltpu.CompilerParams(
            dimension_semantics=("parallel",), vmem_limit_bytes=VMEM_LIMIT),
        name="gated_deltanet",
    )(proj, proj, gates, out_gain)


AT_BQ = 512
AT_BK = 256
AT_CW = 256
AT_NQ = 2 * AT_BQ
AT_CHAINS = AT_NQ // AT_CW
AT_CPM = AT_BQ // AT_CW
AT_VROWS = DF_HEAD_V + 16
AT_PROJ_DELAY = 2
AT_HPS = 2
OUT_TN = 256


def _attention_tiles(n_heads, n_qblocks):
    all_chains = tuple(range(AT_CHAINS))
    lo = tuple(c for c in all_chains if c % AT_CPM == 0)
    hi = tuple(c for c in all_chains if c % AT_CPM == 1)
    tiles = []
    for hd in range(n_heads):
        for qi in range(n_qblocks):
            for kt in range(AT_CPM * qi + AT_CPM):
                d = kt - AT_CPM * qi
                chains, masked = ((all_chains, ()) if d < 0 else
                                  (all_chains, lo) if d == 0 else (hi, hi))
                tiles.append(dict(hd=hd, qi=qi, qb=hd * n_qblocks + qi, kt=kt, chains=chains,
                                  masked=masked, first=kt == 0))
    return tiles


def _diff_attn_kernel(q_ref, k_ref, vt_ref, zg_ref, lq1_ref, lk1_ref, lq2_ref, lk2_ref,
                      og_ref, x_ref, dn_ref, wout_ref, out_ref,
                      qq_s, vta_s, m_s, acc_s, y_s, s_a, s_b, c_a, c_b, p_a, p_b):
    t = q_ref.shape[0]
    n_qblocks = t // AT_BQ
    all_chains = tuple(range(AT_CHAINS))

    @pl.when(pl.program_id(1) == 0)
    def _():
        out_ref[...] = x_ref[...]

    def lanes(c):
        return slice(c * AT_CW, (c + 1) * AT_CW)

    first = lax.broadcasted_iota(jnp.int32, (AT_BQ, LANES), 1) < DF_HEAD_QK
    zero = jnp.zeros((AT_BQ, LANES), BF16)
    hcols = [slice(hd * LANES, (hd + 1) * LANES) for hd in range(AT_HPS)]
    for hd in range(AT_HPS):
        for qi in range(n_qblocks):
            blk = q_ref[qi * AT_BQ:(qi + 1) * AT_BQ, hcols[hd]]
            qq_s[hd * n_qblocks + qi, :AT_BQ, :] = jnp.where(first, blk, zero)
            qq_s[hd * n_qblocks + qi, AT_BQ:, :] = jnp.where(first, zero, blk)
        vta_s[hd, :DF_HEAD_V, :] = vt_ref[hcols[hd], :]
        vta_s[hd, DF_HEAD_V:, :] = jnp.ones((AT_VROWS - DF_HEAD_V, t), BF16)

    lam = (jnp.exp(jnp.sum(lq1_ref[...] * lk1_ref[...], axis=-1, keepdims=True))
           - jnp.exp(jnp.sum(lq2_ref[...] * lk2_ref[...], axis=-1, keepdims=True))
           + LAMBDA_INIT)

    krow = lax.broadcasted_iota(jnp.int32, (AT_BK, AT_CW), 0)
    qcol = lax.broadcasted_iota(jnp.int32, (AT_BK, AT_CW), 1)
    causal = krow <= qcol
    bufs = ((s_a, c_a, p_a), (s_b, c_b, p_b))

    def scores(tile, c, buf):
        rows = slice(tile["kt"] * AT_BK, (tile["kt"] + 1) * AT_BK)
        s = _dot_nt(k_ref[rows, hcols[tile["hd"]]], qq_s[tile["qb"], lanes(c), :])
        if c in tile["masked"]:
            s = jnp.where(causal, s, -jnp.inf)
        buf[0][:, lanes(c)] = s
        buf[1][:, lanes(c)] = jnp.max(s, axis=0, keepdims=True)

    def values(tile, c, buf):
        rows = slice(tile["kt"] * AT_BK, (tile["kt"] + 1) * AT_BK)
        return jnp.dot(vta_s[tile["hd"], :, rows], buf[2][:, lanes(c)],
                       preferred_element_type=F32)

    def finish(tile):
        qb = tile["qb"]
        rows = slice(tile["qi"] * AT_BQ, (tile["qi"] + 1) * AT_BQ)
        denom = acc_s[qb, DF_HEAD_V:DF_HEAD_V + 1, :]
        o_all = acc_s[qb, :DF_HEAD_V, :] * (1.0 / denom)
        o = o_all[:, :AT_BQ] - lam * o_all[:, AT_BQ:]
        y = (o * lax.rsqrt(jnp.mean(o * o, axis=0, keepdims=True) + EPS)).T
        y = y * og_ref[...] * (1.0 - LAMBDA_INIT)
        y_s[qb] = (y * zg_ref[rows, hcols[tile["hd"]]].astype(F32)).astype(BF16)

    def project(tile):
        rows = slice(tile["qi"] * AT_BQ, (tile["qi"] + 1) * AT_BQ)
        hd = tile["hd"]
        lhs = jnp.concatenate([y_s[tile["qb"]], dn_ref[rows, hcols[hd]]], axis=1)
        for nc in range(D_MODEL // OUT_TN):
            cols = slice(nc * OUT_TN, (nc + 1) * OUT_TN)
            out_ref[rows, cols] = out_ref[rows, cols] + jnp.dot(
                lhs, wout_ref[hd, :, cols], preferred_element_type=F32)

    tiles = _attention_tiles(AT_HPS, n_qblocks)
    due = {}
    for c in tiles[0]["chains"]:
        scores(tiles[0], c, bufs[0])
    for n, cur in enumerate(tiles):
        nxt = tiles[n + 1] if n + 1 < len(tiles) else None
        prv = tiles[n - 1] if n > 0 else None
        s_cur, c_cur, p_cur = bufs[n % 2]
        qb = cur["qb"]
        for c in all_chains:
            alpha = None
            if c in cur["chains"]:
                if cur["first"]:
                    m_new = c_cur[:, lanes(c)]
                else:
                    m_old = m_s[qb, :, lanes(c)]
                    m_new = jnp.maximum(m_old, c_cur[:, lanes(c)])
                    alpha = jnp.exp2(m_old - m_new)
                p = jnp.exp2(s_cur[:, lanes(c)] - m_new)
                m_s[qb, :, lanes(c)] = m_new
                p_cur[:, lanes(c)] = p.astype(BF16)
            if nxt is not None and c in nxt["chains"]:
                scores(nxt, c, bufs[(n + 1) % 2])
            if prv is not None and c in prv["chains"]:
                pv = values(prv, c, bufs[(n - 1) % 2])
                pq = prv["qb"]
                acc = pv if prv["first"] else acc_s[pq, :, lanes(c)] + pv
                if pq == qb and alpha is not None:
                    acc = alpha * acc
                acc_s[pq, :, lanes(c)] = acc
        if prv is not None and prv["qb"] != qb:
            finish(prv)
            due[n + AT_PROJ_DELAY] = prv
        if n in due:
            project(due.pop(n))
    last = tiles[-1]
    for c in last["chains"]:
        acc_s[last["qb"], :, lanes(c)] = (acc_s[last["qb"], :, lanes(c)]
                                          + values(last, c, bufs[(len(tiles) - 1) % 2]))
    finish(last)
    for tile in sorted(due.values(), key=lambda tl: tl["qb"]) + [last]:
        project(tile)


def _diff_attention(proj, vt, lq1, lk1, lq2, lk2, out_gain, x, mixed_dn, w_heads):
    b, t, _ = proj.shape
    assert AT_BQ == AT_CPM * AT_BK and AT_CW == AT_BK and AT_CPM == 2 and t % AT_BQ == 0
    assert DF_HEADS == DN_HEADS and DF_HEAD_V == DN_HEAD and D_MODEL % OUT_TN == 0
    assert DF_HEADS % AT_HPS == 0 and all(
        blk % AT_HPS == 0 for blk in (_BLK_DF_Q, _BLK_DF_K, _BLK_DF_Z))
    n_qblocks = t // AT_BQ
    n_blocks = AT_HPS * n_qblocks
    wide = AT_HPS * LANES
    small = lambda n: pl.BlockSpec((1, n), lambda i, h: (0, 0))
    return pl.pallas_call(
        _diff_attn_kernel,
        grid=(b, DF_HEADS // AT_HPS),
        in_specs=[
            pl.BlockSpec((None, t, wide), lambda i, h: (i, 0, _BLK_DF_Q // AT_HPS + h)),
            pl.BlockSpec((None, t, wide), lambda i, h: (i, 0, _BLK_DF_K // AT_HPS + h)),
            pl.BlockSpec((None, wide, t), lambda i, h: (i, h, 0)),
            pl.BlockSpec((None, t, wide), lambda i, h: (i, 0, _BLK_DF_Z // AT_HPS + h)),
            small(DF_HEAD_QK), small(DF_HEAD_QK), small(DF_HEAD_QK), small(DF_HEAD_QK),
            small(DF_HEAD_V),
            pl.BlockSpec((None, t, D_MODEL), lambda i, h: (i, 0, 0)),
            pl.BlockSpec((None, t, wide), lambda i, h: (i, 0, h)),
            pl.BlockSpec((AT_HPS, DF_HEAD_V + DN_HEAD, D_MODEL), lambda i, h: (h, 0, 0)),
        ],
        out_specs=pl.BlockSpec((None, t, D_MODEL), lambda i, h: (i, 0, 0)),
        out_shape=jax.ShapeDtypeStruct((b, t, D_MODEL), F32),
        scratch_shapes=[
            pltpu.VMEM((n_blocks, AT_NQ, LANES), BF16),
            pltpu.VMEM((AT_HPS, AT_VROWS, t), BF16),
            pltpu.VMEM((n_blocks, 1, AT_NQ), F32),
            pltpu.VMEM((n_blocks, AT_VROWS, AT_NQ), F32),
            pltpu.VMEM((n_blocks, AT_BQ, DF_HEAD_V), BF16),
            pltpu.VMEM((AT_BK, AT_NQ), F32),
            pltpu.VMEM((AT_BK, AT_NQ), F32),
            pltpu.VMEM((1, AT_NQ), F32),
            pltpu.VMEM((1, AT_NQ), F32),
            pltpu.VMEM((AT_BK, AT_NQ), BF16),
            pltpu.VMEM((AT_BK, AT_NQ), BF16),
        ],
        compiler_params=pltpu.CompilerParams(
            dimension_semantics=("parallel", "arbitrary"), vmem_limit_bytes=VMEM_LIMIT),
        name="diff_attention",
    )(proj, proj, vt, proj, lq1, lk1, lq2, lk2, out_gain, x, mixed_dn, w_heads)


def kernel(x, norm_gain, w_in, conv_w, a_log, dt_bias, dn_out_gain, q_gain, k_gain,
           lambda_q1, lambda_k1, lambda_q2, lambda_k2, df_out_gain, w_out):
    b, t, d = x.shape
    assert d == D_MODEL and norm_gain.shape[0] == 1
    x2d = x.reshape(b * t, d)

    w = w_in[0]
    gate_lo = 4 * DN_WIDTH
    df_lo = gate_lo + N_GATES
    w_main = jnp.concatenate([w[:, :gate_lo], w[:, df_lo:df_lo + 2 * DF_WIDTH],
                              w[:, df_lo + 3 * DF_WIDTH:]], axis=1).astype(BF16)
    w_t = jnp.concatenate([w[:, df_lo + 2 * DF_WIDTH:df_lo + 3 * DF_WIDTH],
                           jnp.pad(w[:, gate_lo:df_lo], ((0, 0), (0, GATE_ROWS - N_GATES)))],
                          axis=1).astype(BF16).T
    half = jnp.arange(SEG_W, dtype=jnp.int32) // DF_HEAD_QK
    seg = (half[:, None] == half[None, :]).astype(BF16)
    reps = DF_WIDTH // DF_HEAD_QK
    qk_gain = jnp.concatenate([jnp.tile(q_gain, (1, reps)) * (DF_HEAD_QK ** -0.5 * LOG2_E),
                               jnp.tile(k_gain, (1, reps))], axis=0)

    pad_rows = lambda v: jnp.pad(v.reshape(DN_HEADS, 1),
                                 ((DN_HEADS, GATE_ROWS - 2 * DN_HEADS), (0, 0)))
    proj, vt, gates_t = _in_projection(x2d, t, norm_gain, w_main, w_t, seg, qk_gain,
                                       pad_rows(a_log), pad_rows(dt_bias), conv_w[0])
    proj = proj.reshape(b, t, PROJ_COLS)
    gates = jnp.swapaxes(gates_t[:, :N_GATES, :], 1, 2)

    mixed_dn = _gated_deltanet(proj, gates, dn_out_gain)

    wo = w_out[0].astype(BF16)
    w_heads = jnp.concatenate([wo[DN_WIDTH:].reshape(DF_HEADS, DF_HEAD_V, d),
                               wo[:DN_WIDTH].reshape(DN_HEADS, DN_HEAD, d)], axis=1)
    return _diff_attention(proj, vt, lambda_q1, lambda_k1, lambda_q2, lambda_k2,
                           df_out_gain, x, mixed_dn, w_heads)
```

```python
import functools
import math

import jax
import jax.numpy as jnp
from jax import lax
from jax.experimental import pallas as pl
from jax.experimental.pallas import tpu as pltpu

F32 = jnp.float32
BF16 = jnp.bfloat16

D_MODEL = 1024
DN_HEADS = 4
DN_HEAD = 128
DN_WIDTH = DN_HEADS * DN_HEAD
CONV_WIDTH = 4
CHUNK = 64
DF_HEADS = 4
DF_HEAD_QK = 64
DF_HEAD_V = 128
DF_WIDTH = DF_HEADS * DF_HEAD_V
EPS = 1e-6
LAMBDA_INIT = 0.8 - 0.6 * math.exp(-0.3 * 0)
LOG2_E = math.log2(math.e)

N_GATES = 2 * DN_HEADS
GATE_ROWS = 16
PROJ_COLS = 4 * DN_WIDTH + 3 * DF_WIDTH
LANES = 128
SUBLANES = 8
QKV_COLS = 3 * DN_WIDTH
VMEM_LIMIT = 56 * 1024 * 1024

_BLK_DN_Z = 3 * DN_HEADS
_BLK_DF_Q = 4 * DN_HEADS
_BLK_DF_K = _BLK_DF_Q + DF_HEADS
_BLK_DF_Z = _BLK_DF_K + DF_HEADS


def _silu(x):
    half = 0.5 * x
    return half + half * jnp.tanh(half)


def _dot(a, b):
    return jnp.dot(a.astype(BF16), b.astype(BF16), preferred_element_type=F32)


def _dot_nt(a, b):
    return lax.dot_general(a.astype(BF16), b.astype(BF16), (((1,), (1,)), ((), ())),
                           preferred_element_type=F32)


IN_TM = 1024
IN_TN = 512
SEG_W = 256
CONV_ROWS = 512


def _inproj_kernel(x_ref, gain_ref, w_ref, wt_ref, seg_ref, qkg_ref, alog_ref, dtb_ref,
                   convw_ref, proj_ref, vt_ref, gate_ref, h_s, cstage_s, tail_s, *, tiles_per_seq):
    @pl.when(pl.program_id(0) % tiles_per_seq == 0)
    def _():
        tail_s[...] = jnp.zeros_like(tail_s)

    def stage(y, j):
        for hd in range(DN_HEADS):
            yh = y[:, hd * DN_HEAD:(hd + 1) * DN_HEAD]
            cstage_s[j * DN_HEADS + hd, :SUBLANES, :] = tail_s[j * DN_HEADS + hd]
            cstage_s[j * DN_HEADS + hd, SUBLANES:, :] = yh
            tail_s[j * DN_HEADS + hd] = yh[IN_TM - SUBLANES:, :]

    def conv_silu_norm(j, part):
        r0 = SUBLANES + part * CONV_ROWS
        for hd in range(DN_HEADS):
            cols = slice(j * IN_TN + hd * DN_HEAD, j * IN_TN + (hd + 1) * DN_HEAD)
            buf = cstage_s.at[j * DN_HEADS + hd]
            w = convw_ref[:, cols]
            acc = buf[r0:r0 + CONV_ROWS, :] * w[CONV_WIDTH - 1:CONV_WIDTH]
            for d in range(1, CONV_WIDTH):
                acc = acc + buf[r0 - d:r0 - d + CONV_ROWS, :] * w[CONV_WIDTH - 1 - d:CONV_WIDTH - d]
            a = _silu(acc)
            if j < 2:
                inv_norm = lax.rsqrt(jnp.sum(a * a, axis=-1, keepdims=True) + EPS)
                a = a * (inv_norm * (DN_HEAD ** -0.5) if j == 0 else inv_norm)
            proj_ref[part * CONV_ROWS:(part + 1) * CONV_ROWS, cols] = a.astype(BF16)

    def head_phase():
        x = x_ref[...]
        ms = jnp.mean(x * x, axis=-1, keepdims=True)
        h = (x * lax.rsqrt(ms + EPS) * gain_ref[...]).astype(BF16)
        h_s[...] = h
        yt = _dot_nt(wt_ref[...], h)
        vt_ref[...] = yt[:DF_WIDTH].astype(BF16)
        pre = yt[DF_WIDTH:]
        g = -jnp.exp(alog_ref[...]) * jax.nn.softplus(pre + dtb_ref[...])
        pos = lax.broadcasted_iota(jnp.int32, g.shape, 1) % CHUNK
        s = 1
        while s < CHUNK:
            g = g + jnp.where(pos >= s, pltpu.roll(g, s, 1), 0.0)
            s *= 2
        row = lax.broadcasted_iota(jnp.int32, g.shape, 0)
        gate_ref[...] = jnp.where(row < DN_HEADS, jax.nn.sigmoid(pre), g)

    def chunk_phase(j, conv_items):
        cols = slice(j * IN_TN, (j + 1) * IN_TN)
        y = jnp.dot(h_s[...], w_ref[:, cols], preferred_element_type=F32)
        for item in conv_items:
            conv_silu_norm(*item)
        if j < QKV_COLS // IN_TN:
            stage(y, j)
            return
        if j in (_BLK_DF_Q * LANES // IN_TN, _BLK_DF_K * LANES // IN_TN):
            which = 0 if j == _BLK_DF_Q * LANES // IN_TN else 1
            sq = (y * y).astype(BF16)
            msq = jnp.concatenate(
                [jnp.dot(sq[:, i * SEG_W:(i + 1) * SEG_W], seg_ref[...],
                         preferred_element_type=F32) for i in range(IN_TN // SEG_W)], axis=1)
            y = y * lax.rsqrt(msq + EPS) * qkg_ref[which:which + 1, :]
        elif j in (_BLK_DN_Z * LANES // IN_TN, _BLK_DF_Z * LANES // IN_TN):
            y = _silu(y)
        proj_ref[:, cols] = y.astype(BF16)

    head_phase()
    n_chunks = PROJ_COLS // IN_TN
    per_phase = -(-(QKV_COLS // IN_TN) * (IN_TM // CONV_ROWS) // (n_chunks - 1))
    pending = []
    for j in range(n_chunks):
        items = [pending.pop(0) for _ in range(min(per_phase, len(pending)))]
        chunk_phase(j, items)
        if j < QKV_COLS // IN_TN:
            pending += [(j, part) for part in range(IN_TM // CONV_ROWS)]
    for item in pending:
        conv_silu_norm(*item)


def _in_projection(x2d, t, gain, w_main, w_t, seg, qk_gain, alog_col, dtb_col, conv_w):
    m = x2d.shape[0]
    assert DF_WIDTH == IN_TN and DN_WIDTH == IN_TN and t % IN_TM == 0 and IN_TM % CHUNK == 0
    tiles_per_seq = t // IN_TM
    const = lambda shape: pl.BlockSpec(shape, lambda i: (0, 0))
    return pl.pallas_call(
        functools.partial(_inproj_kernel, tiles_per_seq=tiles_per_seq),
        grid=(m // IN_TM,),
        in_specs=[
            pl.BlockSpec((IN_TM, D_MODEL), lambda i: (i, 0)),
            const((1, D_MODEL)),
            const((D_MODEL, PROJ_COLS)),
            const((DF_WIDTH + GATE_ROWS, D_MODEL)),
            const((SEG_W, SEG_W)),
            const((2, DF_WIDTH)),
            const((GATE_ROWS, 1)),
            const((GATE_ROWS, 1)),
            const((CONV_WIDTH, QKV_COLS)),
        ],
        out_specs=[
            pl.BlockSpec((IN_TM, PROJ_COLS), lambda i: (i, 0)),
            pl.BlockSpec((None, DF_WIDTH, IN_TM),
                         lambda i: (i // tiles_per_seq, 0, i % tiles_per_seq)),
            pl.BlockSpec((None, GATE_ROWS, IN_TM),
                         lambda i: (i // tiles_per_seq, 0, i % tiles_per_seq)),
        ],
        out_shape=[
            jax.ShapeDtypeStruct((m, PROJ_COLS), BF16),
            jax.ShapeDtypeStruct((m // t, DF_WIDTH, t), BF16),
            jax.ShapeDtypeStruct((m // t, GATE_ROWS, t), F32),
        ],
        scratch_shapes=[
            pltpu.VMEM((IN_TM, D_MODEL), BF16),
            pltpu.VMEM((QKV_COLS // LANES, IN_TM + SUBLANES, LANES), F32),
            pltpu.VMEM((QKV_COLS // LANES, SUBLANES, LANES), F32),
        ],
        compiler_params=pltpu.CompilerParams(
            dimension_semantics=("arbitrary",), vmem_limit_bytes=VMEM_LIMIT),
        name="in_projection",
    )(x2d, gain, w_main, w_t, seg, qk_gain, alog_col, dtb_col, conv_w)


PREP_CHUNKS = 4


def _gdn_kernel(qkv_ref, zg_ref, gate_ref, gain_ref,
                out_ref, u_r, wq_r, aqk_r, kdt_r, state_s):
    t = qkv_ref.shape[0]
    n_chunks = t // CHUNK
    group_rows = PREP_CHUNKS * CHUNK

    def group_base(g):
        return g * group_rows if isinstance(g, int) else pl.multiple_of(g * group_rows, group_rows)

    row = lax.broadcasted_iota(jnp.int32, (CHUNK, 2 * CHUNK), 0)
    col = lax.broadcasted_iota(jnp.int32, (CHUNK, 2 * CHUNK), 1)
    eye_f = (row == col).astype(F32)
    lower_f = (row >= col).astype(F32)
    strict_f = (row > col).astype(F32)
    upper_f = (col >= CHUNK).astype(F32)
    eye_hi = (col - CHUNK == row).astype(F32)
    zero_rows = jnp.zeros((CHUNK, DN_HEAD), BF16)

    items = [(cc, h) for cc in range(PREP_CHUNKS) for h in range(DN_HEADS)]
    heads = range(DN_HEADS)
    gain = gain_ref[...]
    hcols = [slice(h * LANES, (h + 1) * LANES) for h in heads]
    hcols_c = [slice(h * LANES, h * LANES + CHUNK) for h in heads]

    def halves(fn, seq):
        out = []
        mid = len(seq) // 2
        for part in (seq[:mid], seq[mid:]):
            out.extend(fn(x) for x in part)
            yield out

    def prep(g, slot):
        base = group_base(g)
        gates = [gate_ref[pl.ds(base + cc * CHUNK, CHUNK), :] for cc in range(PREP_CHUNKS)]
        ks, lhs, decay, rhs, qe, kdt = [], [], [], [], [], []
        for cc, h in items:
            trows = pl.ds(base + cc * CHUNK, CHUNK)
            q = qkv_ref[trows, hcols[h]].astype(F32)
            k = qkv_ref[trows, slice(DN_WIDTH + h * LANES, DN_WIDTH + (h + 1) * LANES)].astype(F32)
            v = qkv_ref[trows, slice(2 * DN_WIDTH + h * LANES,
                                     2 * DN_WIDTH + (h + 1) * LANES)].astype(F32)
            beta_b = jnp.broadcast_to(gates[cc][:, h:h + 1], (CHUNK, DN_HEAD))
            gcb = jnp.broadcast_to(gates[cc][:, DN_HEADS + h:DN_HEADS + h + 1], (CHUNK, DN_HEAD))
            gr = jnp.sum(gcb * eye_f, axis=0, keepdims=True)
            decay.append(jnp.exp(jnp.minimum(gcb - gr, 0.0)))
            g_last = gcb[CHUNK - 1:CHUNK, :]
            eg = jnp.exp(gcb)
            kb = k * beta_b
            ks.append(jnp.concatenate([k.astype(BF16), zero_rows], axis=0))
            lhs.append(jnp.concatenate([kb, q], axis=0).astype(BF16))
            rhs.append(jnp.concatenate([v * beta_b, kb * eg], axis=1).astype(BF16))
            qe.append((q * eg).astype(BF16))
            kdt.append((k * jnp.exp(g_last - gcb)).T.astype(BF16))
        yield

        kq = None
        for kq in halves(lambda ab: _dot_nt(*ab), list(zip(lhs, ks))):
            yield
        m = [x[:CHUNK] * (d * strict_f) for x, d in zip(kq, decay)]
        a_qk = [(x[CHUNK:, :CHUNK] * (d * lower_f)[:, :CHUNK]).astype(BF16)
                for x, d in zip(kq, decay)]

        w = None
        for w in halves(lambda x: _dot(x[:, :CHUNK], x - eye_hi) + eye_hi, m):
            yield
        for _ in range(5):
            prev = w
            for w in halves(lambda x: _dot(x[:, :CHUNK], x) + x * upper_f, prev):
                yield
        sol = None
        for sol in halves(lambda xb: _dot(xb[0][:, CHUNK:], xb[1]), list(zip(w, rhs))):
            yield

        for n, (cc, h) in enumerate(items):
            rows = slice(cc * CHUNK, (cc + 1) * CHUNK)
            rows_w = slice(2 * cc * CHUNK, (2 * cc + 1) * CHUNK)
            rows_q = slice((2 * cc + 1) * CHUNK, (2 * cc + 2) * CHUNK)
            rows_t = slice(2 * cc * CHUNK, (2 * cc + 2) * CHUNK)
            u_r[slot, rows, hcols[h]] = sol[n][:, :DN_HEAD]
            wq_r[slot, rows_w, hcols[h]] = sol[n][:, DN_HEAD:].astype(BF16)
            wq_r[slot, rows_q, hcols[h]] = qe[n]
            aqk_r[slot, rows, hcols_c[h]] = a_qk[n]
            kdt_r[slot, rows_t, hcols_c[h]] = kdt[n]
        yield

    def rec(g, slot):
        base = group_base(g)
        for cc in range(PREP_CHUNKS):
            rows = slice(cc * CHUNK, (cc + 1) * CHUNK)
            rows2 = slice(2 * cc * CHUNK, (2 * cc + 2) * CHUNK)
            trows = pl.ds(base + cc * CHUNK, CHUNK)
            st = [state_s[h] for h in heads]
            r = [jnp.dot(wq_r[slot, rows2, hcols[h]], st[h].astype(BF16),
                         preferred_element_type=F32) for h in heads]
            yield
            v_new = [(u_r[slot, rows, hcols[h]] - r[h][:CHUNK]).astype(BF16) for h in heads]
            upd = [jnp.dot(kdt_r[slot, rows2, hcols_c[h]], v_new[h], preferred_element_type=F32)
                   for h in heads]
            tail = gate_ref[pl.ds(base + (cc + 1) * CHUNK - 8, 8), :]
            decay_last = jnp.exp(tail[7:8, :])
            for h in heads:
                state_s[h] = st[h] * decay_last[:, DN_HEADS + h:DN_HEADS + h + 1] + upd[h]
            yield
            o = [r[h][CHUNK:] + jnp.dot(aqk_r[slot, rows, hcols_c[h]], v_new[h],
                                        preferred_element_type=F32) for h in heads]
            for h in heads:
                y = o[h] * lax.rsqrt(jnp.mean(o[h] * o[h], axis=-1, keepdims=True) + EPS) * gain
                gate = zg_ref[trows, hcols[h]].astype(F32)
                out_ref[trows, hcols[h]] = (y * gate).astype(out_ref.dtype)
            yield

    def run_interleaved(*gens):
        gens = list(gens)
        while gens:
            for gen in list(gens):
                if next(gen, StopIteration) is StopIteration:
                    gens.remove(gen)

    n_groups = n_chunks // PREP_CHUNKS
    state_s[...] = jnp.zeros_like(state_s)
    run_interleaved(prep(0, 0))

    def pair_body(j, carry):
        run_interleaved(rec(2 * j, 0), prep(2 * j + 1, 1))
        run_interleaved(rec(2 * j + 1, 1), prep(2 * j + 2, 0))
        return carry

    lax.fori_loop(0, (n_groups - 2) // 2, pair_body, 0)
    run_interleaved(rec(n_groups - 2, 0), prep(n_groups - 1, 1))
    run_interleaved(rec(n_groups - 1, 1))


def _gated_deltanet(proj, gates, out_gain):
    b, t, _ = proj.shape
    group_rows = PREP_CHUNKS * CHUNK
    assert t % (2 * group_rows) == 0 and t // group_rows >= 4
    ring = lambda rows, dtype: pltpu.VMEM((2, rows, DN_WIDTH), dtype)
    return pl.pallas_call(
        _gdn_kernel,
        grid=(b,),
        in_specs=[
            pl.BlockSpec((None, t, QKV_COLS), lambda i: (i, 0, 0)),
            pl.BlockSpec((None, t, DN_WIDTH), lambda i: (i, 0, QKV_COLS // DN_WIDTH)),
            pl.BlockSpec((None, t, N_GATES), lambda i: (i, 0, 0)),
            pl.BlockSpec((1, DN_HEAD), lambda i: (0, 0)),
        ],
        out_specs=pl.BlockSpec((None, t, DN_WIDTH), lambda i: (i, 0, 0)),
        out_shape=jax.ShapeDtypeStruct((b, t, DN_WIDTH), BF16),
        scratch_shapes=[
            ring(group_rows, F32),
            ring(2 * group_rows, BF16),
            ring(group_rows, BF16),
            ring(2 * group_rows, BF16),
            pltpu.VMEM((DN_HEADS, DN_HEAD, DN_HEAD), F32),
        ],
        compiler_params=pltpu.CompilerParams(
            dimension_semantics=("parallel",), vmem_limit_bytes=VMEM_LIMIT),
        name="gated_deltanet",
    )(proj, proj, gates, out_gain)


AT_BQ = 512
AT_BK = 256
AT_CW = 256
AT_NQ = 2 * AT_BQ
AT_CHAINS = AT_NQ // AT_CW
AT_CPM = AT_BQ // AT_CW
AT_VROWS = DF_HEAD_V + 16
AT_PROJ_DELAY = 2
AT_HPS = 2
OUT_TN = 256


def _attention_tiles(n_heads, n_qblocks):
    all_chains = tuple(range(AT_CHAINS))
    lo = tuple(c for c in all_chains if c % AT_CPM == 0)
    hi = tuple(c for c in all_chains if c % AT_CPM == 1)
    tiles = []
    for hd in range(n_heads):
        for qi in range(n_qblocks):
            for kt in range(AT_CPM * qi + AT_CPM):
                d = kt - AT_CPM * qi
                chains, masked = ((all_chains, ()) if d < 0 else
                                  (all_chains, lo) if d == 0 else (hi, hi))
                tiles.append(dict(hd=hd, qi=qi, qb=hd * n_qblocks + qi, kt=kt, chains=chains,
                                  masked=masked, first=kt == 0))
    return tiles


def _diff_attn_kernel(q_ref, k_ref, vt_ref, zg_ref, lq1_ref, lk1_ref, lq2_ref, lk2_ref,
                      og_ref, x_ref, dn_ref, wout_ref, out_ref,
                      qq_s, vta_s, m_s, acc_s, y_s, s_a, s_b, c_a, c_b, p_a, p_b):
    t = q_ref.shape[0]
    n_qblocks = t // AT_BQ
    all_chains = tuple(range(AT_CHAINS))

    @pl.when(pl.program_id(1) == 0)
    def _():
        out_ref[...] = x_ref[...]

    def lanes(c):
        return slice(c * AT_CW, (c + 1) * AT_CW)

    first = lax.broadcasted_iota(jnp.int32, (AT_BQ, LANES), 1) < DF_HEAD_QK
    zero = jnp.zeros((AT_BQ, LANES), BF16)
    hcols = [slice(hd * LANES, (hd + 1) * LANES) for hd in range(AT_HPS)]
    for hd in range(AT_HPS):
        for qi in range(n_qblocks):
            blk = q_ref[qi * AT_BQ:(qi + 1) * AT_BQ, hcols[hd]]
            qq_s[hd * n_qblocks + qi, :AT_BQ, :] = jnp.where(first, blk, zero)
            qq_s[hd * n_qblocks + qi, AT_BQ:, :] = jnp.where(first, zero, blk)
        vta_s[hd, :DF_HEAD_V, :] = vt_ref[hcols[hd], :]
        vta_s[hd, DF_HEAD_V:, :] = jnp.ones((AT_VROWS - DF_HEAD_V, t), BF16)

    lam = (jnp.exp(jnp.sum(lq1_ref[...] * lk1_ref[...], axis=-1, keepdims=True))
           - jnp.exp(jnp.sum(lq2_ref[...] * lk2_ref[...], axis=-1, keepdims=True))
           + LAMBDA_INIT)

    krow = lax.broadcasted_iota(jnp.int32, (AT_BK, AT_CW), 0)
    qcol = lax.broadcasted_iota(jnp.int32, (AT_BK, AT_CW), 1)
    causal = krow <= qcol
    bufs = ((s_a, c_a, p_a), (s_b, c_b, p_b))

    def scores(tile, c, buf):
        rows = slice(tile["kt"] * AT_BK, (tile["kt"] + 1) * AT_BK)
        s = _dot_nt(k_ref[rows, hcols[tile["hd"]]], qq_s[tile["qb"], lanes(c), :])
        if c in tile["masked"]:
            s = jnp.where(causal, s, -jnp.inf)
        buf[0][:, lanes(c)] = s
        buf[1][:, lanes(c)] = jnp.max(s, axis=0, keepdims=True)

    def values(tile, c, buf):
        rows = slice(tile["kt"] * AT_BK, (tile["kt"] + 1) * AT_BK)
        return jnp.dot(vta_s[tile["hd"], :, rows], buf[2][:, lanes(c)],
                       preferred_element_type=F32)

    def finish(tile):
        qb = tile["qb"]
        rows = slice(tile["qi"] * AT_BQ, (tile["qi"] + 1) * AT_BQ)
        denom = acc_s[qb, DF_HEAD_V:DF_HEAD_V + 1, :]
        o_all = acc_s[qb, :DF_HEAD_V, :] * (1.0 / denom)
        o = o_all[:, :AT_BQ] - lam * o_all[:, AT_BQ:]
        y = (o * lax.rsqrt(jnp.mean(o * o, axis=0, keepdims=True) + EPS)).T
        y = y * og_ref[...] * (1.0 - LAMBDA_INIT)
        y_s[qb] = (y * zg_ref[rows, hcols[tile["hd"]]].astype(F32)).astype(BF16)

    def project(tile):
        rows = slice(tile["qi"] * AT_BQ, (tile["qi"] + 1) * AT_BQ)
        hd = tile["hd"]
        lhs = jnp.concatenate([y_s[tile["qb"]], dn_ref[rows, hcols[hd]]], axis=1)
        for nc in range(D_MODEL // OUT_TN):
            cols = slice(nc * OUT_TN, (nc + 1) * OUT_TN)
            out_ref[rows, cols] = out_ref[rows, cols] + jnp.dot(
                lhs, wout_ref[hd, :, cols], preferred_element_type=F32)

    tiles = _attention_tiles(AT_HPS, n_qblocks)
    due = {}
    for c in tiles[0]["chains"]:
        scores(tiles[0], c, bufs[0])
    for n, cur in enumerate(tiles):
        nxt = tiles[n + 1] if n + 1 < len(tiles) else None
        prv = tiles[n - 1] if n > 0 else None
        s_cur, c_cur, p_cur = bufs[n % 2]
        qb = cur["qb"]
        for c in all_chains:
            alpha = None
            if c in cur["chains"]:
                if cur["first"]:
                    m_new = c_cur[:, lanes(c)]
                else:
                    m_old = m_s[qb, :, lanes(c)]
                    m_new = jnp.maximum(m_old, c_cur[:, lanes(c)])
                    alpha = jnp.exp2(m_old - m_new)
                p = jnp.exp2(s_cur[:, lanes(c)] - m_new)
                m_s[qb, :, lanes(c)] = m_new
                p_cur[:, lanes(c)] = p.astype(BF16)
            if nxt is not None and c in nxt["chains"]:
                scores(nxt, c, bufs[(n + 1) % 2])
            if prv is not None and c in prv["chains"]:
                pv = values(prv, c, bufs[(n - 1) % 2])
                pq = prv["qb"]
                acc = pv if prv["first"] else acc_s[pq, :, lanes(c)] + pv
                if pq == qb and alpha is not None:
                    acc = alpha * acc
                acc_s[pq, :, lanes(c)] = acc
        if prv is not None and prv["qb"] != qb:
            finish(prv)
            due[n + AT_PROJ_DELAY] = prv
        if n in due:
            project(due.pop(n))
    last = tiles[-1]
    for c in last["chains"]:
        acc_s[last["qb"], :, lanes(c)] = (acc_s[last["qb"], :, lanes(c)]
                                          + values(last, c, bufs[(len(tiles) - 1) % 2]))
    finish(last)
    for tile in sorted(due.values(), key=lambda tl: tl["qb"]) + [last]:
        project(tile)


def _diff_attention(proj, vt, lq1, lk1, lq2, lk2, out_gain, x, mixed_dn, w_heads):
    b, t, _ = proj.shape
    assert AT_BQ == AT_CPM * AT_BK and AT_CW == AT_BK and AT_CPM == 2 and t % AT_BQ == 0
    assert DF_HEADS == DN_HEADS and DF_HEAD_V == DN_HEAD and D_MODEL % OUT_TN == 0
    assert DF_HEADS % AT_HPS == 0 and all(
        blk % AT_HPS == 0 for blk in (_BLK_DF_Q, _BLK_DF_K, _BLK_DF_Z))
    n_qblocks = t // AT_BQ
    n_blocks = AT_HPS * n_qblocks
    wide = AT_HPS * LANES
    small = lambda n: pl.BlockSpec((1, n), lambda i, h: (0, 0))
    return pl.pallas_call(
        _diff_attn_kernel,
        grid=(b, DF_HEADS // AT_HPS),
        in_specs=[
            pl.BlockSpec((None, t, wide), lambda i, h: (i, 0, _BLK_DF_Q // AT_HPS + h)),
            pl.BlockSpec((None, t, wide), lambda i, h: (i, 0, _BLK_DF_K // AT_HPS + h)),
            pl.BlockSpec((None, wide, t), lambda i, h: (i, h, 0)),
            pl.BlockSpec((None, t, wide), lambda i, h: (i, 0, _BLK_DF_Z // AT_HPS + h)),
            small(DF_HEAD_QK), small(DF_HEAD_QK), small(DF_HEAD_QK), small(DF_HEAD_QK),
            small(DF_HEAD_V),
            pl.BlockSpec((None, t, D_MODEL), lambda i, h: (i, 0, 0)),
            pl.BlockSpec((None, t, wide), lambda i, h: (i, 0, h)),
            pl.BlockSpec((AT_HPS, DF_HEAD_V + DN_HEAD, D_MODEL), lambda i, h: (h, 0, 0)),
        ],
        out_specs=pl.BlockSpec((None, t, D_MODEL), lambda i, h: (i, 0, 0)),
        out_shape=jax.ShapeDtypeStruct((b, t, D_MODEL), F32),
        scratch_shapes=[
            pltpu.VMEM((n_blocks, AT_NQ, LANES), BF16),
            pltpu.VMEM((AT_HPS, AT_VROWS, t), BF16),
            pltpu.VMEM((n_blocks, 1, AT_NQ), F32),
            pltpu.VMEM((n_blocks, AT_VROWS, AT_NQ), F32),
            pltpu.VMEM((n_blocks, AT_BQ, DF_HEAD_V), BF16),
            pltpu.VMEM((AT_BK, AT_NQ), F32),
            pltpu.VMEM((AT_BK, AT_NQ), F32),
            pltpu.VMEM((1, AT_NQ), F32),
            pltpu.VMEM((1, AT_NQ), F32),
            pltpu.VMEM((AT_BK, AT_NQ), BF16),
            pltpu.VMEM((AT_BK, AT_NQ), BF16),
        ],
        compiler_params=pltpu.CompilerParams(
            dimension_semantics=("parallel", "arbitrary"), vmem_limit_bytes=VMEM_LIMIT),
        name="diff_attention",
    )(proj, proj, vt, proj, lq1, lk1, lq2, lk2, out_gain, x, mixed_dn, w_heads)


def kernel(x, norm_gain, w_in, conv_w, a_log, dt_bias, dn_out_gain, q_gain, k_gain,
           lambda_q1, lambda_k1, lambda_q2, lambda_k2, df_out_gain, w_out):
    b, t, d = x.shape
    assert d == D_MODEL and norm_gain.shape[0] == 1
    x2d = x.reshape(b * t, d)

    w = w_in[0]
    gate_lo = 4 * DN_WIDTH
    df_lo = gate_lo + N_GATES
    w_main = jnp.concatenate([w[:, :gate_lo], w[:, df_lo:df_lo + 2 * DF_WIDTH],
                              w[:, df_lo + 3 * DF_WIDTH:]], axis=1).astype(BF16)
    w_t = jnp.concatenate([w[:, df_lo + 2 * DF_WIDTH:df_lo + 3 * DF_WIDTH],
                           jnp.pad(w[:, gate_lo:df_lo], ((0, 0), (0, GATE_ROWS - N_GATES)))],
                          axis=1).astype(BF16).T
    half = jnp.arange(SEG_W, dtype=jnp.int32) // DF_HEAD_QK
    seg = ((half[:, None] == half[None, :]).astype(F32) / DF_HEAD_QK).astype(BF16)
    reps = DF_WIDTH // DF_HEAD_QK
    qk_gain = jnp.concatenate([jnp.tile(q_gain, (1, reps)) * (DF_HEAD_QK ** -0.5 * LOG2_E),
                               jnp.tile(k_gain, (1, reps))], axis=0)

    pad_rows = lambda v: jnp.pad(v.reshape(DN_HEADS, 1),
                                 ((DN_HEADS, GATE_ROWS - 2 * DN_HEADS), (0, 0)))
    proj, vt, gates_t = _in_projection(x2d, t, norm_gain, w_main, w_t, seg, qk_gain,
                                       pad_rows(a_log), pad_rows(dt_bias), conv_w[0])
    proj = proj.reshape(b, t, PROJ_COLS)
    gates = jnp.swapaxes(gates_t[:, :N_GATES, :], 1, 2)

    mixed_dn = _gated_deltanet(proj, gates, dn_out_gain)

    wo = w_out[0].astype(BF16)
    w_heads = jnp.concatenate([wo[DN_WIDTH:].reshape(DF_HEADS, DF_HEAD_V, d),
                               wo[:DN_WIDTH].reshape(DN_HEADS, DN_HEAD, d)], axis=1)
    return _diff_attention(proj, vt, lambda_q1, lambda_k1, lambda_q2, lambda_k2,
                           df_out_gain, x, mixed_dn, w_heads)
```

```python
import functools
import math

import jax
import jax.numpy as jnp
from jax import lax
from jax.experimental import pallas as pl
from jax.experimental.pallas import tpu as pltpu

F32 = jnp.float32
BF16 = jnp.bfloat16

D_MODEL = 1024
DN_HEADS = 4
DN_HEAD = 128
DN_WIDTH = DN_HEADS * DN_HEAD
CONV_WIDTH = 4
CHUNK = 64
DF_HEADS = 4
DF_HEAD_QK = 64
DF_HEAD_V = 128
DF_WIDTH = DF_HEADS * DF_HEAD_V
EPS = 1e-6
LAMBDA_INIT = 0.8 - 0.6 * math.exp(-0.3 * 0)
LOG2_E = math.log2(math.e)

N_GATES = 2 * DN_HEADS
GATE_ROWS = 16
PROJ_COLS = 4 * DN_WIDTH + 3 * DF_WIDTH
LANES = 128
SUBLANES = 8
QKV_COLS = 3 * DN_WIDTH
VMEM_LIMIT = 56 * 1024 * 1024

_BLK_DN_Z = 3 * DN_HEADS
_BLK_DF_Q = 4 * DN_HEADS
_BLK_DF_K = _BLK_DF_Q + DF_HEADS
_BLK_DF_Z = _BLK_DF_K + DF_HEADS


def _silu(x):
    half = 0.5 * x
    return half + half * jnp.tanh(half)


def _dot(a, b):
    return jnp.dot(a.astype(BF16), b.astype(BF16), preferred_element_type=F32)


def _dot_nt(a, b):
    return lax.dot_general(a.astype(BF16), b.astype(BF16), (((1,), (1,)), ((), ())),
                           preferred_element_type=F32)


IN_TM = 1024
IN_TN = 512
SEG_W = 256
CONV_ROWS = 512


def _inproj_kernel(x_ref, gain_ref, w_ref, wt_ref, seg_ref, qkg_ref, alog_ref, dtb_ref,
                   convw_ref, proj_ref, vt_ref, gate_ref, h_s, cstage_s, tail_s, *, tiles_per_seq):
    @pl.when(pl.program_id(0) % tiles_per_seq == 0)
    def _():
        tail_s[...] = jnp.zeros_like(tail_s)

    def stage(y, j):
        for hd in range(DN_HEADS):
            yh = y[:, hd * DN_HEAD:(hd + 1) * DN_HEAD]
            cstage_s[j * DN_HEADS + hd, :SUBLANES, :] = tail_s[j * DN_HEADS + hd]
            cstage_s[j * DN_HEADS + hd, SUBLANES:, :] = yh
            tail_s[j * DN_HEADS + hd] = yh[IN_TM - SUBLANES:, :]

    def conv_silu_norm(j, part):
        r0 = SUBLANES + part * CONV_ROWS
        for hd in range(DN_HEADS):
            cols = slice(j * IN_TN + hd * DN_HEAD, j * IN_TN + (hd + 1) * DN_HEAD)
            buf = cstage_s.at[j * DN_HEADS + hd]
            w = convw_ref[:, cols]
            acc = buf[r0:r0 + CONV_ROWS, :] * w[CONV_WIDTH - 1:CONV_WIDTH]
            for d in range(1, CONV_WIDTH):
                acc = acc + buf[r0 - d:r0 - d + CONV_ROWS, :] * w[CONV_WIDTH - 1 - d:CONV_WIDTH - d]
            a = _silu(acc)
            if j < 2:
                inv_norm = lax.rsqrt(jnp.sum(a * a, axis=-1, keepdims=True) + EPS)
                a = a * (inv_norm * (DN_HEAD ** -0.5) if j == 0 else inv_norm)
            proj_ref[part * CONV_ROWS:(part + 1) * CONV_ROWS, cols] = a.astype(BF16)

    def head_phase():
        x = x_ref[...]
        ms = jnp.mean(x * x, axis=-1, keepdims=True)
        h = (x * lax.rsqrt(ms + EPS) * gain_ref[...]).astype(BF16)
        h_s[...] = h
        yt = _dot_nt(wt_ref[...], h)
        vt_ref[...] = yt[:DF_WIDTH].astype(BF16)
        pre = yt[DF_WIDTH:]
        g = -jnp.exp(alog_ref[...]) * jax.nn.softplus(pre + dtb_ref[...])
        pos = lax.broadcasted_iota(jnp.int32, g.shape, 1) % CHUNK
        s = 1
        while s < CHUNK:
            g = g + jnp.where(pos >= s, pltpu.roll(g, s, 1), 0.0)
            s *= 2
        row = lax.broadcasted_iota(jnp.int32, g.shape, 0)
        gate_ref[...] = jnp.where(row < DN_HEADS, jax.nn.sigmoid(pre), g)

    def chunk_phase(j, conv_items):
        cols = slice(j * IN_TN, (j + 1) * IN_TN)
        y = jnp.dot(h_s[...], w_ref[:, cols], preferred_element_type=F32)
        for item in conv_items:
            conv_silu_norm(*item)
        if j < QKV_COLS // IN_TN:
            stage(y, j)
            return
        if j in (_BLK_DF_Q * LANES // IN_TN, _BLK_DF_K * LANES // IN_TN):
            which = 0 if j == _BLK_DF_Q * LANES // IN_TN else 1
            sq = (y * y).astype(BF16)
            msq = jnp.concatenate(
                [jnp.dot(sq[:, i * SEG_W:(i + 1) * SEG_W], seg_ref[...],
                         preferred_element_type=F32) for i in range(IN_TN // SEG_W)], axis=1)
            y = y * lax.rsqrt(msq + EPS) * qkg_ref[which:which + 1, :]
        proj_ref[:, cols] = y.astype(BF16)

    head_phase()
    n_chunks = PROJ_COLS // IN_TN
    per_phase = -(-(QKV_COLS // IN_TN) * (IN_TM // CONV_ROWS) // (n_chunks - 1))
    pending = []
    for j in range(n_chunks):
        items = [pending.pop(0) for _ in range(min(per_phase, len(pending)))]
        chunk_phase(j, items)
        if j < QKV_COLS // IN_TN:
            pending += [(j, part) for part in range(IN_TM // CONV_ROWS)]
    for item in pending:
        conv_silu_norm(*item)


def _in_projection(x2d, t, gain, w_main, w_t, seg, qk_gain, alog_col, dtb_col, conv_w):
    m = x2d.shape[0]
    assert DF_WIDTH == IN_TN and DN_WIDTH == IN_TN and t % IN_TM == 0 and IN_TM % CHUNK == 0
    tiles_per_seq = t // IN_TM
    const = lambda shape: pl.BlockSpec(shape, lambda i: (0, 0))
    return pl.pallas_call(
        functools.partial(_inproj_kernel, tiles_per_seq=tiles_per_seq),
        grid=(m // IN_TM,),
        in_specs=[
            pl.BlockSpec((IN_TM, D_MODEL), lambda i: (i, 0)),
            const((1, D_MODEL)),
            const((D_MODEL, PROJ_COLS)),
            const((DF_WIDTH + GATE_ROWS, D_MODEL)),
            const((SEG_W, SEG_W)),
            const((2, DF_WIDTH)),
            const((GATE_ROWS, 1)),
            const((GATE_ROWS, 1)),
            const((CONV_WIDTH, QKV_COLS)),
        ],
        out_specs=[
            pl.BlockSpec((IN_TM, PROJ_COLS), lambda i: (i, 0)),
            pl.BlockSpec((None, DF_WIDTH, IN_TM),
                         lambda i: (i // tiles_per_seq, 0, i % tiles_per_seq)),
            pl.BlockSpec((None, GATE_ROWS, IN_TM),
                         lambda i: (i // tiles_per_seq, 0, i % tiles_per_seq)),
        ],
        out_shape=[
            jax.ShapeDtypeStruct((m, PROJ_COLS), BF16),
            jax.ShapeDtypeStruct((m // t, DF_WIDTH, t), BF16),
            jax.ShapeDtypeStruct((m // t, GATE_ROWS, t), F32),
        ],
        scratch_shapes=[
            pltpu.VMEM((IN_TM, D_MODEL), BF16),
            pltpu.VMEM((QKV_COLS // LANES, IN_TM + SUBLANES, LANES), F32),
            pltpu.VMEM((QKV_COLS // LANES, SUBLANES, LANES), F32),
        ],
        compiler_params=pltpu.CompilerParams(
            dimension_semantics=("arbitrary",), vmem_limit_bytes=VMEM_LIMIT),
        name="in_projection",
    )(x2d, gain, w_main, w_t, seg, qk_gain, alog_col, dtb_col, conv_w)


PREP_CHUNKS = 4


def _gdn_kernel(qkv_ref, zg_ref, gate_ref, gain_ref,
                out_ref, u_r, wq_r, aqk_r, kdt_r, state_s):
    t = qkv_ref.shape[0]
    n_chunks = t // CHUNK
    group_rows = PREP_CHUNKS * CHUNK

    def group_base(g):
        return g * group_rows if isinstance(g, int) else pl.multiple_of(g * group_rows, group_rows)

    row = lax.broadcasted_iota(jnp.int32, (CHUNK, 2 * CHUNK), 0)
    col = lax.broadcasted_iota(jnp.int32, (CHUNK, 2 * CHUNK), 1)
    eye_f = (row == col).astype(F32)
    lower_f = (row >= col).astype(F32)
    strict_f = (row > col).astype(F32)
    upper_f = (col >= CHUNK).astype(F32)
    eye_hi = (col - CHUNK == row).astype(F32)
    zero_rows = jnp.zeros((CHUNK, DN_HEAD), BF16)

    items = [(cc, h) for cc in range(PREP_CHUNKS) for h in range(DN_HEADS)]
    heads = range(DN_HEADS)
    gain = gain_ref[...]
    hcols = [slice(h * LANES, (h + 1) * LANES) for h in heads]
    hcols_c = [slice(h * LANES, h * LANES + CHUNK) for h in heads]

    def halves(fn, seq):
        out = []
        mid = len(seq) // 2
        for part in (seq[:mid], seq[mid:]):
            out.extend(fn(x) for x in part)
            yield out

    def prep(g, slot):
        base = group_base(g)
        gates = [gate_ref[pl.ds(base + cc * CHUNK, CHUNK), :] for cc in range(PREP_CHUNKS)]
        ks, lhs, decay, rhs, qe, kdt = [], [], [], [], [], []
        for cc, h in items:
            trows = pl.ds(base + cc * CHUNK, CHUNK)
            q = qkv_ref[trows, hcols[h]].astype(F32)
            k = qkv_ref[trows, slice(DN_WIDTH + h * LANES, DN_WIDTH + (h + 1) * LANES)].astype(F32)
            v = qkv_ref[trows, slice(2 * DN_WIDTH + h * LANES,
                                     2 * DN_WIDTH + (h + 1) * LANES)].astype(F32)
            beta_b = jnp.broadcast_to(gates[cc][:, h:h + 1], (CHUNK, DN_HEAD))
            gcb = jnp.broadcast_to(gates[cc][:, DN_HEADS + h:DN_HEADS + h + 1], (CHUNK, DN_HEAD))
            gr = jnp.sum(gcb * eye_f, axis=0, keepdims=True)
            decay.append(jnp.exp(jnp.minimum(gcb - gr, 0.0)))
            g_last = gcb[CHUNK - 1:CHUNK, :]
            eg = jnp.exp(gcb)
            kb = k * beta_b
            ks.append(jnp.concatenate([k.astype(BF16), zero_rows], axis=0))
            lhs.append(jnp.concatenate([kb, q], axis=0).astype(BF16))
            rhs.append(jnp.concatenate([v * beta_b, kb * eg], axis=1).astype(BF16))
            qe.append((q * eg).astype(BF16))
            kdt.append((k * jnp.exp(g_last - gcb)).T.astype(BF16))
        yield

        kq = None
        for kq in halves(lambda ab: _dot_nt(*ab), list(zip(lhs, ks))):
            yield
        m = [x[:CHUNK] * (d * strict_f) for x, d in zip(kq, decay)]
        a_qk = [(x[CHUNK:, :CHUNK] * (d * lower_f)[:, :CHUNK]).astype(BF16)
                for x, d in zip(kq, decay)]

        w = None
        for w in halves(lambda x: _dot(x[:, :CHUNK], x - eye_hi) + eye_hi, m):
            yield
        for _ in range(5):
            prev = w
            for w in halves(lambda x: _dot(x[:, :CHUNK], x) + x * upper_f, prev):
                yield
        sol = None
        for sol in halves(lambda xb: _dot(xb[0][:, CHUNK:], xb[1]), list(zip(w, rhs))):
            yield

        for n, (cc, h) in enumerate(items):
            rows = slice(cc * CHUNK, (cc + 1) * CHUNK)
            rows_w = slice(2 * cc * CHUNK, (2 * cc + 1) * CHUNK)
            rows_q = slice((2 * cc + 1) * CHUNK, (2 * cc + 2) * CHUNK)
            rows_t = slice(2 * cc * CHUNK, (2 * cc + 2) * CHUNK)
            u_r[slot, rows, hcols[h]] = sol[n][:, :DN_HEAD]
            wq_r[slot, rows_w, hcols[h]] = sol[n][:, DN_HEAD:].astype(BF16)
            wq_r[slot, rows_q, hcols[h]] = qe[n]
            aqk_r[slot, rows, hcols_c[h]] = a_qk[n]
            kdt_r[slot, rows_t, hcols_c[h]] = kdt[n]
        yield

    def rec(g, slot):
        base = group_base(g)
        for cc in range(PREP_CHUNKS):
            rows = slice(cc * CHUNK, (cc + 1) * CHUNK)
            rows2 = slice(2 * cc * CHUNK, (2 * cc + 2) * CHUNK)
            trows = pl.ds(base + cc * CHUNK, CHUNK)
            st = [state_s[h] for h in heads]
            r = [jnp.dot(wq_r[slot, rows2, hcols[h]], st[h].astype(BF16),
                         preferred_element_type=F32) for h in heads]
            yield
            v_new = [(u_r[slot, rows, hcols[h]] - r[h][:CHUNK]).astype(BF16) for h in heads]
            upd = [jnp.dot(kdt_r[slot, rows2, hcols_c[h]], v_new[h], preferred_element_type=F32)
                   for h in heads]
            tail = gate_ref[pl.ds(base + (cc + 1) * CHUNK - 8, 8), :]
            decay_last = jnp.exp(tail[7:8, :])
            for h in heads:
                state_s[h] = st[h] * decay_last[:, DN_HEADS + h:DN_HEADS + h + 1] + upd[h]
            yield
            o = [r[h][CHUNK:] + jnp.dot(aqk_r[slot, rows, hcols_c[h]], v_new[h],
                                        preferred_element_type=F32) for h in heads]
            for h in heads:
                y = o[h] * lax.rsqrt(jnp.mean(o[h] * o[h], axis=-1, keepdims=True) + EPS) * gain
                gate = _silu(zg_ref[trows, hcols[h]].astype(F32))
                out_ref[trows, hcols[h]] = (y * gate).astype(out_ref.dtype)
            yield

    def run_interleaved(*gens):
        gens = list(gens)
        while gens:
            for gen in list(gens):
                if next(gen, StopIteration) is StopIteration:
                    gens.remove(gen)

    n_groups = n_chunks // PREP_CHUNKS
    state_s[...] = jnp.zeros_like(state_s)
    run_interleaved(prep(0, 0))

    def pair_body(j, carry):
        run_interleaved(rec(2 * j, 0), prep(2 * j + 1, 1))
        run_interleaved(rec(2 * j + 1, 1), prep(2 * j + 2, 0))
        return carry

    lax.fori_loop(0, (n_groups - 2) // 2, pair_body, 0)
    run_interleaved(rec(n_groups - 2, 0), prep(n_groups - 1, 1))
    run_interleaved(rec(n_groups - 1, 1))


def _gated_deltanet(proj, gates, out_gain):
    b, t, _ = proj.shape
    group_rows = PREP_CHUNKS * CHUNK
    assert t % (2 * group_rows) == 0 and t // group_rows >= 4
    ring = lambda rows, dtype: pltpu.VMEM((2, rows, DN_WIDTH), dtype)
    return pl.pallas_call(
        _gdn_kernel,
        grid=(b,),
        in_specs=[
            pl.BlockSpec((None, t, QKV_COLS), lambda i: (i, 0, 0)),
            pl.BlockSpec((None, t, DN_WIDTH), lambda i: (i, 0, QKV_COLS // DN_WIDTH)),
            pl.BlockSpec((None, t, N_GATES), lambda i: (i, 0, 0)),
            pl.BlockSpec((1, DN_HEAD), lambda i: (0, 0)),
        ],
        out_specs=pl.BlockSpec((None, t, DN_WIDTH), lambda i: (i, 0, 0)),
        out_shape=jax.ShapeDtypeStruct((b, t, DN_WIDTH), BF16),
        scratch_shapes=[
            ring(group_rows, F32),
            ring(2 * group_rows, BF16),
            ring(group_rows, BF16),
            ring(2 * group_rows, BF16),
            pltpu.VMEM((DN_HEADS, DN_HEAD, DN_HEAD), F32),
        ],
        compiler_params=pltpu.CompilerParams(
            dimension_semantics=("parallel",), vmem_limit_bytes=VMEM_LIMIT),
        name="gated_deltanet",
    )(proj, proj, gates, out_gain)


AT_BQ = 512
AT_BK = 256
AT_CW = 256
AT_NQ = 2 * AT_BQ
AT_CHAINS = AT_NQ // AT_CW
AT_CPM = AT_BQ // AT_CW
AT_VROWS = DF_HEAD_V + 16
AT_PROJ_DELAY = 2
AT_HPS = 2
OUT_TN = 256


def _attention_tiles(n_heads, n_qblocks):
    all_chains = tuple(range(AT_CHAINS))
    lo = tuple(c for c in all_chains if c % AT_CPM == 0)
    hi = tuple(c for c in all_chains if c % AT_CPM == 1)
    tiles = []
    for hd in range(n_heads):
        for qi in range(n_qblocks):
            for kt in range(AT_CPM * qi + AT_CPM):
                d = kt - AT_CPM * qi
                chains, masked = ((all_chains, ()) if d < 0 else
                                  (all_chains, lo) if d == 0 else (hi, hi))
                tiles.append(dict(hd=hd, qi=qi, qb=hd * n_qblocks + qi, kt=kt, chains=chains,
                                  masked=masked, first=kt == 0))
    return tiles


def _diff_attn_kernel(q_ref, k_ref, vt_ref, zg_ref, lq1_ref, lk1_ref, lq2_ref, lk2_ref,
                      og_ref, x_ref, dn_ref, wout_ref, out_ref,
                      qq_s, vta_s, m_s, acc_s, y_s, s_a, s_b, c_a, c_b, p_a, p_b):
    t = q_ref.shape[0]
    n_qblocks = t // AT_BQ
    all_chains = tuple(range(AT_CHAINS))

    @pl.when(pl.program_id(1) == 0)
    def _():
        out_ref[...] = x_ref[...]

    def lanes(c):
        return slice(c * AT_CW, (c + 1) * AT_CW)

    first = lax.broadcasted_iota(jnp.int32, (AT_BQ, LANES), 1) < DF_HEAD_QK
    zero = jnp.zeros((AT_BQ, LANES), BF16)
    hcols = [slice(hd * LANES, (hd + 1) * LANES) for hd in range(AT_HPS)]
    for hd in range(AT_HPS):
        for qi in range(n_qblocks):
            blk = q_ref[qi * AT_BQ:(qi + 1) * AT_BQ, hcols[hd]]
            qq_s[hd * n_qblocks + qi, :AT_BQ, :] = jnp.where(first, blk, zero)
            qq_s[hd * n_qblocks + qi, AT_BQ:, :] = jnp.where(first, zero, blk)
        vta_s[hd, :DF_HEAD_V, :] = vt_ref[hcols[hd], :]
        vta_s[hd, DF_HEAD_V:, :] = jnp.ones((AT_VROWS - DF_HEAD_V, t), BF16)

    lam = (jnp.exp(jnp.sum(lq1_ref[...] * lk1_ref[...], axis=-1, keepdims=True))
           - jnp.exp(jnp.sum(lq2_ref[...] * lk2_ref[...], axis=-1, keepdims=True))
           + LAMBDA_INIT)

    krow = lax.broadcasted_iota(jnp.int32, (AT_BK, AT_CW), 0)
    qcol = lax.broadcasted_iota(jnp.int32, (AT_BK, AT_CW), 1)
    causal = krow <= qcol
    bufs = ((s_a, c_a, p_a), (s_b, c_b, p_b))

    def scores(tile, c, buf):
        rows = slice(tile["kt"] * AT_BK, (tile["kt"] + 1) * AT_BK)
        s = _dot_nt(k_ref[rows, hcols[tile["hd"]]], qq_s[tile["qb"], lanes(c), :])
        if c in tile["masked"]:
            s = jnp.where(causal, s, -jnp.inf)
        buf[0][:, lanes(c)] = s
        buf[1][:, lanes(c)] = jnp.max(s, axis=0, keepdims=True)

    def values(tile, c, buf):
        rows = slice(tile["kt"] * AT_BK, (tile["kt"] + 1) * AT_BK)
        return jnp.dot(vta_s[tile["hd"], :, rows], buf[2][:, lanes(c)],
                       preferred_element_type=F32)

    def finish(tile):
        qb = tile["qb"]
        rows = slice(tile["qi"] * AT_BQ, (tile["qi"] + 1) * AT_BQ)
        denom = acc_s[qb, DF_HEAD_V:DF_HEAD_V + 1, :]
        o_all = acc_s[qb, :DF_HEAD_V, :] * (1.0 / denom)
        o = o_all[:, :AT_BQ] - lam * o_all[:, AT_BQ:]
        y = (o * lax.rsqrt(jnp.mean(o * o, axis=0, keepdims=True) + EPS)).T
        y = y * og_ref[...] * (1.0 - LAMBDA_INIT)
        y_s[qb] = (y * _silu(zg_ref[rows, hcols[tile["hd"]]].astype(F32))).astype(BF16)

    def project(tile):
        rows = slice(tile["qi"] * AT_BQ, (tile["qi"] + 1) * AT_BQ)
        hd = tile["hd"]
        lhs = jnp.concatenate([y_s[tile["qb"]], dn_ref[rows, hcols[hd]]], axis=1)
        for nc in range(D_MODEL // OUT_TN):
            cols = slice(nc * OUT_TN, (nc + 1) * OUT_TN)
            out_ref[rows, cols] = out_ref[rows, cols] + jnp.dot(
                lhs, wout_ref[hd, :, cols], preferred_element_type=F32)

    tiles = _attention_tiles(AT_HPS, n_qblocks)
    due = {}
    for c in tiles[0]["chains"]:
        scores(tiles[0], c, bufs[0])
    for n, cur in enumerate(tiles):
        nxt = tiles[n + 1] if n + 1 < len(tiles) else None
        prv = tiles[n - 1] if n > 0 else None
        s_cur, c_cur, p_cur = bufs[n % 2]
        qb = cur["qb"]
        for c in all_chains:
            alpha = None
            if c in cur["chains"]:
                if cur["first"]:
                    m_new = c_cur[:, lanes(c)]
                else:
                    m_old = m_s[qb, :, lanes(c)]
                    m_new = jnp.maximum(m_old, c_cur[:, lanes(c)])
                    alpha = jnp.exp2(m_old - m_new)
                p = jnp.exp2(s_cur[:, lanes(c)] - m_new)
                m_s[qb, :, lanes(c)] = m_new
                p_cur[:, lanes(c)] = p.astype(BF16)
            if nxt is not None and c in nxt["chains"]:
                scores(nxt, c, bufs[(n + 1) % 2])
            if prv is not None and c in prv["chains"]:
                pv = values(prv, c, bufs[(n - 1) % 2])
                pq = prv["qb"]
                acc = pv if prv["first"] else acc_s[pq, :, lanes(c)] + pv
                if pq == qb and alpha is not None:
                    acc = alpha * acc
                acc_s[pq, :, lanes(c)] = acc
        if prv is not None and prv["qb"] != qb:
            finish(prv)
            due[n + AT_PROJ_DELAY] = prv
        if n in due:
            project(due.pop(n))
    last = tiles[-1]
    for c in last["chains"]:
        acc_s[last["qb"], :, lanes(c)] = (acc_s[last["qb"], :, lanes(c)]
                                          + values(last, c, bufs[(len(tiles) - 1) % 2]))
    finish(last)
    for tile in sorted(due.values(), key=lambda tl: tl["qb"]) + [last]:
        project(tile)


def _diff_attention(proj, vt, lq1, lk1, lq2, lk2, out_gain, x, mixed_dn, w_heads):
    b, t, _ = proj.shape
    assert AT_BQ == AT_CPM * AT_BK and AT_CW == AT_BK and AT_CPM == 2 and t % AT_BQ == 0
    assert DF_HEADS == DN_HEADS and DF_HEAD_V == DN_HEAD and D_MODEL % OUT_TN == 0
    assert DF_HEADS % AT_HPS == 0 and all(
        blk % AT_HPS == 0 for blk in (_BLK_DF_Q, _BLK_DF_K, _BLK_DF_Z))
    n_qblocks = t // AT_BQ
    n_blocks = AT_HPS * n_qblocks
    wide = AT_HPS * LANES
    small = lambda n: pl.BlockSpec((1, n), lambda i, h: (0, 0))
    return pl.pallas_call(
        _diff_attn_kernel,
        grid=(b, DF_HEADS // AT_HPS),
        in_specs=[
            pl.BlockSpec((None, t, wide), lambda i, h: (i, 0, _BLK_DF_Q // AT_HPS + h)),
            pl.BlockSpec((None, t, wide), lambda i, h: (i, 0, _BLK_DF_K // AT_HPS + h)),
            pl.BlockSpec((None, wide, t), lambda i, h: (i, h, 0)),
            pl.BlockSpec((None, t, wide), lambda i, h: (i, 0, _BLK_DF_Z // AT_HPS + h)),
            small(DF_HEAD_QK), small(DF_HEAD_QK), small(DF_HEAD_QK), small(DF_HEAD_QK),
            small(DF_HEAD_V),
            pl.BlockSpec((None, t, D_MODEL), lambda i, h: (i, 0, 0)),
            pl.BlockSpec((None, t, wide), lambda i, h: (i, 0, h)),
            pl.BlockSpec((AT_HPS, DF_HEAD_V + DN_HEAD, D_MODEL), lambda i, h: (h, 0, 0)),
        ],
        out_specs=pl.BlockSpec((None, t, D_MODEL), lambda i, h: (i, 0, 0)),
        out_shape=jax.ShapeDtypeStruct((b, t, D_MODEL), F32),
        scratch_shapes=[
            pltpu.VMEM((n_blocks, AT_NQ, LANES), BF16),
            pltpu.VMEM((AT_HPS, AT_VROWS, t), BF16),
            pltpu.VMEM((n_blocks, 1, AT_NQ), F32),
            pltpu.VMEM((n_blocks, AT_VROWS, AT_NQ), F32),
            pltpu.VMEM((n_blocks, AT_BQ, DF_HEAD_V), BF16),
            pltpu.VMEM((AT_BK, AT_NQ), F32),
            pltpu.VMEM((AT_BK, AT_NQ), F32),
            pltpu.VMEM((1, AT_NQ), F32),
            pltpu.VMEM((1, AT_NQ), F32),
            pltpu.VMEM((AT_BK, AT_NQ), BF16),
            pltpu.VMEM((AT_BK, AT_NQ), BF16),
        ],
        compiler_params=pltpu.CompilerParams(
            dimension_semantics=("parallel", "arbitrary"), vmem_limit_bytes=VMEM_LIMIT),
        name="diff_attention",
    )(proj, proj, vt, proj, lq1, lk1, lq2, lk2, out_gain, x, mixed_dn, w_heads)


def kernel(x, norm_gain, w_in, conv_w, a_log, dt_bias, dn_out_gain, q_gain, k_gain,
           lambda_q1, lambda_k1, lambda_q2, lambda_k2, df_out_gain, w_out):
    b, t, d = x.shape
    assert d == D_MODEL and norm_gain.shape[0] == 1
    x2d = x.reshape(b * t, d)

    w = w_in[0]
    gate_lo = 4 * DN_WIDTH
    df_lo = gate_lo + N_GATES
    w_main = jnp.concatenate([w[:, :gate_lo], w[:, df_lo:df_lo + 2 * DF_WIDTH],
                              w[:, df_lo + 3 * DF_WIDTH:]], axis=1).astype(BF16)
    w_t = jnp.concatenate([w[:, df_lo + 2 * DF_WIDTH:df_lo + 3 * DF_WIDTH],
                           jnp.pad(w[:, gate_lo:df_lo], ((0, 0), (0, GATE_ROWS - N_GATES)))],
                          axis=1).astype(BF16).T
    half = jnp.arange(SEG_W, dtype=jnp.int32) // DF_HEAD_QK
    seg = ((half[:, None] == half[None, :]).astype(F32) / DF_HEAD_QK).astype(BF16)
    reps = DF_WIDTH // DF_HEAD_QK
    qk_gain = jnp.concatenate([jnp.tile(q_gain, (1, reps)) * (DF_HEAD_QK ** -0.5 * LOG2_E),
                               jnp.tile(k_gain, (1, reps))], axis=0)

    pad_rows = lambda v: jnp.pad(v.reshape(DN_HEADS, 1),
                                 ((DN_HEADS, GATE_ROWS - 2 * DN_HEADS), (0, 0)))
    proj, vt, gates_t = _in_projection(x2d, t, norm_gain, w_main, w_t, seg, qk_gain,
                                       pad_rows(a_log), pad_rows(dt_bias), conv_w[0])
    proj = proj.reshape(b, t, PROJ_COLS)
    gates = jnp.swapaxes(gates_t[:, :N_GATES, :], 1, 2)

    mixed_dn = _gated_deltanet(proj, gates, dn_out_gain)

    wo = w_out[0].astype(BF16)
    w_heads = jnp.concatenate([wo[DN_WIDTH:].reshape(DF_HEADS, DF_HEAD_V, d),
                               wo[:DN_WIDTH].reshape(DN_HEADS, DN_HEAD, d)], axis=1)
    return _diff_attention(proj, vt, lambda_q1, lambda_k1, lambda_q2, lambda_k2,
                           df_out_gain, x, mixed_dn, w_heads)
```

```python
import functools
import math

import jax
import jax.numpy as jnp
from jax import lax
from jax.experimental import pallas as pl
from jax.experimental.pallas import tpu as pltpu

F32 = jnp.float32
BF16 = jnp.bfloat16

D_MODEL = 1024
DN_HEADS = 4
DN_HEAD = 128
DN_WIDTH = DN_HEADS * DN_HEAD
CONV_WIDTH = 4
CHUNK = 64
DF_HEADS = 4
DF_HEAD_QK = 64
DF_HEAD_V = 128
DF_WIDTH = DF_HEADS * DF_HEAD_V
EPS = 1e-6
LAMBDA_INIT = 0.8 - 0.6 * math.exp(-0.3 * 0)
LOG2_E = math.log2(math.e)

N_GATES = 2 * DN_HEADS
GATE_ROWS = 16
PROJ_COLS = 4 * DN_WIDTH + 3 * DF_WIDTH
LANES = 128
SUBLANES = 8
QKV_COLS = 3 * DN_WIDTH
VMEM_LIMIT = 56 * 1024 * 1024

_BLK_DN_Z = 3 * DN_HEADS
_BLK_DF_Q = 4 * DN_HEADS
_BLK_DF_K = _BLK_DF_Q + DF_HEADS
_BLK_DF_Z = _BLK_DF_K + DF_HEADS


def _silu(x):
    half = 0.5 * x
    return half + half * jnp.tanh(half)


def _dot(a, b):
    return jnp.dot(a.astype(BF16), b.astype(BF16), preferred_element_type=F32)


def _dot_nt(a, b):
    return lax.dot_general(a.astype(BF16), b.astype(BF16), (((1,), (1,)), ((), ())),
                           preferred_element_type=F32)


IN_TM = 1024
IN_TN = 512
SEG_W = 256
CONV_ROWS = 512


def _inproj_kernel(x_ref, gain_ref, w_ref, wt_ref, seg_ref, qkg_ref, alog_ref, dtb_ref,
                   convw_ref, proj_ref, vt_ref, gate_ref, h_s, cstage_s, tail_s, *, tiles_per_seq):
    @pl.when(pl.program_id(0) % tiles_per_seq == 0)
    def _():
        tail_s[...] = jnp.zeros_like(tail_s)

    def stage(y, j):
        for hd in range(DN_HEADS):
            yh = y[:, hd * DN_HEAD:(hd + 1) * DN_HEAD]
            cstage_s[j * DN_HEADS + hd, :SUBLANES, :] = tail_s[j * DN_HEADS + hd]
            cstage_s[j * DN_HEADS + hd, SUBLANES:, :] = yh
            tail_s[j * DN_HEADS + hd] = yh[IN_TM - SUBLANES:, :]

    def conv_silu_norm(j, part):
        r0 = SUBLANES + part * CONV_ROWS
        for hd in range(DN_HEADS):
            cols = slice(j * IN_TN + hd * DN_HEAD, j * IN_TN + (hd + 1) * DN_HEAD)
            buf = cstage_s.at[j * DN_HEADS + hd]
            w = convw_ref[:, cols]
            acc = buf[r0:r0 + CONV_ROWS, :] * w[CONV_WIDTH - 1:CONV_WIDTH]
            for d in range(1, CONV_WIDTH):
                acc = acc + buf[r0 - d:r0 - d + CONV_ROWS, :] * w[CONV_WIDTH - 1 - d:CONV_WIDTH - d]
            a = _silu(acc)
            if j < 2:
                inv_norm = lax.rsqrt(jnp.sum(a * a, axis=-1, keepdims=True) + EPS)
                a = a * (inv_norm * (DN_HEAD ** -0.5) if j == 0 else inv_norm)
            proj_ref[part * CONV_ROWS:(part + 1) * CONV_ROWS, cols] = a.astype(BF16)

    def head_phase():
        x = x_ref[...]
        ms = jnp.mean(x * x, axis=-1, keepdims=True)
        h = (x * lax.rsqrt(ms + EPS) * gain_ref[...]).astype(BF16)
        h_s[...] = h
        yt = _dot_nt(wt_ref[...], h)
        vt_ref[...] = yt[:DF_WIDTH].astype(BF16)
        pre = yt[DF_WIDTH:]
        g = -jnp.exp(alog_ref[...]) * jax.nn.softplus(pre + dtb_ref[...])
        pos = lax.broadcasted_iota(jnp.int32, g.shape, 1) % CHUNK
        s = 1
        while s < CHUNK:
            g = g + jnp.where(pos >= s, pltpu.roll(g, s, 1), 0.0)
            s *= 2
        row = lax.broadcasted_iota(jnp.int32, g.shape, 0)
        gate_ref[...] = jnp.where(row < DN_HEADS, jax.nn.sigmoid(pre), g)

    def chunk_phase(j, conv_items):
        cols = slice(j * IN_TN, (j + 1) * IN_TN)
        y = jnp.dot(h_s[...], w_ref[:, cols], preferred_element_type=F32)
        for item in conv_items:
            conv_silu_norm(*item)
        if j < QKV_COLS // IN_TN:
            stage(y, j)
            return
        if j in (_BLK_DF_Q * LANES // IN_TN, _BLK_DF_K * LANES // IN_TN):
            which = 0 if j == _BLK_DF_Q * LANES // IN_TN else 1
            sq = (y * y).astype(BF16)
            msq = jnp.concatenate(
                [jnp.dot(sq[:, i * SEG_W:(i + 1) * SEG_W], seg_ref[...],
                         preferred_element_type=F32) for i in range(IN_TN // SEG_W)], axis=1)
            y = y * lax.rsqrt(msq + EPS) * qkg_ref[which:which + 1, :]
        elif j in (_BLK_DN_Z * LANES // IN_TN, _BLK_DF_Z * LANES // IN_TN):
            y = _silu(y)
        proj_ref[:, cols] = y.astype(BF16)

    head_phase()
    n_chunks = PROJ_COLS // IN_TN
    per_phase = -(-(QKV_COLS // IN_TN) * (IN_TM // CONV_ROWS) // (n_chunks - 1))
    pending = []
    for j in range(n_chunks):
        items = [pending.pop(0) for _ in range(min(per_phase, len(pending)))]
        chunk_phase(j, items)
        if j < QKV_COLS // IN_TN:
            pending += [(j, part) for part in range(IN_TM // CONV_ROWS)]
    for item in pending:
        conv_silu_norm(*item)


def _in_projection(x2d, t, gain, w_main, w_t, seg, qk_gain, alog_col, dtb_col, conv_w):
    m = x2d.shape[0]
    assert DF_WIDTH == IN_TN and DN_WIDTH == IN_TN and t % IN_TM == 0 and IN_TM % CHUNK == 0
    tiles_per_seq = t // IN_TM
    const = lambda shape: pl.BlockSpec(shape, lambda i: (0, 0))
    return pl.pallas_call(
        functools.partial(_inproj_kernel, tiles_per_seq=tiles_per_seq),
        grid=(m // IN_TM,),
        in_specs=[
            pl.BlockSpec((IN_TM, D_MODEL), lambda i: (i, 0)),
            const((1, D_MODEL)),
            const((D_MODEL, PROJ_COLS)),
            const((DF_WIDTH + GATE_ROWS, D_MODEL)),
            const((SEG_W, SEG_W)),
            const((2, DF_WIDTH)),
            const((GATE_ROWS, 1)),
            const((GATE_ROWS, 1)),
            const((CONV_WIDTH, QKV_COLS)),
        ],
        out_specs=[
            pl.BlockSpec((IN_TM, PROJ_COLS), lambda i: (i, 0)),
            pl.BlockSpec((None, DF_WIDTH, IN_TM),
                         lambda i: (i // tiles_per_seq, 0, i % tiles_per_seq)),
            pl.BlockSpec((None, GATE_ROWS, IN_TM),
                         lambda i: (i // tiles_per_seq, 0, i % tiles_per_seq)),
        ],
        out_shape=[
            jax.ShapeDtypeStruct((m, PROJ_COLS), BF16),
            jax.ShapeDtypeStruct((m // t, DF_WIDTH, t), BF16),
            jax.ShapeDtypeStruct((m // t, GATE_ROWS, t), F32),
        ],
        scratch_shapes=[
            pltpu.VMEM((IN_TM, D_MODEL), BF16),
            pltpu.VMEM((QKV_COLS // LANES, IN_TM + SUBLANES, LANES), F32),
            pltpu.VMEM((QKV_COLS // LANES, SUBLANES, LANES), F32),
        ],
        compiler_params=pltpu.CompilerParams(
            dimension_semantics=("arbitrary",), vmem_limit_bytes=VMEM_LIMIT),
        name="in_projection",
    )(x2d, gain, w_main, w_t, seg, qk_gain, alog_col, dtb_col, conv_w)


PREP_CHUNKS = 4


def _gdn_kernel(qkv_ref, zg_ref, gate_ref, gain_ref,
                out_ref, u_r, wq_r, aqk_r, kdt_r, state_s):
    t = qkv_ref.shape[0]
    n_chunks = t // CHUNK
    group_rows = PREP_CHUNKS * CHUNK

    def group_base(g):
        return g * group_rows if isinstance(g, int) else pl.multiple_of(g * group_rows, group_rows)

    row = lax.broadcasted_iota(jnp.int32, (CHUNK, 2 * CHUNK), 0)
    col = lax.broadcasted_iota(jnp.int32, (CHUNK, 2 * CHUNK), 1)
    eye_f = (row == col).astype(F32)
    lower_f = (row >= col).astype(F32)
    strict_f = (row > col).astype(F32)
    upper_f = (col >= CHUNK).astype(F32)
    eye_hi = (col - CHUNK == row).astype(F32)
    zero_rows = jnp.zeros((CHUNK, DN_HEAD), BF16)

    items = [(cc, h) for cc in range(PREP_CHUNKS) for h in range(DN_HEADS)]
    heads = range(DN_HEADS)
    gain = gain_ref[...]
    hcols = [slice(h * LANES, (h + 1) * LANES) for h in heads]
    hcols_c = [slice(h * LANES, h * LANES + CHUNK) for h in heads]

    def halves(fn, seq):
        out = []
        mid = len(seq) // 2
        for part in (seq[:mid], seq[mid:]):
            out.extend(fn(x) for x in part)
            yield out

    def prep(g, slot):
        base = group_base(g)
        gates = [gate_ref[pl.ds(base + cc * CHUNK, CHUNK), :] for cc in range(PREP_CHUNKS)]
        ks, lhs, decay, rhs, qe, kdt = [], [], [], [], [], []
        for cc, h in items:
            trows = pl.ds(base + cc * CHUNK, CHUNK)
            q = qkv_ref[trows, hcols[h]].astype(F32)
            k = qkv_ref[trows, slice(DN_WIDTH + h * LANES, DN_WIDTH + (h + 1) * LANES)].astype(F32)
            v = qkv_ref[trows, slice(2 * DN_WIDTH + h * LANES,
                                     2 * DN_WIDTH + (h + 1) * LANES)].astype(F32)
            beta_b = jnp.broadcast_to(gates[cc][:, h:h + 1], (CHUNK, DN_HEAD))
            gcb = jnp.broadcast_to(gates[cc][:, DN_HEADS + h:DN_HEADS + h + 1], (CHUNK, DN_HEAD))
            gr = jnp.sum(gcb * eye_f, axis=0, keepdims=True)
            decay.append(jnp.exp(jnp.minimum(gcb - gr, 0.0)))
            g_last = gcb[CHUNK - 1:CHUNK, :]
            eg = jnp.exp(gcb)
            kb = k * beta_b
            ks.append(jnp.concatenate([k.astype(BF16), zero_rows], axis=0))
            lhs.append(jnp.concatenate([kb, q], axis=0).astype(BF16))
            rhs.append(jnp.concatenate([v * beta_b, kb * eg], axis=1).astype(BF16))
            qe.append((q * eg).astype(BF16))
            kdt.append((k * jnp.exp(g_last - gcb)).T.astype(BF16))
        yield

        kq = None
        for kq in halves(lambda ab: _dot_nt(*ab), list(zip(lhs, ks))):
            yield
        m = [x[:CHUNK] * (d * strict_f) for x, d in zip(kq, decay)]
        a_qk = [(x[CHUNK:, :CHUNK] * (d * lower_f)[:, :CHUNK]).astype(BF16)
                for x, d in zip(kq, decay)]

        w = None
        for w in halves(lambda x: _dot(x[:, :CHUNK], x - eye_hi) + eye_hi, m):
            yield
        for _ in range(5):
            prev = w
            for w in halves(lambda x: _dot(x[:, :CHUNK], x) + x * upper_f, prev):
                yield
        sol = None
        for sol in halves(lambda xb: _dot(xb[0][:, CHUNK:], xb[1]), list(zip(w, rhs))):
            yield

        for n, (cc, h) in enumerate(items):
            rows = slice(cc * CHUNK, (cc + 1) * CHUNK)
            rows_w = slice(2 * cc * CHUNK, (2 * cc + 1) * CHUNK)
            rows_q = slice((2 * cc + 1) * CHUNK, (2 * cc + 2) * CHUNK)
            rows_t = slice(2 * cc * CHUNK, (2 * cc + 2) * CHUNK)
            u_r[slot, rows, hcols[h]] = sol[n][:, :DN_HEAD]
            wq_r[slot, rows_w, hcols[h]] = sol[n][:, DN_HEAD:].astype(BF16)
            wq_r[slot, rows_q, hcols[h]] = qe[n]
            aqk_r[slot, rows, hcols_c[h]] = a_qk[n]
            kdt_r[slot, rows_t, hcols_c[h]] = kdt[n]
        yield

    def rec(g, slot):
        base = group_base(g)
        for cc in range(PREP_CHUNKS):
            rows = slice(cc * CHUNK, (cc + 1) * CHUNK)
            rows2 = slice(2 * cc * CHUNK, (2 * cc + 2) * CHUNK)
            trows = pl.ds(base + cc * CHUNK, CHUNK)
            st = [state_s[h] for h in heads]
            r = [jnp.dot(wq_r[slot, rows2, hcols[h]], st[h].astype(BF16),
                         preferred_element_type=F32) for h in heads]
            yield
            v_new = [(u_r[slot, rows, hcols[h]] - r[h][:CHUNK]).astype(BF16) for h in heads]
            upd = [jnp.dot(kdt_r[slot, rows2, hcols_c[h]], v_new[h], preferred_element_type=F32)
                   for h in heads]
            tail = gate_ref[pl.ds(base + (cc + 1) * CHUNK - 8, 8), :]
            decay_last = jnp.exp(tail[7:8, :])
            for h in heads:
                state_s[h] = st[h] * decay_last[:, DN_HEADS + h:DN_HEADS + h + 1] + upd[h]
            yield
            o = [r[h][CHUNK:] + jnp.dot(aqk_r[slot, rows, hcols_c[h]], v_new[h],
                                        preferred_element_type=F32) for h in heads]
            for h in heads:
                y = o[h] * lax.rsqrt(jnp.mean(o[h] * o[h], axis=-1, keepdims=True) + EPS) * gain
                gate = zg_ref[trows, hcols[h]].astype(F32)
                out_ref[trows, hcols[h]] = (y * gate).astype(out_ref.dtype)
            yield

    def run_interleaved(*gens):
        gens = list(gens)
        while gens:
            for gen in list(gens):
                if next(gen, StopIteration) is StopIteration:
                    gens.remove(gen)

    n_groups = n_chunks // PREP_CHUNKS
    state_s[...] = jnp.zeros_like(state_s)
    run_interleaved(prep(0, 0))

    def pair_body(j, carry):
        run_interleaved(rec(2 * j, 0), prep(2 * j + 1, 1))
        run_interleaved(rec(2 * j + 1, 1), prep(2 * j + 2, 0))
        return carry

    lax.fori_loop(0, (n_groups - 2) // 2, pair_body, 0)
    run_interleaved(rec(n_groups - 2, 0), prep(n_groups - 1, 1))
    run_interleaved(rec(n_groups - 1, 1))


def _gated_deltanet(proj, gates, out_gain):
    b, t, _ = proj.shape
    group_rows = PREP_CHUNKS * CHUNK
    assert t % (2 * group_rows) == 0 and t // group_rows >= 4
    ring = lambda rows, dtype: pltpu.VMEM((2, rows, DN_WIDTH), dtype)
    return pl.pallas_call(
        _gdn_kernel,
        grid=(b,),
        in_specs=[
            pl.BlockSpec((None, t, QKV_COLS), lambda i: (i, 0, 0)),
            pl.BlockSpec((None, t, DN_WIDTH), lambda i: (i, 0, QKV_COLS // DN_WIDTH)),
            pl.BlockSpec((None, t, N_GATES), lambda i: (i, 0, 0)),
            pl.BlockSpec((1, DN_HEAD), lambda i: (0, 0)),
        ],
        out_specs=pl.BlockSpec((None, t, DN_WIDTH), lambda i: (i, 0, 0)),
        out_shape=jax.ShapeDtypeStruct((b, t, DN_WIDTH), BF16),
        scratch_shapes=[
            ring(group_rows, F32),
            ring(2 * group_rows, BF16),
            ring(group_rows, BF16),
            ring(2 * group_rows, BF16),
            pltpu.VMEM((DN_HEADS, DN_HEAD, DN_HEAD), F32),
        ],
        compiler_params=pltpu.CompilerParams(
            dimension_semantics=("parallel",), vmem_limit_bytes=VMEM_LIMIT),
        name="gated_deltanet",
    )(proj, proj, gates, out_gain)


AT_BQ = 512
AT_BK = 256
AT_CW = 256
AT_NQ = 2 * AT_BQ
AT_CHAINS = AT_NQ // AT_CW
AT_CPM = AT_BQ // AT_CW
AT_VROWS = DF_HEAD_V + 16
AT_PROJ_DELAY = 2
AT_HPS = 2
OUT_TN = 256


def _attention_tiles(n_heads, n_qblocks):
    all_chains = tuple(range(AT_CHAINS))
    lo = tuple(c for c in all_chains if c % AT_CPM == 0)
    hi = tuple(c for c in all_chains if c % AT_CPM == 1)
    tiles = []
    for hd in range(n_heads):
        for qi in range(n_qblocks):
            for kt in range(AT_CPM * qi + AT_CPM):
                d = kt - AT_CPM * qi
                chains, masked = ((all_chains, ()) if d < 0 else
                                  (all_chains, lo) if d == 0 else (hi, hi))
                tiles.append(dict(hd=hd, qi=qi, qb=hd * n_qblocks + qi, kt=kt, chains=chains,
                                  masked=masked, first=kt == 0))
    return tiles


def _diff_attn_kernel(q_ref, k_ref, vt_ref, zg_ref, lq1_ref, lk1_ref, lq2_ref, lk2_ref,
                      og_ref, x_ref, dn_ref, wout_ref, out_ref,
                      qq_s, vta_s, m_s, acc_s, y_s, s_a, s_b, c_a, c_b, p_a, p_b):
    t = q_ref.shape[0]
    n_qblocks = t // AT_BQ
    all_chains = tuple(range(AT_CHAINS))

    @pl.when(pl.program_id(1) == 0)
    def _():
        out_ref[...] = x_ref[...]

    def lanes(c):
        return slice(c * AT_CW, (c + 1) * AT_CW)

    first = lax.broadcasted_iota(jnp.int32, (AT_BQ, LANES), 1) < DF_HEAD_QK
    zero = jnp.zeros((AT_BQ, LANES), BF16)
    hcols = [slice(hd * LANES, (hd + 1) * LANES) for hd in range(AT_HPS)]
    for hd in range(AT_HPS):
        for qi in range(n_qblocks):
            blk = q_ref[qi * AT_BQ:(qi + 1) * AT_BQ, hcols[hd]]
            qq_s[hd * n_qblocks + qi, :AT_BQ, :] = jnp.where(first, blk, zero)
            qq_s[hd * n_qblocks + qi, AT_BQ:, :] = jnp.where(first, zero, blk)
        vta_s[hd, :DF_HEAD_V, :] = vt_ref[hcols[hd], :]
        vta_s[hd, DF_HEAD_V:, :] = jnp.ones((AT_VROWS - DF_HEAD_V, t), BF16)

    lam = (jnp.exp(jnp.sum(lq1_ref[...] * lk1_ref[...], axis=-1, keepdims=True))
           - jnp.exp(jnp.sum(lq2_ref[...] * lk2_ref[...], axis=-1, keepdims=True))
           + LAMBDA_INIT)

    krow = lax.broadcasted_iota(jnp.int32, (AT_BK, AT_CW), 0)
    qcol = lax.broadcasted_iota(jnp.int32, (AT_BK, AT_CW), 1)
    causal = krow <= qcol
    bufs = ((s_a, c_a, p_a), (s_b, c_b, p_b))

    def scores(tile, c, buf):
        rows = slice(tile["kt"] * AT_BK, (tile["kt"] + 1) * AT_BK)
        s = _dot_nt(k_ref[rows, hcols[tile["hd"]]], qq_s[tile["qb"], lanes(c), :])
        if c in tile["masked"]:
            s = jnp.where(causal, s, -jnp.inf)
        buf[0][:, lanes(c)] = s
        buf[1][:, lanes(c)] = jnp.max(s, axis=0, keepdims=True)

    def values(tile, c, buf):
        rows = slice(tile["kt"] * AT_BK, (tile["kt"] + 1) * AT_BK)
        return jnp.dot(vta_s[tile["hd"], :, rows], buf[2][:, lanes(c)],
                       preferred_element_type=F32)

    def finish(tile):
        qb = tile["qb"]
        rows = slice(tile["qi"] * AT_BQ, (tile["qi"] + 1) * AT_BQ)
        denom = acc_s[qb, DF_HEAD_V:DF_HEAD_V + 1, :]
        o_all = acc_s[qb, :DF_HEAD_V, :] * (1.0 / denom)
        o = o_all[:, :AT_BQ] - lam * o_all[:, AT_BQ:]
        y = (o * lax.rsqrt(jnp.mean(o * o, axis=0, keepdims=True) + EPS)).T
        y = y * og_ref[...] * (1.0 - LAMBDA_INIT)
        y_s[qb] = (y * zg_ref[rows, hcols[tile["hd"]]].astype(F32)).astype(BF16)

    def project(qi):
        rows = slice(qi * AT_BQ, (qi + 1) * AT_BQ)
        lhs = jnp.concatenate(
            [part for hd in range(AT_HPS)
             for part in (y_s[hd * n_qblocks + qi], dn_ref[rows, hcols[hd]])], axis=1)
        for nc in range(D_MODEL // OUT_TN):
            cols = slice(nc * OUT_TN, (nc + 1) * OUT_TN)
            w_cols = jnp.concatenate([wout_ref[hd, :, cols] for hd in range(AT_HPS)], axis=0)
            out_ref[rows, cols] = out_ref[rows, cols] + jnp.dot(
                lhs, w_cols, preferred_element_type=F32)

    tiles = _attention_tiles(AT_HPS, n_qblocks)
    due = {}
    for c in tiles[0]["chains"]:
        scores(tiles[0], c, bufs[0])
    for n, cur in enumerate(tiles):
        nxt = tiles[n + 1] if n + 1 < len(tiles) else None
        prv = tiles[n - 1] if n > 0 else None
        s_cur, c_cur, p_cur = bufs[n % 2]
        qb = cur["qb"]
        for c in all_chains:
            alpha = None
            if c in cur["chains"]:
                if cur["first"]:
                    m_new = c_cur[:, lanes(c)]
                else:
                    m_old = m_s[qb, :, lanes(c)]
                    m_new = jnp.maximum(m_old, c_cur[:, lanes(c)])
                    alpha = jnp.exp2(m_old - m_new)
                p = jnp.exp2(s_cur[:, lanes(c)] - m_new)
                m_s[qb, :, lanes(c)] = m_new
                p_cur[:, lanes(c)] = p.astype(BF16)
            if nxt is not None and c in nxt["chains"]:
                scores(nxt, c, bufs[(n + 1) % 2])
            if prv is not None and c in prv["chains"]:
                pv = values(prv, c, bufs[(n - 1) % 2])
                pq = prv["qb"]
                acc = pv if prv["first"] else acc_s[pq, :, lanes(c)] + pv
                if pq == qb and alpha is not None:
                    acc = alpha * acc
                acc_s[pq, :, lanes(c)] = acc
        if prv is not None and prv["qb"] != qb:
            finish(prv)
            if prv["hd"] == AT_HPS - 1:
                due[n + AT_PROJ_DELAY] = prv["qi"]
        if n in due:
            project(due.pop(n))
    last = tiles[-1]
    for c in last["chains"]:
        acc_s[last["qb"], :, lanes(c)] = (acc_s[last["qb"], :, lanes(c)]
                                          + values(last, c, bufs[(len(tiles) - 1) % 2]))
    finish(last)
    for qi in sorted(due.values()) + [last["qi"]]:
        project(qi)


def _diff_attention(proj, vt, lq1, lk1, lq2, lk2, out_gain, x, mixed_dn, w_heads):
    b, t, _ = proj.shape
    assert AT_BQ == AT_CPM * AT_BK and AT_CW == AT_BK and AT_CPM == 2 and t % AT_BQ == 0
    assert DF_HEADS == DN_HEADS and DF_HEAD_V == DN_HEAD and D_MODEL % OUT_TN == 0
    assert DF_HEADS % AT_HPS == 0 and all(
        blk % AT_HPS == 0 for blk in (_BLK_DF_Q, _BLK_DF_K, _BLK_DF_Z))
    n_qblocks = t // AT_BQ
    n_blocks = AT_HPS * n_qblocks
    wide = AT_HPS * LANES
    small = lambda n: pl.BlockSpec((1, n), lambda i, h: (0, 0))
    return pl.pallas_call(
        _diff_attn_kernel,
        grid=(b, DF_HEADS // AT_HPS),
        in_specs=[
            pl.BlockSpec((None, t, wide), lambda i, h: (i, 0, _BLK_DF_Q // AT_HPS + h)),
            pl.BlockSpec((None, t, wide), lambda i, h: (i, 0, _BLK_DF_K // AT_HPS + h)),
            pl.BlockSpec((None, wide, t), lambda i, h: (i, h, 0)),
            pl.BlockSpec((None, t, wide), lambda i, h: (i, 0, _BLK_DF_Z // AT_HPS + h)),
            small(DF_HEAD_QK), small(DF_HEAD_QK), small(DF_HEAD_QK), small(DF_HEAD_QK),
            small(DF_HEAD_V),
            pl.BlockSpec((None, t, D_MODEL), lambda i, h: (i, 0, 0)),
            pl.BlockSpec((None, t, wide), lambda i, h: (i, 0, h)),
            pl.BlockSpec((AT_HPS, DF_HEAD_V + DN_HEAD, D_MODEL), lambda i, h: (h, 0, 0)),
        ],
        out_specs=pl.BlockSpec((None, t, D_MODEL), lambda i, h: (i, 0, 0)),
        out_shape=jax.ShapeDtypeStruct((b, t, D_MODEL), F32),
        scratch_shapes=[
            pltpu.VMEM((n_blocks, AT_NQ, LANES), BF16),
            pltpu.VMEM((AT_HPS, AT_VROWS, t), BF16),
            pltpu.VMEM((n_blocks, 1, AT_NQ), F32),
            pltpu.VMEM((n_blocks, AT_VROWS, AT_NQ), F32),
            pltpu.VMEM((n_blocks, AT_BQ, DF_HEAD_V), BF16),
            pltpu.VMEM((AT_BK, AT_NQ), F32),
            pltpu.VMEM((AT_BK, AT_NQ), F32),
            pltpu.VMEM((1, AT_NQ), F32),
            pltpu.VMEM((1, AT_NQ), F32),
            pltpu.VMEM((AT_BK, AT_NQ), BF16),
            pltpu.VMEM((AT_BK, AT_NQ), BF16),
        ],
        compiler_params=pltpu.CompilerParams(
            dimension_semantics=("parallel", "arbitrary"), vmem_limit_bytes=VMEM_LIMIT),
        name="diff_attention",
    )(proj, proj, vt, proj, lq1, lk1, lq2, lk2, out_gain, x, mixed_dn, w_heads)


def kernel(x, norm_gain, w_in, conv_w, a_log, dt_bias, dn_out_gain, q_gain, k_gain,
           lambda_q1, lambda_k1, lambda_q2, lambda_k2, df_out_gain, w_out):
    b, t, d = x.shape
    assert d == D_MODEL and norm_gain.shape[0] == 1
    x2d = x.reshape(b * t, d)

    w = w_in[0]
    gate_lo = 4 * DN_WIDTH
    df_lo = gate_lo + N_GATES
    w_main = jnp.concatenate([w[:, :gate_lo], w[:, df_lo:df_lo + 2 * DF_WIDTH],
                              w[:, df_lo + 3 * DF_WIDTH:]], axis=1).astype(BF16)
    w_t = jnp.concatenate([w[:, df_lo + 2 * DF_WIDTH:df_lo + 3 * DF_WIDTH],
                           jnp.pad(w[:, gate_lo:df_lo], ((0, 0), (0, GATE_ROWS - N_GATES)))],
                          axis=1).astype(BF16).T
    half = jnp.arange(SEG_W, dtype=jnp.int32) // DF_HEAD_QK
    seg = ((half[:, None] == half[None, :]).astype(F32) / DF_HEAD_QK).astype(BF16)
    reps = DF_WIDTH // DF_HEAD_QK
    qk_gain = jnp.concatenate([jnp.tile(q_gain, (1, reps)) * (DF_HEAD_QK ** -0.5 * LOG2_E),
                               jnp.tile(k_gain, (1, reps))], axis=0)

    pad_rows = lambda v: jnp.pad(v.reshape(DN_HEADS, 1),
                                 ((DN_HEADS, GATE_ROWS - 2 * DN_HEADS), (0, 0)))
    proj, vt, gates_t = _in_projection(x2d, t, norm_gain, w_main, w_t, seg, qk_gain,
                                       pad_rows(a_log), pad_rows(dt_bias), conv_w[0])
    proj = proj.reshape(b, t, PROJ_COLS)
    gates = jnp.swapaxes(gates_t[:, :N_GATES, :], 1, 2)

    mixed_dn = _gated_deltanet(proj, gates, dn_out_gain)

    wo = w_out[0].astype(BF16)
    w_heads = jnp.concatenate([wo[DN_WIDTH:].reshape(DF_HEADS, DF_HEAD_V, d),
                               wo[:DN_WIDTH].reshape(DN_HEADS, DN_HEAD, d)], axis=1)
    return _diff_attention(proj, vt, lambda_q1, lambda_k1, lambda_q2, lambda_k2,
                           df_out_gain, x, mixed_dn, w_heads)
```

```python
import functools
import math

import jax
import jax.numpy as jnp
from jax import lax
from jax.experimental import pallas as pl
from jax.experimental.pallas import tpu as pltpu

F32 = jnp.float32
BF16 = jnp.bfloat16

D_MODEL = 1024
DN_HEADS = 4
DN_HEAD = 128
DN_WIDTH = DN_HEADS * DN_HEAD
CONV_WIDTH = 4
CHUNK = 64
DF_HEADS = 4
DF_HEAD_QK = 64
DF_HEAD_V = 128
DF_WIDTH = DF_HEADS * DF_HEAD_V
EPS = 1e-6
LAMBDA_INIT = 0.8 - 0.6 * math.exp(-0.3 * 0)
LOG2_E = math.log2(math.e)

N_GATES = 2 * DN_HEADS
GATE_ROWS = 16
PROJ_COLS = 4 * DN_WIDTH + 3 * DF_WIDTH
LANES = 128
SUBLANES = 8
QKV_COLS = 3 * DN_WIDTH
VMEM_LIMIT = 56 * 1024 * 1024

_BLK_DN_Z = 3 * DN_HEADS
_BLK_DF_Q = 4 * DN_HEADS
_BLK_DF_K = _BLK_DF_Q + DF_HEADS
_BLK_DF_Z = _BLK_DF_K + DF_HEADS


def _silu(x):
    half = 0.5 * x
    return half + half * jnp.tanh(half)


def _dot(a, b):
    return jnp.dot(a.astype(BF16), b.astype(BF16), preferred_element_type=F32)


def _dot_nt(a, b):
    return lax.dot_general(a.astype(BF16), b.astype(BF16), (((1,), (1,)), ((), ())),
                           preferred_element_type=F32)


IN_TM = 1024
IN_TN = 512
SEG_W = 256
CONV_ROWS = 512


def _inproj_kernel(x_ref, gain_ref, w_ref, wt_ref, seg_ref, qkg_ref, alog_ref, dtb_ref,
                   convw_ref, proj_ref, vt_ref, gate_ref, h_s, cstage_s, tail_s, *, tiles_per_seq):
    @pl.when(pl.program_id(0) % tiles_per_seq == 0)
    def _():
        tail_s[...] = jnp.zeros_like(tail_s)

    def stage(y, j):
        for hd in range(DN_HEADS):
            yh = y[:, hd * DN_HEAD:(hd + 1) * DN_HEAD]
            cstage_s[j * DN_HEADS + hd, :SUBLANES, :] = tail_s[j * DN_HEADS + hd]
            cstage_s[j * DN_HEADS + hd, SUBLANES:, :] = yh
            tail_s[j * DN_HEADS + hd] = yh[IN_TM - SUBLANES:, :]

    def conv_silu_norm(j, part):
        r0 = SUBLANES + part * CONV_ROWS
        for hd in range(DN_HEADS):
            cols = slice(j * IN_TN + hd * DN_HEAD, j * IN_TN + (hd + 1) * DN_HEAD)
            buf = cstage_s.at[j * DN_HEADS + hd]
            w = convw_ref[:, cols]
            acc = buf[r0:r0 + CONV_ROWS, :] * w[CONV_WIDTH - 1:CONV_WIDTH]
            for d in range(1, CONV_WIDTH):
                acc = acc + buf[r0 - d:r0 - d + CONV_ROWS, :] * w[CONV_WIDTH - 1 - d:CONV_WIDTH - d]
            a = _silu(acc)
            if j < 2:
                inv_norm = lax.rsqrt(jnp.sum(a * a, axis=-1, keepdims=True) + EPS)
                a = a * (inv_norm * (DN_HEAD ** -0.5) if j == 0 else inv_norm)
            proj_ref[part * CONV_ROWS:(part + 1) * CONV_ROWS, cols] = a.astype(BF16)

    def head_phase():
        x = x_ref[...]
        ms = jnp.mean(x * x, axis=-1, keepdims=True)
        h = (x * lax.rsqrt(ms + EPS) * gain_ref[...]).astype(BF16)
        h_s[...] = h
        yt = _dot_nt(wt_ref[...], h)
        vt_ref[...] = yt[:DF_WIDTH].astype(BF16)
        pre = yt[DF_WIDTH:]
        g = -jnp.exp(alog_ref[...]) * jax.nn.softplus(pre + dtb_ref[...])
        pos = lax.broadcasted_iota(jnp.int32, g.shape, 1) % CHUNK
        s = 1
        while s < CHUNK:
            g = g + jnp.where(pos >= s, pltpu.roll(g, s, 1), 0.0)
            s *= 2
        row = lax.broadcasted_iota(jnp.int32, g.shape, 0)
        gate_ref[...] = jnp.where(row < DN_HEADS, jax.nn.sigmoid(pre), g)

    def chunk_phase(j, conv_items):
        cols = slice(j * IN_TN, (j + 1) * IN_TN)
        y = jnp.dot(h_s[...], w_ref[:, cols], preferred_element_type=F32)
        for item in conv_items:
            conv_silu_norm(*item)
        if j < QKV_COLS // IN_TN:
            stage(y, j)
            return
        if j in (_BLK_DF_Q * LANES // IN_TN, _BLK_DF_K * LANES // IN_TN):
            which = 0 if j == _BLK_DF_Q * LANES // IN_TN else 1
            sq = (y * y).astype(BF16)
            msq = jnp.concatenate(
                [jnp.dot(sq[:, i * SEG_W:(i + 1) * SEG_W], seg_ref[...],
                         preferred_element_type=F32) for i in range(IN_TN // SEG_W)], axis=1)
            y = y * lax.rsqrt(msq + EPS) * qkg_ref[which:which + 1, :]
        elif j in (_BLK_DN_Z * LANES // IN_TN, _BLK_DF_Z * LANES // IN_TN):
            y = _silu(y)
        proj_ref[:, cols] = y.astype(BF16)

    head_phase()
    n_chunks = PROJ_COLS // IN_TN
    per_phase = -(-(QKV_COLS // IN_TN) * (IN_TM // CONV_ROWS) // (n_chunks - 1))
    pending = []
    for j in range(n_chunks):
        items = [pending.pop(0) for _ in range(min(per_phase, len(pending)))]
        chunk_phase(j, items)
        if j < QKV_COLS // IN_TN:
            pending += [(j, part) for part in range(IN_TM // CONV_ROWS)]
    for item in pending:
        conv_silu_norm(*item)


def _in_projection(x2d, t, gain, w_main, w_t, seg, qk_gain, alog_col, dtb_col, conv_w):
    m = x2d.shape[0]
    assert DF_WIDTH == IN_TN and DN_WIDTH == IN_TN and t % IN_TM == 0 and IN_TM % CHUNK == 0
    tiles_per_seq = t // IN_TM
    const = lambda shape: pl.BlockSpec(shape, lambda i: (0, 0))
    return pl.pallas_call(
        functools.partial(_inproj_kernel, tiles_per_seq=tiles_per_seq),
        grid=(m // IN_TM,),
        in_specs=[
            pl.BlockSpec((IN_TM, D_MODEL), lambda i: (i, 0)),
            const((1, D_MODEL)),
            const((D_MODEL, PROJ_COLS)),
            const((DF_WIDTH + GATE_ROWS, D_MODEL)),
            const((SEG_W, SEG_W)),
            const((2, DF_WIDTH)),
            const((GATE_ROWS, 1)),
            const((GATE_ROWS, 1)),
            const((CONV_WIDTH, QKV_COLS)),
        ],
        out_specs=[
            pl.BlockSpec((IN_TM, PROJ_COLS), lambda i: (i, 0)),
            pl.BlockSpec((None, DF_WIDTH, IN_TM),
                         lambda i: (i // tiles_per_seq, 0, i % tiles_per_seq)),
            pl.BlockSpec((None, GATE_ROWS, IN_TM),
                         lambda i: (i // tiles_per_seq, 0, i % tiles_per_seq)),
        ],
        out_shape=[
            jax.ShapeDtypeStruct((m, PROJ_COLS), BF16),
            jax.ShapeDtypeStruct((m // t, DF_WIDTH, t), BF16),
            jax.ShapeDtypeStruct((m // t, GATE_ROWS, t), F32),
        ],
        scratch_shapes=[
            pltpu.VMEM((IN_TM, D_MODEL), BF16),
            pltpu.VMEM((QKV_COLS // LANES, IN_TM + SUBLANES, LANES), F32),
            pltpu.VMEM((QKV_COLS // LANES, SUBLANES, LANES), F32),
        ],
        compiler_params=pltpu.CompilerParams(
            dimension_semantics=("arbitrary",), vmem_limit_bytes=VMEM_LIMIT),
        name="in_projection",
    )(x2d, gain, w_main, w_t, seg, qk_gain, alog_col, dtb_col, conv_w)


PREP_CHUNKS = 4


def _gdn_kernel(qkv_ref, zg_ref, gate_ref, gain_ref,
                out_ref, u_r, wq_r, aqk_r, kdt_r, state_s):
    t = qkv_ref.shape[0]
    n_chunks = t // CHUNK
    group_rows = PREP_CHUNKS * CHUNK

    def group_base(g):
        return g * group_rows if isinstance(g, int) else pl.multiple_of(g * group_rows, group_rows)

    row = lax.broadcasted_iota(jnp.int32, (CHUNK, 2 * CHUNK), 0)
    col = lax.broadcasted_iota(jnp.int32, (CHUNK, 2 * CHUNK), 1)
    eye_f = (row == col).astype(F32)
    lower_f = (row >= col).astype(F32)
    strict_f = (row > col).astype(F32)
    upper_f = (col >= CHUNK).astype(F32)
    eye_hi = (col - CHUNK == row).astype(F32)
    zero_rows = jnp.zeros((CHUNK, DN_HEAD), BF16)

    items = [(cc, h) for cc in range(PREP_CHUNKS) for h in range(DN_HEADS)]
    heads = range(DN_HEADS)
    gain = gain_ref[...]
    hcols = [slice(h * LANES, (h + 1) * LANES) for h in heads]
    hcols_c = [slice(h * LANES, h * LANES + CHUNK) for h in heads]

    def halves(fn, seq):
        out = []
        mid = len(seq) // 2
        for part in (seq[:mid], seq[mid:]):
            out.extend(fn(x) for x in part)
            yield out

    def prep(g, slot):
        base = group_base(g)
        gates = [gate_ref[pl.ds(base + cc * CHUNK, CHUNK), :] for cc in range(PREP_CHUNKS)]
        ks, lhs, decay, rhs, qe, kdt = [], [], [], [], [], []
        for cc, h in items:
            trows = pl.ds(base + cc * CHUNK, CHUNK)
            q = qkv_ref[trows, hcols[h]].astype(F32)
            k = qkv_ref[trows, slice(DN_WIDTH + h * LANES, DN_WIDTH + (h + 1) * LANES)].astype(F32)
            v = qkv_ref[trows, slice(2 * DN_WIDTH + h * LANES,
                                     2 * DN_WIDTH + (h + 1) * LANES)].astype(F32)
            beta_b = jnp.broadcast_to(gates[cc][:, h:h + 1], (CHUNK, DN_HEAD))
            gcb = jnp.broadcast_to(gates[cc][:, DN_HEADS + h:DN_HEADS + h + 1], (CHUNK, DN_HEAD))
            gr = jnp.sum(gcb * eye_f, axis=0, keepdims=True)
            decay.append(jnp.exp(jnp.minimum(gcb - gr, 0.0)))
            g_last = gcb[CHUNK - 1:CHUNK, :]
            eg = jnp.exp(gcb)
            kb = k * beta_b
            ks.append(jnp.concatenate([k.astype(BF16), zero_rows], axis=0))
            lhs.append(jnp.concatenate([kb, q], axis=0).astype(BF16))
            rhs.append(jnp.concatenate([v * beta_b, kb * eg], axis=1).astype(BF16))
            qe.append((q * eg).astype(BF16))
            kdt.append((k * jnp.exp(g_last - gcb)).T.astype(BF16))
        yield

        kq = None
        for kq in halves(lambda ab: _dot_nt(*ab), list(zip(lhs, ks))):
            yield
        m = [x[:CHUNK] * (d * strict_f) for x, d in zip(kq, decay)]
        a_qk = [(x[CHUNK:, :CHUNK] * (d * lower_f)[:, :CHUNK]).astype(BF16)
                for x, d in zip(kq, decay)]

        w = None
        for w in halves(lambda x: _dot(x[:, :CHUNK], x - eye_hi) + eye_hi, m):
            yield
        for _ in range(5):
            prev = w
            for w in halves(lambda x: _dot(x[:, :CHUNK], x) + x * upper_f, prev):
                yield
        sol = None
        for sol in halves(lambda xb: _dot(xb[0][:, CHUNK:], xb[1]), list(zip(w, rhs))):
            yield

        for n, (cc, h) in enumerate(items):
            rows = slice(cc * CHUNK, (cc + 1) * CHUNK)
            rows_w = slice(2 * cc * CHUNK, (2 * cc + 1) * CHUNK)
            rows_q = slice((2 * cc + 1) * CHUNK, (2 * cc + 2) * CHUNK)
            rows_t = slice(2 * cc * CHUNK, (2 * cc + 2) * CHUNK)
            u_r[slot, rows, hcols[h]] = sol[n][:, :DN_HEAD]
            wq_r[slot, rows_w, hcols[h]] = sol[n][:, DN_HEAD:].astype(BF16)
            wq_r[slot, rows_q, hcols[h]] = qe[n]
            aqk_r[slot, rows, hcols_c[h]] = a_qk[n]
            kdt_r[slot, rows_t, hcols_c[h]] = kdt[n]
        yield

    def rec(g, slot):
        base = group_base(g)
        for cc in range(PREP_CHUNKS):
            rows = slice(cc * CHUNK, (cc + 1) * CHUNK)
            rows2 = slice(2 * cc * CHUNK, (2 * cc + 2) * CHUNK)
            trows = pl.ds(base + cc * CHUNK, CHUNK)
            st = [state_s[h] for h in heads]
            r = [jnp.dot(wq_r[slot, rows2, hcols[h]], st[h].astype(BF16),
                         preferred_element_type=F32) for h in heads]
            yield
            v_new = [(u_r[slot, rows, hcols[h]] - r[h][:CHUNK]).astype(BF16) for h in heads]
            upd = [jnp.dot(kdt_r[slot, rows2, hcols_c[h]], v_new[h], preferred_element_type=F32)
                   for h in heads]
            tail = gate_ref[pl.ds(base + (cc + 1) * CHUNK - 8, 8), :]
            decay_last = jnp.exp(tail[7:8, :])
            for h in heads:
                state_s[h] = st[h] * decay_last[:, DN_HEADS + h:DN_HEADS + h + 1] + upd[h]
            yield
            o = [r[h][CHUNK:] + jnp.dot(aqk_r[slot, rows, hcols_c[h]], v_new[h],
                                        preferred_element_type=F32) for h in heads]
            for h in heads:
                y = o[h] * lax.rsqrt(jnp.mean(o[h] * o[h], axis=-1, keepdims=True) + EPS) * gain
                gate = zg_ref[trows, hcols[h]].astype(F32)
                out_ref[trows, hcols[h]] = (y * gate).astype(out_ref.dtype)
            yield

    def run_interleaved(*gens):
        gens = list(gens)
        while gens:
            for gen in list(gens):
                if next(gen, StopIteration) is StopIteration:
                    gens.remove(gen)

    n_groups = n_chunks // PREP_CHUNKS
    state_s[...] = jnp.zeros_like(state_s)
    run_interleaved(prep(0, 0))

    def pair_body(j, carry):
        run_interleaved(rec(2 * j, 0), prep(2 * j + 1, 1))
        run_interleaved(rec(2 * j + 1, 1), prep(2 * j + 2, 0))
        return carry

    lax.fori_loop(0, (n_groups - 2) // 2, pair_body, 0)
    run_interleaved(rec(n_groups - 2, 0), prep(n_groups - 1, 1))
    run_interleaved(rec(n_groups - 1, 1))


def _gated_deltanet(proj, gates, out_gain):
    b, t, _ = proj.shape
    group_rows = PREP_CHUNKS * CHUNK
    assert t % (2 * group_rows) == 0 and t // group_rows >= 4
    ring = lambda rows, dtype: pltpu.VMEM((2, rows, DN_WIDTH), dtype)
    return pl.pallas_call(
        _gdn_kernel,
        grid=(b,),
        in_specs=[
            pl.BlockSpec((None, t, QKV_COLS), lambda i: (i, 0, 0)),
            pl.BlockSpec((None, t, DN_WIDTH), lambda i: (i, 0, QKV_COLS // DN_WIDTH)),
            pl.BlockSpec((None, t, N_GATES), lambda i: (i, 0, 0)),
            pl.BlockSpec((1, DN_HEAD), lambda i: (0, 0)),
        ],
        out_specs=pl.BlockSpec((None, t, DN_WIDTH), lambda i: (i, 0, 0)),
        out_shape=jax.ShapeDtypeStruct((b, t, DN_WIDTH), BF16),
        scratch_shapes=[
            ring(group_rows, F32),
            ring(2 * group_rows, BF16),
            ring(group_rows, BF16),
            ring(2 * group_rows, BF16),
            pltpu.VMEM((DN_HEADS, DN_HEAD, DN_HEAD), F32),
        ],
        compiler_params=pltpu.CompilerParams(
            dimension_semantics=("parallel",), vmem_limit_bytes=VMEM_LIMIT),
        name="gated_deltanet",
    )(proj, proj, gates, out_gain)


AT_BQ = 512
AT_BK = 256
AT_CW = 256
AT_NQ = 2 * AT_BQ
AT_CHAINS = AT_NQ // AT_CW
AT_CPM = AT_BQ // AT_CW
AT_VROWS = DF_HEAD_V + 16
AT_PROJ_DELAY = 2
AT_HPS = 2
OUT_TN = 256


def _attention_tiles(n_heads, n_qblocks):
    all_chains = tuple(range(AT_CHAINS))
    lo = tuple(c for c in all_chains if c % AT_CPM == 0)
    hi = tuple(c for c in all_chains if c % AT_CPM == 1)
    tiles = []
    for hd in range(n_heads):
        for qi in range(n_qblocks):
            for kt in range(AT_CPM * qi + AT_CPM):
                d = kt - AT_CPM * qi
                chains, masked = ((all_chains, ()) if d < 0 else
                                  (all_chains, lo) if d == 0 else (hi, hi))
                tiles.append(dict(hd=hd, qi=qi, qb=hd * n_qblocks + qi, kt=kt, chains=chains,
                                  masked=masked, first=kt == 0))
    return tiles


def _diff_attn_kernel(q_ref, k_ref, vt_ref, zg_ref, lq1_ref, lk1_ref, lq2_ref, lk2_ref,
                      og_ref, x_ref, dn_ref, wout_ref, out_ref,
                      qq_s, vta_s, m_s, acc_s, y_s, s_a, s_b, c_a, c_b, p_a, p_b):
    t = q_ref.shape[0]
    n_qblocks = t // AT_BQ
    all_chains = tuple(range(AT_CHAINS))

    @pl.when(pl.program_id(1) == 0)
    def _():
        out_ref[...] = x_ref[...]

    def lanes(c):
        return slice(c * AT_CW, (c + 1) * AT_CW)

    first = lax.broadcasted_iota(jnp.int32, (AT_BQ, LANES), 1) < DF_HEAD_QK
    zero = jnp.zeros((AT_BQ, LANES), BF16)
    hcols = [slice(hd * LANES, (hd + 1) * LANES) for hd in range(AT_HPS)]
    for hd in range(AT_HPS):
        for qi in range(n_qblocks):
            blk = q_ref[qi * AT_BQ:(qi + 1) * AT_BQ, hcols[hd]]
            qq_s[hd * n_qblocks + qi, :AT_BQ, :] = jnp.where(first, blk, zero)
            qq_s[hd * n_qblocks + qi, AT_BQ:, :] = jnp.where(first, zero, blk)
        vta_s[hd, :DF_HEAD_V, :] = vt_ref[hcols[hd], :]
        vta_s[hd, DF_HEAD_V:, :] = jnp.ones((AT_VROWS - DF_HEAD_V, t), BF16)

    lam = (jnp.exp(jnp.sum(lq1_ref[...] * lk1_ref[...], axis=-1, keepdims=True))
           - jnp.exp(jnp.sum(lq2_ref[...] * lk2_ref[...], axis=-1, keepdims=True))
           + LAMBDA_INIT)
    out_scale = og_ref[...] * (1.0 - LAMBDA_INIT)

    krow = lax.broadcasted_iota(jnp.int32, (AT_BK, AT_CW), 0)
    qcol = lax.broadcasted_iota(jnp.int32, (AT_BK, AT_CW), 1)
    causal = krow <= qcol
    bufs = ((s_a, c_a, p_a), (s_b, c_b, p_b))

    def scores(tile, c, buf):
        rows = slice(tile["kt"] * AT_BK, (tile["kt"] + 1) * AT_BK)
        s = _dot_nt(k_ref[rows, hcols[tile["hd"]]], qq_s[tile["qb"], lanes(c), :])
        if c in tile["masked"]:
            s = jnp.where(causal, s, -jnp.inf)
        buf[0][:, lanes(c)] = s
        buf[1][:, lanes(c)] = jnp.max(s, axis=0, keepdims=True)

    def values(tile, c, buf):
        rows = slice(tile["kt"] * AT_BK, (tile["kt"] + 1) * AT_BK)
        return jnp.dot(vta_s[tile["hd"], :, rows], buf[2][:, lanes(c)],
                       preferred_element_type=F32)

    def finish(tile):
        qb = tile["qb"]
        rows = slice(tile["qi"] * AT_BQ, (tile["qi"] + 1) * AT_BQ)
        inv = 1.0 / acc_s[qb, DF_HEAD_V:DF_HEAD_V + 1, :]
        o = (acc_s[qb, :DF_HEAD_V, :AT_BQ] * inv[:, :AT_BQ]
             - acc_s[qb, :DF_HEAD_V, AT_BQ:] * (lam * inv[:, AT_BQ:]))
        y = (o * lax.rsqrt(jnp.mean(o * o, axis=0, keepdims=True) + EPS)).T
        y = y * out_scale
        y_s[qb] = (y * zg_ref[rows, hcols[tile["hd"]]].astype(F32)).astype(BF16)

    def project(tile):
        rows = slice(tile["qi"] * AT_BQ, (tile["qi"] + 1) * AT_BQ)
        hd = tile["hd"]
        lhs = jnp.concatenate([y_s[tile["qb"]], dn_ref[rows, hcols[hd]]], axis=1)
        for nc in range(D_MODEL // OUT_TN):
            cols = slice(nc * OUT_TN, (nc + 1) * OUT_TN)
            out_ref[rows, cols] = out_ref[rows, cols] + jnp.dot(
                lhs, wout_ref[hd, :, cols], preferred_element_type=F32)

    tiles = _attention_tiles(AT_HPS, n_qblocks)
    due = {}
    for c in tiles[0]["chains"]:
        scores(tiles[0], c, bufs[0])
    for n, cur in enumerate(tiles):
        nxt = tiles[n + 1] if n + 1 < len(tiles) else None
        prv = tiles[n - 1] if n > 0 else None
        s_cur, c_cur, p_cur = bufs[n % 2]
        qb = cur["qb"]
        for c in all_chains:
            alpha = None
            if c in cur["chains"]:
                if cur["first"]:
                    m_new = c_cur[:, lanes(c)]
                else:
                    m_old = m_s[qb, :, lanes(c)]
                    m_new = jnp.maximum(m_old, c_cur[:, lanes(c)])
                    alpha = jnp.exp2(m_old - m_new)
                p = jnp.exp2(s_cur[:, lanes(c)] - m_new)
                m_s[qb, :, lanes(c)] = m_new
                p_cur[:, lanes(c)] = p.astype(BF16)
            if nxt is not None and c in nxt["chains"]:
                scores(nxt, c, bufs[(n + 1) % 2])
            if prv is not None and c in prv["chains"]:
                pv = values(prv, c, bufs[(n - 1) % 2])
                pq = prv["qb"]
                acc = pv if prv["first"] else acc_s[pq, :, lanes(c)] + pv
                if pq == qb and alpha is not None:
                    acc = alpha * acc
                acc_s[pq, :, lanes(c)] = acc
        if prv is not None and prv["qb"] != qb:
            finish(prv)
            due[n + AT_PROJ_DELAY] = prv
        if n in due:
            project(due.pop(n))
    last = tiles[-1]
    for c in last["chains"]:
        acc_s[last["qb"], :, lanes(c)] = (acc_s[last["qb"], :, lanes(c)]
                                          + values(last, c, bufs[(len(tiles) - 1) % 2]))
    finish(last)
    for tile in sorted(due.values(), key=lambda tl: tl["qb"]) + [last]:
        project(tile)


def _diff_attention(proj, vt, lq1, lk1, lq2, lk2, out_gain, x, mixed_dn, w_heads):
    b, t, _ = proj.shape
    assert AT_BQ == AT_CPM * AT_BK and AT_CW == AT_BK and AT_CPM == 2 and t % AT_BQ == 0
    assert DF_HEADS == DN_HEADS and DF_HEAD_V == DN_HEAD and D_MODEL % OUT_TN == 0
    assert DF_HEADS % AT_HPS == 0 and all(
        blk % AT_HPS == 0 for blk in (_BLK_DF_Q, _BLK_DF_K, _BLK_DF_Z))
    n_qblocks = t // AT_BQ
    n_blocks = AT_HPS * n_qblocks
    wide = AT_HPS * LANES
    small = lambda n: pl.BlockSpec((1, n), lambda i, h: (0, 0))
    return pl.pallas_call(
        _diff_attn_kernel,
        grid=(b, DF_HEADS // AT_HPS),
        in_specs=[
            pl.BlockSpec((None, t, wide), lambda i, h: (i, 0, _BLK_DF_Q // AT_HPS + h)),
            pl.BlockSpec((None, t, wide), lambda i, h: (i, 0, _BLK_DF_K // AT_HPS + h)),
            pl.BlockSpec((None, wide, t), lambda i, h: (i, h, 0)),
            pl.BlockSpec((None, t, wide), lambda i, h: (i, 0, _BLK_DF_Z // AT_HPS + h)),
            small(DF_HEAD_QK), small(DF_HEAD_QK), small(DF_HEAD_QK), small(DF_HEAD_QK),
            small(DF_HEAD_V),
            pl.BlockSpec((None, t, D_MODEL), lambda i, h: (i, 0, 0)),
            pl.BlockSpec((None, t, wide), lambda i, h: (i, 0, h)),
            pl.BlockSpec((AT_HPS, DF_HEAD_V + DN_HEAD, D_MODEL), lambda i, h: (h, 0, 0)),
        ],
        out_specs=pl.BlockSpec((None, t, D_MODEL), lambda i, h: (i, 0, 0)),
        out_shape=jax.ShapeDtypeStruct((b, t, D_MODEL), F32),
        scratch_shapes=[
            pltpu.VMEM((n_blocks, AT_NQ, LANES), BF16),
            pltpu.VMEM((AT_HPS, AT_VROWS, t), BF16),
            pltpu.VMEM((n_blocks, 1, AT_NQ), F32),
            pltpu.VMEM((n_blocks, AT_VROWS, AT_NQ), F32),
            pltpu.VMEM((n_blocks, AT_BQ, DF_HEAD_V), BF16),
            pltpu.VMEM((AT_BK, AT_NQ), F32),
            pltpu.VMEM((AT_BK, AT_NQ), F32),
            pltpu.VMEM((1, AT_NQ), F32),
            pltpu.VMEM((1, AT_NQ), F32),
            pltpu.VMEM((AT_BK, AT_NQ), BF16),
            pltpu.VMEM((AT_BK, AT_NQ), BF16),
        ],
        compiler_params=pltpu.CompilerParams(
            dimension_semantics=("parallel", "arbitrary"), vmem_limit_bytes=VMEM_LIMIT),
        name="diff_attention",
    )(proj, proj, vt, proj, lq1, lk1, lq2, lk2, out_gain, x, mixed_dn, w_heads)


def kernel(x, norm_gain, w_in, conv_w, a_log, dt_bias, dn_out_gain, q_gain, k_gain,
           lambda_q1, lambda_k1, lambda_q2, lambda_k2, df_out_gain, w_out):
    b, t, d = x.shape
    assert d == D_MODEL and norm_gain.shape[0] == 1
    x2d = x.reshape(b * t, d)

    w = w_in[0]
    gate_lo = 4 * DN_WIDTH
    df_lo = gate_lo + N_GATES
    w_main = jnp.concatenate([w[:, :gate_lo], w[:, df_lo:df_lo + 2 * DF_WIDTH],
                              w[:, df_lo + 3 * DF_WIDTH:]], axis=1).astype(BF16)
    w_t = jnp.concatenate([w[:, df_lo + 2 * DF_WIDTH:df_lo + 3 * DF_WIDTH],
                           jnp.pad(w[:, gate_lo:df_lo], ((0, 0), (0, GATE_ROWS - N_GATES)))],
                          axis=1).astype(BF16).T
    half = jnp.arange(SEG_W, dtype=jnp.int32) // DF_HEAD_QK
    seg = ((half[:, None] == half[None, :]).astype(F32) / DF_HEAD_QK).astype(BF16)
    reps = DF_WIDTH // DF_HEAD_QK
    qk_gain = jnp.concatenate([jnp.tile(q_gain, (1, reps)) * (DF_HEAD_QK ** -0.5 * LOG2_E),
                               jnp.tile(k_gain, (1, reps))], axis=0)

    pad_rows = lambda v: jnp.pad(v.reshape(DN_HEADS, 1),
                                 ((DN_HEADS, GATE_ROWS - 2 * DN_HEADS), (0, 0)))
    proj, vt, gates_t = _in_projection(x2d, t, norm_gain, w_main, w_t, seg, qk_gain,
                                       pad_rows(a_log), pad_rows(dt_bias), conv_w[0])
    proj = proj.reshape(b, t, PROJ_COLS)
    gates = jnp.swapaxes(gates_t[:, :N_GATES, :], 1, 2)

    mixed_dn = _gated_deltanet(proj, gates, dn_out_gain)

    wo = w_out[0].astype(BF16)
    w_heads = jnp.concatenate([wo[DN_WIDTH:].reshape(DF_HEADS, DF_HEAD_V, d),
                               wo[:DN_WIDTH].reshape(DN_HEADS, DN_HEAD, d)], axis=1)
    return _diff_attention(proj, vt, lambda_q1, lambda_k1, lambda_q2, lambda_k2,
                           df_out_gain, x, mixed_dn, w_heads)
```

```python
import functools
import math

import jax
import jax.numpy as jnp
from jax import lax
from jax.experimental import pallas as pl
from jax.experimental.pallas import tpu as pltpu

F32 = jnp.float32
BF16 = jnp.bfloat16

D_MODEL = 1024
DN_HEADS = 4
DN_HEAD = 128
DN_WIDTH = DN_HEADS * DN_HEAD
CONV_WIDTH = 4
CHUNK = 64
DF_HEADS = 4
DF_HEAD_QK = 64
DF_HEAD_V = 128
DF_WIDTH = DF_HEADS * DF_HEAD_V
EPS = 1e-6
LAMBDA_INIT = 0.8 - 0.6 * math.exp(-0.3 * 0)
LOG2_E = math.log2(math.e)

N_GATES = 2 * DN_HEADS
GATE_ROWS = 16
PROJ_COLS = 4 * DN_WIDTH + 3 * DF_WIDTH
LANES = 128
SUBLANES = 8
QKV_COLS = 3 * DN_WIDTH
VMEM_LIMIT = 56 * 1024 * 1024

_BLK_DN_Z = 3 * DN_HEADS
_BLK_DF_Q = 4 * DN_HEADS
_BLK_DF_K = _BLK_DF_Q + DF_HEADS
_BLK_DF_Z = _BLK_DF_K + DF_HEADS


def _silu(x):
    half = 0.5 * x
    return half + half * jnp.tanh(half)


def _dot(a, b):
    return jnp.dot(a.astype(BF16), b.astype(BF16), preferred_element_type=F32)


def _dot_nt(a, b):
    return lax.dot_general(a.astype(BF16), b.astype(BF16), (((1,), (1,)), ((), ())),
                           preferred_element_type=F32)


IN_TM = 1024
IN_TN = 512
SEG_W = 256
CONV_ROWS = 512


def _inproj_kernel(x_ref, gain_ref, w_ref, wt_ref, seg_ref, qkg_ref, alog_ref, dtb_ref,
                   convw_ref, proj_ref, vt_ref, gate_ref, h_s, cstage_s, tail_s, *, tiles_per_seq):
    @pl.when(pl.program_id(0) % tiles_per_seq == 0)
    def _():
        tail_s[...] = jnp.zeros_like(tail_s)

    def stage(y, j):
        for hd in range(DN_HEADS):
            yh = y[:, hd * DN_HEAD:(hd + 1) * DN_HEAD]
            cstage_s[j * DN_HEADS + hd, :SUBLANES, :] = tail_s[j * DN_HEADS + hd]
            cstage_s[j * DN_HEADS + hd, SUBLANES:, :] = yh
            tail_s[j * DN_HEADS + hd] = yh[IN_TM - SUBLANES:, :]

    def conv_silu_norm(j, part):
        r0 = SUBLANES + part * CONV_ROWS
        for hd in range(DN_HEADS):
            cols = slice(j * IN_TN + hd * DN_HEAD, j * IN_TN + (hd + 1) * DN_HEAD)
            buf = cstage_s.at[j * DN_HEADS + hd]
            w = convw_ref[:, cols]
            acc = buf[r0:r0 + CONV_ROWS, :] * w[CONV_WIDTH - 1:CONV_WIDTH]
            for d in range(1, CONV_WIDTH):
                acc = acc + buf[r0 - d:r0 - d + CONV_ROWS, :] * w[CONV_WIDTH - 1 - d:CONV_WIDTH - d]
            a = _silu(acc)
            if j < 2:
                inv_norm = lax.rsqrt(jnp.sum(a * a, axis=-1, keepdims=True) + EPS)
                a = a * (inv_norm * (DN_HEAD ** -0.5) if j == 0 else inv_norm)
            proj_ref[part * CONV_ROWS:(part + 1) * CONV_ROWS, cols] = a.astype(BF16)

    def head_phase():
        x = x_ref[...]
        ms = jnp.mean(x * x, axis=-1, keepdims=True)
        h_s[...] = (x * lax.rsqrt(ms + EPS) * gain_ref[...]).astype(BF16)

    def feature_major_phase(conv_items):
        yt = _dot_nt(wt_ref[...], h_s[...])
        for item in conv_items:
            conv_silu_norm(*item)
        vt_ref[...] = yt[:DF_WIDTH].astype(BF16)
        pre = yt[DF_WIDTH:]
        g = -jnp.exp(alog_ref[...]) * jax.nn.softplus(pre + dtb_ref[...])
        pos = lax.broadcasted_iota(jnp.int32, g.shape, 1) % CHUNK
        s = 1
        while s < CHUNK:
            g = g + jnp.where(pos >= s, pltpu.roll(g, s, 1), 0.0)
            s *= 2
        row = lax.broadcasted_iota(jnp.int32, g.shape, 0)
        gate_ref[...] = jnp.where(row < DN_HEADS, jax.nn.sigmoid(pre), g)

    def chunk_phase(j, conv_items):
        cols = slice(j * IN_TN, (j + 1) * IN_TN)
        y = jnp.dot(h_s[...], w_ref[:, cols], preferred_element_type=F32)
        for item in conv_items:
            conv_silu_norm(*item)
        if j < QKV_COLS // IN_TN:
            stage(y, j)
            return
        if j in (_BLK_DF_Q * LANES // IN_TN, _BLK_DF_K * LANES // IN_TN):
            which = 0 if j == _BLK_DF_Q * LANES // IN_TN else 1
            sq = (y * y).astype(BF16)
            msq = jnp.concatenate(
                [jnp.dot(sq[:, i * SEG_W:(i + 1) * SEG_W], seg_ref[...],
                         preferred_element_type=F32) for i in range(IN_TN // SEG_W)], axis=1)
            y = y * lax.rsqrt(msq + EPS) * qkg_ref[which:which + 1, :]
        elif j in (_BLK_DN_Z * LANES // IN_TN, _BLK_DF_Z * LANES // IN_TN):
            y = _silu(y)
        proj_ref[:, cols] = y.astype(BF16)

    head_phase()
    n_conv = QKV_COLS // IN_TN
    pending = []
    for j in range(n_conv):
        chunk_phase(j, [pending.pop(0)] if pending else [])
        pending += [(j, part) for part in range(IN_TM // CONV_ROWS)]
    light = [j for j in range(n_conv, PROJ_COLS // IN_TN)
             if j not in (_BLK_DF_Q * LANES // IN_TN, _BLK_DF_K * LANES // IN_TN)]

    def take(phases_left):
        return [pending.pop(0) for _ in range(-(-len(pending) // phases_left))]

    feature_major_phase(take(len(light) + 1))
    for j in range(n_conv, PROJ_COLS // IN_TN):
        chunk_phase(j, take(sum(k >= j for k in light)) if j in light else [])
    assert not pending


def _in_projection(x2d, t, gain, w_main, w_t, seg, qk_gain, alog_col, dtb_col, conv_w):
    m = x2d.shape[0]
    assert DF_WIDTH == IN_TN and DN_WIDTH == IN_TN and t % IN_TM == 0 and IN_TM % CHUNK == 0
    tiles_per_seq = t // IN_TM
    const = lambda shape: pl.BlockSpec(shape, lambda i: (0, 0))
    return pl.pallas_call(
        functools.partial(_inproj_kernel, tiles_per_seq=tiles_per_seq),
        grid=(m // IN_TM,),
        in_specs=[
            pl.BlockSpec((IN_TM, D_MODEL), lambda i: (i, 0)),
            const((1, D_MODEL)),
            const((D_MODEL, PROJ_COLS)),
            const((DF_WIDTH + GATE_ROWS, D_MODEL)),
            const((SEG_W, SEG_W)),
            const((2, DF_WIDTH)),
            const((GATE_ROWS, 1)),
            const((GATE_ROWS, 1)),
            const((CONV_WIDTH, QKV_COLS)),
        ],
        out_specs=[
            pl.BlockSpec((IN_TM, PROJ_COLS), lambda i: (i, 0)),
            pl.BlockSpec((None, DF_WIDTH, IN_TM),
                         lambda i: (i // tiles_per_seq, 0, i % tiles_per_seq)),
            pl.BlockSpec((None, GATE_ROWS, IN_TM),
                         lambda i: (i // tiles_per_seq, 0, i % tiles_per_seq)),
        ],
        out_shape=[
            jax.ShapeDtypeStruct((m, PROJ_COLS), BF16),
            jax.ShapeDtypeStruct((m // t, DF_WIDTH, t), BF16),
            jax.ShapeDtypeStruct((m // t, GATE_ROWS, t), F32),
        ],
        scratch_shapes=[
            pltpu.VMEM((IN_TM, D_MODEL), BF16),
            pltpu.VMEM((QKV_COLS // LANES, IN_TM + SUBLANES, LANES), F32),
            pltpu.VMEM((QKV_COLS // LANES, SUBLANES, LANES), F32),
        ],
        compiler_params=pltpu.CompilerParams(
            dimension_semantics=("arbitrary",), vmem_limit_bytes=VMEM_LIMIT),
        name="in_projection",
    )(x2d, gain, w_main, w_t, seg, qk_gain, alog_col, dtb_col, conv_w)


PREP_CHUNKS = 4


def _gdn_kernel(qkv_ref, zg_ref, gate_ref, gain_ref,
                out_ref, u_r, wq_r, aqk_r, kdt_r, state_s):
    t = qkv_ref.shape[0]
    n_chunks = t // CHUNK
    group_rows = PREP_CHUNKS * CHUNK

    def group_base(g):
        return g * group_rows if isinstance(g, int) else pl.multiple_of(g * group_rows, group_rows)

    row = lax.broadcasted_iota(jnp.int32, (CHUNK, 2 * CHUNK), 0)
    col = lax.broadcasted_iota(jnp.int32, (CHUNK, 2 * CHUNK), 1)
    eye_f = (row == col).astype(F32)
    lower_f = (row >= col).astype(F32)
    strict_f = (row > col).astype(F32)
    upper_f = (col >= CHUNK).astype(F32)
    eye_hi = (col - CHUNK == row).astype(F32)
    zero_rows = jnp.zeros((CHUNK, DN_HEAD), BF16)

    items = [(cc, h) for cc in range(PREP_CHUNKS) for h in range(DN_HEADS)]
    heads = range(DN_HEADS)
    gain = gain_ref[...]
    hcols = [slice(h * LANES, (h + 1) * LANES) for h in heads]
    hcols_c = [slice(h * LANES, h * LANES + CHUNK) for h in heads]

    def halves(fn, seq):
        out = []
        mid = len(seq) // 2
        for part in (seq[:mid], seq[mid:]):
            out.extend(fn(x) for x in part)
            yield out

    def prep(g, slot):
        base = group_base(g)
        gates = [gate_ref[pl.ds(base + cc * CHUNK, CHUNK), :] for cc in range(PREP_CHUNKS)]
        ks, lhs, decay, rhs, qe, kdt = [], [], [], [], [], []
        for cc, h in items:
            trows = pl.ds(base + cc * CHUNK, CHUNK)
            q = qkv_ref[trows, hcols[h]].astype(F32)
            k = qkv_ref[trows, slice(DN_WIDTH + h * LANES, DN_WIDTH + (h + 1) * LANES)].astype(F32)
            v = qkv_ref[trows, slice(2 * DN_WIDTH + h * LANES,
                                     2 * DN_WIDTH + (h + 1) * LANES)].astype(F32)
            beta_b = jnp.broadcast_to(gates[cc][:, h:h + 1], (CHUNK, DN_HEAD))
            gcb = jnp.broadcast_to(gates[cc][:, DN_HEADS + h:DN_HEADS + h + 1], (CHUNK, DN_HEAD))
            gr = jnp.sum(gcb * eye_f, axis=0, keepdims=True)
            decay.append(jnp.exp(jnp.minimum(gcb - gr, 0.0)))
            g_last = gcb[CHUNK - 1:CHUNK, :]
            eg = jnp.exp(gcb)
            kb = k * beta_b
            ks.append(jnp.concatenate([k.astype(BF16), zero_rows], axis=0))
            lhs.append(jnp.concatenate([kb, q], axis=0).astype(BF16))
            rhs.append(jnp.concatenate([v * beta_b, kb * eg], axis=1).astype(BF16))
            qe.append((q * eg).astype(BF16))
            kdt.append((k * jnp.exp(g_last - gcb)).T.astype(BF16))
        yield

        kq = None
        for kq in halves(lambda ab: _dot_nt(*ab), list(zip(lhs, ks))):
            yield
        m = [x[:CHUNK] * (d * strict_f) for x, d in zip(kq, decay)]
        a_qk = [(x[CHUNK:, :CHUNK] * (d * lower_f)[:, :CHUNK]).astype(BF16)
                for x, d in zip(kq, decay)]

        w = None
        for w in halves(lambda x: _dot(x[:, :CHUNK], x - eye_hi) + eye_hi, m):
            yield
        for _ in range(5):
            prev = w
            for w in halves(lambda x: _dot(x[:, :CHUNK], x) + x * upper_f, prev):
                yield
        sol = None
        for sol in halves(lambda xb: _dot(xb[0][:, CHUNK:], xb[1]), list(zip(w, rhs))):
            yield

        for n, (cc, h) in enumerate(items):
            rows = slice(cc * CHUNK, (cc + 1) * CHUNK)
            rows_w = slice(2 * cc * CHUNK, (2 * cc + 1) * CHUNK)
            rows_q = slice((2 * cc + 1) * CHUNK, (2 * cc + 2) * CHUNK)
            rows_t = slice(2 * cc * CHUNK, (2 * cc + 2) * CHUNK)
            u_r[slot, rows, hcols[h]] = sol[n][:, :DN_HEAD]
            wq_r[slot, rows_w, hcols[h]] = sol[n][:, DN_HEAD:].astype(BF16)
            wq_r[slot, rows_q, hcols[h]] = qe[n]
            aqk_r[slot, rows, hcols_c[h]] = a_qk[n]
            kdt_r[slot, rows_t, hcols_c[h]] = kdt[n]
        yield

    def rec(g, slot):
        base = group_base(g)
        for cc in range(PREP_CHUNKS):
            rows = slice(cc * CHUNK, (cc + 1) * CHUNK)
            rows2 = slice(2 * cc * CHUNK, (2 * cc + 2) * CHUNK)
            trows = pl.ds(base + cc * CHUNK, CHUNK)
            st = [state_s[h] for h in heads]
            r = [jnp.dot(wq_r[slot, rows2, hcols[h]], st[h].astype(BF16),
                         preferred_element_type=F32) for h in heads]
            yield
            v_new = [(u_r[slot, rows, hcols[h]] - r[h][:CHUNK]).astype(BF16) for h in heads]
            upd = [jnp.dot(kdt_r[slot, rows2, hcols_c[h]], v_new[h], preferred_element_type=F32)
                   for h in heads]
            tail = gate_ref[pl.ds(base + (cc + 1) * CHUNK - 8, 8), :]
            decay_last = jnp.exp(tail[7:8, :])
            for h in heads:
                state_s[h] = st[h] * decay_last[:, DN_HEADS + h:DN_HEADS + h + 1] + upd[h]
            yield
            o = [r[h][CHUNK:] + jnp.dot(aqk_r[slot, rows, hcols_c[h]], v_new[h],
                                        preferred_element_type=F32) for h in heads]
            for h in heads:
                y = o[h] * lax.rsqrt(jnp.mean(o[h] * o[h], axis=-1, keepdims=True) + EPS) * gain
                gate = zg_ref[trows, hcols[h]].astype(F32)
                out_ref[trows, hcols[h]] = (y * gate).astype(out_ref.dtype)
            yield

    def run_interleaved(*gens):
        gens = list(gens)
        while gens:
            for gen in list(gens):
                if next(gen, StopIteration) is StopIteration:
                    gens.remove(gen)

    n_groups = n_chunks // PREP_CHUNKS
    state_s[...] = jnp.zeros_like(state_s)
    run_interleaved(prep(0, 0))

    def pair_body(j, carry):
        run_interleaved(rec(2 * j, 0), prep(2 * j + 1, 1))
        run_interleaved(rec(2 * j + 1, 1), prep(2 * j + 2, 0))
        return carry

    lax.fori_loop(0, (n_groups - 2) // 2, pair_body, 0)
    run_interleaved(rec(n_groups - 2, 0), prep(n_groups - 1, 1))
    run_interleaved(rec(n_groups - 1, 1))


def _gated_deltanet(proj, gates, out_gain):
    b, t, _ = proj.shape
    group_rows = PREP_CHUNKS * CHUNK
    assert t % (2 * group_rows) == 0 and t // group_rows >= 4
    ring = lambda rows, dtype: pltpu.VMEM((2, rows, DN_WIDTH), dtype)
    return pl.pallas_call(
        _gdn_kernel,
        grid=(b,),
        in_specs=[
            pl.BlockSpec((None, t, QKV_COLS), lambda i: (i, 0, 0)),
            pl.BlockSpec((None, t, DN_WIDTH), lambda i: (i, 0, QKV_COLS // DN_WIDTH)),
            pl.BlockSpec((None, t, N_GATES), lambda i: (i, 0, 0)),
            pl.BlockSpec((1, DN_HEAD), lambda i: (0, 0)),
        ],
        out_specs=pl.BlockSpec((None, t, DN_WIDTH), lambda i: (i, 0, 0)),
        out_shape=jax.ShapeDtypeStruct((b, t, DN_WIDTH), BF16),
        scratch_shapes=[
            ring(group_rows, F32),
            ring(2 * group_rows, BF16),
            ring(group_rows, BF16),
            ring(2 * group_rows, BF16),
            pltpu.VMEM((DN_HEADS, DN_HEAD, DN_HEAD), F32),
        ],
        compiler_params=pltpu.CompilerParams(
            dimension_semantics=("parallel",), vmem_limit_bytes=VMEM_LIMIT),
        name="gated_deltanet",
    )(proj, proj, gates, out_gain)


AT_BQ = 512
AT_BK = 256
AT_CW = 256
AT_NQ = 2 * AT_BQ
AT_CHAINS = AT_NQ // AT_CW
AT_CPM = AT_BQ // AT_CW
AT_VROWS = DF_HEAD_V + 16
AT_PROJ_DELAY = 2
AT_HPS = 2
OUT_TN = 256


def _attention_tiles(n_heads, n_qblocks):
    all_chains = tuple(range(AT_CHAINS))
    lo = tuple(c for c in all_chains if c % AT_CPM == 0)
    hi = tuple(c for c in all_chains if c % AT_CPM == 1)
    tiles = []
    for hd in range(n_heads):
        for qi in range(n_qblocks):
            for kt in range(AT_CPM * qi + AT_CPM):
                d = kt - AT_CPM * qi
                chains, masked = ((all_chains, ()) if d < 0 else
                                  (all_chains, lo) if d == 0 else (hi, hi))
                tiles.append(dict(hd=hd, qi=qi, qb=hd * n_qblocks + qi, kt=kt, chains=chains,
                                  masked=masked, first=kt == 0))
    return tiles


def _diff_attn_kernel(q_ref, k_ref, vt_ref, zg_ref, lq1_ref, lk1_ref, lq2_ref, lk2_ref,
                      og_ref, x_ref, dn_ref, wout_ref, out_ref,
                      qq_s, vta_s, m_s, acc_s, y_s, s_a, s_b, c_a, c_b, p_a, p_b):
    t = q_ref.shape[0]
    n_qblocks = t // AT_BQ
    all_chains = tuple(range(AT_CHAINS))

    @pl.when(pl.program_id(1) == 0)
    def _():
        out_ref[...] = x_ref[...]

    def lanes(c):
        return slice(c * AT_CW, (c + 1) * AT_CW)

    first = lax.broadcasted_iota(jnp.int32, (AT_BQ, LANES), 1) < DF_HEAD_QK
    zero = jnp.zeros((AT_BQ, LANES), BF16)
    hcols = [slice(hd * LANES, (hd + 1) * LANES) for hd in range(AT_HPS)]
    for hd in range(AT_HPS):
        for qi in range(n_qblocks):
            blk = q_ref[qi * AT_BQ:(qi + 1) * AT_BQ, hcols[hd]]
            qq_s[hd * n_qblocks + qi, :AT_BQ, :] = jnp.where(first, blk, zero)
            qq_s[hd * n_qblocks + qi, AT_BQ:, :] = jnp.where(first, zero, blk)
        vta_s[hd, :DF_HEAD_V, :] = vt_ref[hcols[hd], :]
        vta_s[hd, DF_HEAD_V:, :] = jnp.ones((AT_VROWS - DF_HEAD_V, t), BF16)

    lam = (jnp.exp(jnp.sum(lq1_ref[...] * lk1_ref[...], axis=-1, keepdims=True))
           - jnp.exp(jnp.sum(lq2_ref[...] * lk2_ref[...], axis=-1, keepdims=True))
           + LAMBDA_INIT)
    out_scale = og_ref[...] * (1.0 - LAMBDA_INIT)

    krow = lax.broadcasted_iota(jnp.int32, (AT_BK, AT_CW), 0)
    qcol = lax.broadcasted_iota(jnp.int32, (AT_BK, AT_CW), 1)
    causal = krow <= qcol
    bufs = ((s_a, c_a, p_a), (s_b, c_b, p_b))

    def scores(tile, c, buf):
        rows = slice(tile["kt"] * AT_BK, (tile["kt"] + 1) * AT_BK)
        s = _dot_nt(k_ref[rows, hcols[tile["hd"]]], qq_s[tile["qb"], lanes(c), :])
        if c in tile["masked"]:
            s = jnp.where(causal, s, -jnp.inf)
        buf[0][:, lanes(c)] = s
        buf[1][:, lanes(c)] = jnp.max(s, axis=0, keepdims=True)

    def values(tile, c, buf):
        rows = slice(tile["kt"] * AT_BK, (tile["kt"] + 1) * AT_BK)
        return jnp.dot(vta_s[tile["hd"], :, rows], buf[2][:, lanes(c)],
                       preferred_element_type=F32)

    def finish(tile):
        qb = tile["qb"]
        rows = slice(tile["qi"] * AT_BQ, (tile["qi"] + 1) * AT_BQ)
        inv = 1.0 / acc_s[qb, DF_HEAD_V:DF_HEAD_V + 1, :]
        o = (acc_s[qb, :DF_HEAD_V, :AT_BQ] * inv[:, :AT_BQ]
             - acc_s[qb, :DF_HEAD_V, AT_BQ:] * (lam * inv[:, AT_BQ:]))
        y = (o * lax.rsqrt(jnp.mean(o * o, axis=0, keepdims=True) + EPS)).T
        y = y * out_scale
        y_s[qb] = (y * zg_ref[rows, hcols[tile["hd"]]].astype(F32)).astype(BF16)

    def project(tile):
        rows = slice(tile["qi"] * AT_BQ, (tile["qi"] + 1) * AT_BQ)
        hd = tile["hd"]
        lhs = jnp.concatenate([y_s[tile["qb"]], dn_ref[rows, hcols[hd]]], axis=1)
        for nc in range(D_MODEL // OUT_TN):
            cols = slice(nc * OUT_TN, (nc + 1) * OUT_TN)
            out_ref[rows, cols] = out_ref[rows, cols] + jnp.dot(
                lhs, wout_ref[hd, :, cols], preferred_element_type=F32)

    tiles = _attention_tiles(AT_HPS, n_qblocks)
    due = {}
    for c in tiles[0]["chains"]:
        scores(tiles[0], c, bufs[0])
    for n, cur in enumerate(tiles):
        nxt = tiles[n + 1] if n + 1 < len(tiles) else None
        prv = tiles[n - 1] if n > 0 else None
        s_cur, c_cur, p_cur = bufs[n % 2]
        qb = cur["qb"]
        for c in all_chains:
            alpha = None
            if c in cur["chains"]:
                if cur["first"]:
                    m_new = c_cur[:, lanes(c)]
                else:
                    m_old = m_s[qb, :, lanes(c)]
                    m_new = jnp.maximum(m_old, c_cur[:, lanes(c)])
                    alpha = jnp.exp2(m_old - m_new)
                p = jnp.exp2(s_cur[:, lanes(c)] - m_new)
                m_s[qb, :, lanes(c)] = m_new
                p_cur[:, lanes(c)] = p.astype(BF16)
            if nxt is not None and c in nxt["chains"]:
                scores(nxt, c, bufs[(n + 1) % 2])
            if prv is not None and c in prv["chains"]:
                pv = values(prv, c, bufs[(n - 1) % 2])
                pq = prv["qb"]
                acc = pv if prv["first"] else acc_s[pq, :, lanes(c)] + pv
                if pq == qb and alpha is not None:
                    acc = alpha * acc
                acc_s[pq, :, lanes(c)] = acc
        if prv is not None and prv["qb"] != qb:
            finish(prv)
            due[n + AT_PROJ_DELAY] = prv
        if n in due:
            project(due.pop(n))
    last = tiles[-1]
    for c in last["chains"]:
        acc_s[last["qb"], :, lanes(c)] = (acc_s[last["qb"], :, lanes(c)]
                                          + values(last, c, bufs[(len(tiles) - 1) % 2]))
    finish(last)
    for tile in sorted(due.values(), key=lambda tl: tl["qb"]) + [last]:
        project(tile)


def _diff_attention(proj, vt, lq1, lk1, lq2, lk2, out_gain, x, mixed_dn, w_heads):
    b, t, _ = proj.shape
    assert AT_BQ == AT_CPM * AT_BK and AT_CW == AT_BK and AT_CPM == 2 and t % AT_BQ == 0
    assert DF_HEADS == DN_HEADS and DF_HEAD_V == DN_HEAD and D_MODEL % OUT_TN == 0
    assert DF_HEADS % AT_HPS == 0 and all(
        blk % AT_HPS == 0 for blk in (_BLK_DF_Q, _BLK_DF_K, _BLK_DF_Z))
    n_qblocks = t // AT_BQ
    n_blocks = AT_HPS * n_qblocks
    wide = AT_HPS * LANES
    small = lambda n: pl.BlockSpec((1, n), lambda i, h: (0, 0))
    return pl.pallas_call(
        _diff_attn_kernel,
        grid=(b, DF_HEADS // AT_HPS),
        in_specs=[
            pl.BlockSpec((None, t, wide), lambda i, h: (i, 0, _BLK_DF_Q // AT_HPS + h)),
            pl.BlockSpec((None, t, wide), lambda i, h: (i, 0, _BLK_DF_K // AT_HPS + h)),
            pl.BlockSpec((None, wide, t), lambda i, h: (i, h, 0)),
            pl.BlockSpec((None, t, wide), lambda i, h: (i, 0, _BLK_DF_Z // AT_HPS + h)),
            small(DF_HEAD_QK), small(DF_HEAD_QK), small(DF_HEAD_QK), small(DF_HEAD_QK),
            small(DF_HEAD_V),
            pl.BlockSpec((None, t, D_MODEL), lambda i, h: (i, 0, 0)),
            pl.BlockSpec((None, t, wide), lambda i, h: (i, 0, h)),
            pl.BlockSpec((AT_HPS, DF_HEAD_V + DN_HEAD, D_MODEL), lambda i, h: (h, 0, 0)),
        ],
        out_specs=pl.BlockSpec((None, t, D_MODEL), lambda i, h: (i, 0, 0)),
        out_shape=jax.ShapeDtypeStruct((b, t, D_MODEL), F32),
        scratch_shapes=[
            pltpu.VMEM((n_blocks, AT_NQ, LANES), BF16),
            pltpu.VMEM((AT_HPS, AT_VROWS, t), BF16),
            pltpu.VMEM((n_blocks, 1, AT_NQ), F32),
            pltpu.VMEM((n_blocks, AT_VROWS, AT_NQ), F32),
            pltpu.VMEM((n_blocks, AT_BQ, DF_HEAD_V), BF16),
            pltpu.VMEM((AT_BK, AT_NQ), F32),
            pltpu.VMEM((AT_BK, AT_NQ), F32),
            pltpu.VMEM((1, AT_NQ), F32),
            pltpu.VMEM((1, AT_NQ), F32),
            pltpu.VMEM((AT_BK, AT_NQ), BF16),
            pltpu.VMEM((AT_BK, AT_NQ), BF16),
        ],
        compiler_params=pltpu.CompilerParams(
            dimension_semantics=("parallel", "arbitrary"), vmem_limit_bytes=VMEM_LIMIT),
        name="diff_attention",
    )(proj, proj, vt, proj, lq1, lk1, lq2, lk2, out_gain, x, mixed_dn, w_heads)


def kernel(x, norm_gain, w_in, conv_w, a_log, dt_bias, dn_out_gain, q_gain, k_gain,
           lambda_q1, lambda_k1, lambda_q2, lambda_k2, df_out_gain, w_out):
    b, t, d = x.shape
    assert d == D_MODEL and norm_gain.shape[0] == 1
    x2d = x.reshape(b * t, d)

    w = w_in[0]
    gate_lo = 4 * DN_WIDTH
    df_lo = gate_lo + N_GATES
    w_main = jnp.concatenate([w[:, :gate_lo], w[:, df_lo:df_lo + 2 * DF_WIDTH],
                              w[:, df_lo + 3 * DF_WIDTH:]], axis=1).astype(BF16)
    w_t = jnp.concatenate([w[:, df_lo + 2 * DF_WIDTH:df_lo + 3 * DF_WIDTH],
                           jnp.pad(w[:, gate_lo:df_lo], ((0, 0), (0, GATE_ROWS - N_GATES)))],
                          axis=1).astype(BF16).T
    half = jnp.arange(SEG_W, dtype=jnp.int32) // DF_HEAD_QK
    seg = ((half[:, None] == half[None, :]).astype(F32) / DF_HEAD_QK).astype(BF16)
    reps = DF_WIDTH // DF_HEAD_QK
    qk_gain = jnp.concatenate([jnp.tile(q_gain, (1, reps)) * (DF_HEAD_QK ** -0.5 * LOG2_E),
                               jnp.tile(k_gain, (1, reps))], axis=0)

    pad_rows = lambda v: jnp.pad(v.reshape(DN_HEADS, 1),
                                 ((DN_HEADS, GATE_ROWS - 2 * DN_HEADS), (0, 0)))
    proj, vt, gates_t = _in_projection(x2d, t, norm_gain, w_main, w_t, seg, qk_gain,
                                       pad_rows(a_log), pad_rows(dt_bias), conv_w[0])
    proj = proj.reshape(b, t, PROJ_COLS)
    gates = jnp.swapaxes(gates_t[:, :N_GATES, :], 1, 2)

    mixed_dn = _gated_deltanet(proj, gates, dn_out_gain)

    wo = w_out[0].astype(BF16)
    w_heads = jnp.concatenate([wo[DN_WIDTH:].reshape(DF_HEADS, DF_HEAD_V, d),
                               wo[:DN_WIDTH].reshape(DN_HEADS, DN_HEAD, d)], axis=1)
    return _diff_attention(proj, vt, lambda_q1, lambda_k1, lambda_q2, lambda_k2,
                           df_out_gain, x, mixed_dn, w_heads)
```

```python
import functools
import math

import jax
import jax.numpy as jnp
from jax import lax
from jax.experimental import pallas as pl
from jax.experimental.pallas import tpu as pltpu

F32 = jnp.float32
BF16 = jnp.bfloat16

D_MODEL = 1024
DN_HEADS = 4
DN_HEAD = 128
DN_WIDTH = DN_HEADS * DN_HEAD
CONV_WIDTH = 4
CHUNK = 64
DF_HEADS = 4
DF_HEAD_QK = 64
DF_HEAD_V = 128
DF_WIDTH = DF_HEADS * DF_HEAD_V
EPS = 1e-6
LAMBDA_INIT = 0.8 - 0.6 * math.exp(-0.3 * 0)
LOG2_E = math.log2(math.e)

N_GATES = 2 * DN_HEADS
GATE_ROWS = 16
PROJ_COLS = 4 * DN_WIDTH + 3 * DF_WIDTH
LANES = 128
SUBLANES = 8
QKV_COLS = 3 * DN_WIDTH
VMEM_LIMIT = 56 * 1024 * 1024

_BLK_DN_Z = 3 * DN_HEADS
_BLK_DF_Q = 4 * DN_HEADS
_BLK_DF_K = _BLK_DF_Q + DF_HEADS
_BLK_DF_Z = _BLK_DF_K + DF_HEADS


def _silu(x):
    half = 0.5 * x
    return half + half * jnp.tanh(half)


def _dot(a, b):
    return jnp.dot(a.astype(BF16), b.astype(BF16), preferred_element_type=F32)


def _dot_nt(a, b):
    return lax.dot_general(a.astype(BF16), b.astype(BF16), (((1,), (1,)), ((), ())),
                           preferred_element_type=F32)


IN_TM = 1024
IN_TN = 512
SEG_W = 256
CONV_ROWS = 512


def _inproj_kernel(x_ref, gain_ref, w_ref, wt_ref, seg_ref, qkg_ref, alog_ref, dtb_ref,
                   convw_ref, proj_ref, vt_ref, gate_ref, h_s, cstage_s, tail_s, *, tiles_per_seq):
    @pl.when(pl.program_id(0) % tiles_per_seq == 0)
    def _():
        tail_s[...] = jnp.zeros_like(tail_s)

    def stage(y, j):
        for hd in range(DN_HEADS):
            yh = y[:, hd * DN_HEAD:(hd + 1) * DN_HEAD]
            cstage_s[j * DN_HEADS + hd, :SUBLANES, :] = tail_s[j * DN_HEADS + hd]
            cstage_s[j * DN_HEADS + hd, SUBLANES:, :] = yh
            tail_s[j * DN_HEADS + hd] = yh[IN_TM - SUBLANES:, :]

    def conv_silu_norm(j, part):
        r0 = SUBLANES + part * CONV_ROWS
        for hd in range(DN_HEADS):
            cols = slice(j * IN_TN + hd * DN_HEAD, j * IN_TN + (hd + 1) * DN_HEAD)
            buf = cstage_s.at[j * DN_HEADS + hd]
            w = convw_ref[:, cols]
            acc = buf[r0:r0 + CONV_ROWS, :] * w[CONV_WIDTH - 1:CONV_WIDTH]
            for d in range(1, CONV_WIDTH):
                acc = acc + buf[r0 - d:r0 - d + CONV_ROWS, :] * w[CONV_WIDTH - 1 - d:CONV_WIDTH - d]
            a = _silu(acc)
            if j < 2:
                inv_norm = lax.rsqrt(jnp.sum(a * a, axis=-1, keepdims=True) + EPS)
                a = a * (inv_norm * (DN_HEAD ** -0.5) if j == 0 else inv_norm)
            proj_ref[part * CONV_ROWS:(part + 1) * CONV_ROWS, cols] = a.astype(BF16)

    def head_phase():
        x = x_ref[...]
        ms = jnp.mean(x * x, axis=-1, keepdims=True)
        h = (x * lax.rsqrt(ms + EPS) * gain_ref[...]).astype(BF16)
        h_s[...] = h
        yt = _dot_nt(wt_ref[...], h)
        vt_ref[...] = yt[:DF_WIDTH].astype(BF16)
        pre = yt[DF_WIDTH:]
        g = -jnp.exp(alog_ref[...]) * jax.nn.softplus(pre + dtb_ref[...])
        pos = lax.broadcasted_iota(jnp.int32, g.shape, 1) % CHUNK
        s = 1
        while s < CHUNK:
            g = g + jnp.where(pos >= s, pltpu.roll(g, s, 1), 0.0)
            s *= 2
        row = lax.broadcasted_iota(jnp.int32, g.shape, 0)
        gate_ref[...] = jnp.where(row < DN_HEADS, jax.nn.sigmoid(pre), g)

    def chunk_phase(j, conv_items):
        cols = slice(j * IN_TN, (j + 1) * IN_TN)
        y = jnp.dot(h_s[...], w_ref[:, cols], preferred_element_type=F32)
        for item in conv_items:
            conv_silu_norm(*item)
        if j < QKV_COLS // IN_TN:
            stage(y, j)
            return
        if j in (_BLK_DF_Q * LANES // IN_TN, _BLK_DF_K * LANES // IN_TN):
            which = 0 if j == _BLK_DF_Q * LANES // IN_TN else 1
            sq = (y * y).astype(BF16)
            msq = jnp.concatenate(
                [jnp.dot(sq[:, i * SEG_W:(i + 1) * SEG_W], seg_ref[...],
                         preferred_element_type=F32) for i in range(IN_TN // SEG_W)], axis=1)
            y = y * lax.rsqrt(msq + EPS) * qkg_ref[which:which + 1, :]
        elif j in (_BLK_DN_Z * LANES // IN_TN, _BLK_DF_Z * LANES // IN_TN):
            y = _silu(y)
        proj_ref[:, cols] = y.astype(BF16)

    head_phase()
    n_chunks = PROJ_COLS // IN_TN
    per_phase = -(-(QKV_COLS // IN_TN) * (IN_TM // CONV_ROWS) // (n_chunks - 1))
    pending = []
    for j in range(n_chunks):
        items = [pending.pop(0) for _ in range(min(per_phase, len(pending)))]
        chunk_phase(j, items)
        if j < QKV_COLS // IN_TN:
            pending += [(j, part) for part in range(IN_TM // CONV_ROWS)]
    for item in pending:
        conv_silu_norm(*item)


def _in_projection(x2d, t, gain, w_main, w_t, seg, qk_gain, alog_col, dtb_col, conv_w):
    m = x2d.shape[0]
    assert DF_WIDTH == IN_TN and DN_WIDTH == IN_TN and t % IN_TM == 0 and IN_TM % CHUNK == 0
    tiles_per_seq = t // IN_TM
    const = lambda shape: pl.BlockSpec(shape, lambda i: (0, 0))
    return pl.pallas_call(
        functools.partial(_inproj_kernel, tiles_per_seq=tiles_per_seq),
        grid=(m // IN_TM,),
        in_specs=[
            pl.BlockSpec((IN_TM, D_MODEL), lambda i: (i, 0)),
            const((1, D_MODEL)),
            const((D_MODEL, PROJ_COLS)),
            const((DF_WIDTH + GATE_ROWS, D_MODEL)),
            const((SEG_W, SEG_W)),
            const((2, DF_WIDTH)),
            const((GATE_ROWS, 1)),
            const((GATE_ROWS, 1)),
            const((CONV_WIDTH, QKV_COLS)),
        ],
        out_specs=[
            pl.BlockSpec((IN_TM, PROJ_COLS), lambda i: (i, 0)),
            pl.BlockSpec((None, DF_WIDTH, IN_TM),
                         lambda i: (i // tiles_per_seq, 0, i % tiles_per_seq)),
            pl.BlockSpec((None, GATE_ROWS, IN_TM),
                         lambda i: (i // tiles_per_seq, 0, i % tiles_per_seq)),
        ],
        out_shape=[
            jax.ShapeDtypeStruct((m, PROJ_COLS), BF16),
            jax.ShapeDtypeStruct((m // t, DF_WIDTH, t), BF16),
            jax.ShapeDtypeStruct((m // t, GATE_ROWS, t), F32),
        ],
        scratch_shapes=[
            pltpu.VMEM((IN_TM, D_MODEL), BF16),
            pltpu.VMEM((QKV_COLS // LANES, IN_TM + SUBLANES, LANES), F32),
            pltpu.VMEM((QKV_COLS // LANES, SUBLANES, LANES), F32),
        ],
        compiler_params=pltpu.CompilerParams(
            dimension_semantics=("arbitrary",), vmem_limit_bytes=VMEM_LIMIT),
        name="in_projection",
    )(x2d, gain, w_main, w_t, seg, qk_gain, alog_col, dtb_col, conv_w)


GDN_BATCH = 2
PREP_CHUNKS = 4


def _gdn_kernel(qkv_ref, zg_ref, gate_ref, gain_ref,
                out_ref, u_r, wq_r, aqk_r, kdt_r, state_s):
    n_batch, t = qkv_ref.shape[:2]
    n_chunks = t // CHUNK
    group_rows = PREP_CHUNKS * CHUNK

    def group_base(g):
        return g * group_rows if isinstance(g, int) else pl.multiple_of(g * group_rows, group_rows)

    row = lax.broadcasted_iota(jnp.int32, (CHUNK, 2 * CHUNK), 0)
    col = lax.broadcasted_iota(jnp.int32, (CHUNK, 2 * CHUNK), 1)
    eye_f = (row == col).astype(F32)
    lower_f = (row >= col).astype(F32)
    strict_f = (row > col).astype(F32)
    upper_f = (col >= CHUNK).astype(F32)
    eye_hi = (col - CHUNK == row).astype(F32)
    zero_rows = jnp.zeros((CHUNK, DN_HEAD), BF16)

    items = [(cc, h) for cc in range(PREP_CHUNKS) for h in range(DN_HEADS)]
    heads = range(DN_HEADS)
    gain = gain_ref[...]
    hcols = [slice(h * LANES, (h + 1) * LANES) for h in heads]
    hcols_c = [slice(h * LANES, h * LANES + CHUNK) for h in heads]

    def halves(fn, seq):
        out = []
        mid = len(seq) // 2
        for part in (seq[:mid], seq[mid:]):
            out.extend(fn(x) for x in part)
            yield out

    def prep(g, slot, bi):
        base = group_base(g)
        gates = [gate_ref[bi, pl.ds(base + cc * CHUNK, CHUNK), :] for cc in range(PREP_CHUNKS)]
        ks, lhs, decay, rhs, qe, kdt = [], [], [], [], [], []
        for cc, h in items:
            trows = pl.ds(base + cc * CHUNK, CHUNK)
            q = qkv_ref[bi, trows, hcols[h]].astype(F32)
            k = qkv_ref[bi, trows,
                        slice(DN_WIDTH + h * LANES, DN_WIDTH + (h + 1) * LANES)].astype(F32)
            v = qkv_ref[bi, trows, slice(2 * DN_WIDTH + h * LANES,
                                     2 * DN_WIDTH + (h + 1) * LANES)].astype(F32)
            beta_b = jnp.broadcast_to(gates[cc][:, h:h + 1], (CHUNK, DN_HEAD))
            gcb = jnp.broadcast_to(gates[cc][:, DN_HEADS + h:DN_HEADS + h + 1], (CHUNK, DN_HEAD))
            gr = jnp.sum(gcb * eye_f, axis=0, keepdims=True)
            decay.append(jnp.exp(jnp.minimum(gcb - gr, 0.0)))
            g_last = gcb[CHUNK - 1:CHUNK, :]
            eg = jnp.exp(gcb)
            kb = k * beta_b
            ks.append(jnp.concatenate([k.astype(BF16), zero_rows], axis=0))
            lhs.append(jnp.concatenate([kb, q], axis=0).astype(BF16))
            rhs.append(jnp.concatenate([v * beta_b, kb * eg], axis=1).astype(BF16))
            qe.append((q * eg).astype(BF16))
            kdt.append((k * jnp.exp(g_last - gcb)).T.astype(BF16))
        yield

        kq = None
        for kq in halves(lambda ab: _dot_nt(*ab), list(zip(lhs, ks))):
            yield
        m = [x[:CHUNK] * (d * strict_f) for x, d in zip(kq, decay)]
        a_qk = [(x[CHUNK:, :CHUNK] * (d * lower_f)[:, :CHUNK]).astype(BF16)
                for x, d in zip(kq, decay)]

        w = None
        for w in halves(lambda x: _dot(x[:, :CHUNK], x - eye_hi) + eye_hi, m):
            yield
        for _ in range(5):
            prev = w
            for w in halves(lambda x: _dot(x[:, :CHUNK], x) + x * upper_f, prev):
                yield
        sol = None
        for sol in halves(lambda xb: _dot(xb[0][:, CHUNK:], xb[1]), list(zip(w, rhs))):
            yield

        for n, (cc, h) in enumerate(items):
            rows = slice(cc * CHUNK, (cc + 1) * CHUNK)
            rows_w = slice(2 * cc * CHUNK, (2 * cc + 1) * CHUNK)
            rows_q = slice((2 * cc + 1) * CHUNK, (2 * cc + 2) * CHUNK)
            rows_t = slice(2 * cc * CHUNK, (2 * cc + 2) * CHUNK)
            u_r[slot, rows, hcols[h]] = sol[n][:, :DN_HEAD]
            wq_r[slot, rows_w, hcols[h]] = sol[n][:, DN_HEAD:].astype(BF16)
            wq_r[slot, rows_q, hcols[h]] = qe[n]
            aqk_r[slot, rows, hcols_c[h]] = a_qk[n]
            kdt_r[slot, rows_t, hcols_c[h]] = kdt[n]
        yield

    def rec(g, slot, bi):
        base = group_base(g)
        for cc in range(PREP_CHUNKS):
            rows = slice(cc * CHUNK, (cc + 1) * CHUNK)
            rows2 = slice(2 * cc * CHUNK, (2 * cc + 2) * CHUNK)
            trows = pl.ds(base + cc * CHUNK, CHUNK)
            st = [state_s[h] for h in heads]
            r = [jnp.dot(wq_r[slot, rows2, hcols[h]], st[h].astype(BF16),
                         preferred_element_type=F32) for h in heads]
            yield
            v_new = [(u_r[slot, rows, hcols[h]] - r[h][:CHUNK]).astype(BF16) for h in heads]
            upd = [jnp.dot(kdt_r[slot, rows2, hcols_c[h]], v_new[h], preferred_element_type=F32)
                   for h in heads]
            tail = gate_ref[bi, pl.ds(base + (cc + 1) * CHUNK - 8, 8), :]
            decay_last = jnp.exp(tail[7:8, :])
            for h in heads:
                state_s[h] = st[h] * decay_last[:, DN_HEADS + h:DN_HEADS + h + 1] + upd[h]
            yield
            o = [r[h][CHUNK:] + jnp.dot(aqk_r[slot, rows, hcols_c[h]], v_new[h],
                                        preferred_element_type=F32) for h in heads]
            for h in heads:
                y = o[h] * lax.rsqrt(jnp.mean(o[h] * o[h], axis=-1, keepdims=True) + EPS) * gain
                gate = zg_ref[bi, trows, hcols[h]].astype(F32)
                out_ref[bi, trows, hcols[h]] = (y * gate).astype(out_ref.dtype)
            yield

    def run_interleaved(*gens):
        gens = list(gens)
        while gens:
            for gen in list(gens):
                if next(gen, StopIteration) is StopIteration:
                    gens.remove(gen)

    n_groups = n_chunks // PREP_CHUNKS
    run_interleaved(prep(0, 0, 0))
    for bi in range(n_batch):
        state_s[...] = jnp.zeros_like(state_s)

        def pair_body(j, carry, bi=bi):
            run_interleaved(rec(2 * j, 0, bi), prep(2 * j + 1, 1, bi))
            run_interleaved(rec(2 * j + 1, 1, bi), prep(2 * j + 2, 0, bi))
            return carry

        lax.fori_loop(0, (n_groups - 2) // 2, pair_body, 0)
        run_interleaved(rec(n_groups - 2, 0, bi), prep(n_groups - 1, 1, bi))
        if bi + 1 < n_batch:
            run_interleaved(rec(n_groups - 1, 1, bi), prep(0, 0, bi + 1))
        else:
            run_interleaved(rec(n_groups - 1, 1, bi))


def _gated_deltanet(proj, gates, out_gain):
    b, t, _ = proj.shape
    group_rows = PREP_CHUNKS * CHUNK
    assert t % (2 * group_rows) == 0 and t // group_rows >= 4 and b % GDN_BATCH == 0
    ring = lambda rows, dtype: pltpu.VMEM((2, rows, DN_WIDTH), dtype)
    return pl.pallas_call(
        _gdn_kernel,
        grid=(b // GDN_BATCH,),
        in_specs=[
            pl.BlockSpec((GDN_BATCH, t, QKV_COLS), lambda i: (i, 0, 0)),
            pl.BlockSpec((GDN_BATCH, t, DN_WIDTH), lambda i: (i, 0, QKV_COLS // DN_WIDTH)),
            pl.BlockSpec((GDN_BATCH, t, N_GATES), lambda i: (i, 0, 0)),
            pl.BlockSpec((1, DN_HEAD), lambda i: (0, 0)),
        ],
        out_specs=pl.BlockSpec((GDN_BATCH, t, DN_WIDTH), lambda i: (i, 0, 0)),
        out_shape=jax.ShapeDtypeStruct((b, t, DN_WIDTH), BF16),
        scratch_shapes=[
            ring(group_rows, F32),
            ring(2 * group_rows, BF16),
            ring(group_rows, BF16),
            ring(2 * group_rows, BF16),
            pltpu.VMEM((DN_HEADS, DN_HEAD, DN_HEAD), F32),
        ],
        compiler_params=pltpu.CompilerParams(
            dimension_semantics=("parallel",), vmem_limit_bytes=VMEM_LIMIT),
        name="gated_deltanet",
    )(proj, proj, gates, out_gain)


AT_BQ = 512
AT_BK = 256
AT_CW = 256
AT_NQ = 2 * AT_BQ
AT_CHAINS = AT_NQ // AT_CW
AT_CPM = AT_BQ // AT_CW
AT_VROWS = DF_HEAD_V + 16
AT_PROJ_DELAY = 2
AT_HPS = 2
OUT_TN = 256


def _attention_tiles(n_heads, n_qblocks):
    all_chains = tuple(range(AT_CHAINS))
    lo = tuple(c for c in all_chains if c % AT_CPM == 0)
    hi = tuple(c for c in all_chains if c % AT_CPM == 1)
    tiles = []
    for hd in range(n_heads):
        for qi in range(n_qblocks):
            for kt in range(AT_CPM * qi + AT_CPM):
                d = kt - AT_CPM * qi
                chains, masked = ((all_chains, ()) if d < 0 else
                                  (all_chains, lo) if d == 0 else (hi, hi))
                tiles.append(dict(hd=hd, qi=qi, qb=hd * n_qblocks + qi, kt=kt, chains=chains,
                                  masked=masked, first=kt == 0))
    return tiles


def _diff_attn_kernel(q_ref, k_ref, vt_ref, zg_ref, lq1_ref, lk1_ref, lq2_ref, lk2_ref,
                      og_ref, x_ref, dn_ref, wout_ref, out_ref,
                      qq_s, vta_s, m_s, acc_s, y_s, s_a, s_b, c_a, c_b, p_a, p_b):
    t = q_ref.shape[0]
    n_qblocks = t // AT_BQ
    all_chains = tuple(range(AT_CHAINS))

    @pl.when(pl.program_id(1) == 0)
    def _():
        out_ref[...] = x_ref[...]

    def lanes(c):
        return slice(c * AT_CW, (c + 1) * AT_CW)

    first = lax.broadcasted_iota(jnp.int32, (AT_BQ, LANES), 1) < DF_HEAD_QK
    zero = jnp.zeros((AT_BQ, LANES), BF16)
    hcols = [slice(hd * LANES, (hd + 1) * LANES) for hd in range(AT_HPS)]
    for hd in range(AT_HPS):
        for qi in range(n_qblocks):
            blk = q_ref[qi * AT_BQ:(qi + 1) * AT_BQ, hcols[hd]]
            qq_s[hd * n_qblocks + qi, :AT_BQ, :] = jnp.where(first, blk, zero)
            qq_s[hd * n_qblocks + qi, AT_BQ:, :] = jnp.where(first, zero, blk)
        vta_s[hd, :DF_HEAD_V, :] = vt_ref[hcols[hd], :]
        vta_s[hd, DF_HEAD_V:, :] = jnp.ones((AT_VROWS - DF_HEAD_V, t), BF16)

    lam = (jnp.exp(jnp.sum(lq1_ref[...] * lk1_ref[...], axis=-1, keepdims=True))
           - jnp.exp(jnp.sum(lq2_ref[...] * lk2_ref[...], axis=-1, keepdims=True))
           + LAMBDA_INIT)
    out_scale = og_ref[...] * (1.0 - LAMBDA_INIT)

    krow = lax.broadcasted_iota(jnp.int32, (AT_BK, AT_CW), 0)
    qcol = lax.broadcasted_iota(jnp.int32, (AT_BK, AT_CW), 1)
    causal = krow <= qcol
    bufs = ((s_a, c_a, p_a), (s_b, c_b, p_b))

    def scores(tile, c, buf):
        rows = slice(tile["kt"] * AT_BK, (tile["kt"] + 1) * AT_BK)
        s = _dot_nt(k_ref[rows, hcols[tile["hd"]]], qq_s[tile["qb"], lanes(c), :])
        if c in tile["masked"]:
            s = jnp.where(causal, s, -jnp.inf)
        buf[0][:, lanes(c)] = s
        buf[1][:, lanes(c)] = jnp.max(s, axis=0, keepdims=True)

    def values(tile, c, buf):
        rows = slice(tile["kt"] * AT_BK, (tile["kt"] + 1) * AT_BK)
        return jnp.dot(vta_s[tile["hd"], :, rows], buf[2][:, lanes(c)],
                       preferred_element_type=F32)

    def finish(tile):
        qb = tile["qb"]
        rows = slice(tile["qi"] * AT_BQ, (tile["qi"] + 1) * AT_BQ)
        inv = 1.0 / acc_s[qb, DF_HEAD_V:DF_HEAD_V + 1, :]
        o = (acc_s[qb, :DF_HEAD_V, :AT_BQ] * inv[:, :AT_BQ]
             - acc_s[qb, :DF_HEAD_V, AT_BQ:] * (lam * inv[:, AT_BQ:]))
        y = (o * lax.rsqrt(jnp.mean(o * o, axis=0, keepdims=True) + EPS)).T
        y = y * out_scale
        y_s[qb] = (y * zg_ref[rows, hcols[tile["hd"]]].astype(F32)).astype(BF16)

    def project(tile):
        rows = slice(tile["qi"] * AT_BQ, (tile["qi"] + 1) * AT_BQ)
        hd = tile["hd"]
        lhs = jnp.concatenate([y_s[tile["qb"]], dn_ref[rows, hcols[hd]]], axis=1)
        for nc in range(D_MODEL // OUT_TN):
            cols = slice(nc * OUT_TN, (nc + 1) * OUT_TN)
            out_ref[rows, cols] = out_ref[rows, cols] + jnp.dot(
                lhs, wout_ref[hd, :, cols], preferred_element_type=F32)

    tiles = _attention_tiles(AT_HPS, n_qblocks)
    due = {}
    for c in tiles[0]["chains"]:
        scores(tiles[0], c, bufs[0])
    for n, cur in enumerate(tiles):
        nxt = tiles[n + 1] if n + 1 < len(tiles) else None
        prv = tiles[n - 1] if n > 0 else None
        s_cur, c_cur, p_cur = bufs[n % 2]
        qb = cur["qb"]
        for c in all_chains:
            alpha = None
            if c in cur["chains"]:
                if cur["first"]:
                    m_new = c_cur[:, lanes(c)]
                else:
                    m_old = m_s[qb, :, lanes(c)]
                    m_new = jnp.maximum(m_old, c_cur[:, lanes(c)])
                    alpha = jnp.exp2(m_old - m_new)
                p = jnp.exp2(s_cur[:, lanes(c)] - m_new)
                m_s[qb, :, lanes(c)] = m_new
                p_cur[:, lanes(c)] = p.astype(BF16)
            if nxt is not None and c in nxt["chains"]:
                scores(nxt, c, bufs[(n + 1) % 2])
            if prv is not None and c in prv["chains"]:
                pv = values(prv, c, bufs[(n - 1) % 2])
                pq = prv["qb"]
                acc = pv if prv["first"] else acc_s[pq, :, lanes(c)] + pv
                if pq == qb and alpha is not None:
                    acc = alpha * acc
                acc_s[pq, :, lanes(c)] = acc
        if prv is not None and prv["qb"] != qb:
            finish(prv)
            due[n + AT_PROJ_DELAY] = prv
        if n in due:
            project(due.pop(n))
    last = tiles[-1]
    for c in last["chains"]:
        acc_s[last["qb"], :, lanes(c)] = (acc_s[last["qb"], :, lanes(c)]
                                          + values(last, c, bufs[(len(tiles) - 1) % 2]))
    finish(last)
    for tile in sorted(due.values(), key=lambda tl: tl["qb"]) + [last]:
        project(tile)


def _diff_attention(proj, vt, lq1, lk1, lq2, lk2, out_gain, x, mixed_dn, w_heads):
    b, t, _ = proj.shape
    assert AT_BQ == AT_CPM * AT_BK and AT_CW == AT_BK and AT_CPM == 2 and t % AT_BQ == 0
    assert DF_HEADS == DN_HEADS and DF_HEAD_V == DN_HEAD and D_MODEL % OUT_TN == 0
    assert DF_HEADS % AT_HPS == 0 and all(
        blk % AT_HPS == 0 for blk in (_BLK_DF_Q, _BLK_DF_K, _BLK_DF_Z))
    n_qblocks = t // AT_BQ
    n_blocks = AT_HPS * n_qblocks
    wide = AT_HPS * LANES
    small = lambda n: pl.BlockSpec((1, n), lambda i, h: (0, 0))
    return pl.pallas_call(
        _diff_attn_kernel,
        grid=(b, DF_HEADS // AT_HPS),
        in_specs=[
            pl.BlockSpec((None, t, wide), lambda i, h: (i, 0, _BLK_DF_Q // AT_HPS + h)),
            pl.BlockSpec((None, t, wide), lambda i, h: (i, 0, _BLK_DF_K // AT_HPS + h)),
            pl.BlockSpec((None, wide, t), lambda i, h: (i, h, 0)),
            pl.BlockSpec((None, t, wide), lambda i, h: (i, 0, _BLK_DF_Z // AT_HPS + h)),
            small(DF_HEAD_QK), small(DF_HEAD_QK), small(DF_HEAD_QK), small(DF_HEAD_QK),
            small(DF_HEAD_V),
            pl.BlockSpec((None, t, D_MODEL), lambda i, h: (i, 0, 0)),
            pl.BlockSpec((None, t, wide), lambda i, h: (i, 0, h)),
            pl.BlockSpec((AT_HPS, DF_HEAD_V + DN_HEAD, D_MODEL), lambda i, h: (h, 0, 0)),
        ],
        out_specs=pl.BlockSpec((None, t, D_MODEL), lambda i, h: (i, 0, 0)),
        out_shape=jax.ShapeDtypeStruct((b, t, D_MODEL), F32),
        scratch_shapes=[
            pltpu.VMEM((n_blocks, AT_NQ, LANES), BF16),
            pltpu.VMEM((AT_HPS, AT_VROWS, t), BF16),
            pltpu.VMEM((n_blocks, 1, AT_NQ), F32),
            pltpu.VMEM((n_blocks, AT_VROWS, AT_NQ), F32),
            pltpu.VMEM((n_blocks, AT_BQ, DF_HEAD_V), BF16),
            pltpu.VMEM((AT_BK, AT_NQ), F32),
            pltpu.VMEM((AT_BK, AT_NQ), F32),
            pltpu.VMEM((1, AT_NQ), F32),
            pltpu.VMEM((1, AT_NQ), F32),
            pltpu.VMEM((AT_BK, AT_NQ), BF16),
            pltpu.VMEM((AT_BK, AT_NQ), BF16),
        ],
        compiler_params=pltpu.CompilerParams(
            dimension_semantics=("parallel", "arbitrary"), vmem_limit_bytes=VMEM_LIMIT),
        name="diff_attention",
    )(proj, proj, vt, proj, lq1, lk1, lq2, lk2, out_gain, x, mixed_dn, w_heads)


def kernel(x, norm_gain, w_in, conv_w, a_log, dt_bias, dn_out_gain, q_gain, k_gain,
           lambda_q1, lambda_k1, lambda_q2, lambda_k2, df_out_gain, w_out):
    b, t, d = x.shape
    assert d == D_MODEL and norm_gain.shape[0] == 1
    x2d = x.reshape(b * t, d)

    w = w_in[0]
    gate_lo = 4 * DN_WIDTH
    df_lo = gate_lo + N_GATES
    w_main = jnp.concatenate([w[:, :gate_lo], w[:, df_lo:df_lo + 2 * DF_WIDTH],
                              w[:, df_lo + 3 * DF_WIDTH:]], axis=1).astype(BF16)
    w_t = jnp.concatenate([w[:, df_lo + 2 * DF_WIDTH:df_lo + 3 * DF_WIDTH],
                           jnp.pad(w[:, gate_lo:df_lo], ((0, 0), (0, GATE_ROWS - N_GATES)))],
                          axis=1).astype(BF16).T
    half = jnp.arange(SEG_W, dtype=jnp.int32) // DF_HEAD_QK
    seg = ((half[:, None] == half[None, :]).astype(F32) / DF_HEAD_QK).astype(BF16)
    reps = DF_WIDTH // DF_HEAD_QK
    qk_gain = jnp.concatenate([jnp.tile(q_gain, (1, reps)) * (DF_HEAD_QK ** -0.5 * LOG2_E),
                               jnp.tile(k_gain, (1, reps))], axis=0)

    pad_rows = lambda v: jnp.pad(v.reshape(DN_HEADS, 1),
                                 ((DN_HEADS, GATE_ROWS - 2 * DN_HEADS), (0, 0)))
    proj, vt, gates_t = _in_projection(x2d, t, norm_gain, w_main, w_t, seg, qk_gain,
                                       pad_rows(a_log), pad_rows(dt_bias), conv_w[0])
    proj = proj.reshape(b, t, PROJ_COLS)
    gates = jnp.swapaxes(gates_t[:, :N_GATES, :], 1, 2)

    mixed_dn = _gated_deltanet(proj, gates, dn_out_gain)

    wo = w_out[0].astype(BF16)
    w_heads = jnp.concatenate([wo[DN_WIDTH:].reshape(DF_HEADS, DF_HEAD_V, d),
                               wo[:DN_WIDTH].reshape(DN_HEADS, DN_HEAD, d)], axis=1)
    return _diff_attention(proj, vt, lambda_q1, lambda_k1, lambda_q2, lambda_k2,
                           df_out_gain, x, mixed_dn, w_heads)
```

```python
import functools
import math

import jax
import jax.numpy as jnp
from jax import lax
from jax.experimental import pallas as pl
from jax.experimental.pallas import tpu as pltpu

F32 = jnp.float32
BF16 = jnp.bfloat16

D_MODEL = 1024
DN_HEADS = 4
DN_HEAD = 128
DN_WIDTH = DN_HEADS * DN_HEAD
CONV_WIDTH = 4
CHUNK = 64
DF_HEADS = 4
DF_HEAD_QK = 64
DF_HEAD_V = 128
DF_WIDTH = DF_HEADS * DF_HEAD_V
EPS = 1e-6
LAMBDA_INIT = 0.8 - 0.6 * math.exp(-0.3 * 0)
LOG2_E = math.log2(math.e)

N_GATES = 2 * DN_HEADS
GATE_ROWS = 16
PROJ_COLS = 4 * DN_WIDTH + 3 * DF_WIDTH
LANES = 128
SUBLANES = 8
QKV_COLS = 3 * DN_WIDTH
VMEM_LIMIT = 56 * 1024 * 1024

_BLK_DN_Z = 3 * DN_HEADS
_BLK_DF_Q = 4 * DN_HEADS
_BLK_DF_K = _BLK_DF_Q + DF_HEADS
_BLK_DF_Z = _BLK_DF_K + DF_HEADS


def _silu(x):
    half = 0.5 * x
    return half + half * jnp.tanh(half)


def _dot(a, b):
    return jnp.dot(a.astype(BF16), b.astype(BF16), preferred_element_type=F32)


def _dot_nt(a, b):
    return lax.dot_general(a.astype(BF16), b.astype(BF16), (((1,), (1,)), ((), ())),
                           preferred_element_type=F32)


IN_TM = 1024
IN_TN = 512
SEG_W = 256
CONV_ROWS = 512


def _inproj_kernel(x_ref, gain_ref, w_ref, wt_ref, seg_ref, qkg_ref, alog_ref, dtb_ref,
                   convw_ref, proj_ref, vt_ref, gate_ref, cstage_s, tail_s, *, tiles_per_seq):
    @pl.when(pl.program_id(0) % tiles_per_seq == 0)
    def _():
        tail_s[...] = jnp.zeros_like(tail_s)

    def stage(y, j):
        for hd in range(DN_HEADS):
            yh = y[:, hd * DN_HEAD:(hd + 1) * DN_HEAD]
            cstage_s[j * DN_HEADS + hd, :SUBLANES, :] = tail_s[j * DN_HEADS + hd]
            cstage_s[j * DN_HEADS + hd, SUBLANES:, :] = yh
            tail_s[j * DN_HEADS + hd] = yh[IN_TM - SUBLANES:, :]

    def conv_silu_norm(j, part):
        r0 = SUBLANES + part * CONV_ROWS
        for hd in range(DN_HEADS):
            cols = slice(j * IN_TN + hd * DN_HEAD, j * IN_TN + (hd + 1) * DN_HEAD)
            buf = cstage_s.at[j * DN_HEADS + hd]
            w = convw_ref[:, cols]
            acc = buf[r0:r0 + CONV_ROWS, :] * w[CONV_WIDTH - 1:CONV_WIDTH]
            for d in range(1, CONV_WIDTH):
                acc = acc + buf[r0 - d:r0 - d + CONV_ROWS, :] * w[CONV_WIDTH - 1 - d:CONV_WIDTH - d]
            a = _silu(acc)
            if j < 2:
                inv_norm = lax.rsqrt(jnp.sum(a * a, axis=-1, keepdims=True) + EPS)
                a = a * (inv_norm * (DN_HEAD ** -0.5) if j == 0 else inv_norm)
            proj_ref[part * CONV_ROWS:(part + 1) * CONV_ROWS, cols] = a.astype(BF16)

    def head_phase():
        x = x_ref[...]
        ms = jnp.mean(x * x, axis=-1, keepdims=True)
        h = (x * lax.rsqrt(ms + EPS) * gain_ref[...]).astype(BF16)
        yt = _dot_nt(wt_ref[...], h)
        vt_ref[...] = yt[:DF_WIDTH].astype(BF16)
        pre = yt[DF_WIDTH:]
        g = -jnp.exp(alog_ref[...]) * jax.nn.softplus(pre + dtb_ref[...])
        pos = lax.broadcasted_iota(jnp.int32, g.shape, 1) % CHUNK
        s = 1
        while s < CHUNK:
            g = g + jnp.where(pos >= s, pltpu.roll(g, s, 1), 0.0)
            s *= 2
        row = lax.broadcasted_iota(jnp.int32, g.shape, 0)
        gate_ref[...] = jnp.where(row < DN_HEADS, jax.nn.sigmoid(pre), g)
        return h

    def chunk_phase(j, conv_items):
        cols = slice(j * IN_TN, (j + 1) * IN_TN)
        y = jnp.dot(h, w_ref[:, cols], preferred_element_type=F32)
        for item in conv_items:
            conv_silu_norm(*item)
        if j < QKV_COLS // IN_TN:
            stage(y, j)
            return
        if j in (_BLK_DF_Q * LANES // IN_TN, _BLK_DF_K * LANES // IN_TN):
            which = 0 if j == _BLK_DF_Q * LANES // IN_TN else 1
            sq = (y * y).astype(BF16)
            msq = jnp.concatenate(
                [jnp.dot(sq[:, i * SEG_W:(i + 1) * SEG_W], seg_ref[...],
                         preferred_element_type=F32) for i in range(IN_TN // SEG_W)], axis=1)
            y = y * lax.rsqrt(msq + EPS) * qkg_ref[which:which + 1, :]
        elif j in (_BLK_DN_Z * LANES // IN_TN, _BLK_DF_Z * LANES // IN_TN):
            y = _silu(y)
        proj_ref[:, cols] = y.astype(BF16)

    h = head_phase()
    n_chunks = PROJ_COLS // IN_TN
    per_phase = -(-(QKV_COLS // IN_TN) * (IN_TM // CONV_ROWS) // (n_chunks - 1))
    pending = []
    for j in range(n_chunks):
        items = [pending.pop(0) for _ in range(min(per_phase, len(pending)))]
        chunk_phase(j, items)
        if j < QKV_COLS // IN_TN:
            pending += [(j, part) for part in range(IN_TM // CONV_ROWS)]
    for item in pending:
        conv_silu_norm(*item)


def _in_projection(x2d, t, gain, w_main, w_t, seg, qk_gain, alog_col, dtb_col, conv_w):
    m = x2d.shape[0]
    assert DF_WIDTH == IN_TN and DN_WIDTH == IN_TN and t % IN_TM == 0 and IN_TM % CHUNK == 0
    tiles_per_seq = t // IN_TM
    const = lambda shape: pl.BlockSpec(shape, lambda i: (0, 0))
    return pl.pallas_call(
        functools.partial(_inproj_kernel, tiles_per_seq=tiles_per_seq),
        grid=(m // IN_TM,),
        in_specs=[
            pl.BlockSpec((IN_TM, D_MODEL), lambda i: (i, 0)),
            const((1, D_MODEL)),
            const((D_MODEL, PROJ_COLS)),
            const((DF_WIDTH + GATE_ROWS, D_MODEL)),
            const((SEG_W, SEG_W)),
            const((2, DF_WIDTH)),
            const((GATE_ROWS, 1)),
            const((GATE_ROWS, 1)),
            const((CONV_WIDTH, QKV_COLS)),
        ],
        out_specs=[
            pl.BlockSpec((IN_TM, PROJ_COLS), lambda i: (i, 0)),
            pl.BlockSpec((None, DF_WIDTH, IN_TM),
                         lambda i: (i // tiles_per_seq, 0, i % tiles_per_seq)),
            pl.BlockSpec((None, GATE_ROWS, IN_TM),
                         lambda i: (i // tiles_per_seq, 0, i % tiles_per_seq)),
        ],
        out_shape=[
            jax.ShapeDtypeStruct((m, PROJ_COLS), BF16),
            jax.ShapeDtypeStruct((m // t, DF_WIDTH, t), BF16),
            jax.ShapeDtypeStruct((m // t, GATE_ROWS, t), F32),
        ],
        scratch_shapes=[
            pltpu.VMEM((QKV_COLS // LANES, IN_TM + SUBLANES, LANES), F32),
            pltpu.VMEM((QKV_COLS // LANES, SUBLANES, LANES), F32),
        ],
        compiler_params=pltpu.CompilerParams(
            dimension_semantics=("arbitrary",), vmem_limit_bytes=VMEM_LIMIT),
        name="in_projection",
    )(x2d, gain, w_main, w_t, seg, qk_gain, alog_col, dtb_col, conv_w)


PREP_CHUNKS = 4


def _gdn_kernel(qkv_ref, zg_ref, gate_ref, gain_ref,
                out_ref, u_r, wq_r, aqk_r, kdt_r, state_s):
    t = qkv_ref.shape[0]
    n_chunks = t // CHUNK
    group_rows = PREP_CHUNKS * CHUNK

    def group_base(g):
        return g * group_rows if isinstance(g, int) else pl.multiple_of(g * group_rows, group_rows)

    row = lax.broadcasted_iota(jnp.int32, (CHUNK, 2 * CHUNK), 0)
    col = lax.broadcasted_iota(jnp.int32, (CHUNK, 2 * CHUNK), 1)
    eye_f = (row == col).astype(F32)
    lower_f = (row >= col).astype(F32)
    strict_f = (row > col).astype(F32)
    upper_f = (col >= CHUNK).astype(F32)
    eye_hi = (col - CHUNK == row).astype(F32)
    zero_rows = jnp.zeros((CHUNK, DN_HEAD), BF16)

    items = [(cc, h) for cc in range(PREP_CHUNKS) for h in range(DN_HEADS)]
    heads = range(DN_HEADS)
    gain = gain_ref[...]
    hcols = [slice(h * LANES, (h + 1) * LANES) for h in heads]
    hcols_c = [slice(h * LANES, h * LANES + CHUNK) for h in heads]

    def halves(fn, seq):
        out = []
        mid = len(seq) // 2
        for part in (seq[:mid], seq[mid:]):
            out.extend(fn(x) for x in part)
            yield out

    def prep(g, slot):
        base = group_base(g)
        gates = [gate_ref[pl.ds(base + cc * CHUNK, CHUNK), :] for cc in range(PREP_CHUNKS)]
        ks, lhs, decay, rhs, qe, kdt = [], [], [], [], [], []
        for cc, h in items:
            trows = pl.ds(base + cc * CHUNK, CHUNK)
            q = qkv_ref[trows, hcols[h]].astype(F32)
            k = qkv_ref[trows, slice(DN_WIDTH + h * LANES, DN_WIDTH + (h + 1) * LANES)].astype(F32)
            v = qkv_ref[trows, slice(2 * DN_WIDTH + h * LANES,
                                     2 * DN_WIDTH + (h + 1) * LANES)].astype(F32)
            beta_b = jnp.broadcast_to(gates[cc][:, h:h + 1], (CHUNK, DN_HEAD))
            gcb = jnp.broadcast_to(gates[cc][:, DN_HEADS + h:DN_HEADS + h + 1], (CHUNK, DN_HEAD))
            gr = jnp.sum(gcb * eye_f, axis=0, keepdims=True)
            decay.append(jnp.exp(jnp.minimum(gcb - gr, 0.0)))
            g_last = gcb[CHUNK - 1:CHUNK, :]
            eg = jnp.exp(gcb)
            kb = k * beta_b
            ks.append(jnp.concatenate([k.astype(BF16), zero_rows], axis=0))
            lhs.append(jnp.concatenate([kb, q], axis=0).astype(BF16))
            rhs.append(jnp.concatenate([v * beta_b, kb * eg], axis=1).astype(BF16))
            qe.append((q * eg).astype(BF16))
            kdt.append((k * jnp.exp(g_last - gcb)).T.astype(BF16))
        yield

        kq = None
        for kq in halves(lambda ab: _dot_nt(*ab), list(zip(lhs, ks))):
            yield
        m = [x[:CHUNK] * (d * strict_f) for x, d in zip(kq, decay)]
        a_qk = [(x[CHUNK:, :CHUNK] * (d * lower_f)[:, :CHUNK]).astype(BF16)
                for x, d in zip(kq, decay)]

        w = None
        for w in halves(lambda x: _dot(x[:, :CHUNK], x - eye_hi) + eye_hi, m):
            yield
        for _ in range(5):
            prev = w
            for w in halves(lambda x: _dot(x[:, :CHUNK], x) + x * upper_f, prev):
                yield
        sol = None
        for sol in halves(lambda xb: _dot(xb[0][:, CHUNK:], xb[1]), list(zip(w, rhs))):
            yield

        for n, (cc, h) in enumerate(items):
            rows = slice(cc * CHUNK, (cc + 1) * CHUNK)
            rows_w = slice(2 * cc * CHUNK, (2 * cc + 1) * CHUNK)
            rows_q = slice((2 * cc + 1) * CHUNK, (2 * cc + 2) * CHUNK)
            rows_t = slice(2 * cc * CHUNK, (2 * cc + 2) * CHUNK)
            u_r[slot, rows, hcols[h]] = sol[n][:, :DN_HEAD]
            wq_r[slot, rows_w, hcols[h]] = sol[n][:, DN_HEAD:].astype(BF16)
            wq_r[slot, rows_q, hcols[h]] = qe[n]
            aqk_r[slot, rows, hcols_c[h]] = a_qk[n]
            kdt_r[slot, rows_t, hcols_c[h]] = kdt[n]
        yield

    def rec(g, slot):
        base = group_base(g)
        for cc in range(PREP_CHUNKS):
            rows = slice(cc * CHUNK, (cc + 1) * CHUNK)
            rows2 = slice(2 * cc * CHUNK, (2 * cc + 2) * CHUNK)
            trows = pl.ds(base + cc * CHUNK, CHUNK)
            st = [state_s[h] for h in heads]
            r = [jnp.dot(wq_r[slot, rows2, hcols[h]], st[h].astype(BF16),
                         preferred_element_type=F32) for h in heads]
            yield
            v_new = [(u_r[slot, rows, hcols[h]] - r[h][:CHUNK]).astype(BF16) for h in heads]
            upd = [jnp.dot(kdt_r[slot, rows2, hcols_c[h]], v_new[h], preferred_element_type=F32)
                   for h in heads]
            tail = gate_ref[pl.ds(base + (cc + 1) * CHUNK - 8, 8), :]
            decay_last = jnp.exp(tail[7:8, :])
            for h in heads:
                state_s[h] = st[h] * decay_last[:, DN_HEADS + h:DN_HEADS + h + 1] + upd[h]
            yield
            o = [r[h][CHUNK:] + jnp.dot(aqk_r[slot, rows, hcols_c[h]], v_new[h],
                                        preferred_element_type=F32) for h in heads]
            for h in heads:
                y = o[h] * lax.rsqrt(jnp.mean(o[h] * o[h], axis=-1, keepdims=True) + EPS) * gain
                gate = zg_ref[trows, hcols[h]].astype(F32)
                out_ref[trows, hcols[h]] = (y * gate).astype(out_ref.dtype)
            yield

    def run_interleaved(*gens):
        gens = list(gens)
        while gens:
            for gen in list(gens):
                if next(gen, StopIteration) is StopIteration:
                    gens.remove(gen)

    n_groups = n_chunks // PREP_CHUNKS
    state_s[...] = jnp.zeros_like(state_s)
    run_interleaved(prep(0, 0))

    def pair_body(j, carry):
        run_interleaved(rec(2 * j, 0), prep(2 * j + 1, 1))
        run_interleaved(rec(2 * j + 1, 1), prep(2 * j + 2, 0))
        return carry

    lax.fori_loop(0, (n_groups - 2) // 2, pair_body, 0)
    run_interleaved(rec(n_groups - 2, 0), prep(n_groups - 1, 1))
    run_interleaved(rec(n_groups - 1, 1))


def _gated_deltanet(proj, gates, out_gain):
    b, t, _ = proj.shape
    group_rows = PREP_CHUNKS * CHUNK
    assert t % (2 * group_rows) == 0 and t // group_rows >= 4
    ring = lambda rows, dtype: pltpu.VMEM((2, rows, DN_WIDTH), dtype)
    return pl.pallas_call(
        _gdn_kernel,
        grid=(b,),
        in_specs=[
            pl.BlockSpec((None, t, QKV_COLS), lambda i: (i, 0, 0)),
            pl.BlockSpec((None, t, DN_WIDTH), lambda i: (i, 0, QKV_COLS // DN_WIDTH)),
            pl.BlockSpec((None, t, N_GATES), lambda i: (i, 0, 0)),
            pl.BlockSpec((1, DN_HEAD), lambda i: (0, 0)),
        ],
        out_specs=pl.BlockSpec((None, t, DN_WIDTH), lambda i: (i, 0, 0)),
        out_shape=jax.ShapeDtypeStruct((b, t, DN_WIDTH), BF16),
        scratch_shapes=[
            ring(group_rows, F32),
            ring(2 * group_rows, BF16),
            ring(group_rows, BF16),
            ring(2 * group_rows, BF16),
            pltpu.VMEM((DN_HEADS, DN_HEAD, DN_HEAD), F32),
        ],
        compiler_params=pltpu.CompilerParams(
            dimension_semantics=("parallel",), vmem_limit_bytes=VMEM_LIMIT),
        name="gated_deltanet",
    )(proj, proj, gates, out_gain)


AT_BQ = 512
AT_BK = 256
AT_CW = 256
AT_NQ = 2 * AT_BQ
AT_CHAINS = AT_NQ // AT_CW
AT_CPM = AT_BQ // AT_CW
AT_VROWS = DF_HEAD_V + 16
AT_PROJ_DELAY = 2
AT_HPS = 2
OUT_TN = 256


def _attention_tiles(n_heads, n_qblocks):
    all_chains = tuple(range(AT_CHAINS))
    lo = tuple(c for c in all_chains if c % AT_CPM == 0)
    hi = tuple(c for c in all_chains if c % AT_CPM == 1)
    tiles = []
    for hd in range(n_heads):
        for qi in range(n_qblocks):
            for kt in range(AT_CPM * qi + AT_CPM):
                d = kt - AT_CPM * qi
                chains, masked = ((all_chains, ()) if d < 0 else
                                  (all_chains, lo) if d == 0 else (hi, hi))
                tiles.append(dict(hd=hd, qi=qi, qb=hd * n_qblocks + qi, kt=kt, chains=chains,
                                  masked=masked, first=kt == 0))
    return tiles


def _diff_attn_kernel(q_ref, k_ref, vt_ref, zg_ref, lq1_ref, lk1_ref, lq2_ref, lk2_ref,
                      og_ref, x_ref, dn_ref, wout_ref, out_ref,
                      qq_s, vta_s, m_s, acc_s, y_s, s_a, s_b, c_a, c_b, p_a, p_b):
    t = q_ref.shape[0]
    n_qblocks = t // AT_BQ
    all_chains = tuple(range(AT_CHAINS))

    @pl.when(pl.program_id(1) == 0)
    def _():
        out_ref[...] = x_ref[...]

    def lanes(c):
        return slice(c * AT_CW, (c + 1) * AT_CW)

    first = lax.broadcasted_iota(jnp.int32, (AT_BQ, LANES), 1) < DF_HEAD_QK
    zero = jnp.zeros((AT_BQ, LANES), BF16)
    hcols = [slice(hd * LANES, (hd + 1) * LANES) for hd in range(AT_HPS)]
    for hd in range(AT_HPS):
        for qi in range(n_qblocks):
            blk = q_ref[qi * AT_BQ:(qi + 1) * AT_BQ, hcols[hd]]
            qq_s[hd * n_qblocks + qi, :AT_BQ, :] = jnp.where(first, blk, zero)
            qq_s[hd * n_qblocks + qi, AT_BQ:, :] = jnp.where(first, zero, blk)
        vta_s[hd, :DF_HEAD_V, :] = vt_ref[hcols[hd], :]
        vta_s[hd, DF_HEAD_V:, :] = jnp.ones((AT_VROWS - DF_HEAD_V, t), BF16)

    lam = (jnp.exp(jnp.sum(lq1_ref[...] * lk1_ref[...], axis=-1, keepdims=True))
           - jnp.exp(jnp.sum(lq2_ref[...] * lk2_ref[...], axis=-1, keepdims=True))
           + LAMBDA_INIT)
    out_scale = og_ref[...] * (1.0 - LAMBDA_INIT)

    krow = lax.broadcasted_iota(jnp.int32, (AT_BK, AT_CW), 0)
    qcol = lax.broadcasted_iota(jnp.int32, (AT_BK, AT_CW), 1)
    causal = krow <= qcol
    bufs = ((s_a, c_a, p_a), (s_b, c_b, p_b))

    def scores(tile, c, buf):
        rows = slice(tile["kt"] * AT_BK, (tile["kt"] + 1) * AT_BK)
        s = _dot_nt(k_ref[rows, hcols[tile["hd"]]], qq_s[tile["qb"], lanes(c), :])
        if c in tile["masked"]:
            s = jnp.where(causal, s, -jnp.inf)
        buf[0][:, lanes(c)] = s
        buf[1][:, lanes(c)] = jnp.max(s, axis=0, keepdims=True)

    def values(tile, c, buf):
        rows = slice(tile["kt"] * AT_BK, (tile["kt"] + 1) * AT_BK)
        return jnp.dot(vta_s[tile["hd"], :, rows], buf[2][:, lanes(c)],
                       preferred_element_type=F32)

    def finish(tile):
        qb = tile["qb"]
        rows = slice(tile["qi"] * AT_BQ, (tile["qi"] + 1) * AT_BQ)
        inv = 1.0 / acc_s[qb, DF_HEAD_V:DF_HEAD_V + 1, :]
        o = (acc_s[qb, :DF_HEAD_V, :AT_BQ] * inv[:, :AT_BQ]
             - acc_s[qb, :DF_HEAD_V, AT_BQ:] * (lam * inv[:, AT_BQ:]))
        y = (o * lax.rsqrt(jnp.mean(o * o, axis=0, keepdims=True) + EPS)).T
        y = y * out_scale
        y_s[qb] = (y * zg_ref[rows, hcols[tile["hd"]]].astype(F32)).astype(BF16)

    def project(tile):
        rows = slice(tile["qi"] * AT_BQ, (tile["qi"] + 1) * AT_BQ)
        hd = tile["hd"]
        lhs = jnp.concatenate([y_s[tile["qb"]], dn_ref[rows, hcols[hd]]], axis=1)
        for nc in range(D_MODEL // OUT_TN):
            cols = slice(nc * OUT_TN, (nc + 1) * OUT_TN)
            out_ref[rows, cols] = out_ref[rows, cols] + jnp.dot(
                lhs, wout_ref[hd, :, cols], preferred_element_type=F32)

    tiles = _attention_tiles(AT_HPS, n_qblocks)
    due = {}
    for c in tiles[0]["chains"]:
        scores(tiles[0], c, bufs[0])
    for n, cur in enumerate(tiles):
        nxt = tiles[n + 1] if n + 1 < len(tiles) else None
        prv = tiles[n - 1] if n > 0 else None
        s_cur, c_cur, p_cur = bufs[n % 2]
        qb = cur["qb"]
        for c in all_chains:
            alpha = None
            if c in cur["chains"]:
                if cur["first"]:
                    m_new = c_cur[:, lanes(c)]
                else:
                    m_old = m_s[qb, :, lanes(c)]
                    m_new = jnp.maximum(m_old, c_cur[:, lanes(c)])
                    alpha = jnp.exp2(m_old - m_new)
                p = jnp.exp2(s_cur[:, lanes(c)] - m_new)
                m_s[qb, :, lanes(c)] = m_new
                p_cur[:, lanes(c)] = p.astype(BF16)
            if nxt is not None and c in nxt["chains"]:
                scores(nxt, c, bufs[(n + 1) % 2])
            if prv is not None and c in prv["chains"]:
                pv = values(prv, c, bufs[(n - 1) % 2])
                pq = prv["qb"]
                acc = pv if prv["first"] else acc_s[pq, :, lanes(c)] + pv
                if pq == qb and alpha is not None:
                    acc = alpha * acc
                acc_s[pq, :, lanes(c)] = acc
        if prv is not None and prv["qb"] != qb:
            finish(prv)
            due[n + AT_PROJ_DELAY] = prv
        if n in due:
            project(due.pop(n))
    last = tiles[-1]
    for c in last["chains"]:
        acc_s[last["qb"], :, lanes(c)] = (acc_s[last["qb"], :, lanes(c)]
                                          + values(last, c, bufs[(len(tiles) - 1) % 2]))
    finish(last)
    for tile in sorted(due.values(), key=lambda tl: tl["qb"]) + [last]:
        project(tile)


def _diff_attention(proj, vt, lq1, lk1, lq2, lk2, out_gain, x, mixed_dn, w_heads):
    b, t, _ = proj.shape
    assert AT_BQ == AT_CPM * AT_BK and AT_CW == AT_BK and AT_CPM == 2 and t % AT_BQ == 0
    assert DF_HEADS == DN_HEADS and DF_HEAD_V == DN_HEAD and D_MODEL % OUT_TN == 0
    assert DF_HEADS % AT_HPS == 0 and all(
        blk % AT_HPS == 0 for blk in (_BLK_DF_Q, _BLK_DF_K, _BLK_DF_Z))
    n_qblocks = t // AT_BQ
    n_blocks = AT_HPS * n_qblocks
    wide = AT_HPS * LANES
    small = lambda n: pl.BlockSpec((1, n), lambda i, h: (0, 0))
    return pl.pallas_call(
        _diff_attn_kernel,
        grid=(b, DF_HEADS // AT_HPS),
        in_specs=[
            pl.BlockSpec((None, t, wide), lambda i, h: (i, 0, _BLK_DF_Q // AT_HPS + h)),
            pl.BlockSpec((None, t, wide), lambda i, h: (i, 0, _BLK_DF_K // AT_HPS + h)),
            pl.BlockSpec((None, wide, t), lambda i, h: (i, h, 0)),
            pl.BlockSpec((None, t, wide), lambda i, h: (i, 0, _BLK_DF_Z // AT_HPS + h)),
            small(DF_HEAD_QK), small(DF_HEAD_QK), small(DF_HEAD_QK), small(DF_HEAD_QK),
            small(DF_HEAD_V),
            pl.BlockSpec((None, t, D_MODEL), lambda i, h: (i, 0, 0)),
            pl.BlockSpec((None, t, wide), lambda i, h: (i, 0, h)),
            pl.BlockSpec((AT_HPS, DF_HEAD_V + DN_HEAD, D_MODEL), lambda i, h: (h, 0, 0)),
        ],
        out_specs=pl.BlockSpec((None, t, D_MODEL), lambda i, h: (i, 0, 0)),
        out_shape=jax.ShapeDtypeStruct((b, t, D_MODEL), F32),
        scratch_shapes=[
            pltpu.VMEM((n_blocks, AT_NQ, LANES), BF16),
            pltpu.VMEM((AT_HPS, AT_VROWS, t), BF16),
            pltpu.VMEM((n_blocks, 1, AT_NQ), F32),
            pltpu.VMEM((n_blocks, AT_VROWS, AT_NQ), F32),
            pltpu.VMEM((n_blocks, AT_BQ, DF_HEAD_V), BF16),
            pltpu.VMEM((AT_BK, AT_NQ), F32),
            pltpu.VMEM((AT_BK, AT_NQ), F32),
            pltpu.VMEM((1, AT_NQ), F32),
            pltpu.VMEM((1, AT_NQ), F32),
            pltpu.VMEM((AT_BK, AT_NQ), BF16),
            pltpu.VMEM((AT_BK, AT_NQ), BF16),
        ],
        compiler_params=pltpu.CompilerParams(
            dimension_semantics=("parallel", "arbitrary"), vmem_limit_bytes=VMEM_LIMIT),
        name="diff_attention",
    )(proj, proj, vt, proj, lq1, lk1, lq2, lk2, out_gain, x, mixed_dn, w_heads)


def kernel(x, norm_gain, w_in, conv_w, a_log, dt_bias, dn_out_gain, q_gain, k_gain,
           lambda_q1, lambda_k1, lambda_q2, lambda_k2, df_out_gain, w_out):
    b, t, d = x.shape
    assert d == D_MODEL and norm_gain.shape[0] == 1
    x2d = x.reshape(b * t, d)

    w = w_in[0]
    gate_lo = 4 * DN_WIDTH
    df_lo = gate_lo + N_GATES
    w_main = jnp.concatenate([w[:, :gate_lo], w[:, df_lo:df_lo + 2 * DF_WIDTH],
                              w[:, df_lo + 3 * DF_WIDTH:]], axis=1).astype(BF16)
    w_t = jnp.concatenate([w[:, df_lo + 2 * DF_WIDTH:df_lo + 3 * DF_WIDTH],
                           jnp.pad(w[:, gate_lo:df_lo], ((0, 0), (0, GATE_ROWS - N_GATES)))],
                          axis=1).astype(BF16).T
    half = jnp.arange(SEG_W, dtype=jnp.int32) // DF_HEAD_QK
    seg = ((half[:, None] == half[None, :]).astype(F32) / DF_HEAD_QK).astype(BF16)
    reps = DF_WIDTH // DF_HEAD_QK
    qk_gain = jnp.concatenate([jnp.tile(q_gain, (1, reps)) * (DF_HEAD_QK ** -0.5 * LOG2_E),
                               jnp.tile(k_gain, (1, reps))], axis=0)

    pad_rows = lambda v: jnp.pad(v.reshape(DN_HEADS, 1),
                                 ((DN_HEADS, GATE_ROWS - 2 * DN_HEADS), (0, 0)))
    proj, vt, gates_t = _in_projection(x2d, t, norm_gain, w_main, w_t, seg, qk_gain,
                                       pad_rows(a_log), pad_rows(dt_bias), conv_w[0])
    proj = proj.reshape(b, t, PROJ_COLS)
    gates = jnp.swapaxes(gates_t[:, :N_GATES, :], 1, 2)

    mixed_dn = _gated_deltanet(proj, gates, dn_out_gain)

    wo = w_out[0].astype(BF16)
    w_heads = jnp.concatenate([wo[DN_WIDTH:].reshape(DF_HEADS, DF_HEAD_V, d),
                               wo[:DN_WIDTH].reshape(DN_HEADS, DN_HEAD, d)], axis=1)
    return _diff_attention(proj, vt, lambda_q1, lambda_k1, lambda_q2, lambda_k2,
                           df_out_gain, x, mixed_dn, w_heads)
```

```python
import functools
import math

import jax
import jax.numpy as jnp
from jax import lax
from jax.experimental import pallas as pl
from jax.experimental.pallas import tpu as pltpu

F32 = jnp.float32
BF16 = jnp.bfloat16

D_MODEL = 1024
DN_HEADS = 4
DN_HEAD = 128
DN_WIDTH = DN_HEADS * DN_HEAD
CONV_WIDTH = 4
CHUNK = 64
DF_HEADS = 4
DF_HEAD_QK = 64
DF_HEAD_V = 128
DF_WIDTH = DF_HEADS * DF_HEAD_V
EPS = 1e-6
LAMBDA_INIT = 0.8 - 0.6 * math.exp(-0.3 * 0)
LOG2_E = math.log2(math.e)

N_GATES = 2 * DN_HEADS
GATE_ROWS = 16
PROJ_COLS = 4 * DN_WIDTH + 3 * DF_WIDTH
LANES = 128
SUBLANES = 8
QKV_COLS = 3 * DN_WIDTH
VMEM_LIMIT = 56 * 1024 * 1024

_BLK_DN_Z = 3 * DN_HEADS
_BLK_DF_Q = 4 * DN_HEADS
_BLK_DF_K = _BLK_DF_Q + DF_HEADS
_BLK_DF_Z = _BLK_DF_K + DF_HEADS


def _silu(x):
    half = 0.5 * x
    return half + half * jnp.tanh(half)


def _dot(a, b):
    return jnp.dot(a.astype(BF16), b.astype(BF16), preferred_element_type=F32)


def _dot_nt(a, b):
    return lax.dot_general(a.astype(BF16), b.astype(BF16), (((1,), (1,)), ((), ())),
                           preferred_element_type=F32)


IN_TM = 1024
IN_TN = 512
SEG_W = 256
CONV_ROWS = 512


def _inproj_kernel(x_ref, gain_ref, w_ref, wt_ref, seg_ref, qkg_ref, alog_ref, dtb_ref,
                   convw_ref, proj_ref, vt_ref, gate_ref, cstage_s, tail_s, *, tiles_per_seq):
    @pl.when(pl.program_id(0) % tiles_per_seq == 0)
    def _():
        tail_s[...] = jnp.zeros_like(tail_s)

    def stage(y, j):
        for hd in range(DN_HEADS):
            yh = y[:, hd * DN_HEAD:(hd + 1) * DN_HEAD]
            cstage_s[j * DN_HEADS + hd, :SUBLANES, :] = tail_s[j * DN_HEADS + hd]
            cstage_s[j * DN_HEADS + hd, SUBLANES:, :] = yh
            tail_s[j * DN_HEADS + hd] = yh[IN_TM - SUBLANES:, :]

    def conv_silu_norm(j, part):
        r0 = SUBLANES + part * CONV_ROWS
        for hd in range(DN_HEADS):
            cols = slice(j * IN_TN + hd * DN_HEAD, j * IN_TN + (hd + 1) * DN_HEAD)
            buf = cstage_s.at[j * DN_HEADS + hd]
            w = convw_ref[:, cols]
            acc = buf[r0:r0 + CONV_ROWS, :] * w[CONV_WIDTH - 1:CONV_WIDTH]
            for d in range(1, CONV_WIDTH):
                acc = acc + buf[r0 - d:r0 - d + CONV_ROWS, :] * w[CONV_WIDTH - 1 - d:CONV_WIDTH - d]
            a = _silu(acc)
            if j < 2:
                inv_norm = lax.rsqrt(jnp.sum(a * a, axis=-1, keepdims=True) + EPS)
                a = a * (inv_norm * (DN_HEAD ** -0.5) if j == 0 else inv_norm)
            proj_ref[part * CONV_ROWS:(part + 1) * CONV_ROWS, cols] = a.astype(BF16)

    def head_phase():
        x = x_ref[...]
        ms = jnp.mean(x * x, axis=-1, keepdims=True)
        h = (x * lax.rsqrt(ms + EPS) * gain_ref[...]).astype(BF16)
        yt = _dot_nt(wt_ref[...], h)
        vt_ref[...] = yt[:DF_WIDTH].astype(BF16)
        pre = yt[DF_WIDTH:]
        g = -jnp.exp(alog_ref[...]) * jax.nn.softplus(pre + dtb_ref[...])
        pos = lax.broadcasted_iota(jnp.int32, g.shape, 1) % CHUNK
        s = 1
        while s < CHUNK:
            g = g + jnp.where(pos >= s, pltpu.roll(g, s, 1), 0.0)
            s *= 2
        row = lax.broadcasted_iota(jnp.int32, g.shape, 0)
        gate_ref[...] = jnp.where(row < DN_HEADS, jax.nn.sigmoid(pre), g)
        return h

    def chunk_phase(j, conv_items):
        cols = slice(j * IN_TN, (j + 1) * IN_TN)
        y = jnp.dot(h, w_ref[:, cols], preferred_element_type=F32)
        for item in conv_items:
            conv_silu_norm(*item)
        if j < QKV_COLS // IN_TN:
            stage(y, j)
            return
        if j in (_BLK_DF_Q * LANES // IN_TN, _BLK_DF_K * LANES // IN_TN):
            which = 0 if j == _BLK_DF_Q * LANES // IN_TN else 1
            sq = (y * y).astype(BF16)
            msq = jnp.concatenate(
                [jnp.dot(sq[:, i * SEG_W:(i + 1) * SEG_W], seg_ref[...],
                         preferred_element_type=F32) for i in range(IN_TN // SEG_W)], axis=1)
            y = y * lax.rsqrt(msq + EPS) * qkg_ref[which:which + 1, :]
        elif j in (_BLK_DN_Z * LANES // IN_TN, _BLK_DF_Z * LANES // IN_TN):
            y = _silu(y)
        proj_ref[:, cols] = y.astype(BF16)

    h = head_phase()
    n_chunks = PROJ_COLS // IN_TN
    per_phase = -(-(QKV_COLS // IN_TN) * (IN_TM // CONV_ROWS) // (n_chunks - 1))
    pending = []
    for j in range(n_chunks):
        items = [pending.pop(0) for _ in range(min(per_phase, len(pending)))]
        chunk_phase(j, items)
        if j < QKV_COLS // IN_TN:
            pending += [(j, part) for part in range(IN_TM // CONV_ROWS)]
    for item in pending:
        conv_silu_norm(*item)


def _in_projection(x2d, t, gain, w_main, w_t, seg, qk_gain, alog_col, dtb_col, conv_w):
    m = x2d.shape[0]
    assert DF_WIDTH == IN_TN and DN_WIDTH == IN_TN and t % IN_TM == 0 and IN_TM % CHUNK == 0
    tiles_per_seq = t // IN_TM
    const = lambda shape: pl.BlockSpec(shape, lambda i: (0, 0))
    return pl.pallas_call(
        functools.partial(_inproj_kernel, tiles_per_seq=tiles_per_seq),
        grid=(m // IN_TM,),
        in_specs=[
            pl.BlockSpec((IN_TM, D_MODEL), lambda i: (i, 0)),
            const((1, D_MODEL)),
            const((D_MODEL, PROJ_COLS)),
            const((DF_WIDTH + GATE_ROWS, D_MODEL)),
            const((SEG_W, SEG_W)),
            const((2, DF_WIDTH)),
            const((GATE_ROWS, 1)),
            const((GATE_ROWS, 1)),
            const((CONV_WIDTH, QKV_COLS)),
        ],
        out_specs=[
            pl.BlockSpec((IN_TM, PROJ_COLS), lambda i: (i, 0)),
            pl.BlockSpec((None, DF_WIDTH, IN_TM),
                         lambda i: (i // tiles_per_seq, 0, i % tiles_per_seq)),
            pl.BlockSpec((None, GATE_ROWS, IN_TM),
                         lambda i: (i // tiles_per_seq, 0, i % tiles_per_seq)),
        ],
        out_shape=[
            jax.ShapeDtypeStruct((m, PROJ_COLS), BF16),
            jax.ShapeDtypeStruct((m // t, DF_WIDTH, t), BF16),
            jax.ShapeDtypeStruct((m // t, GATE_ROWS, t), F32),
        ],
        scratch_shapes=[
            pltpu.VMEM((QKV_COLS // LANES, IN_TM + SUBLANES, LANES), F32),
            pltpu.VMEM((QKV_COLS // LANES, SUBLANES, LANES), F32),
        ],
        compiler_params=pltpu.CompilerParams(
            dimension_semantics=("arbitrary",), vmem_limit_bytes=VMEM_LIMIT),
        name="in_projection",
    )(x2d, gain, w_main, w_t, seg, qk_gain, alog_col, dtb_col, conv_w)


PREP_CHUNKS = 4


def _gdn_kernel(qkv_ref, zg_ref, gate_ref, gain_ref,
                out_ref, u_r, wq_r, aqk_r, kdt_r, state_s):
    t = qkv_ref.shape[0]
    n_chunks = t // CHUNK
    group_rows = PREP_CHUNKS * CHUNK

    def group_base(g):
        return g * group_rows if isinstance(g, int) else pl.multiple_of(g * group_rows, group_rows)

    row = lax.broadcasted_iota(jnp.int32, (CHUNK, 2 * CHUNK), 0)
    col = lax.broadcasted_iota(jnp.int32, (CHUNK, 2 * CHUNK), 1)
    eye_f = (row == col).astype(F32)
    lower_f = (row >= col).astype(F32)
    strict_f = (row > col).astype(F32)
    upper_f = (col >= CHUNK).astype(F32)
    eye_hi = (col - CHUNK == row).astype(F32)
    zero_rows = jnp.zeros((CHUNK, DN_HEAD), BF16)

    items = [(cc, h) for cc in range(PREP_CHUNKS) for h in range(DN_HEADS)]
    heads = range(DN_HEADS)
    gain = gain_ref[...]
    hcols = [slice(h * LANES, (h + 1) * LANES) for h in heads]
    hcols_c = [slice(h * LANES, h * LANES + CHUNK) for h in heads]

    def halves(fn, seq):
        out = []
        mid = len(seq) // 2
        for part in (seq[:mid], seq[mid:]):
            out.extend(fn(x) for x in part)
            yield out

    def prep(g, slot):
        base = group_base(g)
        gates = [gate_ref[pl.ds(base + cc * CHUNK, CHUNK), :] for cc in range(PREP_CHUNKS)]
        ks, lhs, decay, rhs, qe, kdt = [], [], [], [], [], []
        for cc, h in items:
            trows = pl.ds(base + cc * CHUNK, CHUNK)
            q = qkv_ref[trows, hcols[h]].astype(F32)
            k = qkv_ref[trows, slice(DN_WIDTH + h * LANES, DN_WIDTH + (h + 1) * LANES)].astype(F32)
            v = qkv_ref[trows, slice(2 * DN_WIDTH + h * LANES,
                                     2 * DN_WIDTH + (h + 1) * LANES)].astype(F32)
            beta_b = jnp.broadcast_to(gates[cc][:, h:h + 1], (CHUNK, DN_HEAD))
            gcb = jnp.broadcast_to(gates[cc][:, DN_HEADS + h:DN_HEADS + h + 1], (CHUNK, DN_HEAD))
            gr = jnp.sum(gcb * eye_f, axis=0, keepdims=True)
            decay.append(jnp.exp(jnp.minimum(gcb - gr, 0.0)))
            g_last = gcb[CHUNK - 1:CHUNK, :]
            eg = jnp.exp(gcb)
            kb = k * beta_b
            ks.append(jnp.concatenate([k.astype(BF16), zero_rows], axis=0))
            lhs.append(jnp.concatenate([kb, q], axis=0).astype(BF16))
            rhs.append(jnp.concatenate([v * beta_b, kb * eg], axis=1).astype(BF16))
            qe.append((q * eg).astype(BF16))
            kdt.append((k * jnp.exp(g_last - gcb)).T.astype(BF16))
        yield

        kq = None
        for kq in halves(lambda ab: _dot_nt(*ab), list(zip(lhs, ks))):
            yield
        m = [x[:CHUNK] * (d * strict_f) for x, d in zip(kq, decay)]
        a_qk = [(x[CHUNK:, :CHUNK] * (d * lower_f)[:, :CHUNK]).astype(BF16)
                for x, d in zip(kq, decay)]

        w = None
        for w in halves(lambda x: _dot(x[:, :CHUNK], x - eye_hi) + eye_hi, m):
            yield
        for _ in range(5):
            prev = w
            for w in halves(lambda x: _dot(x[:, :CHUNK], x) + x * upper_f, prev):
                yield
        sol = None
        for sol in halves(lambda xb: _dot(xb[0][:, CHUNK:], xb[1]), list(zip(w, rhs))):
            yield

        for n, (cc, h) in enumerate(items):
            rows = slice(cc * CHUNK, (cc + 1) * CHUNK)
            rows_w = slice(2 * cc * CHUNK, (2 * cc + 1) * CHUNK)
            rows_q = slice((2 * cc + 1) * CHUNK, (2 * cc + 2) * CHUNK)
            rows_t = slice(2 * cc * CHUNK, (2 * cc + 2) * CHUNK)
            u_r[slot, rows, hcols[h]] = sol[n][:, :DN_HEAD]
            wq_r[slot, rows_w, hcols[h]] = sol[n][:, DN_HEAD:].astype(BF16)
            wq_r[slot, rows_q, hcols[h]] = qe[n]
            aqk_r[slot, rows, hcols_c[h]] = a_qk[n]
            kdt_r[slot, rows_t, hcols_c[h]] = kdt[n]
        yield

    def rec(g, slot):
        base = group_base(g)
        for cc in range(PREP_CHUNKS):
            rows = slice(cc * CHUNK, (cc + 1) * CHUNK)
            rows2 = slice(2 * cc * CHUNK, (2 * cc + 2) * CHUNK)
            trows = pl.ds(base + cc * CHUNK, CHUNK)
            st = [state_s[h] for h in heads]
            r = [jnp.dot(wq_r[slot, rows2, hcols[h]], st[h].astype(BF16),
                         preferred_element_type=F32) for h in heads]
            yield
            v_new = [(u_r[slot, rows, hcols[h]] - r[h][:CHUNK]).astype(BF16) for h in heads]
            upd = [jnp.dot(kdt_r[slot, rows2, hcols_c[h]], v_new[h], preferred_element_type=F32)
                   for h in heads]
            tail = gate_ref[pl.ds(base + (cc + 1) * CHUNK - 8, 8), :]
            decay_last = jnp.exp(tail[7:8, :])
            for h in heads:
                state_s[h] = st[h] * decay_last[:, DN_HEADS + h:DN_HEADS + h + 1] + upd[h]
            yield
            o = [r[h][CHUNK:] + jnp.dot(aqk_r[slot, rows, hcols_c[h]], v_new[h],
                                        preferred_element_type=F32) for h in heads]
            for h in heads:
                y = o[h] * lax.rsqrt(jnp.mean(o[h] * o[h], axis=-1, keepdims=True) + EPS) * gain
                gate = zg_ref[trows, hcols[h]].astype(F32)
                out_ref[trows, hcols[h]] = (y * gate).astype(out_ref.dtype)
            yield

    def run_interleaved(*gens):
        gens = list(gens)
        while gens:
            for gen in list(gens):
                if next(gen, StopIteration) is StopIteration:
                    gens.remove(gen)

    n_groups = n_chunks // PREP_CHUNKS
    state_s[...] = jnp.zeros_like(state_s)
    run_interleaved(prep(0, 0))

    def pair_body(j, carry):
        run_interleaved(rec(2 * j, 0), prep(2 * j + 1, 1))
        run_interleaved(rec(2 * j + 1, 1), prep(2 * j + 2, 0))
        return carry

    lax.fori_loop(0, (n_groups - 2) // 2, pair_body, 0)
    run_interleaved(rec(n_groups - 2, 0), prep(n_groups - 1, 1))
    run_interleaved(rec(n_groups - 1, 1))


def _gated_deltanet(proj, gates, out_gain):
    b, t, _ = proj.shape
    group_rows = PREP_CHUNKS * CHUNK
    assert t % (2 * group_rows) == 0 and t // group_rows >= 4
    ring = lambda rows, dtype: pltpu.VMEM((2, rows, DN_WIDTH), dtype)
    return pl.pallas_call(
        _gdn_kernel,
        grid=(b,),
        in_specs=[
            pl.BlockSpec((None, t, QKV_COLS), lambda i: (i, 0, 0)),
            pl.BlockSpec((None, t, DN_WIDTH), lambda i: (i, 0, QKV_COLS // DN_WIDTH)),
            pl.BlockSpec((None, t, N_GATES), lambda i: (i, 0, 0)),
            pl.BlockSpec((1, DN_HEAD), lambda i: (0, 0)),
        ],
        out_specs=pl.BlockSpec((None, t, DN_WIDTH), lambda i: (i, 0, 0)),
        out_shape=jax.ShapeDtypeStruct((b, t, DN_WIDTH), BF16),
        scratch_shapes=[
            ring(group_rows, F32),
            ring(2 * group_rows, BF16),
            ring(group_rows, BF16),
            ring(2 * group_rows, BF16),
            pltpu.VMEM((DN_HEADS, DN_HEAD, DN_HEAD), F32),
        ],
        compiler_params=pltpu.CompilerParams(
            dimension_semantics=("parallel",), vmem_limit_bytes=VMEM_LIMIT),
        name="gated_deltanet",
    )(proj, proj, gates, out_gain)


AT_BQ = 512
AT_BK = 256
AT_CW = 256
AT_NQ = 2 * AT_BQ
AT_CHAINS = AT_NQ // AT_CW
AT_CPM = AT_BQ // AT_CW
AT_VROWS = DF_HEAD_V + 16
AT_PROJ_DELAY = 1
AT_HPS = 2
OUT_TN = 256


def _attention_tiles(n_heads, n_qblocks):
    all_chains = tuple(range(AT_CHAINS))
    lo = tuple(c for c in all_chains if c % AT_CPM == 0)
    hi = tuple(c for c in all_chains if c % AT_CPM == 1)
    tiles = []
    for hd in range(n_heads):
        for qi in range(n_qblocks):
            for kt in range(AT_CPM * qi + AT_CPM):
                d = kt - AT_CPM * qi
                chains, masked = ((all_chains, ()) if d < 0 else
                                  (all_chains, lo) if d == 0 else (hi, hi))
                tiles.append(dict(hd=hd, qi=qi, qb=hd * n_qblocks + qi, kt=kt, chains=chains,
                                  masked=masked, first=kt == 0))
    return tiles


def _diff_attn_kernel(q_ref, k_ref, vt_ref, zg_ref, lq1_ref, lk1_ref, lq2_ref, lk2_ref,
                      og_ref, x_ref, dn_ref, wout_ref, out_ref,
                      qq_s, vta_s, m_s, acc_s, y_s, s_a, s_b, c_a, c_b, p_a, p_b):
    t = q_ref.shape[0]
    n_qblocks = t // AT_BQ
    all_chains = tuple(range(AT_CHAINS))

    @pl.when(pl.program_id(1) == 0)
    def _():
        out_ref[...] = x_ref[...]

    def lanes(c):
        return slice(c * AT_CW, (c + 1) * AT_CW)

    first = lax.broadcasted_iota(jnp.int32, (AT_BQ, LANES), 1) < DF_HEAD_QK
    zero = jnp.zeros((AT_BQ, LANES), BF16)
    hcols = [slice(hd * LANES, (hd + 1) * LANES) for hd in range(AT_HPS)]
    for hd in range(AT_HPS):
        for qi in range(n_qblocks):
            blk = q_ref[qi * AT_BQ:(qi + 1) * AT_BQ, hcols[hd]]
            qq_s[hd * n_qblocks + qi, :AT_BQ, :] = jnp.where(first, blk, zero)
            qq_s[hd * n_qblocks + qi, AT_BQ:, :] = jnp.where(first, zero, blk)
        vta_s[hd, :DF_HEAD_V, :] = vt_ref[hcols[hd], :]
        vta_s[hd, DF_HEAD_V:, :] = jnp.ones((AT_VROWS - DF_HEAD_V, t), BF16)

    lam = (jnp.exp(jnp.sum(lq1_ref[...] * lk1_ref[...], axis=-1, keepdims=True))
           - jnp.exp(jnp.sum(lq2_ref[...] * lk2_ref[...], axis=-1, keepdims=True))
           + LAMBDA_INIT)
    out_scale = og_ref[...] * (1.0 - LAMBDA_INIT)

    krow = lax.broadcasted_iota(jnp.int32, (AT_BK, AT_CW), 0)
    qcol = lax.broadcasted_iota(jnp.int32, (AT_BK, AT_CW), 1)
    causal = krow <= qcol
    bufs = ((s_a, c_a, p_a), (s_b, c_b, p_b))

    def scores(tile, c, buf):
        rows = slice(tile["kt"] * AT_BK, (tile["kt"] + 1) * AT_BK)
        s = _dot_nt(k_ref[rows, hcols[tile["hd"]]], qq_s[tile["qb"], lanes(c), :])
        if c in tile["masked"]:
            s = jnp.where(causal, s, -jnp.inf)
        buf[0][:, lanes(c)] = s
        buf[1][:, lanes(c)] = jnp.max(s, axis=0, keepdims=True)

    def values(tile, c, buf):
        rows = slice(tile["kt"] * AT_BK, (tile["kt"] + 1) * AT_BK)
        return jnp.dot(vta_s[tile["hd"], :, rows], buf[2][:, lanes(c)],
                       preferred_element_type=F32)

    def finish(tile):
        qb = tile["qb"]
        rows = slice(tile["qi"] * AT_BQ, (tile["qi"] + 1) * AT_BQ)
        inv = 1.0 / acc_s[qb, DF_HEAD_V:DF_HEAD_V + 1, :]
        o = (acc_s[qb, :DF_HEAD_V, :AT_BQ] * inv[:, :AT_BQ]
             - acc_s[qb, :DF_HEAD_V, AT_BQ:] * (lam * inv[:, AT_BQ:]))
        y = (o * lax.rsqrt(jnp.mean(o * o, axis=0, keepdims=True) + EPS)).T
        y = y * out_scale
        y_s[qb] = (y * zg_ref[rows, hcols[tile["hd"]]].astype(F32)).astype(BF16)

    def project(tile):
        rows = slice(tile["qi"] * AT_BQ, (tile["qi"] + 1) * AT_BQ)
        hd = tile["hd"]
        lhs = jnp.concatenate([y_s[tile["qb"]], dn_ref[rows, hcols[hd]]], axis=1)
        for nc in range(D_MODEL // OUT_TN):
            cols = slice(nc * OUT_TN, (nc + 1) * OUT_TN)
            out_ref[rows, cols] = out_ref[rows, cols] + jnp.dot(
                lhs, wout_ref[hd, :, cols], preferred_element_type=F32)

    tiles = _attention_tiles(AT_HPS, n_qblocks)
    due = {}
    for c in tiles[0]["chains"]:
        scores(tiles[0], c, bufs[0])
    for n, cur in enumerate(tiles):
        nxt = tiles[n + 1] if n + 1 < len(tiles) else None
        prv = tiles[n - 1] if n > 0 else None
        s_cur, c_cur, p_cur = bufs[n % 2]
        qb = cur["qb"]
        for c in all_chains:
            alpha = None
            if c in cur["chains"]:
                if cur["first"]:
                    m_new = c_cur[:, lanes(c)]
                else:
                    m_old = m_s[qb, :, lanes(c)]
                    m_new = jnp.maximum(m_old, c_cur[:, lanes(c)])
                    alpha = jnp.exp2(m_old - m_new)
                p = jnp.exp2(s_cur[:, lanes(c)] - m_new)
                m_s[qb, :, lanes(c)] = m_new
                p_cur[:, lanes(c)] = p.astype(BF16)
            if nxt is not None and c in nxt["chains"]:
                scores(nxt, c, bufs[(n + 1) % 2])
            if prv is not None and c in prv["chains"]:
                pv = values(prv, c, bufs[(n - 1) % 2])
                pq = prv["qb"]
                acc = pv if prv["first"] else acc_s[pq, :, lanes(c)] + pv
                if pq == qb and alpha is not None:
                    acc = alpha * acc
                acc_s[pq, :, lanes(c)] = acc
        if prv is not None and prv["qb"] != qb:
            finish(prv)
            due[n + AT_PROJ_DELAY] = prv
        if n in due:
            project(due.pop(n))
    last = tiles[-1]
    for c in last["chains"]:
        acc_s[last["qb"], :, lanes(c)] = (acc_s[last["qb"], :, lanes(c)]
                                          + values(last, c, bufs[(len(tiles) - 1) % 2]))
    finish(last)
    for tile in sorted(due.values(), key=lambda tl: tl["qb"]) + [last]:
        project(tile)


def _diff_attention(proj, vt, lq1, lk1, lq2, lk2, out_gain, x, mixed_dn, w_heads):
    b, t, _ = proj.shape
    assert AT_BQ == AT_CPM * AT_BK and AT_CW == AT_BK and AT_CPM == 2 and t % AT_BQ == 0
    assert DF_HEADS == DN_HEADS and DF_HEAD_V == DN_HEAD and D_MODEL % OUT_TN == 0
    assert DF_HEADS % AT_HPS == 0 and all(
        blk % AT_HPS == 0 for blk in (_BLK_DF_Q, _BLK_DF_K, _BLK_DF_Z))
    n_qblocks = t // AT_BQ
    n_blocks = AT_HPS * n_qblocks
    wide = AT_HPS * LANES
    small = lambda n: pl.BlockSpec((1, n), lambda i, h: (0, 0))
    return pl.pallas_call(
        _diff_attn_kernel,
        grid=(b, DF_HEADS // AT_HPS),
        in_specs=[
            pl.BlockSpec((None, t, wide), lambda i, h: (i, 0, _BLK_DF_Q // AT_HPS + h)),
            pl.BlockSpec((None, t, wide), lambda i, h: (i, 0, _BLK_DF_K // AT_HPS + h)),
            pl.BlockSpec((None, wide, t), lambda i, h: (i, h, 0)),
            pl.BlockSpec((None, t, wide), lambda i, h: (i, 0, _BLK_DF_Z // AT_HPS + h)),
            small(DF_HEAD_QK), small(DF_HEAD_QK), small(DF_HEAD_QK), small(DF_HEAD_QK),
            small(DF_HEAD_V),
            pl.BlockSpec((None, t, D_MODEL), lambda i, h: (i, 0, 0)),
            pl.BlockSpec((None, t, wide), lambda i, h: (i, 0, h)),
            pl.BlockSpec((AT_HPS, DF_HEAD_V + DN_HEAD, D_MODEL), lambda i, h: (h, 0, 0)),
        ],
        out_specs=pl.BlockSpec((None, t, D_MODEL), lambda i, h: (i, 0, 0)),
        out_shape=jax.ShapeDtypeStruct((b, t, D_MODEL), F32),
        scratch_shapes=[
            pltpu.VMEM((n_blocks, AT_NQ, LANES), BF16),
            pltpu.VMEM((AT_HPS, AT_VROWS, t), BF16),
            pltpu.VMEM((n_blocks, 1, AT_NQ), F32),
            pltpu.VMEM((n_blocks, AT_VROWS, AT_NQ), F32),
            pltpu.VMEM((n_blocks, AT_BQ, DF_HEAD_V), BF16),
            pltpu.VMEM((AT_BK, AT_NQ), F32),
            pltpu.VMEM((AT_BK, AT_NQ), F32),
            pltpu.VMEM((1, AT_NQ), F32),
            pltpu.VMEM((1, AT_NQ), F32),
            pltpu.VMEM((AT_BK, AT_NQ), BF16),
            pltpu.VMEM((AT_BK, AT_NQ), BF16),
        ],
        compiler_params=pltpu.CompilerParams(
            dimension_semantics=("parallel", "arbitrary"), vmem_limit_bytes=VMEM_LIMIT),
        name="diff_attention",
    )(proj, proj, vt, proj, lq1, lk1, lq2, lk2, out_gain, x, mixed_dn, w_heads)


def kernel(x, norm_gain, w_in, conv_w, a_log, dt_bias, dn_out_gain, q_gain, k_gain,
           lambda_q1, lambda_k1, lambda_q2, lambda_k2, df_out_gain, w_out):
    b, t, d = x.shape
    assert d == D_MODEL and norm_gain.shape[0] == 1
    x2d = x.reshape(b * t, d)

    w = w_in[0]
    gate_lo = 4 * DN_WIDTH
    df_lo = gate_lo + N_GATES
    w_main = jnp.concatenate([w[:, :gate_lo], w[:, df_lo:df_lo + 2 * DF_WIDTH],
                              w[:, df_lo + 3 * DF_WIDTH:]], axis=1).astype(BF16)
    w_t = jnp.concatenate([w[:, df_lo + 2 * DF_WIDTH:df_lo + 3 * DF_WIDTH],
                           jnp.pad(w[:, gate_lo:df_lo], ((0, 0), (0, GATE_ROWS - N_GATES)))],
                          axis=1).astype(BF16).T
    half = jnp.arange(SEG_W, dtype=jnp.int32) // DF_HEAD_QK
    seg = ((half[:, None] == half[None, :]).astype(F32) / DF_HEAD_QK).astype(BF16)
    reps = DF_WIDTH // DF_HEAD_QK
    qk_gain = jnp.concatenate([jnp.tile(q_gain, (1, reps)) * (DF_HEAD_QK ** -0.5 * LOG2_E),
                               jnp.tile(k_gain, (1, reps))], axis=0)

    pad_rows = lambda v: jnp.pad(v.reshape(DN_HEADS, 1),
                                 ((DN_HEADS, GATE_ROWS - 2 * DN_HEADS), (0, 0)))
    proj, vt, gates_t = _in_projection(x2d, t, norm_gain, w_main, w_t, seg, qk_gain,
                                       pad_rows(a_log), pad_rows(dt_bias), conv_w[0])
    proj = proj.reshape(b, t, PROJ_COLS)
    gates = jnp.swapaxes(gates_t[:, :N_GATES, :], 1, 2)

    mixed_dn = _gated_deltanet(proj, gates, dn_out_gain)

    wo = w_out[0].astype(BF16)
    w_heads = jnp.concatenate([wo[DN_WIDTH:].reshape(DF_HEADS, DF_HEAD_V, d),
                               wo[:DN_WIDTH].reshape(DN_HEADS, DN_HEAD, d)], axis=1)
    return _diff_attention(proj, vt, lambda_q1, lambda_k1, lambda_q2, lambda_k2,
                           df_out_gain, x, mixed_dn, w_heads)
```

```python
import functools
import math

import jax
import jax.numpy as jnp
from jax import lax
from jax.experimental import pallas as pl
from jax.experimental.pallas import tpu as pltpu

F32 = jnp.float32
BF16 = jnp.bfloat16

D_MODEL = 1024
DN_HEADS = 4
DN_HEAD = 128
DN_WIDTH = DN_HEADS * DN_HEAD
CONV_WIDTH = 4
CHUNK = 64
DF_HEADS = 4
DF_HEAD_QK = 64
DF_HEAD_V = 128
DF_WIDTH = DF_HEADS * DF_HEAD_V
EPS = 1e-6
LAMBDA_INIT = 0.8 - 0.6 * math.exp(-0.3 * 0)
LOG2_E = math.log2(math.e)

N_GATES = 2 * DN_HEADS
GATE_ROWS = 16
PROJ_COLS = 4 * DN_WIDTH + 3 * DF_WIDTH
LANES = 128
SUBLANES = 8
QKV_COLS = 3 * DN_WIDTH
VMEM_LIMIT = 56 * 1024 * 1024

_BLK_DN_Z = 3 * DN_HEADS
_BLK_DF_Q = 4 * DN_HEADS
_BLK_DF_K = _BLK_DF_Q + DF_HEADS
_BLK_DF_Z = _BLK_DF_K + DF_HEADS


def _silu(x):
    half = 0.5 * x
    return half + half * jnp.tanh(half)


def _dot(a, b):
    return jnp.dot(a.astype(BF16), b.astype(BF16), preferred_element_type=F32)


def _dot_nt(a, b):
    return lax.dot_general(a.astype(BF16), b.astype(BF16), (((1,), (1,)), ((), ())),
                           preferred_element_type=F32)


IN_TM = 1024
IN_TN = 512
SEG_W = 256
CONV_ROWS = 512


def _inproj_kernel(x_ref, gain_ref, w_ref, wt_ref, seg_ref, qkg_ref, alog_ref, dtb_ref,
                   convw_ref, proj_ref, vt_ref, gate_ref, cstage_s, tail_s, *, tiles_per_seq):
    @pl.when(pl.program_id(0) % tiles_per_seq == 0)
    def _():
        tail_s[...] = jnp.zeros_like(tail_s)

    def stage(y, j):
        for hd in range(DN_HEADS):
            yh = y[:, hd * DN_HEAD:(hd + 1) * DN_HEAD]
            cstage_s[j * DN_HEADS + hd, :SUBLANES, :] = tail_s[j * DN_HEADS + hd]
            cstage_s[j * DN_HEADS + hd, SUBLANES:, :] = yh
            tail_s[j * DN_HEADS + hd] = yh[IN_TM - SUBLANES:, :]

    def conv_silu_norm(j, part):
        r0 = SUBLANES + part * CONV_ROWS
        for hd in range(DN_HEADS):
            cols = slice(j * IN_TN + hd * DN_HEAD, j * IN_TN + (hd + 1) * DN_HEAD)
            buf = cstage_s.at[j * DN_HEADS + hd]
            w = convw_ref[:, cols]
            acc = buf[r0:r0 + CONV_ROWS, :] * w[CONV_WIDTH - 1:CONV_WIDTH]
            for d in range(1, CONV_WIDTH):
                acc = acc + buf[r0 - d:r0 - d + CONV_ROWS, :] * w[CONV_WIDTH - 1 - d:CONV_WIDTH - d]
            a = _silu(acc)
            if j < 2:
                inv_norm = lax.rsqrt(jnp.sum(a * a, axis=-1, keepdims=True) + EPS)
                a = a * (inv_norm * (DN_HEAD ** -0.5) if j == 0 else inv_norm)
            proj_ref[part * CONV_ROWS:(part + 1) * CONV_ROWS, cols] = a.astype(BF16)

    def head_phase():
        x = x_ref[...]
        ms = jnp.mean(x * x, axis=-1, keepdims=True)
        h = (x * lax.rsqrt(ms + EPS) * gain_ref[...]).astype(BF16)
        yt = _dot_nt(wt_ref[...], h)
        vt_ref[...] = yt[:DF_WIDTH].astype(BF16)
        pre = yt[DF_WIDTH:]
        g = -jnp.exp(alog_ref[...]) * jax.nn.softplus(pre + dtb_ref[...])
        pos = lax.broadcasted_iota(jnp.int32, g.shape, 1) % CHUNK
        s = 1
        while s < CHUNK:
            g = g + jnp.where(pos >= s, pltpu.roll(g, s, 1), 0.0)
            s *= 2
        row = lax.broadcasted_iota(jnp.int32, g.shape, 0)
        gate_ref[...] = jnp.where(row < DN_HEADS, jax.nn.sigmoid(pre), g)
        return h

    def chunk_phase(j, conv_items):
        cols = slice(j * IN_TN, (j + 1) * IN_TN)
        y = jnp.dot(h, w_ref[:, cols], preferred_element_type=F32)
        for item in conv_items:
            conv_silu_norm(*item)
        if j < QKV_COLS // IN_TN:
            stage(y, j)
            return
        if j in (_BLK_DF_Q * LANES // IN_TN, _BLK_DF_K * LANES // IN_TN):
            which = 0 if j == _BLK_DF_Q * LANES // IN_TN else 1
            sq = (y * y).astype(BF16)
            msq = jnp.concatenate(
                [jnp.dot(sq[:, i * SEG_W:(i + 1) * SEG_W], seg_ref[...],
                         preferred_element_type=F32) for i in range(IN_TN // SEG_W)], axis=1)
            y = y * lax.rsqrt(msq + EPS) * qkg_ref[which:which + 1, :]
        elif j in (_BLK_DN_Z * LANES // IN_TN, _BLK_DF_Z * LANES // IN_TN):
            y = _silu(y)
        proj_ref[:, cols] = y.astype(BF16)

    h = head_phase()
    n_chunks = PROJ_COLS // IN_TN
    per_phase = -(-(QKV_COLS // IN_TN) * (IN_TM // CONV_ROWS) // (n_chunks - 1))
    pending = []
    for j in range(n_chunks):
        items = [pending.pop(0) for _ in range(min(per_phase, len(pending)))]
        chunk_phase(j, items)
        if j < QKV_COLS // IN_TN:
            pending += [(j, part) for part in range(IN_TM // CONV_ROWS)]
    for item in pending:
        conv_silu_norm(*item)


def _in_projection(x2d, t, gain, w_main, w_t, seg, qk_gain, alog_col, dtb_col, conv_w):
    m = x2d.shape[0]
    assert DF_WIDTH == IN_TN and DN_WIDTH == IN_TN and t % IN_TM == 0 and IN_TM % CHUNK == 0
    tiles_per_seq = t // IN_TM
    const = lambda shape: pl.BlockSpec(shape, lambda i: (0, 0))
    return pl.pallas_call(
        functools.partial(_inproj_kernel, tiles_per_seq=tiles_per_seq),
        grid=(m // IN_TM,),
        in_specs=[
            pl.BlockSpec((IN_TM, D_MODEL), lambda i: (i, 0)),
            const((1, D_MODEL)),
            const((D_MODEL, PROJ_COLS)),
            const((DF_WIDTH + GATE_ROWS, D_MODEL)),
            const((SEG_W, SEG_W)),
            const((2, DF_WIDTH)),
            const((GATE_ROWS, 1)),
            const((GATE_ROWS, 1)),
            const((CONV_WIDTH, QKV_COLS)),
        ],
        out_specs=[
            pl.BlockSpec((IN_TM, PROJ_COLS), lambda i: (i, 0)),
            pl.BlockSpec((None, DF_WIDTH, IN_TM),
                         lambda i: (i // tiles_per_seq, 0, i % tiles_per_seq)),
            pl.BlockSpec((None, GATE_ROWS, IN_TM),
                         lambda i: (i // tiles_per_seq, 0, i % tiles_per_seq)),
        ],
        out_shape=[
            jax.ShapeDtypeStruct((m, PROJ_COLS), BF16),
            jax.ShapeDtypeStruct((m // t, DF_WIDTH, t), BF16),
            jax.ShapeDtypeStruct((m // t, GATE_ROWS, t), F32),
        ],
        scratch_shapes=[
            pltpu.VMEM((QKV_COLS // LANES, IN_TM + SUBLANES, LANES), F32),
            pltpu.VMEM((QKV_COLS // LANES, SUBLANES, LANES), F32),
        ],
        compiler_params=pltpu.CompilerParams(
            dimension_semantics=("arbitrary",), vmem_limit_bytes=VMEM_LIMIT),
        name="in_projection",
    )(x2d, gain, w_main, w_t, seg, qk_gain, alog_col, dtb_col, conv_w)


PREP_CHUNKS = 4


def _gdn_kernel(qkv_ref, zg_ref, gate_ref, gain_ref,
                out_ref, u_r, wq_r, aqk_r, kdt_r, state_s):
    t = qkv_ref.shape[0]
    n_chunks = t // CHUNK
    group_rows = PREP_CHUNKS * CHUNK

    def group_base(g):
        return g * group_rows if isinstance(g, int) else pl.multiple_of(g * group_rows, group_rows)

    row = lax.broadcasted_iota(jnp.int32, (CHUNK, 2 * CHUNK), 0)
    col = lax.broadcasted_iota(jnp.int32, (CHUNK, 2 * CHUNK), 1)
    eye_f = (row == col).astype(F32)
    lower_f = (row >= col).astype(F32)
    strict_f = (row > col).astype(F32)
    upper_f = (col >= CHUNK).astype(F32)
    eye_hi = (col - CHUNK == row).astype(F32)
    zero_rows = jnp.zeros((CHUNK, DN_HEAD), BF16)

    items = [(cc, h) for cc in range(PREP_CHUNKS) for h in range(DN_HEADS)]
    heads = range(DN_HEADS)
    gain = gain_ref[...]
    hcols = [slice(h * LANES, (h + 1) * LANES) for h in heads]
    hcols_c = [slice(h * LANES, h * LANES + CHUNK) for h in heads]

    def halves(fn, seq):
        out = []
        mid = len(seq) // 2
        for part in (seq[:mid], seq[mid:]):
            out.extend(fn(x) for x in part)
            yield out

    def prep(g, slot):
        base = group_base(g)
        gates = [gate_ref[pl.ds(base + cc * CHUNK, CHUNK), :] for cc in range(PREP_CHUNKS)]
        ks, lhs, decay, rhs, qe, kdt = [], [], [], [], [], []
        for cc, h in items:
            trows = pl.ds(base + cc * CHUNK, CHUNK)
            q = qkv_ref[trows, hcols[h]].astype(F32)
            k = qkv_ref[trows, slice(DN_WIDTH + h * LANES, DN_WIDTH + (h + 1) * LANES)].astype(F32)
            v = qkv_ref[trows, slice(2 * DN_WIDTH + h * LANES,
                                     2 * DN_WIDTH + (h + 1) * LANES)].astype(F32)
            beta_b = jnp.broadcast_to(gates[cc][:, h:h + 1], (CHUNK, DN_HEAD))
            gcb = jnp.broadcast_to(gates[cc][:, DN_HEADS + h:DN_HEADS + h + 1], (CHUNK, DN_HEAD))
            gr = jnp.sum(gcb * eye_f, axis=0, keepdims=True)
            decay.append(jnp.exp(jnp.minimum(gcb - gr, 0.0)))
            g_last = gcb[CHUNK - 1:CHUNK, :]
            eg = jnp.exp(gcb)
            kb = k * beta_b
            ks.append(jnp.concatenate([k.astype(BF16), zero_rows], axis=0))
            lhs.append(jnp.concatenate([kb, q], axis=0).astype(BF16))
            rhs.append(jnp.concatenate([v * beta_b, kb * eg], axis=1).astype(BF16))
            qe.append((q * eg).astype(BF16))
            kdt.append((k * jnp.exp(g_last - gcb)).T.astype(BF16))
        yield

        kq = None
        for kq in halves(lambda ab: _dot_nt(*ab), list(zip(lhs, ks))):
            yield
        m = [x[:CHUNK] * (d * strict_f) for x, d in zip(kq, decay)]
        a_qk = [(x[CHUNK:, :CHUNK] * (d * lower_f)[:, :CHUNK]).astype(BF16)
                for x, d in zip(kq, decay)]

        w = None
        for w in halves(lambda x: _dot(x[:, :CHUNK], x - eye_hi) + eye_hi, m):
            yield
        for _ in range(5):
            prev = w
            for w in halves(lambda x: _dot(x[:, :CHUNK], x) + x * upper_f, prev):
                yield
        sol = None
        for sol in halves(lambda xb: _dot(xb[0][:, CHUNK:], xb[1]), list(zip(w, rhs))):
            yield

        for n, (cc, h) in enumerate(items):
            rows = slice(cc * CHUNK, (cc + 1) * CHUNK)
            rows_w = slice(2 * cc * CHUNK, (2 * cc + 1) * CHUNK)
            rows_q = slice((2 * cc + 1) * CHUNK, (2 * cc + 2) * CHUNK)
            rows_t = slice(2 * cc * CHUNK, (2 * cc + 2) * CHUNK)
            u_r[slot, rows, hcols[h]] = sol[n][:, :DN_HEAD]
            wq_r[slot, rows_w, hcols[h]] = sol[n][:, DN_HEAD:].astype(BF16)
            wq_r[slot, rows_q, hcols[h]] = qe[n]
            aqk_r[slot, rows, hcols_c[h]] = a_qk[n]
            kdt_r[slot, rows_t, hcols_c[h]] = kdt[n]
        yield

    def rec(g, slot):
        base = group_base(g)
        for cc in range(PREP_CHUNKS):
            rows = slice(cc * CHUNK, (cc + 1) * CHUNK)
            rows2 = slice(2 * cc * CHUNK, (2 * cc + 2) * CHUNK)
            trows = pl.ds(base + cc * CHUNK, CHUNK)
            st = [state_s[h] for h in heads]
            r = [jnp.dot(wq_r[slot, rows2, hcols[h]], st[h].astype(BF16),
                         preferred_element_type=F32) for h in heads]
            yield
            v_new = [(u_r[slot, rows, hcols[h]] - r[h][:CHUNK]).astype(BF16) for h in heads]
            upd = [jnp.dot(kdt_r[slot, rows2, hcols_c[h]], v_new[h], preferred_element_type=F32)
                   for h in heads]
            tail = gate_ref[pl.ds(base + (cc + 1) * CHUNK - 8, 8), :]
            decay_last = jnp.exp(tail[7:8, :])
            for h in heads:
                state_s[h] = st[h] * decay_last[:, DN_HEADS + h:DN_HEADS + h + 1] + upd[h]
            yield
            o = [r[h][CHUNK:] + jnp.dot(aqk_r[slot, rows, hcols_c[h]], v_new[h],
                                        preferred_element_type=F32) for h in heads]
            for h in heads:
                y = o[h] * lax.rsqrt(jnp.mean(o[h] * o[h], axis=-1, keepdims=True) + EPS) * gain
                gate = zg_ref[trows, hcols[h]].astype(F32)
                out_ref[trows, hcols[h]] = (y * gate).astype(out_ref.dtype)
            yield

    def run_interleaved(*gens):
        gens = list(gens)
        while gens:
            for gen in list(gens):
                if next(gen, StopIteration) is StopIteration:
                    gens.remove(gen)

    n_groups = n_chunks // PREP_CHUNKS
    state_s[...] = jnp.zeros_like(state_s)
    run_interleaved(prep(0, 0))

    def pair_body(j, carry):
        run_interleaved(rec(2 * j, 0), prep(2 * j + 1, 1))
        run_interleaved(rec(2 * j + 1, 1), prep(2 * j + 2, 0))
        return carry

    lax.fori_loop(0, (n_groups - 2) // 2, pair_body, 0)
    run_interleaved(rec(n_groups - 2, 0), prep(n_groups - 1, 1))
    run_interleaved(rec(n_groups - 1, 1))


def _gated_deltanet(proj, gates, out_gain):
    b, t, _ = proj.shape
    group_rows = PREP_CHUNKS * CHUNK
    assert t % (2 * group_rows) == 0 and t // group_rows >= 4
    ring = lambda rows, dtype: pltpu.VMEM((2, rows, DN_WIDTH), dtype)
    return pl.pallas_call(
        _gdn_kernel,
        grid=(b,),
        in_specs=[
            pl.BlockSpec((None, t, QKV_COLS), lambda i: (i, 0, 0)),
            pl.BlockSpec((None, t, DN_WIDTH), lambda i: (i, 0, QKV_COLS // DN_WIDTH)),
            pl.BlockSpec((None, t, N_GATES), lambda i: (i, 0, 0)),
            pl.BlockSpec((1, DN_HEAD), lambda i: (0, 0)),
        ],
        out_specs=pl.BlockSpec((None, t, DN_WIDTH), lambda i: (i, 0, 0)),
        out_shape=jax.ShapeDtypeStruct((b, t, DN_WIDTH), BF16),
        scratch_shapes=[
            ring(group_rows, F32),
            ring(2 * group_rows, BF16),
            ring(group_rows, BF16),
            ring(2 * group_rows, BF16),
            pltpu.VMEM((DN_HEADS, DN_HEAD, DN_HEAD), F32),
        ],
        compiler_params=pltpu.CompilerParams(
            dimension_semantics=("parallel",), vmem_limit_bytes=VMEM_LIMIT),
        name="gated_deltanet",
    )(proj, proj, gates, out_gain)


AT_BQ = 512
AT_BK = 256
AT_CW = 256
AT_NQ = 2 * AT_BQ
AT_CHAINS = AT_NQ // AT_CW
AT_CPM = AT_BQ // AT_CW
AT_VROWS = DF_HEAD_V + 16
AT_PROJ_DELAY = 3
AT_HPS = 2
OUT_TN = 256


def _attention_tiles(n_heads, n_qblocks):
    all_chains = tuple(range(AT_CHAINS))
    lo = tuple(c for c in all_chains if c % AT_CPM == 0)
    hi = tuple(c for c in all_chains if c % AT_CPM == 1)
    tiles = []
    for hd in range(n_heads):
        for qi in range(n_qblocks):
            for kt in range(AT_CPM * qi + AT_CPM):
                d = kt - AT_CPM * qi
                chains, masked = ((all_chains, ()) if d < 0 else
                                  (all_chains, lo) if d == 0 else (hi, hi))
                tiles.append(dict(hd=hd, qi=qi, qb=hd * n_qblocks + qi, kt=kt, chains=chains,
                                  masked=masked, first=kt == 0))
    return tiles


def _diff_attn_kernel(q_ref, k_ref, vt_ref, zg_ref, lq1_ref, lk1_ref, lq2_ref, lk2_ref,
                      og_ref, x_ref, dn_ref, wout_ref, out_ref,
                      qq_s, vta_s, m_s, acc_s, y_s, s_a, s_b, c_a, c_b, p_a, p_b):
    t = q_ref.shape[0]
    n_qblocks = t // AT_BQ
    all_chains = tuple(range(AT_CHAINS))

    @pl.when(pl.program_id(1) == 0)
    def _():
        out_ref[...] = x_ref[...]

    def lanes(c):
        return slice(c * AT_CW, (c + 1) * AT_CW)

    first = lax.broadcasted_iota(jnp.int32, (AT_BQ, LANES), 1) < DF_HEAD_QK
    zero = jnp.zeros((AT_BQ, LANES), BF16)
    hcols = [slice(hd * LANES, (hd + 1) * LANES) for hd in range(AT_HPS)]
    for hd in range(AT_HPS):
        for qi in range(n_qblocks):
            blk = q_ref[qi * AT_BQ:(qi + 1) * AT_BQ, hcols[hd]]
            qq_s[hd * n_qblocks + qi, :AT_BQ, :] = jnp.where(first, blk, zero)
            qq_s[hd * n_qblocks + qi, AT_BQ:, :] = jnp.where(first, zero, blk)
        vta_s[hd, :DF_HEAD_V, :] = vt_ref[hcols[hd], :]
        vta_s[hd, DF_HEAD_V:, :] = jnp.ones((AT_VROWS - DF_HEAD_V, t), BF16)

    lam = (jnp.exp(jnp.sum(lq1_ref[...] * lk1_ref[...], axis=-1, keepdims=True))
           - jnp.exp(jnp.sum(lq2_ref[...] * lk2_ref[...], axis=-1, keepdims=True))
           + LAMBDA_INIT)
    out_scale = og_ref[...] * (1.0 - LAMBDA_INIT)

    krow = lax.broadcasted_iota(jnp.int32, (AT_BK, AT_CW), 0)
    qcol = lax.broadcasted_iota(jnp.int32, (AT_BK, AT_CW), 1)
    causal = krow <= qcol
    bufs = ((s_a, c_a, p_a), (s_b, c_b, p_b))

    def scores(tile, c, buf):
        rows = slice(tile["kt"] * AT_BK, (tile["kt"] + 1) * AT_BK)
        s = _dot_nt(k_ref[rows, hcols[tile["hd"]]], qq_s[tile["qb"], lanes(c), :])
        if c in tile["masked"]:
            s = jnp.where(causal, s, -jnp.inf)
        buf[0][:, lanes(c)] = s
        buf[1][:, lanes(c)] = jnp.max(s, axis=0, keepdims=True)

    def values(tile, c, buf):
        rows = slice(tile["kt"] * AT_BK, (tile["kt"] + 1) * AT_BK)
        return jnp.dot(vta_s[tile["hd"], :, rows], buf[2][:, lanes(c)],
                       preferred_element_type=F32)

    def finish(tile):
        qb = tile["qb"]
        rows = slice(tile["qi"] * AT_BQ, (tile["qi"] + 1) * AT_BQ)
        inv = 1.0 / acc_s[qb, DF_HEAD_V:DF_HEAD_V + 1, :]
        o = (acc_s[qb, :DF_HEAD_V, :AT_BQ] * inv[:, :AT_BQ]
             - acc_s[qb, :DF_HEAD_V, AT_BQ:] * (lam * inv[:, AT_BQ:]))
        y = (o * lax.rsqrt(jnp.mean(o * o, axis=0, keepdims=True) + EPS)).T
        y = y * out_scale
        y_s[qb] = (y * zg_ref[rows, hcols[tile["hd"]]].astype(F32)).astype(BF16)

    def project(tile):
        rows = slice(tile["qi"] * AT_BQ, (tile["qi"] + 1) * AT_BQ)
        hd = tile["hd"]
        lhs = jnp.concatenate([y_s[tile["qb"]], dn_ref[rows, hcols[hd]]], axis=1)
        for nc in range(D_MODEL // OUT_TN):
            cols = slice(nc * OUT_TN, (nc + 1) * OUT_TN)
            out_ref[rows, cols] = out_ref[rows, cols] + jnp.dot(
                lhs, wout_ref[hd, :, cols], preferred_element_type=F32)

    tiles = _attention_tiles(AT_HPS, n_qblocks)
    due = {}
    for c in tiles[0]["chains"]:
        scores(tiles[0], c, bufs[0])
    for n, cur in enumerate(tiles):
        nxt = tiles[n + 1] if n + 1 < len(tiles) else None
        prv = tiles[n - 1] if n > 0 else None
        s_cur, c_cur, p_cur = bufs[n % 2]
        qb = cur["qb"]
        for c in all_chains:
            alpha = None
            if c in cur["chains"]:
                if cur["first"]:
                    m_new = c_cur[:, lanes(c)]
                else:
                    m_old = m_s[qb, :, lanes(c)]
                    m_new = jnp.maximum(m_old, c_cur[:, lanes(c)])
                    alpha = jnp.exp2(m_old - m_new)
                p = jnp.exp2(s_cur[:, lanes(c)] - m_new)
                m_s[qb, :, lanes(c)] = m_new
                p_cur[:, lanes(c)] = p.astype(BF16)
            if nxt is not None and c in nxt["chains"]:
                scores(nxt, c, bufs[(n + 1) % 2])
            if prv is not None and c in prv["chains"]:
                pv = values(prv, c, bufs[(n - 1) % 2])
                pq = prv["qb"]
                acc = pv if prv["first"] else acc_s[pq, :, lanes(c)] + pv
                if pq == qb and alpha is not None:
                    acc = alpha * acc
                acc_s[pq, :, lanes(c)] = acc
        if prv is not None and prv["qb"] != qb:
            finish(prv)
            due[n + AT_PROJ_DELAY] = prv
        if n in due:
            project(due.pop(n))
    last = tiles[-1]
    for c in last["chains"]:
        acc_s[last["qb"], :, lanes(c)] = (acc_s[last["qb"], :, lanes(c)]
                                          + values(last, c, bufs[(len(tiles) - 1) % 2]))
    finish(last)
    for tile in sorted(due.values(), key=lambda tl: tl["qb"]) + [last]:
        project(tile)


def _diff_attention(proj, vt, lq1, lk1, lq2, lk2, out_gain, x, mixed_dn, w_heads):
    b, t, _ = proj.shape
    assert AT_BQ == AT_CPM * AT_BK and AT_CW == AT_BK and AT_CPM == 2 and t % AT_BQ == 0
    assert DF_HEADS == DN_HEADS and DF_HEAD_V == DN_HEAD and D_MODEL % OUT_TN == 0
    assert DF_HEADS % AT_HPS == 0 and all(
        blk % AT_HPS == 0 for blk in (_BLK_DF_Q, _BLK_DF_K, _BLK_DF_Z))
    n_qblocks = t // AT_BQ
    n_blocks = AT_HPS * n_qblocks
    wide = AT_HPS * LANES
    small = lambda n: pl.BlockSpec((1, n), lambda i, h: (0, 0))
    return pl.pallas_call(
        _diff_attn_kernel,
        grid=(b, DF_HEADS // AT_HPS),
        in_specs=[
            pl.BlockSpec((None, t, wide), lambda i, h: (i, 0, _BLK_DF_Q // AT_HPS + h)),
            pl.BlockSpec((None, t, wide), lambda i, h: (i, 0, _BLK_DF_K // AT_HPS + h)),
            pl.BlockSpec((None, wide, t), lambda i, h: (i, h, 0)),
            pl.BlockSpec((None, t, wide), lambda i, h: (i, 0, _BLK_DF_Z // AT_HPS + h)),
            small(DF_HEAD_QK), small(DF_HEAD_QK), small(DF_HEAD_QK), small(DF_HEAD_QK),
            small(DF_HEAD_V),
            pl.BlockSpec((None, t, D_MODEL), lambda i, h: (i, 0, 0)),
            pl.BlockSpec((None, t, wide), lambda i, h: (i, 0, h)),
            pl.BlockSpec((AT_HPS, DF_HEAD_V + DN_HEAD, D_MODEL), lambda i, h: (h, 0, 0)),
        ],
        out_specs=pl.BlockSpec((None, t, D_MODEL), lambda i, h: (i, 0, 0)),
        out_shape=jax.ShapeDtypeStruct((b, t, D_MODEL), F32),
        scratch_shapes=[
            pltpu.VMEM((n_blocks, AT_NQ, LANES), BF16),
            pltpu.VMEM((AT_HPS, AT_VROWS, t), BF16),
            pltpu.VMEM((n_blocks, 1, AT_NQ), F32),
            pltpu.VMEM((n_blocks, AT_VROWS, AT_NQ), F32),
            pltpu.VMEM((n_blocks, AT_BQ, DF_HEAD_V), BF16),
            pltpu.VMEM((AT_BK, AT_NQ), F32),
            pltpu.VMEM((AT_BK, AT_NQ), F32),
            pltpu.VMEM((1, AT_NQ), F32),
            pltpu.VMEM((1, AT_NQ), F32),
            pltpu.VMEM((AT_BK, AT_NQ), BF16),
            pltpu.VMEM((AT_BK, AT_NQ), BF16),
        ],
        compiler_params=pltpu.CompilerParams(
            dimension_semantics=("parallel", "arbitrary"), vmem_limit_bytes=VMEM_LIMIT),
        name="diff_attention",
    )(proj, proj, vt, proj, lq1, lk1, lq2, lk2, out_gain, x, mixed_dn, w_heads)


def kernel(x, norm_gain, w_in, conv_w, a_log, dt_bias, dn_out_gain, q_gain, k_gain,
           lambda_q1, lambda_k1, lambda_q2, lambda_k2, df_out_gain, w_out):
    b, t, d = x.shape
    assert d == D_MODEL and norm_gain.shape[0] == 1
    x2d = x.reshape(b * t, d)

    w = w_in[0]
    gate_lo = 4 * DN_WIDTH
    df_lo = gate_lo + N_GATES
    w_main = jnp.concatenate([w[:, :gate_lo], w[:, df_lo:df_lo + 2 * DF_WIDTH],
                              w[:, df_lo + 3 * DF_WIDTH:]], axis=1).astype(BF16)
    w_t = jnp.concatenate([w[:, df_lo + 2 * DF_WIDTH:df_lo + 3 * DF_WIDTH],
                           jnp.pad(w[:, gate_lo:df_lo], ((0, 0), (0, GATE_ROWS - N_GATES)))],
                          axis=1).astype(BF16).T
    half = jnp.arange(SEG_W, dtype=jnp.int32) // DF_HEAD_QK
    seg = ((half[:, None] == half[None, :]).astype(F32) / DF_HEAD_QK).astype(BF16)
    reps = DF_WIDTH // DF_HEAD_QK
    qk_gain = jnp.concatenate([jnp.tile(q_gain, (1, reps)) * (DF_HEAD_QK ** -0.5 * LOG2_E),
                               jnp.tile(k_gain, (1, reps))], axis=0)

    pad_rows = lambda v: jnp.pad(v.reshape(DN_HEADS, 1),
                                 ((DN_HEADS, GATE_ROWS - 2 * DN_HEADS), (0, 0)))
    proj, vt, gates_t = _in_projection(x2d, t, norm_gain, w_main, w_t, seg, qk_gain,
                                       pad_rows(a_log), pad_rows(dt_bias), conv_w[0])
    proj = proj.reshape(b, t, PROJ_COLS)
    gates = jnp.swapaxes(gates_t[:, :N_GATES, :], 1, 2)

    mixed_dn = _gated_deltanet(proj, gates, dn_out_gain)

    wo = w_out[0].astype(BF16)
    w_heads = jnp.concatenate([wo[DN_WIDTH:].reshape(DF_HEADS, DF_HEAD_V, d),
                               wo[:DN_WIDTH].reshape(DN_HEADS, DN_HEAD, d)], axis=1)
    return _diff_attention(proj, vt, lambda_q1, lambda_k1, lambda_q2, lambda_k2,
                           df_out_gain, x, mixed_dn, w_heads)
```
